```python
import math
import jax, jax.numpy as jnp
from jax import lax
import numpy as np

D_MODEL = 1024
BATCH = 2
SEQ = 16384
DEPTH = 1

N_META = 16
POOL_WINDOWS = (2, 4, 8, 16)
POOL_GROUP = 128
POOL_WIDTH = POOL_GROUP * len(POOL_WINDOWS)
GLA_HEADS = 4
GLA_DK = D_MODEL // 2
GLA_DV = D_MODEL
HEAD_K = GLA_DK // GLA_HEADS
HEAD_V = GLA_DV // GLA_HEADS
GATE_RANK = 16
GATE_NORMALIZER = 16.0
CHUNK = 64
RMS_EPS = 1e-5
N_EXPERTS = 256
TOP_K = 8
N_GROUPS = 8
TOPK_GROUPS = 4
D_EXPERT = 256
D_SHARED = 256
ROUTED_SCALE = 2.5
MOE_BLOCK = 256
DN_ALPHA = (2.0 * DEPTH) ** 0.25
DN_BETA = (8.0 * DEPTH) ** -0.25
LN_EPS = 1e-5
SPLITS = (POOL_WIDTH, GLA_DK, GLA_DK, GLA_DV, GLA_DV, GATE_RANK, GATE_RANK, D_MODEL, D_MODEL)
D_IN = sum(SPLITS)
V_START = POOL_WIDTH + 2 * GLA_DK

kernel_name = 'hybrid_pool_gla_moe_encoder_block'


def layer_norm(x, g, b):
    xf = x.astype(jnp.float32)
    mu = xf.mean(-1, keepdims=True)
    var = jnp.square(xf - mu).mean(-1, keepdims=True)
    y = (xf - mu) * lax.rsqrt(var + LN_EPS) * g.astype(jnp.float32) + b.astype(jnp.float32)
    return y.astype(x.dtype)


def multiscale_pool(u, pool_w, pool_scale):
    B, L, _ = u.shape
    uf = u.astype(jnp.float32)
    cs = jnp.concatenate([jnp.zeros_like(uf[:, :1]), jnp.cumsum(uf, axis=1)], axis=1)
    pos = jnp.arange(L)
    outs = []
    for gi, w in enumerate(POOL_WINDOWS):
        lo = jnp.clip(pos - w // 2, 0, L)
        hi = jnp.clip(pos + w - w // 2, 0, L)
        c0, c1 = gi * POOL_GROUP, (gi + 1) * POOL_GROUP
        csg = cs[:, :, c0:c1]
        mean = (csg[:, hi] - csg[:, lo]) / (hi - lo).astype(jnp.float32)[None, :, None]
        outs.append(mean - uf[:, :, c0:c1])
    d = jnp.stack(outs, axis=2).astype(u.dtype)
    y = jnp.einsum('blgc,gcd->blgd', d, pool_w)
    return y.reshape(B, L, POOL_WIDTH) * pool_scale


def gla_chunk_scan(q, k, v, g):
    C = q.shape[3]
    mask = jnp.tril(jnp.ones((C, C), dtype=bool))[:, :, None]

    def step(S, inp):
        qc, kc, vc, gc = inp
        b = jnp.cumsum(gc, axis=2)
        b_last = b[:, :, -1]
        o_inter = jnp.einsum('bhtk,bhkv->bhtv', qc * jnp.exp(b), S)
        decay = jnp.exp(jnp.where(mask, b[:, :, :, None] - b[:, :, None], -jnp.inf))
        att = jnp.einsum('bhtk,bhsk,bhtsk->bhts', qc, kc, decay)
        o = o_inter + jnp.einsum('bhts,bhsv->bhtv', att, vc)
        S = jnp.exp(b_last)[..., None] * S + jnp.einsum(
            'bhsk,bhsv->bhkv', kc * jnp.exp(b_last[:, :, None] - b), vc)
        return S, o

    S0 = jnp.zeros(q.shape[1:3] + (q.shape[-1], v.shape[-1]), jnp.float32)
    _, o = lax.scan(step, S0, (q, k, v, g))
    return o


def gla_bidirectional(q, k, v, gk_f, gk_b):
    B, L, _ = q.shape
    front = (-N_META) % CHUNK
    back = (-(L - N_META)) % CHUNK
    Lp = L + front + back
    N = Lp // CHUNK

    def chunks(t, hd):
        t = jnp.pad(t.astype(jnp.float32), ((0, 0), (front, back), (0, 0)))
        return t.reshape(B, N, CHUNK, GLA_HEADS, hd).transpose(1, 0, 3, 2, 4)

    qc = chunks(q * (HEAD_K ** -0.5), HEAD_K)
    kc = chunks(k, HEAD_K)
    vc = chunks(v, HEAD_V)
    gfc = chunks(gk_f, HEAD_K)
    gbc = chunks(gk_b, HEAD_K)

    def flip(t):
        return t[::-1, :, :, ::-1]

    o_f = gla_chunk_scan(qc, kc, vc, gfc)
    o_b = flip(gla_chunk_scan(flip(qc), flip(kc), flip(vc), flip(gbc)))
    o = (o_f + o_b).transpose(1, 0, 3, 2, 4).reshape(B, Lp, GLA_HEADS, HEAD_V)
    return o[:, front:front + L]


def token_mixer(h, w_in, b_in, pool_w, pool_scale, gate_w_f, gate_b_f, gate_w_b, gate_b_b,
                gla_norm_g, w_branch_pool, w_branch_gla, w_out):
    B, L, _ = h.shape
    proj = jnp.einsum('bld,de->ble', h, w_in) + b_in
    idx = [int(i) for i in np.cumsum(SPLITS)[:-1]]
    u_pool, q, k, v, g_out, lr_f, lr_b, gate_pool, gate_gla = jnp.split(proj, idx, axis=-1)
    y_pool = multiscale_pool(u_pool, pool_w, pool_scale)
    gk_f = jax.nn.log_sigmoid((lr_f @ gate_w_f + gate_b_f).astype(jnp.float32)) / GATE_NORMALIZER
    gk_b = jax.nn.log_sigmoid((lr_b @ gate_w_b + gate_b_b).astype(jnp.float32)) / GATE_NORMALIZER
    o = gla_bidirectional(q, k, v, gk_f, gk_b)
    o = o * lax.rsqrt(jnp.mean(jnp.square(o), -1, keepdims=True) + RMS_EPS) * gla_norm_g.astype(jnp.float32)
    o = o * jax.nn.silu(g_out.astype(jnp.float32)).reshape(B, L, GLA_HEADS, HEAD_V)
    y_gla = o.reshape(B, L, GLA_DV).astype(h.dtype)
    merged = (jax.nn.sigmoid(gate_pool) * (y_pool @ w_branch_pool)
              + jax.nn.sigmoid(gate_gla) * (y_gla @ w_branch_gla))
    return merged @ w_out


def route(xt, w_router, router_bias):
    T = xt.shape[0]
    scores = jax.nn.sigmoid(xt.astype(jnp.float32) @ w_router.astype(jnp.float32))
    sel = scores + router_bias.astype(jnp.float32)
    grp = sel.reshape(T, N_GROUPS, N_EXPERTS // N_GROUPS)
    grp_score = lax.top_k(grp, 2)[0].sum(-1)
    _, gidx = lax.top_k(grp_score, TOPK_GROUPS)
    gmask = jax.nn.one_hot(gidx, N_GROUPS, dtype=jnp.float32).sum(-2) > 0
    emask = jnp.repeat(gmask, N_EXPERTS // N_GROUPS, axis=-1)
    _, eidx = lax.top_k(jnp.where(emask, sel, -jnp.inf), TOP_K)
    w = jnp.take_along_axis(scores, eidx, axis=-1)
    w = w / w.sum(-1, keepdims=True) * ROUTED_SCALE
    return eidx.astype(jnp.int32), w


def routed_experts(xt, eidx, ew, w_gate, w_up, w_down):
    T, D = xt.shape
    A = T * TOP_K
    e_flat = eidx.reshape(A)
    t_flat = jnp.repeat(jnp.arange(T, dtype=jnp.int32), TOP_K)
    w_flat = ew.reshape(A)
    order = jnp.argsort(e_flat)
    e_s, t_s, w_s = e_flat[order], t_flat[order], w_flat[order]
    counts = jnp.bincount(e_flat, length=N_EXPERTS)
    padded = ((counts + MOE_BLOCK - 1) // MOE_BLOCK) * MOE_BLOCK
    pend = jnp.cumsum(padded)
    pstart = pend - padded
    start = jnp.cumsum(counts) - counts
    dest = pstart[e_s] + jnp.arange(A, dtype=jnp.int32) - start[e_s]
    n_blocks = -(-A // MOE_BLOCK) + N_EXPERTS
    P = n_blocks * MOE_BLOCK
    tok_buf = jnp.full((P,), T, jnp.int32).at[dest].set(t_s)
    w_buf = jnp.zeros((P,), jnp.float32).at[dest].set(w_s)
    block_expert = jnp.minimum(
        jnp.searchsorted(pend, jnp.arange(n_blocks, dtype=pend.dtype) * MOE_BLOCK, side='right'),
        N_EXPERTS - 1)
    x_pad = jnp.concatenate([xt, jnp.zeros((1, D), xt.dtype)], axis=0)

    def step(acc, inp):
        tok, wb, e = inp
        xb = x_pad[tok]
        hb = jax.nn.silu(xb @ w_gate[e]) * (xb @ w_up[e])
        yb = (hb @ w_down[e]).astype(jnp.float32) * wb[:, None]
        return acc.at[tok].add(yb), None

    acc, _ = lax.scan(step, jnp.zeros((T + 1, D), jnp.float32),
                      (tok_buf.reshape(n_blocks, MOE_BLOCK), w_buf.reshape(n_blocks, MOE_BLOCK), block_expert))
    return acc[:T]


def moe_ffn(h, w_router, router_bias, w_exp_gate, w_exp_up, w_exp_down, w_sh_gate, w_sh_up, w_sh_down):
    B, L, D = h.shape
    xt = h.reshape(B * L, D)
    eidx, ew = route(xt, w_router, router_bias)
    routed = routed_experts(xt, eidx, ew, w_exp_gate, w_exp_up, w_exp_down)
    shared = (jax.nn.silu(xt @ w_sh_gate) * (xt @ w_sh_up)) @ w_sh_down
    return (shared + routed.astype(xt.dtype)).reshape(B, L, D)


def setup_inputs(seed: int = 0) -> dict:
    key = jax.random.key(seed)
    ks = jax.random.split(key, 32)
    f32 = jnp.float32

    def nrm(k, shape, s):
        return jax.random.normal(k, shape, f32) * s

    col_scale = jnp.ones((D_IN,), f32).at[V_START:V_START + GLA_DV].set(DN_BETA)
    return {
        'x': nrm(ks[0], (BATCH, SEQ, D_MODEL), 1.0),
        'meta': nrm(ks[1], (N_META, D_MODEL), 1.0),
        'ln0_g': 1.0 + nrm(ks[2], (D_MODEL,), 0.02),
        'ln0_b': nrm(ks[3], (D_MODEL,), 0.02),
        'w_in': nrm(ks[4], (DEPTH, D_MODEL, D_IN), D_MODEL ** -0.5) * col_scale,
        'b_in': nrm(ks[5], (DEPTH, D_IN), 0.02),
        'pool_w': nrm(ks[6], (DEPTH, len(POOL_WINDOWS), POOL_GROUP, POOL_GROUP), POOL_GROUP ** -0.5),
        'pool_scale': 1.0 + nrm(ks[7], (DEPTH, POOL_WIDTH), 0.02),
        'gate_w_f': nrm(ks[8], (DEPTH, GATE_RANK, GLA_DK), GATE_RANK ** -0.5),
        'gate_b_f': nrm(ks[9], (DEPTH, GLA_DK), 0.02),
        'gate_w_b': nrm(ks[10], (DEPTH, GATE_RANK, GLA_DK), GATE_RANK ** -0.5),
        'gate_b_b': nrm(ks[11], (DEPTH, GLA_DK), 0.02),
        'gla_norm_g': 1.0 + nrm(ks[12], (DEPTH, HEAD_V), 0.02),
        'w_branch_pool': nrm(ks[13], (DEPTH, POOL_WIDTH, D_MODEL), POOL_WIDTH ** -0.5),
        'w_branch_gla': nrm(ks[14], (DEPTH, GLA_DV, D_MODEL), GLA_DV ** -0.5),
        'w_out': nrm(ks[15], (DEPTH, D_MODEL, D_MODEL), D_MODEL ** -0.5 * DN_BETA),
        'ln1_g': 1.0 + nrm(ks[16], (DEPTH, D_MODEL), 0.02),
        'ln1_b': nrm(ks[17], (DEPTH, D_MODEL), 0.02),
        'w_router': nrm(ks[18], (DEPTH, D_MODEL, N_EXPERTS), D_MODEL ** -0.5),
        'router_bias': nrm(ks[19], (DEPTH, N_EXPERTS), 0.01),
        'w_exp_gate': nrm(ks[20], (DEPTH, N_EXPERTS, D_MODEL, D_EXPERT), D_MODEL ** -0.5),
        'w_exp_up': nrm(ks[21], (DEPTH, N_EXPERTS, D_MODEL, D_EXPERT), D_MODEL ** -0.5),
        'w_exp_down': nrm(ks[22], (DEPTH, N_EXPERTS, D_EXPERT, D_MODEL), D_EXPERT ** -0.5 * DN_BETA),
        'w_sh_gate': nrm(ks[23], (DEPTH, D_MODEL, D_SHARED), D_MODEL ** -0.5),
        'w_sh_up': nrm(ks[24], (DEPTH, D_MODEL, D_SHARED), D_MODEL ** -0.5),
        'w_sh_down': nrm(ks[25], (DEPTH, D_SHARED, D_MODEL), D_SHARED ** -0.5 * DN_BETA),
        'ln2_g': 1.0 + nrm(ks[26], (DEPTH, D_MODEL), 0.02),
        'ln2_b': nrm(ks[27], (DEPTH, D_MODEL), 0.02),
    }


def reference(x, meta, ln0_g, ln0_b, w_in, b_in, pool_w, pool_scale, gate_w_f, gate_b_f, gate_w_b, gate_b_b,
              gla_norm_g, w_branch_pool, w_branch_gla, w_out, ln1_g, ln1_b, w_router, router_bias,
              w_exp_gate, w_exp_up, w_exp_down, w_sh_gate, w_sh_up, w_sh_down, ln2_g, ln2_b):
    B = x.shape[0]
    h = jnp.concatenate([jnp.broadcast_to(meta[None].astype(x.dtype), (B, N_META, D_MODEL)), x], axis=1)
    h = layer_norm(h, ln0_g, ln0_b)
    for i in range(DEPTH):
        y = token_mixer(h, w_in[i], b_in[i], pool_w[i], pool_scale[i], gate_w_f[i], gate_b_f[i],
                        gate_w_b[i], gate_b_b[i], gla_norm_g[i], w_branch_pool[i], w_branch_gla[i], w_out[i])
        h = layer_norm(DN_ALPHA * h + y, ln1_g[i], ln1_b[i])
        if i == DEPTH - 1:
            h = h[:, N_META:]
        z = moe_ffn(h, w_router[i], router_bias[i], w_exp_gate[i], w_exp_up[i], w_exp_down[i],
                    w_sh_gate[i], w_sh_up[i], w_sh_down[i])
        h = layer_norm(DN_ALPHA * h + z, ln2_g[i], ln2_b[i])
    return h
```

```python
import functools

import jax
import jax.numpy as jnp
from jax import lax
from jax.experimental import pallas as pl
from jax.experimental.pallas import tpu as pltpu

F32 = jnp.float32
BF16 = jnp.bfloat16
I32 = jnp.int32
U32 = jnp.uint32
HIGHEST = lax.Precision.HIGHEST

D_MODEL = 1024
N_META = 16
POOL_WINDOWS = (2, 4, 8, 16)
POOL_GROUP = 128
POOL_WIDTH = POOL_GROUP * len(POOL_WINDOWS)
GLA_HEADS = 4
GLA_DK = 512
GLA_DV = 1024
HEAD_K = GLA_DK // GLA_HEADS
HEAD_V = GLA_DV // GLA_HEADS
GATE_RANK = 16
GATE_NORMALIZER = 16.0
RMS_EPS = 1e-5
N_EXPERTS = 256
TOP_K = 8
N_GROUPS = 8
GROUP_SIZE = N_EXPERTS // N_GROUPS
TOPK_GROUPS = 4
D_EXPERT = 256
ROUTED_SCALE = 2.5
DN_ALPHA = 2.0 ** 0.25
LN_EPS = 1e-5

LANE = 128
ROWS = 256
FRONT = ROWS
GLA_CHUNK = 128
HALO = 16
ROUTE_ROWS = 512
MOE_BLOCK = 256
VMEM_LIMIT = 56 * 1024 * 1024

C_POOL, C_Q, C_K, C_V, C_G, C_GP, C_GG, C_LR = 0, 512, 1024, 1536, 2560, 3584, 4608, 5632
W_PACKED = C_LR + LANE


def _layer_norm(x, g, b):
    mu = jnp.mean(x, axis=-1, keepdims=True)
    xc = x - mu
    var = jnp.mean(xc * xc, axis=-1, keepdims=True)
    return xc * lax.rsqrt(var + LN_EPS) * g + b


def _dot(a, b):
    return jnp.dot(a, b, preferred_element_type=F32)


def _dot_nt(a, b):
    return lax.dot_general(a, b, (((1,), (1,)), ((), ())), preferred_element_type=F32)


def _split_dot(m, g):
    g_hi = g.astype(BF16)
    g_lo = (g - g_hi.astype(F32)).astype(BF16)
    return _dot(m, g_hi) + _dot(m, g_lo)


def _pack_halves(a, b):
    pa = lax.bitcast_convert_type(a.astype(BF16).astype(F32), U32)
    pb = lax.bitcast_convert_type(b.astype(BF16).astype(F32), U32)
    return pa | (pb >> 16)


def _unpack_halves(p):
    a = lax.bitcast_convert_type(p & jnp.uint32(0xFFFF0000), F32)
    b = lax.bitcast_convert_type(p << 16, F32)
    return a, b


def _const_spec(shape):
    return pl.BlockSpec(shape, lambda *_: (0,) * len(shape))


def _inproj_body(x_ref, g0_ref, b0_ref, w_ref, bias_ref, wgate_ref, bgate_ref,
                 u_ref, q_ref, k_ref, v_ref, sg_ref, gp_ref, gg_ref, bf_ref, cb_ref,
                 *, tiles_per_batch):
    i = pl.program_id(0)
    h = _layer_norm(x_ref[...], g0_ref[...], b0_ref[...])
    hb = h.astype(BF16)

    def proj(c0, n):
        return _dot(hb, w_ref[:, c0:c0 + n]) + bias_ref[:, c0:c0 + n]

    row = lax.broadcasted_iota(I32, (ROWS, 1), 0)
    valid = row >= jnp.where(i % tiles_per_batch == 0, FRONT - N_META, 0)

    u_ref[...] = proj(C_POOL, POOL_WIDTH).astype(BF16)
    q_ref[...] = (proj(C_Q, GLA_DK) * (HEAD_K ** -0.5)).astype(BF16)
    k_ref[...] = jnp.where(valid, proj(C_K, GLA_DK), 0.0).astype(BF16)
    v_ref[...] = jnp.where(valid, proj(C_V, GLA_DV), 0.0).astype(BF16)
    g = proj(C_G, GLA_DV)
    sg_ref[...] = (g * jax.nn.sigmoid(g)).astype(BF16)
    gp_ref[...] = jax.nn.sigmoid(proj(C_GP, D_MODEL)).astype(BF16)
    gg_ref[...] = jax.nn.sigmoid(proj(C_GG, D_MODEL)).astype(BF16)

    lr = proj(C_LR, LANE)
    xg = jnp.dot(lr, wgate_ref[...], precision=HIGHEST, preferred_element_type=F32) + bgate_ref[...]
    gk = (jnp.minimum(xg, 0.0) - jnp.log(1.0 + jnp.exp(-jnp.abs(xg)))) * (1.0 / GATE_NORMALIZER)

    r = lax.broadcasted_iota(I32, (GLA_CHUNK, GLA_CHUNK), 0)
    c = lax.broadcasted_iota(I32, (GLA_CHUNK, GLA_CHUNK), 1)
    tril = (r >= c).astype(BF16)
    triu = (r <= c).astype(BF16)
    for ci in range(ROWS // GLA_CHUNK):
        sl = slice(ci * GLA_CHUNK, (ci + 1) * GLA_CHUNK)
        bf_ref[sl, :] = _split_dot(tril, gk[sl, :GLA_DK])
        cb_ref[sl, :] = _split_dot(triu, gk[sl, GLA_DK:])


def _in_proj(xp, ln0_g, ln0_b, w_packed, b_packed, w_gate, b_gate, tiles_per_batch):
    n = xp.shape[0]
    row_spec = lambda width: pl.BlockSpec((ROWS, width), lambda i: (i, 0))
    out_widths = (POOL_WIDTH, GLA_DK, GLA_DK, GLA_DV, GLA_DV, D_MODEL, D_MODEL)
    out_shape = [jax.ShapeDtypeStruct((n, w), BF16) for w in out_widths]
    out_shape += [jax.ShapeDtypeStruct((n, GLA_DK), F32)] * 2
    out_specs = [row_spec(w) for w in out_widths] + [row_spec(GLA_DK)] * 2
    return pl.pallas_call(
        functools.partial(_inproj_body, tiles_per_batch=tiles_per_batch),
        grid=(n // ROWS,),
        in_specs=[row_spec(D_MODEL), _const_spec((1, D_MODEL)), _const_spec((1, D_MODEL)),
                  _const_spec((D_MODEL, W_PACKED)), _const_spec((1, W_PACKED)),
                  _const_spec((LANE, 2 * GLA_DK)), _const_spec((1, 2 * GLA_DK))],
        out_specs=out_specs,
        out_shape=out_shape,
        compiler_params=pltpu.CompilerParams(dimension_semantics=("arbitrary",),
                                             vmem_limit_bytes=VMEM_LIMIT),
        name="in_proj",
    )(xp, ln0_g, ln0_b, w_packed, b_packed, w_gate, b_gate)


def _gla_direction(q_ref, k_ref, v_ref, b_ref, o_ref, s_ref, reverse):
    c = GLA_CHUNK
    r = lax.broadcasted_iota(I32, (c, c), 0)
    s = lax.broadcasted_iota(I32, (c, c), 1)
    keep = (r <= s) if reverse else (r >= s)
    eye = lax.broadcasted_iota(I32, (HEAD_K, HEAD_K), 0) == lax.broadcasted_iota(I32, (HEAD_K, HEAD_K), 1)
    edge = 0 if reverse else c - 1
    for h in range(GLA_HEADS):
        ks = slice(h * HEAD_K, (h + 1) * HEAD_K)
        vs = slice(h * HEAD_V, (h + 1) * HEAD_V)
        b = b_ref[:, ks]
        qh = q_ref[:, ks].astype(F32)
        kh = k_ref[:, ks].astype(F32)
        vh = v_ref[:, vs]
        b_tot = b[edge:edge + 1, :]
        b_mid = b[c // 2:c // 2 + 1, :]
        q_state = (qh * jnp.exp(b)).astype(BF16)
        q_in = (qh * jnp.exp(b - b_mid)).astype(BF16)
        k_in = (kh * jnp.exp(b_mid - b)).astype(BF16)
        k_state = (kh * jnp.exp(b_tot - b)).T.astype(BF16)
        att = jnp.where(keep, _dot_nt(q_in, k_in), 0.0).astype(BF16)
        state = s_ref[h]
        o = _dot(q_state, state.astype(BF16)) + _dot(att, vh)
        o_ref[:, vs] = o.astype(o_ref.dtype)
        decay_col = jnp.sum(jnp.where(eye, jnp.exp(b_tot), 0.0), axis=1, keepdims=True)
        s_ref[h] = decay_col * state + _dot(k_state, vh)


def _gla_body(qf_ref, kf_ref, vf_ref, bf_ref, qb_ref, kb_ref, vb_ref, cb_ref,
              of_ref, ob_ref, sf_ref, sb_ref):
    @pl.when(pl.program_id(1) == 0)
    def _():
        sf_ref[...] = jnp.zeros_like(sf_ref)
        sb_ref[...] = jnp.zeros_like(sb_ref)

    _gla_direction(qf_ref, kf_ref, vf_ref, bf_ref, of_ref, sf_ref, reverse=False)
    _gla_direction(qb_ref, kb_ref, vb_ref, cb_ref, ob_ref, sb_ref, reverse=True)


def _gla(q, k, v, bf, cb, batch):
    n = q.shape[0]
    nch = n // batch // GLA_CHUNK
    fwd = lambda width: pl.BlockSpec((GLA_CHUNK, width), lambda b, i: (b * nch + i, 0))
    bwd = lambda width: pl.BlockSpec((GLA_CHUNK, width), lambda b, i: (b * nch + nch - 1 - i, 0))
    state = pltpu.VMEM((GLA_HEADS, HEAD_K, HEAD_V), F32)
    return pl.pallas_call(
        _gla_body,
        grid=(batch, nch),
        in_specs=[fwd(GLA_DK), fwd(GLA_DK), fwd(GLA_DV), fwd(GLA_DK),
                  bwd(GLA_DK), bwd(GLA_DK), bwd(GLA_DV), bwd(GLA_DK)],
        out_specs=[fwd(GLA_DV), bwd(GLA_DV)],
        out_shape=[jax.ShapeDtypeStruct((n, GLA_DV), BF16)] * 2,
        scratch_shapes=[state, state],
        compiler_params=pltpu.CompilerParams(dimension_semantics=("arbitrary", "arbitrary"),
                                             vmem_limit_bytes=VMEM_LIMIT),
        name="gla",
    )(q, k, v, bf, q, k, v, cb)


def _mix_body(x_ref, up_ref, u_ref, un_ref, gp_ref, gg_ref, sg_ref, of_ref, ob_ref,
              g0_ref, b0_ref, pw_ref, ps_ref, gn_ref, wbp_ref, wbg_ref, wo_ref, g1_ref, b1_ref,
              h1_ref, h1p_ref, *, tiles_per_batch, seq):
    j = pl.program_id(0) % tiles_per_batch
    u_main = u_ref[...]
    u_ext = jnp.concatenate([up_ref[...], u_main, un_ref[...]], axis=0)
    r = lax.broadcasted_iota(I32, (ROWS, ROWS + 2 * HALO), 0)
    e = lax.broadcasted_iota(I32, (ROWS, ROWS + 2 * HALO), 1) - HALO
    in_seq = (j * ROWS + e) < seq
    pos = j * ROWS + lax.broadcasted_iota(I32, (ROWS, 1), 0)
    y_pool = []
    for gi, w in enumerate(POOL_WINDOWS):
        cs = slice(gi * POOL_GROUP, (gi + 1) * POOL_GROUP)
        band = jnp.logical_and(jnp.logical_and(e >= r - w // 2, e < r + w - w // 2), in_seq)
        count = w - jnp.maximum(pos + (w - w // 2) - seq, 0)
        mean = _dot(band.astype(BF16), u_ext[:, cs]) / count.astype(F32)
        d = mean - u_main[:, cs].astype(F32)
        y_pool.append(_dot(d.astype(BF16), pw_ref[gi]))
    y_pool = (jnp.concatenate(y_pool, axis=1) * ps_ref[...]).astype(BF16)

    o = of_ref[...].astype(F32) + ob_ref[...].astype(F32)
    y_gla = []
    for h in range(GLA_HEADS):
        oh = o[:, h * HEAD_V:(h + 1) * HEAD_V]
        y_gla.append(oh * lax.rsqrt(jnp.mean(oh * oh, axis=-1, keepdims=True) + RMS_EPS))
    y_gla = (jnp.concatenate(y_gla, axis=1) * gn_ref[...] * sg_ref[...].astype(F32)).astype(BF16)

    merged = (gp_ref[...].astype(F32) * _dot(y_pool, wbp_ref[...])
              + gg_ref[...].astype(F32) * _dot(y_gla, wbg_ref[...]))
    y = _dot(merged.astype(BF16), wo_ref[...])
    h = _layer_norm(x_ref[...], g0_ref[...], b0_ref[...])
    h1 = _layer_norm(DN_ALPHA * h + y, g1_ref[...], b1_ref[...])
    h1_ref[...] = h1
    h1p_ref[...] = _pack_halves(h1[:, :D_MODEL // 2], h1[:, D_MODEL // 2:])


def _mix(x2, u, gp, gg, sg, o_f, o_b, ln0_g, ln0_b, pool_w, pool_scale, norm_g, wbp, wbg, wo,
         ln1_g, ln1_b, seq):
    t = x2.shape[0]
    tiles_per_batch = seq // ROWS
    padded_tiles = tiles_per_batch + FRONT // ROWS
    halo_per_tile = ROWS // HALO
    last_halo = u.shape[0] // HALO - 1

    def pidx(i):
        return (i // tiles_per_batch) * padded_tiles + FRONT // ROWS + i % tiles_per_batch

    padded = lambda width: pl.BlockSpec((ROWS, width), lambda i: (pidx(i), 0))
    plain = lambda width: pl.BlockSpec((ROWS, width), lambda i: (i, 0))
    prev_halo = pl.BlockSpec((HALO, POOL_WIDTH), lambda i: (pidx(i) * halo_per_tile - 1, 0))
    next_halo = pl.BlockSpec(
        (HALO, POOL_WIDTH), lambda i: (jnp.minimum((pidx(i) + 1) * halo_per_tile, last_halo), 0))
    return pl.pallas_call(
        functools.partial(_mix_body, tiles_per_batch=tiles_per_batch, seq=seq),
        grid=(t // ROWS,),
        in_specs=[plain(D_MODEL), prev_halo, padded(POOL_WIDTH), next_halo,
                  padded(D_MODEL), padded(D_MODEL), padded(GLA_DV), padded(GLA_DV), padded(GLA_DV),
                  _const_spec((1, D_MODEL)), _const_spec((1, D_MODEL)),
                  _const_spec((len(POOL_WINDOWS), POOL_GROUP, POOL_GROUP)), _const_spec((1, POOL_WIDTH)),
                  _const_spec((1, GLA_DV)), _const_spec((POOL_WIDTH, D_MODEL)),
                  _const_spec((GLA_DV, D_MODEL)), _const_spec((D_MODEL, D_MODEL)),
                  _const_spec((1, D_MODEL)), _const_spec((1, D_MODEL))],
        out_specs=[plain(D_MODEL), plain(D_MODEL // 2)],
        out_shape=[jax.ShapeDtypeStruct((t, D_MODEL), F32), jax.ShapeDtypeStruct((t, D_MODEL // 2), U32)],
        compiler_params=pltpu.CompilerParams(dimension_semantics=("arbitrary",),
                                             vmem_limit_bytes=VMEM_LIMIT),
        name="mix",
    )(x2, u, u, u, gp, gg, sg, o_f, o_b, ln0_g, ln0_b, pool_w, pool_scale, norm_g, wbp, wbg, wo,
      ln1_g, ln1_b)


def _first_index_of_max(val, idx, size):
    m = jnp.max(val, axis=0, keepdims=True)
    first = jnp.min(jnp.where(val == m, idx, size), axis=0, keepdims=True)
    return m, first


def _route_body(h_ref, wr_ref, bias_ref, eidx_ref, w_ref, rank_ref, cnt_ref, base_ref):
    n = ROUTE_ROWS

    @pl.when(pl.program_id(0) == 0)
    def _():
        base_ref[...] = jnp.zeros_like(base_ref)

    logits = lax.dot_general(wr_ref[...], h_ref[...], (((1,), (1,)), ((), ())),
                             precision=HIGHEST, preferred_element_type=F32)
    scores = jax.nn.sigmoid(logits)
    sel = scores + bias_ref[...]

    gidx = lax.broadcasted_iota(I32, (GROUP_SIZE, n), 0)
    groups = [sel[g * GROUP_SIZE:(g + 1) * GROUP_SIZE, :] for g in range(N_GROUPS)]
    gscore = []
    for grp in groups:
        m1, first = _first_index_of_max(grp, gidx, GROUP_SIZE)
        m2 = jnp.max(jnp.where(gidx == first, -jnp.inf, grp), axis=0, keepdims=True)
        gscore.append(m1 + m2)

    masked = []
    for gi in range(N_GROUPS):
        beaten = jnp.zeros((1, n), F32)
        for gj in range(N_GROUPS):
            if gj != gi:
                wins = (gscore[gj] >= gscore[gi]) if gj < gi else (gscore[gj] > gscore[gi])
                beaten = beaten + wins.astype(F32)
        masked.append(jnp.where(beaten < TOPK_GROUPS, groups[gi], -jnp.inf))
    val = jnp.concatenate(masked, axis=0)

    eidx = lax.broadcasted_iota(I32, (N_EXPERTS, n), 0)
    chosen_f = jnp.zeros((N_EXPERTS, n), F32)
    picks, weights = [], []
    for _ in range(TOP_K):
        _, first = _first_index_of_max(val, eidx, N_EXPERTS)
        hit = eidx == first
        picks.append(first)
        weights.append(jnp.sum(jnp.where(hit, scores, 0.0), axis=0, keepdims=True))
        chosen_f = jnp.where(hit, 1.0, chosen_f)
        val = jnp.where(hit, -jnp.inf, val)
    wsum = weights[0]
    for wk in weights[1:]:
        wsum = wsum + wk
    scale = ROUTED_SCALE / wsum

    r = lax.broadcasted_iota(I32, (n, n), 0)
    c = lax.broadcasted_iota(I32, (n, n), 1)
    earlier = (r < c).astype(BF16)
    before = _dot(chosen_f.astype(BF16), earlier) + base_ref[...]
    base_ref[...] = base_ref[...] + jnp.sum(chosen_f, axis=1, keepdims=True)
    cnt_ref[...] = jnp.broadcast_to(base_ref[...], cnt_ref.shape).astype(I32)

    for kk in range(TOP_K):
        eidx_ref[kk:kk + 1, :] = picks[kk]
        w_ref[kk:kk + 1, :] = weights[kk] * scale
        rank = jnp.sum(jnp.where(eidx == picks[kk], before, 0.0), axis=0, keepdims=True)
        rank_ref[kk:kk + 1, :] = rank.astype(I32)


def _route(h1, wr_t, bias_col):
    t = h1.shape[0]
    col = lambda dtype: jax.ShapeDtypeStruct((TOP_K, t), dtype)
    kspec = pl.BlockSpec((TOP_K, ROUTE_ROWS), lambda i: (0, i))
    return pl.pallas_call(
        _route_body,
        grid=(t // ROUTE_ROWS,),
        in_specs=[pl.BlockSpec((ROUTE_ROWS, D_MODEL), lambda i: (i, 0)),
                  _const_spec((N_EXPERTS, D_MODEL)), _const_spec((N_EXPERTS, 1))],
        out_specs=[kspec, kspec, kspec, _const_spec((N_EXPERTS, LANE))],
        out_shape=[col(I32), col(F32), col(I32), jax.ShapeDtypeStruct((N_EXPERTS, LANE), I32)],
        scratch_shapes=[pltpu.VMEM((N_EXPERTS, 1), F32)],
        compiler_params=pltpu.CompilerParams(dimension_semantics=("arbitrary",),
                                             vmem_limit_bytes=VMEM_LIMIT),
        name="route",
    )(h1, wr_t, bias_col)


def _dispatch_body(dest_ref, h1p_ref, zeros_ref, xs_ref, sem):
    del zeros_ref
    base = pl.program_id(0) * ROWS

    def issue(t, carry):
        for kk in range(TOP_K):
            pltpu.make_async_copy(h1p_ref.at[pl.ds(base + t, 1)],
                                  xs_ref.at[pl.ds(dest_ref[kk, t], 1)], sem).start()
        return carry

    lax.fori_loop(0, ROWS, issue, 0)
    for _ in range(TOP_K):
        pltpu.make_async_copy(h1p_ref.at[pl.ds(0, ROWS)], xs_ref.at[pl.ds(0, ROWS)], sem).wait()


def _dispatch(dest, h1p, n_rows):
    t, width = h1p.shape
    zeros = jnp.zeros((n_rows, width), U32)
    return pl.pallas_call(
        _dispatch_body,
        grid=(t // ROWS,),
        in_specs=[pl.BlockSpec((TOP_K, ROWS), lambda i: (0, i), memory_space=pltpu.SMEM),
                  pl.BlockSpec(memory_space=pl.ANY), pl.BlockSpec(memory_space=pl.ANY)],
        out_specs=pl.BlockSpec(memory_space=pl.ANY),
        out_shape=jax.ShapeDtypeStruct((n_rows, width), U32),
        scratch_shapes=[pltpu.SemaphoreType.DMA],
        input_output_aliases={2: 0},
        compiler_params=pltpu.CompilerParams(dimension_semantics=("arbitrary",)),
        name="dispatch",
    )(dest, h1p, zeros)


def _experts_body(be_ref, nu_ref, xs_ref, wg_ref, wu_ref, wd_ref, ys_ref):
    del be_ref

    @pl.when(pl.program_id(0) < nu_ref[0])
    def _():
        xa, xb = _unpack_halves(xs_ref[...])
        xa = xa.astype(BF16)
        xb = xb.astype(BF16)
        half = D_MODEL // 2
        wg = wg_ref[0].astype(BF16)
        wu = wu_ref[0].astype(BF16)
        gate = _dot(xa, wg[:half]) + _dot(xb, wg[half:])
        up = _dot(xa, wu[:half]) + _dot(xb, wu[half:])
        hidden = (gate * jax.nn.sigmoid(gate) * up).astype(BF16)
        y = _dot(hidden, wd_ref[0].astype(BF16))
        ys_ref[...] = _pack_halves(y[:, :half], y[:, half:])

    @pl.when(pl.program_id(0) >= nu_ref[0])
    def _():
        ys_ref[...] = jnp.zeros_like(ys_ref)


def _experts(block_expert, n_used, xs, w_gate, w_up, w_down):
    n_rows, width = xs.shape
    n_blocks = n_rows // MOE_BLOCK
    rows = lambda i, be, nu: (jnp.minimum(i, nu[0] - 1), 0)
    wspec = lambda shape: pl.BlockSpec((1,) + shape, lambda i, be, nu: (be[i], 0, 0))
    grid_spec = pltpu.PrefetchScalarGridSpec(
        num_scalar_prefetch=2,
        grid=(n_blocks,),
        in_specs=[pl.BlockSpec((MOE_BLOCK, width), rows),
                  wspec((D_MODEL, D_EXPERT)), wspec((D_MODEL, D_EXPERT)), wspec((D_EXPERT, D_MODEL))],
        out_specs=pl.BlockSpec((MOE_BLOCK, width), lambda i, be, nu: (i, 0)),
    )
    return pl.pallas_call(
        _experts_body,
        grid_spec=grid_spec,
        out_shape=jax.ShapeDtypeStruct((n_rows, width), U32),
        compiler_params=pltpu.CompilerParams(dimension_semantics=("arbitrary",),
                                             vmem_limit_bytes=VMEM_LIMIT),
        name="experts",
    )(block_expert, n_used, xs, w_gate, w_up, w_down)


def _combine_body(dest_ref, h1_ref, w_ref, ys_ref, wsg_ref, wsu_ref, wsd_ref, g2_ref, b2_ref,
                  out_ref, buf_ref, sem):
    def issue(t, carry):
        for kk in range(TOP_K):
            pltpu.make_async_copy(ys_ref.at[pl.ds(dest_ref[kk, t], 1)],
                                  buf_ref.at[kk, pl.ds(t, 1)], sem).start()
        return carry

    lax.fori_loop(0, ROWS, issue, 0)

    h1 = h1_ref[...]
    hb = h1.astype(BF16)
    gate = _dot(hb, wsg_ref[...])
    up = _dot(hb, wsu_ref[...])
    shared = _dot((gate * jax.nn.sigmoid(gate) * up).astype(BF16), wsd_ref[...])

    for kk in range(TOP_K):
        pltpu.make_async_copy(ys_ref.at[pl.ds(0, ROWS)], buf_ref.at[kk], sem).wait()

    half = D_MODEL // 2
    lo = jnp.zeros((ROWS, half), F32)
    hi = jnp.zeros((ROWS, half), F32)
    for kk in range(TOP_K):
        a, b = _unpack_halves(buf_ref[kk])
        wk = w_ref[:, kk:kk + 1]
        lo = lo + wk * a
        hi = hi + wk * b
    z = DN_ALPHA * h1 + shared + jnp.concatenate([lo, hi], axis=1)
    out_ref[...] = _layer_norm(z, g2_ref[...], b2_ref[...])


def _combine(dest, h1, w_rows, ys, wsg, wsu, wsd, ln2_g, ln2_b):
    t = h1.shape[0]
    width = ys.shape[1]
    d_shared = wsg.shape[1]
    return pl.pallas_call(
        _combine_body,
        grid=(t // ROWS,),
        in_specs=[pl.BlockSpec((TOP_K, ROWS), lambda i: (0, i), memory_space=pltpu.SMEM),
                  pl.BlockSpec((ROWS, D_MODEL), lambda i: (i, 0)),
                  pl.BlockSpec((ROWS, TOP_K), lambda i: (i, 0)),
                  pl.BlockSpec(memory_space=pl.ANY),
                  _const_spec((D_MODEL, d_shared)), _const_spec((D_MODEL, d_shared)),
                  _const_spec((d_shared, D_MODEL)),
                  _const_spec((1, D_MODEL)), _const_spec((1, D_MODEL))],
        out_specs=pl.BlockSpec((ROWS, D_MODEL), lambda i: (i, 0)),
        out_shape=jax.ShapeDtypeStruct((t, D_MODEL), F32),
        scratch_shapes=[pltpu.VMEM((TOP_K, ROWS, width), U32), pltpu.SemaphoreType.DMA],
        compiler_params=pltpu.CompilerParams(dimension_semantics=("arbitrary",),
                                             vmem_limit_bytes=VMEM_LIMIT),
        name="combine",
    )(dest, h1, w_rows, ys, wsg, wsu, wsd, ln2_g, ln2_b)


def _pack_in_proj(w_in, b_in, gate_w_f, gate_w_b, gate_b_f, gate_b_b):
    o_pool, o_q, o_k, o_v, o_g = 0, 512, 1024, 1536, 2560
    o_lrf, o_lrb, o_gp, o_gg = 3584, 3600, 3616, 4640
    order = [(o_pool, POOL_WIDTH), (o_q, GLA_DK), (o_k, GLA_DK), (o_v, GLA_DV), (o_g, GLA_DV),
             (o_gp, D_MODEL), (o_gg, D_MODEL), (o_lrf, GATE_RANK), (o_lrb, GATE_RANK)]
    pad = LANE - 2 * GATE_RANK
    w = jnp.concatenate([w_in[:, o:o + n] for o, n in order] + [jnp.zeros((D_MODEL, pad), F32)], axis=1)
    b = jnp.concatenate([b_in[o:o + n] for o, n in order] + [jnp.zeros((pad,), F32)])
    w_gate = jnp.zeros((LANE, 2 * GLA_DK), F32)
    w_gate = w_gate.at[:GATE_RANK, :GLA_DK].set(gate_w_f)
    w_gate = w_gate.at[GATE_RANK:2 * GATE_RANK, GLA_DK:].set(gate_w_b)
    b_gate = jnp.concatenate([gate_b_f, gate_b_b])[None, :]
    return w.astype(BF16), b[None, :], w_gate, b_gate


def kernel(x, meta, ln0_g, ln0_b, w_in, b_in, pool_w, pool_scale, gate_w_f, gate_b_f, gate_w_b, gate_b_b,
           gla_norm_g, w_branch_pool, w_branch_gla, w_out, ln1_g, ln1_b, w_router, router_bias,
           w_exp_gate, w_exp_up, w_exp_down, w_sh_gate, w_sh_up, w_sh_down, ln2_g, ln2_b):
    batch, seq, d = x.shape
    assert d == D_MODEL and seq % ROWS == 0 and w_in.shape[0] == 1
    assert (batch * seq) % ROUTE_ROWS == 0
    t = batch * seq
    row = lambda a: a.reshape(1, -1).astype(F32)

    lead = jnp.concatenate([jnp.zeros((FRONT - N_META, d), x.dtype), meta.astype(x.dtype)], axis=0)
    xp = jnp.concatenate([jnp.broadcast_to(lead[None], (batch, FRONT, d)), x], axis=1)
    tiles_per_batch = (seq + FRONT) // ROWS
    xp = xp.reshape(batch * (seq + FRONT), d)

    w_packed, b_packed, w_gate, b_gate = _pack_in_proj(
        w_in[0], b_in[0], gate_w_f[0], gate_w_b[0], gate_b_f[0], gate_b_b[0])
    u, q, k, v, sg, gp, gg, bf, cb = _in_proj(
        xp, row(ln0_g), row(ln0_b), w_packed, b_packed, w_gate, b_gate, tiles_per_batch)

    o_f, o_b = _gla(q, k, v, bf, cb, batch)

    h1, h1p = _mix(
        x.reshape(t, d), u, gp, gg, sg, o_f, o_b, row(ln0_g), row(ln0_b),
        pool_w[0].astype(BF16), row(pool_scale[0]), row(jnp.tile(gla_norm_g[0], GLA_HEADS)),
        w_branch_pool[0].astype(BF16), w_branch_gla[0].astype(BF16), w_out[0].astype(BF16),
        row(ln1_g[0]), row(ln1_b[0]), seq)

    eidx, ew, rank, counts = _route(h1, w_router[0].T, router_bias[0].reshape(N_EXPERTS, 1))

    counts = counts[:, 0]
    padded = ((counts + MOE_BLOCK - 1) // MOE_BLOCK) * MOE_BLOCK
    pend = jnp.cumsum(padded)
    pstart = pend - padded
    dest = pstart[eidx] + rank
    n_blocks = -(-(t * TOP_K) // MOE_BLOCK) + N_EXPERTS
    n_used = pend[-1] // MOE_BLOCK
    blk = jnp.arange(n_blocks, dtype=I32)
    block_expert = jnp.minimum(
        jnp.searchsorted(pend, blk * MOE_BLOCK, side='right'), N_EXPERTS - 1).astype(I32)
    block_expert = jnp.where(blk < n_used, block_expert, block_expert[n_used - 1])

    xs = _dispatch(dest, h1p, n_blocks * MOE_BLOCK)
    ys = _experts(block_expert, n_used.reshape(1).astype(I32), xs,
                  w_exp_gate[0], w_exp_up[0], w_exp_down[0])
    out = _combine(dest, h1, ew.T, ys, w_sh_gate[0].astype(BF16), w_sh_up[0].astype(BF16),
                   w_sh_down[0].astype(BF16), row(ln2_g[0]), row(ln2_b[0]))
    return out.reshape(batch, seq, d)
```

```python
import functools

import jax
import jax.numpy as jnp
from jax import lax
from jax.experimental import pallas as pl
from jax.experimental.pallas import tpu as pltpu

F32 = jnp.float32
BF16 = jnp.bfloat16
I32 = jnp.int32
U32 = jnp.uint32
HIGHEST = lax.Precision.HIGHEST

D_MODEL = 1024
N_META = 16
POOL_WINDOWS = (2, 4, 8, 16)
POOL_GROUP = 128
POOL_WIDTH = POOL_GROUP * len(POOL_WINDOWS)
GLA_HEADS = 4
GLA_DK = 512
GLA_DV = 1024
HEAD_K = GLA_DK // GLA_HEADS
HEAD_V = GLA_DV // GLA_HEADS
GATE_RANK = 16
GATE_NORMALIZER = 16.0
RMS_EPS = 1e-5
N_EXPERTS = 256
TOP_K = 8
N_GROUPS = 8
GROUP_SIZE = N_EXPERTS // N_GROUPS
TOPK_GROUPS = 4
D_EXPERT = 256
ROUTED_SCALE = 2.5
DN_ALPHA = 2.0 ** 0.25
LN_EPS = 1e-5

LANE = 128
ROWS = 256
FRONT = ROWS
GLA_CHUNK = 128
HALO = 16
ROUTE_ROWS = 512
MOE_BLOCK = 256
VMEM_LIMIT = 56 * 1024 * 1024

C_POOL, C_Q, C_K, C_V, C_G, C_GP, C_GG, C_LR = 0, 512, 1024, 1536, 2560, 3584, 4608, 5632
W_PACKED = C_LR + LANE


def _layer_norm(x, g, b):
    mu = jnp.mean(x, axis=-1, keepdims=True)
    xc = x - mu
    var = jnp.mean(xc * xc, axis=-1, keepdims=True)
    return xc * lax.rsqrt(var + LN_EPS) * g + b


def _dot(a, b):
    return jnp.dot(a, b, preferred_element_type=F32)


def _dot_nt(a, b):
    return lax.dot_general(a, b, (((1,), (1,)), ((), ())), preferred_element_type=F32)


def _split_dot(m, g):
    g_hi = g.astype(BF16)
    g_lo = (g - g_hi.astype(F32)).astype(BF16)
    return _dot(m, g_hi) + _dot(m, g_lo)


def _pack_halves(a, b):
    pa = lax.bitcast_convert_type(a.astype(BF16).astype(F32), U32)
    pb = lax.bitcast_convert_type(b.astype(BF16).astype(F32), U32)
    return pa | (pb >> 16)


def _unpack_halves(p):
    a = lax.bitcast_convert_type(p & jnp.uint32(0xFFFF0000), F32)
    b = lax.bitcast_convert_type(p << 16, F32)
    return a, b


def _const_spec(shape):
    return pl.BlockSpec(shape, lambda *_: (0,) * len(shape))


def _inproj_body(x_ref, g0_ref, b0_ref, w_ref, bias_ref, wgate_ref, bgate_ref,
                 u_ref, q_ref, k_ref, v_ref, sg_ref, gp_ref, gg_ref, bf_ref, cb_ref,
                 *, tiles_per_batch):
    i = pl.program_id(0)
    h = _layer_norm(x_ref[...], g0_ref[...], b0_ref[...])
    hb = h.astype(BF16)

    def proj(c0, n):
        return _dot(hb, w_ref[:, c0:c0 + n]) + bias_ref[:, c0:c0 + n]

    row = lax.broadcasted_iota(I32, (ROWS, 1), 0)
    valid = row >= jnp.where(i % tiles_per_batch == 0, FRONT - N_META, 0)

    u_ref[...] = proj(C_POOL, POOL_WIDTH).astype(BF16)
    q_ref[...] = (proj(C_Q, GLA_DK) * (HEAD_K ** -0.5)).astype(BF16)
    k_ref[...] = jnp.where(valid, proj(C_K, GLA_DK), 0.0).astype(BF16)
    v_ref[...] = jnp.where(valid, proj(C_V, GLA_DV), 0.0).astype(BF16)
    g = proj(C_G, GLA_DV)
    sg_ref[...] = (g * jax.nn.sigmoid(g)).astype(BF16)
    gp_ref[...] = jax.nn.sigmoid(proj(C_GP, D_MODEL)).astype(BF16)
    gg_ref[...] = jax.nn.sigmoid(proj(C_GG, D_MODEL)).astype(BF16)

    lr = proj(C_LR, LANE)
    xg = jnp.dot(lr, wgate_ref[...], precision=HIGHEST, preferred_element_type=F32) + bgate_ref[...]
    gk = (jnp.minimum(xg, 0.0) - jnp.log(1.0 + jnp.exp(-jnp.abs(xg)))) * (1.0 / GATE_NORMALIZER)

    r = lax.broadcasted_iota(I32, (GLA_CHUNK, GLA_CHUNK), 0)
    c = lax.broadcasted_iota(I32, (GLA_CHUNK, GLA_CHUNK), 1)
    tril = (r >= c).astype(BF16)
    triu = (r <= c).astype(BF16)
    for ci in range(ROWS // GLA_CHUNK):
        sl = slice(ci * GLA_CHUNK, (ci + 1) * GLA_CHUNK)
        bf_ref[sl, :] = _split_dot(tril, gk[sl, :GLA_DK])
        cb_ref[sl, :] = _split_dot(triu, gk[sl, GLA_DK:])


def _in_proj(xp, ln0_g, ln0_b, w_packed, b_packed, w_gate, b_gate, tiles_per_batch):
    n = xp.shape[0]
    row_spec = lambda width: pl.BlockSpec((ROWS, width), lambda i: (i, 0))
    out_widths = (POOL_WIDTH, GLA_DK, GLA_DK, GLA_DV, GLA_DV, D_MODEL, D_MODEL)
    out_shape = [jax.ShapeDtypeStruct((n, w), BF16) for w in out_widths]
    out_shape += [jax.ShapeDtypeStruct((n, GLA_DK), F32)] * 2
    out_specs = [row_spec(w) for w in out_widths] + [row_spec(GLA_DK)] * 2
    return pl.pallas_call(
        functools.partial(_inproj_body, tiles_per_batch=tiles_per_batch),
        grid=(n // ROWS,),
        in_specs=[row_spec(D_MODEL), _const_spec((1, D_MODEL)), _const_spec((1, D_MODEL)),
                  _const_spec((D_MODEL, W_PACKED)), _const_spec((1, W_PACKED)),
                  _const_spec((LANE, 2 * GLA_DK)), _const_spec((1, 2 * GLA_DK))],
        out_specs=out_specs,
        out_shape=out_shape,
        compiler_params=pltpu.CompilerParams(dimension_semantics=("arbitrary",),
                                             vmem_limit_bytes=VMEM_LIMIT),
        name="in_proj",
    )(xp, ln0_g, ln0_b, w_packed, b_packed, w_gate, b_gate)


def _gla_direction(q_ref, k_ref, v_ref, b_ref, o_ref, s_ref, reverse):
    c = GLA_CHUNK
    r = lax.broadcasted_iota(I32, (c, c), 0)
    s = lax.broadcasted_iota(I32, (c, c), 1)
    keep = (r <= s) if reverse else (r >= s)
    eye = lax.broadcasted_iota(I32, (HEAD_K, HEAD_K), 0) == lax.broadcasted_iota(I32, (HEAD_K, HEAD_K), 1)
    edge = 0 if reverse else c - 1
    for h in range(GLA_HEADS):
        ks = slice(h * HEAD_K, (h + 1) * HEAD_K)
        vs = slice(h * HEAD_V, (h + 1) * HEAD_V)
        b = b_ref[:, ks]
        qh = q_ref[:, ks].astype(F32)
        kh = k_ref[:, ks].astype(F32)
        vh = v_ref[:, vs]
        b_tot = b[edge:edge + 1, :]
        b_mid = b[c // 2:c // 2 + 1, :]
        q_state = (qh * jnp.exp(b)).astype(BF16)
        q_in = (qh * jnp.exp(b - b_mid)).astype(BF16)
        k_in = (kh * jnp.exp(b_mid - b)).astype(BF16)
        k_state = (kh * jnp.exp(b_tot - b)).T.astype(BF16)
        att = jnp.where(keep, _dot_nt(q_in, k_in), 0.0).astype(BF16)
        state = s_ref[h]
        o = _dot(q_state, state.astype(BF16)) + _dot(att, vh)
        o_ref[:, vs] = o.astype(o_ref.dtype)
        decay_col = jnp.sum(jnp.where(eye, jnp.exp(b_tot), 0.0), axis=1, keepdims=True)
        s_ref[h] = decay_col * state + _dot(k_state, vh)


def _gla_body(qf_ref, kf_ref, vf_ref, bf_ref, qb_ref, kb_ref, vb_ref, cb_ref,
              of_ref, ob_ref, sf_ref, sb_ref):
    @pl.when(pl.program_id(1) == 0)
    def _():
        sf_ref[...] = jnp.zeros_like(sf_ref)
        sb_ref[...] = jnp.zeros_like(sb_ref)

    _gla_direction(qf_ref, kf_ref, vf_ref, bf_ref, of_ref, sf_ref, reverse=False)
    _gla_direction(qb_ref, kb_ref, vb_ref, cb_ref, ob_ref, sb_ref, reverse=True)


def _gla(q, k, v, bf, cb, batch):
    n = q.shape[0]
    nch = n // batch // GLA_CHUNK
    fwd = lambda width: pl.BlockSpec((GLA_CHUNK, width), lambda b, i: (b * nch + i, 0))
    bwd = lambda width: pl.BlockSpec((GLA_CHUNK, width), lambda b, i: (b * nch + nch - 1 - i, 0))
    state = pltpu.VMEM((GLA_HEADS, HEAD_K, HEAD_V), F32)
    return pl.pallas_call(
        _gla_body,
        grid=(batch, nch),
        in_specs=[fwd(GLA_DK), fwd(GLA_DK), fwd(GLA_DV), fwd(GLA_DK),
                  bwd(GLA_DK), bwd(GLA_DK), bwd(GLA_DV), bwd(GLA_DK)],
        out_specs=[fwd(GLA_DV), bwd(GLA_DV)],
        out_shape=[jax.ShapeDtypeStruct((n, GLA_DV), BF16)] * 2,
        scratch_shapes=[state, state],
        compiler_params=pltpu.CompilerParams(dimension_semantics=("arbitrary", "arbitrary"),
                                             vmem_limit_bytes=VMEM_LIMIT),
        name="gla",
    )(q, k, v, bf, q, k, v, cb)


def _mix_body(x_ref, up_ref, u_ref, un_ref, gp_ref, gg_ref, sg_ref, of_ref, ob_ref,
              g0_ref, b0_ref, pw_ref, ps_ref, gn_ref, wbp_ref, wbg_ref, wo_ref, g1_ref, b1_ref,
              h1_ref, h1p_ref, *, tiles_per_batch, seq):
    j = pl.program_id(0) % tiles_per_batch
    u_main = u_ref[...]
    u_ext = jnp.concatenate([up_ref[...], u_main, un_ref[...]], axis=0)
    r = lax.broadcasted_iota(I32, (ROWS, ROWS + 2 * HALO), 0)
    e = lax.broadcasted_iota(I32, (ROWS, ROWS + 2 * HALO), 1) - HALO
    in_seq = (j * ROWS + e) < seq
    pos = j * ROWS + lax.broadcasted_iota(I32, (ROWS, 1), 0)
    y_pool = []
    for gi, w in enumerate(POOL_WINDOWS):
        cs = slice(gi * POOL_GROUP, (gi + 1) * POOL_GROUP)
        band = jnp.logical_and(jnp.logical_and(e >= r - w // 2, e < r + w - w // 2), in_seq)
        count = w - jnp.maximum(pos + (w - w // 2) - seq, 0)
        mean = _dot(band.astype(BF16), u_ext[:, cs]) / count.astype(F32)
        d = mean - u_main[:, cs].astype(F32)
        y_pool.append(_dot(d.astype(BF16), pw_ref[gi]))
    y_pool = (jnp.concatenate(y_pool, axis=1) * ps_ref[...]).astype(BF16)

    o = of_ref[...].astype(F32) + ob_ref[...].astype(F32)
    y_gla = []
    for h in range(GLA_HEADS):
        oh = o[:, h * HEAD_V:(h + 1) * HEAD_V]
        y_gla.append(oh * lax.rsqrt(jnp.mean(oh * oh, axis=-1, keepdims=True) + RMS_EPS))
    y_gla = (jnp.concatenate(y_gla, axis=1) * gn_ref[...] * sg_ref[...].astype(F32)).astype(BF16)

    merged = (gp_ref[...].astype(F32) * _dot(y_pool, wbp_ref[...])
              + gg_ref[...].astype(F32) * _dot(y_gla, wbg_ref[...]))
    y = _dot(merged.astype(BF16), wo_ref[...])
    h = _layer_norm(x_ref[...], g0_ref[...], b0_ref[...])
    h1 = _layer_norm(DN_ALPHA * h + y, g1_ref[...], b1_ref[...])
    h1_ref[...] = h1
    h1p_ref[...] = _pack_halves(h1[:, :D_MODEL // 2], h1[:, D_MODEL // 2:])


def _mix(x2, u, gp, gg, sg, o_f, o_b, ln0_g, ln0_b, pool_w, pool_scale, norm_g, wbp, wbg, wo,
         ln1_g, ln1_b, seq):
    t = x2.shape[0]
    tiles_per_batch = seq // ROWS
    padded_tiles = tiles_per_batch + FRONT // ROWS
    halo_per_tile = ROWS // HALO
    last_halo = u.shape[0] // HALO - 1

    def pidx(i):
        return (i // tiles_per_batch) * padded_tiles + FRONT // ROWS + i % tiles_per_batch

    padded = lambda width: pl.BlockSpec((ROWS, width), lambda i: (pidx(i), 0))
    plain = lambda width: pl.BlockSpec((ROWS, width), lambda i: (i, 0))
    prev_halo = pl.BlockSpec((HALO, POOL_WIDTH), lambda i: (pidx(i) * halo_per_tile - 1, 0))
    next_halo = pl.BlockSpec(
        (HALO, POOL_WIDTH), lambda i: (jnp.minimum((pidx(i) + 1) * halo_per_tile, last_halo), 0))
    return pl.pallas_call(
        functools.partial(_mix_body, tiles_per_batch=tiles_per_batch, seq=seq),
        grid=(t // ROWS,),
        in_specs=[plain(D_MODEL), prev_halo, padded(POOL_WIDTH), next_halo,
                  padded(D_MODEL), padded(D_MODEL), padded(GLA_DV), padded(GLA_DV), padded(GLA_DV),
                  _const_spec((1, D_MODEL)), _const_spec((1, D_MODEL)),
                  _const_spec((len(POOL_WINDOWS), POOL_GROUP, POOL_GROUP)), _const_spec((1, POOL_WIDTH)),
                  _const_spec((1, GLA_DV)), _const_spec((POOL_WIDTH, D_MODEL)),
                  _const_spec((GLA_DV, D_MODEL)), _const_spec((D_MODEL, D_MODEL)),
                  _const_spec((1, D_MODEL)), _const_spec((1, D_MODEL))],
        out_specs=[plain(D_MODEL), plain(D_MODEL // 2)],
        out_shape=[jax.ShapeDtypeStruct((t, D_MODEL), F32), jax.ShapeDtypeStruct((t, D_MODEL // 2), U32)],
        compiler_params=pltpu.CompilerParams(dimension_semantics=("arbitrary",),
                                             vmem_limit_bytes=VMEM_LIMIT),
        name="mix",
    )(x2, u, u, u, gp, gg, sg, o_f, o_b, ln0_g, ln0_b, pool_w, pool_scale, norm_g, wbp, wbg, wo,
      ln1_g, ln1_b)


def _first_index_of_max(val, idx, size):
    m = jnp.max(val, axis=0, keepdims=True)
    first = jnp.min(jnp.where(val == m, idx, size), axis=0, keepdims=True)
    return m, first


def _route_body(h_ref, wr_ref, bias_ref, eidx_ref, w_ref, rank_ref, cnt_ref, base_ref):
    n = ROUTE_ROWS

    @pl.when(pl.program_id(0) == 0)
    def _():
        base_ref[...] = jnp.zeros_like(base_ref)

    logits = lax.dot_general(wr_ref[...], h_ref[...], (((1,), (1,)), ((), ())),
                             precision=HIGHEST, preferred_element_type=F32)
    scores = jax.nn.sigmoid(logits)
    sel = scores + bias_ref[...]

    gidx = lax.broadcasted_iota(I32, (GROUP_SIZE, n), 0)
    groups = [sel[g * GROUP_SIZE:(g + 1) * GROUP_SIZE, :] for g in range(N_GROUPS)]
    gscore = []
    for grp in groups:
        m1, first = _first_index_of_max(grp, gidx, GROUP_SIZE)
        m2 = jnp.max(jnp.where(gidx == first, -jnp.inf, grp), axis=0, keepdims=True)
        gscore.append(m1 + m2)

    masked = []
    for gi in range(N_GROUPS):
        beaten = jnp.zeros((1, n), F32)
        for gj in range(N_GROUPS):
            if gj != gi:
                wins = (gscore[gj] >= gscore[gi]) if gj < gi else (gscore[gj] > gscore[gi])
                beaten = beaten + wins.astype(F32)
        masked.append(jnp.where(beaten < TOPK_GROUPS, groups[gi], -jnp.inf))
    val = jnp.concatenate(masked, axis=0)

    eidx = lax.broadcasted_iota(I32, (N_EXPERTS, n), 0)
    chosen_f = jnp.zeros((N_EXPERTS, n), F32)
    picks, weights = [], []
    for _ in range(TOP_K):
        _, first = _first_index_of_max(val, eidx, N_EXPERTS)
        hit = eidx == first
        picks.append(first)
        weights.append(jnp.sum(jnp.where(hit, scores, 0.0), axis=0, keepdims=True))
        chosen_f = jnp.where(hit, 1.0, chosen_f)
        val = jnp.where(hit, -jnp.inf, val)
    wsum = weights[0]
    for wk in weights[1:]:
        wsum = wsum + wk
    scale = ROUTED_SCALE / wsum

    r = lax.broadcasted_iota(I32, (n, n), 0)
    c = lax.broadcasted_iota(I32, (n, n), 1)
    earlier = (r < c).astype(BF16)
    before = _dot(chosen_f.astype(BF16), earlier) + base_ref[...]
    base_ref[...] = base_ref[...] + jnp.sum(chosen_f, axis=1, keepdims=True)
    cnt_ref[...] = jnp.broadcast_to(base_ref[...], cnt_ref.shape).astype(I32)

    for kk in range(TOP_K):
        eidx_ref[kk:kk + 1, :] = picks[kk]
        w_ref[kk:kk + 1, :] = weights[kk] * scale
        rank = jnp.sum(jnp.where(eidx == picks[kk], before, 0.0), axis=0, keepdims=True)
        rank_ref[kk:kk + 1, :] = rank.astype(I32)


def _route(h1, wr_t, bias_col):
    t = h1.shape[0]
    col = lambda dtype: jax.ShapeDtypeStruct((TOP_K, t), dtype)
    kspec = pl.BlockSpec((TOP_K, ROUTE_ROWS), lambda i: (0, i))
    return pl.pallas_call(
        _route_body,
        grid=(t // ROUTE_ROWS,),
        in_specs=[pl.BlockSpec((ROUTE_ROWS, D_MODEL), lambda i: (i, 0)),
                  _const_spec((N_EXPERTS, D_MODEL)), _const_spec((N_EXPERTS, 1))],
        out_specs=[kspec, kspec, kspec, _const_spec((N_EXPERTS, LANE))],
        out_shape=[col(I32), col(F32), col(I32), jax.ShapeDtypeStruct((N_EXPERTS, LANE), I32)],
        scratch_shapes=[pltpu.VMEM((N_EXPERTS, 1), F32)],
        compiler_params=pltpu.CompilerParams(dimension_semantics=("arbitrary",),
                                             vmem_limit_bytes=VMEM_LIMIT),
        name="route",
    )(h1, wr_t, bias_col)


def _dest_body(eidx_ref, rank_ref, pstart_ref, dest_ref):
    n = eidx_ref.shape[1]
    eiota = lax.broadcasted_iota(I32, (N_EXPERTS, n), 0)
    pstart = pstart_ref[...]
    for kk in range(TOP_K):
        start = jnp.sum(jnp.where(eiota == eidx_ref[kk:kk + 1, :], pstart, 0.0), axis=0, keepdims=True)
        dest_ref[kk:kk + 1, :] = start.astype(I32) + rank_ref[kk:kk + 1, :]


def _dest(eidx, rank, pstart_col):
    t = eidx.shape[1]
    kspec = pl.BlockSpec((TOP_K, ROUTE_ROWS), lambda i: (0, i))
    return pl.pallas_call(
        _dest_body,
        grid=(t // ROUTE_ROWS,),
        in_specs=[kspec, kspec, _const_spec((N_EXPERTS, 1))],
        out_specs=kspec,
        out_shape=jax.ShapeDtypeStruct((TOP_K, t), I32),
        compiler_params=pltpu.CompilerParams(dimension_semantics=("arbitrary",)),
        name="dest",
    )(eidx, rank, pstart_col)


def _dispatch_body(dest_ref, h1p_ref, zeros_ref, xs_ref, sem):
    del zeros_ref

    def issue(t, carry):
        for kk in range(TOP_K):
            pltpu.make_async_copy(h1p_ref.at[pl.ds(t, 1)],
                                  xs_ref.at[pl.ds(dest_ref[kk, t], 1)], sem).start()
        return carry

    lax.fori_loop(0, ROWS, issue, 0, unroll=8)
    for _ in range(TOP_K):
        pltpu.make_async_copy(h1p_ref, xs_ref.at[pl.ds(0, ROWS)], sem).wait()


def _dispatch(dest, h1p, n_rows):
    t, width = h1p.shape
    zeros = jnp.zeros((n_rows, width), U32)
    return pl.pallas_call(
        _dispatch_body,
        grid=(t // ROWS,),
        in_specs=[pl.BlockSpec((TOP_K, ROWS), lambda i: (0, i), memory_space=pltpu.SMEM),
                  pl.BlockSpec((ROWS, width), lambda i: (i, 0)), pl.BlockSpec(memory_space=pl.ANY)],
        out_specs=pl.BlockSpec(memory_space=pl.ANY),
        out_shape=jax.ShapeDtypeStruct((n_rows, width), U32),
        scratch_shapes=[pltpu.SemaphoreType.DMA],
        input_output_aliases={2: 0},
        compiler_params=pltpu.CompilerParams(dimension_semantics=("arbitrary",)),
        name="dispatch",
    )(dest, h1p, zeros)


def _experts_body(be_ref, nu_ref, xs_ref, wg_ref, wu_ref, wd_ref, ys_ref):
    del be_ref

    @pl.when(pl.program_id(0) < nu_ref[0])
    def _():
        xa, xb = _unpack_halves(xs_ref[...])
        xa = xa.astype(BF16)
        xb = xb.astype(BF16)
        half = D_MODEL // 2
        wg = wg_ref[0].astype(BF16)
        wu = wu_ref[0].astype(BF16)
        gate = _dot(xa, wg[:half]) + _dot(xb, wg[half:])
        up = _dot(xa, wu[:half]) + _dot(xb, wu[half:])
        hidden = (gate * jax.nn.sigmoid(gate) * up).astype(BF16)
        y = _dot(hidden, wd_ref[0].astype(BF16))
        ys_ref[...] = _pack_halves(y[:, :half], y[:, half:])

    @pl.when(pl.program_id(0) >= nu_ref[0])
    def _():
        ys_ref[...] = jnp.zeros_like(ys_ref)


def _experts(block_expert, n_used, xs, w_gate, w_up, w_down):
    n_rows, width = xs.shape
    n_blocks = n_rows // MOE_BLOCK
    rows = lambda i, be, nu: (jnp.minimum(i, nu[0] - 1), 0)
    wspec = lambda shape: pl.BlockSpec((1,) + shape, lambda i, be, nu: (be[i], 0, 0))
    grid_spec = pltpu.PrefetchScalarGridSpec(
        num_scalar_prefetch=2,
        grid=(n_blocks,),
        in_specs=[pl.BlockSpec((MOE_BLOCK, width), rows),
                  wspec((D_MODEL, D_EXPERT)), wspec((D_MODEL, D_EXPERT)), wspec((D_EXPERT, D_MODEL))],
        out_specs=pl.BlockSpec((MOE_BLOCK, width), lambda i, be, nu: (i, 0)),
    )
    return pl.pallas_call(
        _experts_body,
        grid_spec=grid_spec,
        out_shape=jax.ShapeDtypeStruct((n_rows, width), U32),
        compiler_params=pltpu.CompilerParams(dimension_semantics=("arbitrary",),
                                             vmem_limit_bytes=VMEM_LIMIT),
        name="experts",
    )(block_expert, n_used, xs, w_gate, w_up, w_down)


def _combine_body(dest_ref, h1_ref, w_ref, ys_ref, wsg_ref, wsu_ref, wsd_ref, g2_ref, b2_ref,
                  out_ref, buf_ref, sem):
    def issue(t, carry):
        for kk in range(TOP_K):
            pltpu.make_async_copy(ys_ref.at[pl.ds(dest_ref[kk, t], 1)],
                                  buf_ref.at[kk, pl.ds(t, 1)], sem).start()
        return carry

    lax.fori_loop(0, ROWS, issue, 0)

    h1 = h1_ref[...]
    hb = h1.astype(BF16)
    gate = _dot(hb, wsg_ref[...])
    up = _dot(hb, wsu_ref[...])
    shared = _dot((gate * jax.nn.sigmoid(gate) * up).astype(BF16), wsd_ref[...])

    for kk in range(TOP_K):
        pltpu.make_async_copy(ys_ref.at[pl.ds(0, ROWS)], buf_ref.at[kk], sem).wait()

    half = D_MODEL // 2
    lo = jnp.zeros((ROWS, half), F32)
    hi = jnp.zeros((ROWS, half), F32)
    for kk in range(TOP_K):
        a, b = _unpack_halves(buf_ref[kk])
        wk = w_ref[:, kk:kk + 1]
        lo = lo + wk * a
        hi = hi + wk * b
    z = DN_ALPHA * h1 + shared + jnp.concatenate([lo, hi], axis=1)
    out_ref[...] = _layer_norm(z, g2_ref[...], b2_ref[...])


def _combine(dest, h1, w_rows, ys, wsg, wsu, wsd, ln2_g, ln2_b):
    t = h1.shape[0]
    width = ys.shape[1]
    d_shared = wsg.shape[1]
    return pl.pallas_call(
        _combine_body,
        grid=(t // ROWS,),
        in_specs=[pl.BlockSpec((TOP_K, ROWS), lambda i: (0, i), memory_space=pltpu.SMEM),
                  pl.BlockSpec((ROWS, D_MODEL), lambda i: (i, 0)),
                  pl.BlockSpec((ROWS, TOP_K), lambda i: (i, 0)),
                  pl.BlockSpec(memory_space=pl.ANY),
                  _const_spec((D_MODEL, d_shared)), _const_spec((D_MODEL, d_shared)),
                  _const_spec((d_shared, D_MODEL)),
                  _const_spec((1, D_MODEL)), _const_spec((1, D_MODEL))],
        out_specs=pl.BlockSpec((ROWS, D_MODEL), lambda i: (i, 0)),
        out_shape=jax.ShapeDtypeStruct((t, D_MODEL), F32),
        scratch_shapes=[pltpu.VMEM((TOP_K, ROWS, width), U32), pltpu.SemaphoreType.DMA],
        compiler_params=pltpu.CompilerParams(dimension_semantics=("arbitrary",),
                                             vmem_limit_bytes=VMEM_LIMIT),
        name="combine",
    )(dest, h1, w_rows, ys, wsg, wsu, wsd, ln2_g, ln2_b)


def _pack_in_proj(w_in, b_in, gate_w_f, gate_w_b, gate_b_f, gate_b_b):
    o_pool, o_q, o_k, o_v, o_g = 0, 512, 1024, 1536, 2560
    o_lrf, o_lrb, o_gp, o_gg = 3584, 3600, 3616, 4640
    order = [(o_pool, POOL_WIDTH), (o_q, GLA_DK), (o_k, GLA_DK), (o_v, GLA_DV), (o_g, GLA_DV),
             (o_gp, D_MODEL), (o_gg, D_MODEL), (o_lrf, GATE_RANK), (o_lrb, GATE_RANK)]
    pad = LANE - 2 * GATE_RANK
    w = jnp.concatenate([w_in[:, o:o + n] for o, n in order] + [jnp.zeros((D_MODEL, pad), F32)], axis=1)
    b = jnp.concatenate([b_in[o:o + n] for o, n in order] + [jnp.zeros((pad,), F32)])
    w_gate = jnp.zeros((LANE, 2 * GLA_DK), F32)
    w_gate = w_gate.at[:GATE_RANK, :GLA_DK].set(gate_w_f)
    w_gate = w_gate.at[GATE_RANK:2 * GATE_RANK, GLA_DK:].set(gate_w_b)
    b_gate = jnp.concatenate([gate_b_f, gate_b_b])[None, :]
    return w.astype(BF16), b[None, :], w_gate, b_gate


def kernel(x, meta, ln0_g, ln0_b, w_in, b_in, pool_w, pool_scale, gate_w_f, gate_b_f, gate_w_b, gate_b_b,
           gla_norm_g, w_branch_pool, w_branch_gla, w_out, ln1_g, ln1_b, w_router, router_bias,
           w_exp_gate, w_exp_up, w_exp_down, w_sh_gate, w_sh_up, w_sh_down, ln2_g, ln2_b):
    batch, seq, d = x.shape
    assert d == D_MODEL and seq % ROWS == 0 and w_in.shape[0] == 1
    assert (batch * seq) % ROUTE_ROWS == 0
    t = batch * seq
    row = lambda a: a.reshape(1, -1).astype(F32)

    lead = jnp.concatenate([jnp.zeros((FRONT - N_META, d), x.dtype), meta.astype(x.dtype)], axis=0)
    xp = jnp.concatenate([jnp.broadcast_to(lead[None], (batch, FRONT, d)), x], axis=1)
    tiles_per_batch = (seq + FRONT) // ROWS
    xp = xp.reshape(batch * (seq + FRONT), d)

    w_packed, b_packed, w_gate, b_gate = _pack_in_proj(
        w_in[0], b_in[0], gate_w_f[0], gate_w_b[0], gate_b_f[0], gate_b_b[0])
    u, q, k, v, sg, gp, gg, bf, cb = _in_proj(
        xp, row(ln0_g), row(ln0_b), w_packed, b_packed, w_gate, b_gate, tiles_per_batch)

    o_f, o_b = _gla(q, k, v, bf, cb, batch)

    h1, h1p = _mix(
        x.reshape(t, d), u, gp, gg, sg, o_f, o_b, row(ln0_g), row(ln0_b),
        pool_w[0].astype(BF16), row(pool_scale[0]), row(jnp.tile(gla_norm_g[0], GLA_HEADS)),
        w_branch_pool[0].astype(BF16), w_branch_gla[0].astype(BF16), w_out[0].astype(BF16),
        row(ln1_g[0]), row(ln1_b[0]), seq)

    eidx, ew, rank, counts = _route(h1, w_router[0].T, router_bias[0].reshape(N_EXPERTS, 1))

    counts = counts[:, 0]
    padded = ((counts + MOE_BLOCK - 1) // MOE_BLOCK) * MOE_BLOCK
    pend = jnp.cumsum(padded)
    pstart = pend - padded
    dest = _dest(eidx, rank, pstart.astype(F32).reshape(N_EXPERTS, 1))
    n_blocks = -(-(t * TOP_K) // MOE_BLOCK) + N_EXPERTS
    n_used = pend[-1] // MOE_BLOCK
    blk = jnp.arange(n_blocks, dtype=I32)
    first_row = jnp.minimum(blk, n_used - 1) * MOE_BLOCK
    block_expert = jnp.sum((pend[None, :] <= first_row[:, None]).astype(I32), axis=1)

    xs = _dispatch(dest, h1p, n_blocks * MOE_BLOCK)
    ys = _experts(block_expert, n_used.reshape(1).astype(I32), xs,
                  w_exp_gate[0], w_exp_up[0], w_exp_down[0])
    out = _combine(dest, h1, ew.T, ys, w_sh_gate[0].astype(BF16), w_sh_up[0].astype(BF16),
                   w_sh_down[0].astype(BF16), row(ln2_g[0]), row(ln2_b[0]))
    return out.reshape(batch, seq, d)
```

```python
import functools

import jax
import jax.numpy as jnp
from jax import lax
from jax.experimental import pallas as pl
from jax.experimental.pallas import tpu as pltpu

F32 = jnp.float32
BF16 = jnp.bfloat16
I32 = jnp.int32
U32 = jnp.uint32
HIGHEST = lax.Precision.HIGHEST

D_MODEL = 1024
N_META = 16
POOL_WINDOWS = (2, 4, 8, 16)
POOL_GROUP = 128
POOL_WIDTH = POOL_GROUP * len(POOL_WINDOWS)
GLA_HEADS = 4
GLA_DK = 512
GLA_DV = 1024
HEAD_K = GLA_DK // GLA_HEADS
HEAD_V = GLA_DV // GLA_HEADS
GATE_RANK = 16
GATE_NORMALIZER = 16.0
RMS_EPS = 1e-5
N_EXPERTS = 256
TOP_K = 8
N_GROUPS = 8
GROUP_SIZE = N_EXPERTS // N_GROUPS
TOPK_GROUPS = 4
D_EXPERT = 256
ROUTED_SCALE = 2.5
DN_ALPHA = 2.0 ** 0.25
LN_EPS = 1e-5

LANE = 128
ROWS = 256
FRONT = ROWS
GLA_CHUNK = 128
HALO = 16
ROUTE_ROWS = 512
MOE_BLOCK = 256
VMEM_LIMIT = 56 * 1024 * 1024

C_POOL, C_Q, C_K, C_V, C_G, C_GP, C_GG, C_LR = 0, 512, 1024, 1536, 2560, 3584, 4608, 5632
W_PACKED = C_LR + LANE


def _layer_norm(x, g, b):
    mu = jnp.mean(x, axis=-1, keepdims=True)
    xc = x - mu
    var = jnp.mean(xc * xc, axis=-1, keepdims=True)
    return xc * lax.rsqrt(var + LN_EPS) * g + b


def _dot(a, b):
    return jnp.dot(a, b, preferred_element_type=F32)


def _dot_nt(a, b):
    return lax.dot_general(a, b, (((1,), (1,)), ((), ())), preferred_element_type=F32)


def _split_dot(m, g):
    g_hi = g.astype(BF16)
    g_lo = (g - g_hi.astype(F32)).astype(BF16)
    return _dot(m, g_hi) + _dot(m, g_lo)


def _pack_halves(a, b):
    pa = lax.bitcast_convert_type(a.astype(BF16).astype(F32), U32)
    pb = lax.bitcast_convert_type(b.astype(BF16).astype(F32), U32)
    return pa | (pb >> 16)


def _unpack_halves(p):
    a = lax.bitcast_convert_type(p & jnp.uint32(0xFFFF0000), F32)
    b = lax.bitcast_convert_type(p << 16, F32)
    return a, b


def _const_spec(shape):
    return pl.BlockSpec(shape, lambda *_: (0,) * len(shape))


def _inproj_body(x_ref, g0_ref, b0_ref, w_ref, bias_ref, wgate_ref, bgate_ref,
                 u_ref, q_ref, k_ref, v_ref, sg_ref, gp_ref, gg_ref, bf_ref, cb_ref,
                 *, tiles_per_batch):
    i = pl.program_id(0)
    h = _layer_norm(x_ref[...], g0_ref[...], b0_ref[...])
    hb = h.astype(BF16)

    def proj(c0, n):
        return _dot(hb, w_ref[:, c0:c0 + n]) + bias_ref[:, c0:c0 + n]

    row = lax.broadcasted_iota(I32, (ROWS, 1), 0)
    valid = row >= jnp.where(i % tiles_per_batch == 0, FRONT - N_META, 0)

    u_ref[...] = proj(C_POOL, POOL_WIDTH).astype(BF16)
    q_ref[...] = (proj(C_Q, GLA_DK) * (HEAD_K ** -0.5)).astype(BF16)
    k_ref[...] = jnp.where(valid, proj(C_K, GLA_DK), 0.0).astype(BF16)
    v_ref[...] = jnp.where(valid, proj(C_V, GLA_DV), 0.0).astype(BF16)
    g = proj(C_G, GLA_DV)
    sg_ref[...] = (g * jax.nn.sigmoid(g)).astype(BF16)
    gp_ref[...] = jax.nn.sigmoid(proj(C_GP, D_MODEL)).astype(BF16)
    gg_ref[...] = jax.nn.sigmoid(proj(C_GG, D_MODEL)).astype(BF16)

    lr = proj(C_LR, LANE)
    xg = jnp.dot(lr, wgate_ref[...], precision=HIGHEST, preferred_element_type=F32) + bgate_ref[...]
    gk = (jnp.minimum(xg, 0.0) - jnp.log(1.0 + jnp.exp(-jnp.abs(xg)))) * (1.0 / GATE_NORMALIZER)

    r = lax.broadcasted_iota(I32, (GLA_CHUNK, GLA_CHUNK), 0)
    c = lax.broadcasted_iota(I32, (GLA_CHUNK, GLA_CHUNK), 1)
    tril = (r >= c).astype(BF16)
    triu = (r <= c).astype(BF16)
    for ci in range(ROWS // GLA_CHUNK):
        sl = slice(ci * GLA_CHUNK, (ci + 1) * GLA_CHUNK)
        bf_ref[sl, :] = _split_dot(tril, gk[sl, :GLA_DK])
        cb_ref[sl, :] = _split_dot(triu, gk[sl, GLA_DK:])


def _in_proj(xp, ln0_g, ln0_b, w_packed, b_packed, w_gate, b_gate, tiles_per_batch):
    n = xp.shape[0]
    row_spec = lambda width: pl.BlockSpec((ROWS, width), lambda i: (i, 0))
    out_widths = (POOL_WIDTH, GLA_DK, GLA_DK, GLA_DV, GLA_DV, D_MODEL, D_MODEL)
    out_shape = [jax.ShapeDtypeStruct((n, w), BF16) for w in out_widths]
    out_shape += [jax.ShapeDtypeStruct((n, GLA_DK), F32)] * 2
    out_specs = [row_spec(w) for w in out_widths] + [row_spec(GLA_DK)] * 2
    return pl.pallas_call(
        functools.partial(_inproj_body, tiles_per_batch=tiles_per_batch),
        grid=(n // ROWS,),
        in_specs=[row_spec(D_MODEL), _const_spec((1, D_MODEL)), _const_spec((1, D_MODEL)),
                  _const_spec((D_MODEL, W_PACKED)), _const_spec((1, W_PACKED)),
                  _const_spec((LANE, 2 * GLA_DK)), _const_spec((1, 2 * GLA_DK))],
        out_specs=out_specs,
        out_shape=out_shape,
        compiler_params=pltpu.CompilerParams(dimension_semantics=("arbitrary",),
                                             vmem_limit_bytes=VMEM_LIMIT),
        name="in_proj",
    )(xp, ln0_g, ln0_b, w_packed, b_packed, w_gate, b_gate)


def _gla_direction(q_ref, k_ref, v_ref, b_ref, o_ref, s_ref, reverse):
    c = GLA_CHUNK
    r = lax.broadcasted_iota(I32, (c, c), 0)
    s = lax.broadcasted_iota(I32, (c, c), 1)
    keep = (r <= s) if reverse else (r >= s)
    eye = lax.broadcasted_iota(I32, (HEAD_K, HEAD_K), 0) == lax.broadcasted_iota(I32, (HEAD_K, HEAD_K), 1)
    edge = 0 if reverse else c - 1
    for h in range(GLA_HEADS):
        ks = slice(h * HEAD_K, (h + 1) * HEAD_K)
        vs = slice(h * HEAD_V, (h + 1) * HEAD_V)
        b = b_ref[:, ks]
        qh = q_ref[:, ks].astype(F32)
        kh = k_ref[:, ks].astype(F32)
        vh = v_ref[:, vs]
        b_tot = b[edge:edge + 1, :]
        b_mid = b[c // 2:c // 2 + 1, :]
        q_state = (qh * jnp.exp(b)).astype(BF16)
        q_in = (qh * jnp.exp(b - b_mid)).astype(BF16)
        k_in = (kh * jnp.exp(b_mid - b)).astype(BF16)
        k_state = (kh * jnp.exp(b_tot - b)).T.astype(BF16)
        att = jnp.where(keep, _dot_nt(q_in, k_in), 0.0).astype(BF16)
        state = s_ref[h]
        o = _dot(q_state, state.astype(BF16)) + _dot(att, vh)
        o_ref[:, vs] = o.astype(o_ref.dtype)
        decay_col = jnp.sum(jnp.where(eye, jnp.exp(b_tot), 0.0), axis=1, keepdims=True)
        s_ref[h] = decay_col * state + _dot(k_state, vh)


def _gla_body(qf_ref, kf_ref, vf_ref, bf_ref, qb_ref, kb_ref, vb_ref, cb_ref,
              of_ref, ob_ref, sf_ref, sb_ref):
    @pl.when(pl.program_id(1) == 0)
    def _():
        sf_ref[...] = jnp.zeros_like(sf_ref)
        sb_ref[...] = jnp.zeros_like(sb_ref)

    _gla_direction(qf_ref, kf_ref, vf_ref, bf_ref, of_ref, sf_ref, reverse=False)
    _gla_direction(qb_ref, kb_ref, vb_ref, cb_ref, ob_ref, sb_ref, reverse=True)


def _gla(q, k, v, bf, cb, batch):
    n = q.shape[0]
    nch = n // batch // GLA_CHUNK
    fwd = lambda width: pl.BlockSpec((GLA_CHUNK, width), lambda b, i: (b * nch + i, 0))
    bwd = lambda width: pl.BlockSpec((GLA_CHUNK, width), lambda b, i: (b * nch + nch - 1 - i, 0))
    state = pltpu.VMEM((GLA_HEADS, HEAD_K, HEAD_V), F32)
    return pl.pallas_call(
        _gla_body,
        grid=(batch, nch),
        in_specs=[fwd(GLA_DK), fwd(GLA_DK), fwd(GLA_DV), fwd(GLA_DK),
                  bwd(GLA_DK), bwd(GLA_DK), bwd(GLA_DV), bwd(GLA_DK)],
        out_specs=[fwd(GLA_DV), bwd(GLA_DV)],
        out_shape=[jax.ShapeDtypeStruct((n, GLA_DV), BF16)] * 2,
        scratch_shapes=[state, state],
        compiler_params=pltpu.CompilerParams(dimension_semantics=("arbitrary", "arbitrary"),
                                             vmem_limit_bytes=VMEM_LIMIT),
        name="gla",
    )(q, k, v, bf, q, k, v, cb)


def _mix_body(x_ref, up_ref, u_ref, un_ref, gp_ref, gg_ref, sg_ref, of_ref, ob_ref,
              g0_ref, b0_ref, pw_ref, ps_ref, gn_ref, wbp_ref, wbg_ref, wo_ref, g1_ref, b1_ref,
              h1_ref, h1p_ref, *, tiles_per_batch, seq):
    j = pl.program_id(0) % tiles_per_batch
    u_main = u_ref[...]
    u_ext = jnp.concatenate([up_ref[...], u_main, un_ref[...]], axis=0)
    r = lax.broadcasted_iota(I32, (ROWS, ROWS + 2 * HALO), 0)
    e = lax.broadcasted_iota(I32, (ROWS, ROWS + 2 * HALO), 1) - HALO
    in_seq = (j * ROWS + e) < seq
    pos = j * ROWS + lax.broadcasted_iota(I32, (ROWS, 1), 0)
    y_pool = []
    for gi, w in enumerate(POOL_WINDOWS):
        cs = slice(gi * POOL_GROUP, (gi + 1) * POOL_GROUP)
        band = jnp.logical_and(jnp.logical_and(e >= r - w // 2, e < r + w - w // 2), in_seq)
        count = w - jnp.maximum(pos + (w - w // 2) - seq, 0)
        mean = _dot(band.astype(BF16), u_ext[:, cs]) / count.astype(F32)
        d = mean - u_main[:, cs].astype(F32)
        y_pool.append(_dot(d.astype(BF16), pw_ref[gi]))
    y_pool = (jnp.concatenate(y_pool, axis=1) * ps_ref[...]).astype(BF16)

    o = of_ref[...].astype(F32) + ob_ref[...].astype(F32)
    y_gla = []
    for h in range(GLA_HEADS):
        oh = o[:, h * HEAD_V:(h + 1) * HEAD_V]
        y_gla.append(oh * lax.rsqrt(jnp.mean(oh * oh, axis=-1, keepdims=True) + RMS_EPS))
    y_gla = (jnp.concatenate(y_gla, axis=1) * gn_ref[...] * sg_ref[...].astype(F32)).astype(BF16)

    merged = (gp_ref[...].astype(F32) * _dot(y_pool, wbp_ref[...])
              + gg_ref[...].astype(F32) * _dot(y_gla, wbg_ref[...]))
    y = _dot(merged.astype(BF16), wo_ref[...])
    h = _layer_norm(x_ref[...], g0_ref[...], b0_ref[...])
    h1 = _layer_norm(DN_ALPHA * h + y, g1_ref[...], b1_ref[...])
    h1_ref[...] = h1
    h1p_ref[...] = _pack_halves(h1[:, :D_MODEL // 2], h1[:, D_MODEL // 2:])


def _mix(x2, u, gp, gg, sg, o_f, o_b, ln0_g, ln0_b, pool_w, pool_scale, norm_g, wbp, wbg, wo,
         ln1_g, ln1_b, seq):
    t = x2.shape[0]
    tiles_per_batch = seq // ROWS
    padded_tiles = tiles_per_batch + FRONT // ROWS
    halo_per_tile = ROWS // HALO
    last_halo = u.shape[0] // HALO - 1

    def pidx(i):
        return (i // tiles_per_batch) * padded_tiles + FRONT // ROWS + i % tiles_per_batch

    padded = lambda width: pl.BlockSpec((ROWS, width), lambda i: (pidx(i), 0))
    plain = lambda width: pl.BlockSpec((ROWS, width), lambda i: (i, 0))
    prev_halo = pl.BlockSpec((HALO, POOL_WIDTH), lambda i: (pidx(i) * halo_per_tile - 1, 0))
    next_halo = pl.BlockSpec(
        (HALO, POOL_WIDTH), lambda i: (jnp.minimum((pidx(i) + 1) * halo_per_tile, last_halo), 0))
    return pl.pallas_call(
        functools.partial(_mix_body, tiles_per_batch=tiles_per_batch, seq=seq),
        grid=(t // ROWS,),
        in_specs=[plain(D_MODEL), prev_halo, padded(POOL_WIDTH), next_halo,
                  padded(D_MODEL), padded(D_MODEL), padded(GLA_DV), padded(GLA_DV), padded(GLA_DV),
                  _const_spec((1, D_MODEL)), _const_spec((1, D_MODEL)),
                  _const_spec((len(POOL_WINDOWS), POOL_GROUP, POOL_GROUP)), _const_spec((1, POOL_WIDTH)),
                  _const_spec((1, GLA_DV)), _const_spec((POOL_WIDTH, D_MODEL)),
                  _const_spec((GLA_DV, D_MODEL)), _const_spec((D_MODEL, D_MODEL)),
                  _const_spec((1, D_MODEL)), _const_spec((1, D_MODEL))],
        out_specs=[plain(D_MODEL), plain(D_MODEL // 2)],
        out_shape=[jax.ShapeDtypeStruct((t, D_MODEL), F32), jax.ShapeDtypeStruct((t, D_MODEL // 2), U32)],
        compiler_params=pltpu.CompilerParams(dimension_semantics=("arbitrary",),
                                             vmem_limit_bytes=VMEM_LIMIT),
        name="mix",
    )(x2, u, u, u, gp, gg, sg, o_f, o_b, ln0_g, ln0_b, pool_w, pool_scale, norm_g, wbp, wbg, wo,
      ln1_g, ln1_b)


def _first_index_of_max(val, idx, size):
    m = jnp.max(val, axis=0, keepdims=True)
    first = jnp.min(jnp.where(val == m, idx, size), axis=0, keepdims=True)
    return m, first


def _route_body(h_ref, wr_ref, bias_ref, eidx_ref, w_ref, rank_ref, cnt_ref, base_ref):
    n = ROUTE_ROWS

    @pl.when(pl.program_id(0) == 0)
    def _():
        base_ref[...] = jnp.zeros_like(base_ref)

    logits = lax.dot_general(wr_ref[...], h_ref[...], (((1,), (1,)), ((), ())),
                             precision=HIGHEST, preferred_element_type=F32)
    scores = jax.nn.sigmoid(logits)
    sel = scores + bias_ref[...]

    gidx = lax.broadcasted_iota(I32, (GROUP_SIZE, n), 0)
    groups = [sel[g * GROUP_SIZE:(g + 1) * GROUP_SIZE, :] for g in range(N_GROUPS)]
    gscore = []
    for grp in groups:
        m1, first = _first_index_of_max(grp, gidx, GROUP_SIZE)
        m2 = jnp.max(jnp.where(gidx == first, -jnp.inf, grp), axis=0, keepdims=True)
        gscore.append(m1 + m2)

    masked = []
    for gi in range(N_GROUPS):
        beaten = jnp.zeros((1, n), F32)
        for gj in range(N_GROUPS):
            if gj != gi:
                wins = (gscore[gj] >= gscore[gi]) if gj < gi else (gscore[gj] > gscore[gi])
                beaten = beaten + wins.astype(F32)
        masked.append(jnp.where(beaten < TOPK_GROUPS, groups[gi], -jnp.inf))
    val = jnp.concatenate(masked, axis=0)

    eidx = lax.broadcasted_iota(I32, (N_EXPERTS, n), 0)
    chosen_f = jnp.zeros((N_EXPERTS, n), F32)
    picks, weights = [], []
    for _ in range(TOP_K):
        _, first = _first_index_of_max(val, eidx, N_EXPERTS)
        hit = eidx == first
        picks.append(first)
        weights.append(jnp.sum(jnp.where(hit, scores, 0.0), axis=0, keepdims=True))
        chosen_f = jnp.where(hit, 1.0, chosen_f)
        val = jnp.where(hit, -jnp.inf, val)
    wsum = weights[0]
    for wk in weights[1:]:
        wsum = wsum + wk
    scale = ROUTED_SCALE / wsum

    r = lax.broadcasted_iota(I32, (n, n), 0)
    c = lax.broadcasted_iota(I32, (n, n), 1)
    earlier = (r < c).astype(BF16)
    before = _dot(chosen_f.astype(BF16), earlier) + base_ref[...]
    base_ref[...] = base_ref[...] + jnp.sum(chosen_f, axis=1, keepdims=True)
    cnt_ref[...] = jnp.broadcast_to(base_ref[...], cnt_ref.shape).astype(I32)

    for kk in range(TOP_K):
        eidx_ref[kk:kk + 1, :] = picks[kk]
        w_ref[kk:kk + 1, :] = weights[kk] * scale
        rank = jnp.sum(jnp.where(eidx == picks[kk], before, 0.0), axis=0, keepdims=True)
        rank_ref[kk:kk + 1, :] = rank.astype(I32)


def _route(h1, wr_t, bias_col):
    t = h1.shape[0]
    col = lambda dtype: jax.ShapeDtypeStruct((TOP_K, t), dtype)
    kspec = pl.BlockSpec((TOP_K, ROUTE_ROWS), lambda i: (0, i))
    return pl.pallas_call(
        _route_body,
        grid=(t // ROUTE_ROWS,),
        in_specs=[pl.BlockSpec((ROUTE_ROWS, D_MODEL), lambda i: (i, 0)),
                  _const_spec((N_EXPERTS, D_MODEL)), _const_spec((N_EXPERTS, 1))],
        out_specs=[kspec, kspec, kspec, _const_spec((N_EXPERTS, LANE))],
        out_shape=[col(I32), col(F32), col(I32), jax.ShapeDtypeStruct((N_EXPERTS, LANE), I32)],
        scratch_shapes=[pltpu.VMEM((N_EXPERTS, 1), F32)],
        compiler_params=pltpu.CompilerParams(dimension_semantics=("arbitrary",),
                                             vmem_limit_bytes=VMEM_LIMIT),
        name="route",
    )(h1, wr_t, bias_col)


def _dest_body(eidx_ref, rank_ref, pstart_ref, dest_ref):
    n = eidx_ref.shape[1]
    eiota = lax.broadcasted_iota(I32, (N_EXPERTS, n), 0)
    pstart = pstart_ref[...]
    for kk in range(TOP_K):
        start = jnp.sum(jnp.where(eiota == eidx_ref[kk:kk + 1, :], pstart, 0.0), axis=0, keepdims=True)
        dest_ref[kk:kk + 1, :] = start.astype(I32) + rank_ref[kk:kk + 1, :]


def _dest(eidx, rank, pstart_col):
    t = eidx.shape[1]
    kspec = pl.BlockSpec((TOP_K, ROUTE_ROWS), lambda i: (0, i))
    return pl.pallas_call(
        _dest_body,
        grid=(t // ROUTE_ROWS,),
        in_specs=[kspec, kspec, _const_spec((N_EXPERTS, 1))],
        out_specs=kspec,
        out_shape=jax.ShapeDtypeStruct((TOP_K, t), I32),
        compiler_params=pltpu.CompilerParams(dimension_semantics=("arbitrary",)),
        name="dest",
    )(eidx, rank, pstart_col)


def _dispatch_body(dest_ref, h1p_ref, zeros_ref, xs_ref, sem):
    del zeros_ref

    def issue(t, carry):
        for kk in range(TOP_K):
            pltpu.make_async_copy(h1p_ref.at[pl.ds(t, 1)],
                                  xs_ref.at[pl.ds(dest_ref[kk, t], 1)], sem).start(priority=kk % 2)
        return carry

    lax.fori_loop(0, ROWS, issue, 0, unroll=8)
    for _ in range(TOP_K):
        pltpu.make_async_copy(h1p_ref, xs_ref.at[pl.ds(0, ROWS)], sem).wait()


def _dispatch(dest, h1p, n_rows):
    t, width = h1p.shape
    zeros = jnp.zeros((n_rows, width), U32)
    return pl.pallas_call(
        _dispatch_body,
        grid=(t // ROWS,),
        in_specs=[pl.BlockSpec((TOP_K, ROWS), lambda i: (0, i), memory_space=pltpu.SMEM),
                  pl.BlockSpec((ROWS, width), lambda i: (i, 0)), pl.BlockSpec(memory_space=pl.ANY)],
        out_specs=pl.BlockSpec(memory_space=pl.ANY),
        out_shape=jax.ShapeDtypeStruct((n_rows, width), U32),
        scratch_shapes=[pltpu.SemaphoreType.DMA],
        input_output_aliases={2: 0},
        compiler_params=pltpu.CompilerParams(dimension_semantics=("arbitrary",)),
        name="dispatch",
    )(dest, h1p, zeros)


def _experts_body(bs_ref, xs_in_ref, wg_ref, wu_ref, wd_ref, xs_ref,
                  xbuf, ybuf, wgb, wub, wdb, sem_in, sem_out):
    del xs_in_ref
    e = pl.program_id(0)
    b0 = bs_ref[e]
    b1 = bs_ref[e + 1]
    n_total = bs_ref[N_EXPERTS]
    half = D_MODEL // 2

    def fetch(b, slot):
        return pltpu.make_async_copy(xs_ref.at[pl.ds(b * MOE_BLOCK, MOE_BLOCK)], xbuf.at[slot],
                                     sem_in.at[slot])

    def store(b, slot):
        return pltpu.make_async_copy(ybuf.at[slot], xs_ref.at[pl.ds(b * MOE_BLOCK, MOE_BLOCK)],
                                     sem_out.at[slot])

    @pl.when(jnp.logical_and(e == 0, n_total > 0))
    def _():
        fetch(0, 0).start()

    @pl.when(b1 > b0)
    def _():
        wgb[...] = wg_ref[0].astype(BF16)
        wub[...] = wu_ref[0].astype(BF16)
        wdb[...] = wd_ref[0].astype(BF16)

    def step(b, carry):
        slot = b % 2

        @pl.when(b + 1 < n_total)
        def _():
            fetch(b + 1, 1 - slot).start()

        fetch(b, slot).wait()
        xa, xb = _unpack_halves(xbuf[slot])
        xa = xa.astype(BF16)
        xb = xb.astype(BF16)
        gate = _dot(xa, wgb[:half, :]) + _dot(xb, wgb[half:, :])
        up = _dot(xa, wub[:half, :]) + _dot(xb, wub[half:, :])
        hidden = (gate * jax.nn.sigmoid(gate) * up).astype(BF16)
        y = _dot(hidden, wdb[...])

        @pl.when(b >= 2)
        def _():
            store(b - 2, slot).wait()

        ybuf[slot] = _pack_halves(y[:, :half], y[:, half:])
        store(b, slot).start()
        return carry

    lax.fori_loop(b0, b1, step, 0)

    @pl.when(e == N_EXPERTS - 1)
    def _():
        @pl.when(n_total >= 2)
        def _():
            store(n_total - 2, n_total % 2).wait()

        @pl.when(n_total >= 1)
        def _():
            store(n_total - 1, (n_total - 1) % 2).wait()


def _experts(block_start, xs, w_gate, w_up, w_down):
    n_rows, width = xs.shape
    wspec = lambda shape: pl.BlockSpec((1,) + shape, lambda e, bs: (e, 0, 0))
    grid_spec = pltpu.PrefetchScalarGridSpec(
        num_scalar_prefetch=1,
        grid=(N_EXPERTS,),
        in_specs=[pl.BlockSpec(memory_space=pl.ANY),
                  wspec((D_MODEL, D_EXPERT)), wspec((D_MODEL, D_EXPERT)), wspec((D_EXPERT, D_MODEL))],
        out_specs=pl.BlockSpec(memory_space=pl.ANY),
        scratch_shapes=[pltpu.VMEM((2, MOE_BLOCK, width), U32), pltpu.VMEM((2, MOE_BLOCK, width), U32),
                        pltpu.VMEM((D_MODEL, D_EXPERT), BF16), pltpu.VMEM((D_MODEL, D_EXPERT), BF16),
                        pltpu.VMEM((D_EXPERT, D_MODEL), BF16),
                        pltpu.SemaphoreType.DMA((2,)), pltpu.SemaphoreType.DMA((2,))],
    )
    return pl.pallas_call(
        _experts_body,
        grid_spec=grid_spec,
        out_shape=jax.ShapeDtypeStruct((n_rows, width), U32),
        input_output_aliases={1: 0},
        compiler_params=pltpu.CompilerParams(dimension_semantics=("arbitrary",),
                                             vmem_limit_bytes=VMEM_LIMIT),
        name="experts",
    )(block_start, xs, w_gate, w_up, w_down)


def _combine_body(dest_ref, h1_ref, w_ref, ys_ref, wsg_ref, wsu_ref, wsd_ref, g2_ref, b2_ref,
                  out_ref, buf_ref, sem):
    def issue(t, carry):
        for kk in range(TOP_K):
            pltpu.make_async_copy(ys_ref.at[pl.ds(dest_ref[kk, t], 1)],
                                  buf_ref.at[kk, pl.ds(t, 1)], sem).start(priority=kk % 2)
        return carry

    lax.fori_loop(0, ROWS, issue, 0, unroll=8)

    h1 = h1_ref[...]
    hb = h1.astype(BF16)
    gate = _dot(hb, wsg_ref[...])
    up = _dot(hb, wsu_ref[...])
    shared = _dot((gate * jax.nn.sigmoid(gate) * up).astype(BF16), wsd_ref[...])

    for kk in range(TOP_K):
        pltpu.make_async_copy(ys_ref.at[pl.ds(0, ROWS)], buf_ref.at[kk], sem).wait()

    half = D_MODEL // 2
    lo = jnp.zeros((ROWS, half), F32)
    hi = jnp.zeros((ROWS, half), F32)
    for kk in range(TOP_K):
        a, b = _unpack_halves(buf_ref[kk])
        wk = w_ref[:, kk:kk + 1]
        lo = lo + wk * a
        hi = hi + wk * b
    z = DN_ALPHA * h1 + shared + jnp.concatenate([lo, hi], axis=1)
    out_ref[...] = _layer_norm(z, g2_ref[...], b2_ref[...])


def _combine(dest, h1, w_rows, ys, wsg, wsu, wsd, ln2_g, ln2_b):
    t = h1.shape[0]
    width = ys.shape[1]
    d_shared = wsg.shape[1]
    return pl.pallas_call(
        _combine_body,
        grid=(t // ROWS,),
        in_specs=[pl.BlockSpec((TOP_K, ROWS), lambda i: (0, i), memory_space=pltpu.SMEM),
                  pl.BlockSpec((ROWS, D_MODEL), lambda i: (i, 0)),
                  pl.BlockSpec((ROWS, TOP_K), lambda i: (i, 0)),
                  pl.BlockSpec(memory_space=pl.ANY),
                  _const_spec((D_MODEL, d_shared)), _const_spec((D_MODEL, d_shared)),
                  _const_spec((d_shared, D_MODEL)),
                  _const_spec((1, D_MODEL)), _const_spec((1, D_MODEL))],
        out_specs=pl.BlockSpec((ROWS, D_MODEL), lambda i: (i, 0)),
        out_shape=jax.ShapeDtypeStruct((t, D_MODEL), F32),
        scratch_shapes=[pltpu.VMEM((TOP_K, ROWS, width), U32), pltpu.SemaphoreType.DMA],
        compiler_params=pltpu.CompilerParams(dimension_semantics=("arbitrary",),
                                             vmem_limit_bytes=VMEM_LIMIT),
        name="combine",
    )(dest, h1, w_rows, ys, wsg, wsu, wsd, ln2_g, ln2_b)


def _pack_in_proj(w_in, b_in, gate_w_f, gate_w_b, gate_b_f, gate_b_b):
    o_pool, o_q, o_k, o_v, o_g = 0, 512, 1024, 1536, 2560
    o_lrf, o_lrb, o_gp, o_gg = 3584, 3600, 3616, 4640
    order = [(o_pool, POOL_WIDTH), (o_q, GLA_DK), (o_k, GLA_DK), (o_v, GLA_DV), (o_g, GLA_DV),
             (o_gp, D_MODEL), (o_gg, D_MODEL), (o_lrf, GATE_RANK), (o_lrb, GATE_RANK)]
    pad = LANE - 2 * GATE_RANK
    w = jnp.concatenate([w_in[:, o:o + n] for o, n in order] + [jnp.zeros((D_MODEL, pad), F32)], axis=1)
    b = jnp.concatenate([b_in[o:o + n] for o, n in order] + [jnp.zeros((pad,), F32)])
    w_gate = jnp.zeros((LANE, 2 * GLA_DK), F32)
    w_gate = w_gate.at[:GATE_RANK, :GLA_DK].set(gate_w_f)
    w_gate = w_gate.at[GATE_RANK:2 * GATE_RANK, GLA_DK:].set(gate_w_b)
    b_gate = jnp.concatenate([gate_b_f, gate_b_b])[None, :]
    return w.astype(BF16), b[None, :], w_gate, b_gate


def kernel(x, meta, ln0_g, ln0_b, w_in, b_in, pool_w, pool_scale, gate_w_f, gate_b_f, gate_w_b, gate_b_b,
           gla_norm_g, w_branch_pool, w_branch_gla, w_out, ln1_g, ln1_b, w_router, router_bias,
           w_exp_gate, w_exp_up, w_exp_down, w_sh_gate, w_sh_up, w_sh_down, ln2_g, ln2_b):
    batch, seq, d = x.shape
    assert d == D_MODEL and seq % ROWS == 0 and w_in.shape[0] == 1
    assert (batch * seq) % ROUTE_ROWS == 0
    t = batch * seq
    row = lambda a: a.reshape(1, -1).astype(F32)

    lead = jnp.concatenate([jnp.zeros((FRONT - N_META, d), x.dtype), meta.astype(x.dtype)], axis=0)
    xp = jnp.concatenate([jnp.broadcast_to(lead[None], (batch, FRONT, d)), x], axis=1)
    tiles_per_batch = (seq + FRONT) // ROWS
    xp = xp.reshape(batch * (seq + FRONT), d)

    w_packed, b_packed, w_gate, b_gate = _pack_in_proj(
        w_in[0], b_in[0], gate_w_f[0], gate_w_b[0], gate_b_f[0], gate_b_b[0])
    u, q, k, v, sg, gp, gg, bf, cb = _in_proj(
        xp, row(ln0_g), row(ln0_b), w_packed, b_packed, w_gate, b_gate, tiles_per_batch)

    o_f, o_b = _gla(q, k, v, bf, cb, batch)

    h1, h1p = _mix(
        x.reshape(t, d), u, gp, gg, sg, o_f, o_b, row(ln0_g), row(ln0_b),
        pool_w[0].astype(BF16), row(pool_scale[0]), row(jnp.tile(gla_norm_g[0], GLA_HEADS)),
        w_branch_pool[0].astype(BF16), w_branch_gla[0].astype(BF16), w_out[0].astype(BF16),
        row(ln1_g[0]), row(ln1_b[0]), seq)

    eidx, ew, rank, counts = _route(h1, w_router[0].T, router_bias[0].reshape(N_EXPERTS, 1))

    counts = counts[:, 0]
    padded = ((counts + MOE_BLOCK - 1) // MOE_BLOCK) * MOE_BLOCK
    pend = jnp.cumsum(padded)
    pstart = pend - padded
    dest = _dest(eidx, rank, pstart.astype(F32).reshape(N_EXPERTS, 1))
    n_blocks = -(-(t * TOP_K) // MOE_BLOCK) + N_EXPERTS
    block_start = (jnp.concatenate([pstart, pend[-1:]]) // MOE_BLOCK).astype(I32)

    xs = _dispatch(dest, h1p, n_blocks * MOE_BLOCK)
    ys = _experts(block_start, xs, w_exp_gate[0], w_exp_up[0], w_exp_down[0])
    out = _combine(dest, h1, ew.T, ys, w_sh_gate[0].astype(BF16), w_sh_up[0].astype(BF16),
                   w_sh_down[0].astype(BF16), row(ln2_g[0]), row(ln2_b[0]))
    return out.reshape(batch, seq, d)
```

```python
import functools

import jax
import jax.numpy as jnp
from jax import lax
from jax.experimental import pallas as pl
from jax.experimental.pallas import tpu as pltpu

F32 = jnp.float32
BF16 = jnp.bfloat16
I32 = jnp.int32
U32 = jnp.uint32
HIGHEST = lax.Precision.HIGHEST

D_MODEL = 1024
N_META = 16
POOL_WINDOWS = (2, 4, 8, 16)
POOL_GROUP = 128
POOL_WIDTH = POOL_GROUP * len(POOL_WINDOWS)
GLA_HEADS = 4
GLA_DK = 512
GLA_DV = 1024
HEAD_K = GLA_DK // GLA_HEADS
HEAD_V = GLA_DV // GLA_HEADS
GATE_RANK = 16
GATE_NORMALIZER = 16.0
RMS_EPS = 1e-5
N_EXPERTS = 256
TOP_K = 8
N_GROUPS = 8
GROUP_SIZE = N_EXPERTS // N_GROUPS
TOPK_GROUPS = 4
D_EXPERT = 256
ROUTED_SCALE = 2.5
DN_ALPHA = 2.0 ** 0.25
LN_EPS = 1e-5

LANE = 128
SUBLANES = 8
ROWS = 256
FRONT = ROWS
GLA_CHUNK = 128
HALO = 16
ROUTE_ROWS = 512
MOE_BLOCK = 256
IN_SLOTS = 4
VMEM_LIMIT = 56 * 1024 * 1024

C_POOL, C_Q, C_K, C_V, C_G, C_GP, C_GG, C_LR = 0, 512, 1024, 1536, 2560, 3584, 4608, 5632
W_PACKED = C_LR + LANE


def _layer_norm(x, g, b):
    mu = jnp.mean(x, axis=-1, keepdims=True)
    xc = x - mu
    var = jnp.mean(xc * xc, axis=-1, keepdims=True)
    return xc * lax.rsqrt(var + LN_EPS) * g + b


def _dot(a, b):
    return jnp.dot(a, b, preferred_element_type=F32)


def _dot_nt(a, b):
    return lax.dot_general(a, b, (((1,), (1,)), ((), ())), preferred_element_type=F32)


def _split_dot(m, g):
    g_hi = g.astype(BF16)
    g_lo = (g - g_hi.astype(F32)).astype(BF16)
    return _dot(m, g_hi) + _dot(m, g_lo)


def _pack_halves(a, b):
    pa = lax.bitcast_convert_type(a.astype(BF16).astype(F32), U32)
    pb = lax.bitcast_convert_type(b.astype(BF16).astype(F32), U32)
    return pa | (pb >> 16)


def _unpack_halves(p):
    a = lax.bitcast_convert_type(p & jnp.uint32(0xFFFF0000), F32)
    b = lax.bitcast_convert_type(p << 16, F32)
    return a, b


def _const_spec(shape):
    return pl.BlockSpec(shape, lambda *_: (0,) * len(shape))


def _inproj_body(lead_ref, x_ref, g0_ref, b0_ref, w_ref, bias_ref, wgate_ref, bgate_ref,
                 u_ref, q_ref, k_ref, v_ref, sg_ref, gp_ref, gg_ref, bf_ref, cb_ref,
                 *, tiles_per_batch):
    i = pl.program_id(0)
    x_in = jnp.where(i % tiles_per_batch == 0, lead_ref[...], x_ref[...])
    h = _layer_norm(x_in, g0_ref[...], b0_ref[...])
    hb = h.astype(BF16)

    def proj(c0, n):
        return _dot(hb, w_ref[:, c0:c0 + n]) + bias_ref[:, c0:c0 + n]

    row = lax.broadcasted_iota(I32, (ROWS, 1), 0)
    valid = row >= jnp.where(i % tiles_per_batch == 0, FRONT - N_META, 0)

    u_ref[...] = proj(C_POOL, POOL_WIDTH).astype(BF16)
    q_ref[...] = (proj(C_Q, GLA_DK) * (HEAD_K ** -0.5)).astype(BF16)
    k_ref[...] = jnp.where(valid, proj(C_K, GLA_DK), 0.0).astype(BF16)
    v_ref[...] = jnp.where(valid, proj(C_V, GLA_DV), 0.0).astype(BF16)
    g = proj(C_G, GLA_DV)
    sg_ref[...] = (g * jax.nn.sigmoid(g)).astype(BF16)
    gp_ref[...] = jax.nn.sigmoid(proj(C_GP, D_MODEL)).astype(BF16)
    gg_ref[...] = jax.nn.sigmoid(proj(C_GG, D_MODEL)).astype(BF16)

    lr = proj(C_LR, LANE)
    xg = jnp.dot(lr, wgate_ref[...], precision=HIGHEST, preferred_element_type=F32) + bgate_ref[...]
    gk = (jnp.minimum(xg, 0.0) - jnp.log(1.0 + jnp.exp(-jnp.abs(xg)))) * (1.0 / GATE_NORMALIZER)

    r = lax.broadcasted_iota(I32, (GLA_CHUNK, GLA_CHUNK), 0)
    c = lax.broadcasted_iota(I32, (GLA_CHUNK, GLA_CHUNK), 1)
    tril = (r >= c).astype(BF16)
    triu = (r <= c).astype(BF16)
    for ci in range(ROWS // GLA_CHUNK):
        sl = slice(ci * GLA_CHUNK, (ci + 1) * GLA_CHUNK)
        bf_ref[sl, :] = _split_dot(tril, gk[sl, :GLA_DK])
        cb_ref[sl, :] = _split_dot(triu, gk[sl, GLA_DK:])


def _in_proj(lead, x2, ln0_g, ln0_b, w_packed, b_packed, w_gate, b_gate, tiles_per_batch):
    x_tiles = tiles_per_batch - FRONT // ROWS
    n = x2.shape[0] // (x_tiles * ROWS) * tiles_per_batch * ROWS
    row_spec = lambda width: pl.BlockSpec((ROWS, width), lambda i: (i, 0))
    x_spec = pl.BlockSpec(
        (ROWS, D_MODEL),
        lambda i: ((i // tiles_per_batch) * x_tiles + jnp.maximum(i % tiles_per_batch - 1, 0), 0))
    out_widths = (POOL_WIDTH, GLA_DK, GLA_DK, GLA_DV, GLA_DV, D_MODEL, D_MODEL)
    out_shape = [jax.ShapeDtypeStruct((n, w), BF16) for w in out_widths]
    out_shape += [jax.ShapeDtypeStruct((n, GLA_DK), F32)] * 2
    out_specs = [row_spec(w) for w in out_widths] + [row_spec(GLA_DK)] * 2
    return pl.pallas_call(
        functools.partial(_inproj_body, tiles_per_batch=tiles_per_batch),
        grid=(n // ROWS,),
        in_specs=[_const_spec((FRONT, D_MODEL)), x_spec,
                  _const_spec((1, D_MODEL)), _const_spec((1, D_MODEL)),
                  _const_spec((D_MODEL, W_PACKED)), _const_spec((1, W_PACKED)),
                  _const_spec((LANE, 2 * GLA_DK)), _const_spec((1, 2 * GLA_DK))],
        out_specs=out_specs,
        out_shape=out_shape,
        compiler_params=pltpu.CompilerParams(dimension_semantics=("arbitrary",),
                                             vmem_limit_bytes=VMEM_LIMIT),
        name="in_proj",
    )(lead, x2, ln0_g, ln0_b, w_packed, b_packed, w_gate, b_gate)


def _gla_direction(q_ref, k_ref, v_ref, b_ref, o_ref, s_ref, reverse):
    c = GLA_CHUNK
    r = lax.broadcasted_iota(I32, (c, c), 0)
    s = lax.broadcasted_iota(I32, (c, c), 1)
    keep = (r <= s) if reverse else (r >= s)
    eye = lax.broadcasted_iota(I32, (HEAD_K, HEAD_K), 0) == lax.broadcasted_iota(I32, (HEAD_K, HEAD_K), 1)
    edge = 0 if reverse else c - 1
    for h in range(GLA_HEADS):
        ks = slice(h * HEAD_K, (h + 1) * HEAD_K)
        vs = slice(h * HEAD_V, (h + 1) * HEAD_V)
        b = b_ref[:, ks]
        qh = q_ref[:, ks].astype(F32)
        kh = k_ref[:, ks].astype(F32)
        vh = v_ref[:, vs]
        b_tot = b[edge:edge + 1, :]
        b_mid = b[c // 2:c // 2 + 1, :]
        q_state = (qh * jnp.exp(b)).astype(BF16)
        q_in = (qh * jnp.exp(b - b_mid)).astype(BF16)
        k_in = (kh * jnp.exp(b_mid - b)).astype(BF16)
        k_state = (kh * jnp.exp(b_tot - b)).T.astype(BF16)
        att = jnp.where(keep, _dot_nt(q_in, k_in), 0.0).astype(BF16)
        state = s_ref[h]
        o = _dot(q_state, state.astype(BF16)) + _dot(att, vh)
        o_ref[:, vs] = o.astype(o_ref.dtype)
        decay_col = jnp.sum(jnp.where(eye, jnp.exp(b_tot), 0.0), axis=1, keepdims=True)
        s_ref[h] = decay_col * state + _dot(k_state, vh)


def _gla_body(qf_ref, kf_ref, vf_ref, bf_ref, qb_ref, kb_ref, vb_ref, cb_ref,
              of_ref, ob_ref, sf_ref, sb_ref):
    @pl.when(pl.program_id(1) == 0)
    def _():
        sf_ref[...] = jnp.zeros_like(sf_ref)
        sb_ref[...] = jnp.zeros_like(sb_ref)

    _gla_direction(qf_ref, kf_ref, vf_ref, bf_ref, of_ref, sf_ref, reverse=False)
    _gla_direction(qb_ref, kb_ref, vb_ref, cb_ref, ob_ref, sb_ref, reverse=True)


def _gla(q, k, v, bf, cb, batch):
    n = q.shape[0]
    nch = n // batch // GLA_CHUNK
    fwd = lambda width: pl.BlockSpec((GLA_CHUNK, width), lambda b, i: (b * nch + i, 0))
    bwd = lambda width: pl.BlockSpec((GLA_CHUNK, width), lambda b, i: (b * nch + nch - 1 - i, 0))
    state = pltpu.VMEM((GLA_HEADS, HEAD_K, HEAD_V), F32)
    return pl.pallas_call(
        _gla_body,
        grid=(batch, nch),
        in_specs=[fwd(GLA_DK), fwd(GLA_DK), fwd(GLA_DV), fwd(GLA_DK),
                  bwd(GLA_DK), bwd(GLA_DK), bwd(GLA_DV), bwd(GLA_DK)],
        out_specs=[fwd(GLA_DV), bwd(GLA_DV)],
        out_shape=[jax.ShapeDtypeStruct((n, GLA_DV), BF16)] * 2,
        scratch_shapes=[state, state],
        compiler_params=pltpu.CompilerParams(dimension_semantics=("arbitrary", "arbitrary"),
                                             vmem_limit_bytes=VMEM_LIMIT),
        name="gla",
    )(q, k, v, bf, q, k, v, cb)


def _mix_body(x_ref, up_ref, u_ref, un_ref, gp_ref, gg_ref, sg_ref, of_ref, ob_ref,
              g0_ref, b0_ref, pw_ref, ps_ref, gn_ref, wbp_ref, wbg_ref, wo_ref, g1_ref, b1_ref,
              h1_ref, h1p_ref, *, tiles_per_batch, seq):
    j = pl.program_id(0) % tiles_per_batch
    u_main = u_ref[...]
    u_ext = jnp.concatenate([up_ref[...], u_main, un_ref[...]], axis=0)
    r = lax.broadcasted_iota(I32, (ROWS, ROWS + 2 * HALO), 0)
    e = lax.broadcasted_iota(I32, (ROWS, ROWS + 2 * HALO), 1) - HALO
    in_seq = (j * ROWS + e) < seq
    pos = j * ROWS + lax.broadcasted_iota(I32, (ROWS, 1), 0)
    y_pool = []
    for gi, w in enumerate(POOL_WINDOWS):
        cs = slice(gi * POOL_GROUP, (gi + 1) * POOL_GROUP)
        band = jnp.logical_and(jnp.logical_and(e >= r - w // 2, e < r + w - w // 2), in_seq)
        count = w - jnp.maximum(pos + (w - w // 2) - seq, 0)
        mean = _dot(band.astype(BF16), u_ext[:, cs]) / count.astype(F32)
        d = mean - u_main[:, cs].astype(F32)
        y_pool.append(_dot(d.astype(BF16), pw_ref[gi]))
    y_pool = (jnp.concatenate(y_pool, axis=1) * ps_ref[...]).astype(BF16)

    o = of_ref[...].astype(F32) + ob_ref[...].astype(F32)
    y_gla = []
    for h in range(GLA_HEADS):
        oh = o[:, h * HEAD_V:(h + 1) * HEAD_V]
        y_gla.append(oh * lax.rsqrt(jnp.mean(oh * oh, axis=-1, keepdims=True) + RMS_EPS))
    y_gla = (jnp.concatenate(y_gla, axis=1) * gn_ref[...] * sg_ref[...].astype(F32)).astype(BF16)

    merged = (gp_ref[...].astype(F32) * _dot(y_pool, wbp_ref[...])
              + gg_ref[...].astype(F32) * _dot(y_gla, wbg_ref[...]))
    y = _dot(merged.astype(BF16), wo_ref[...])
    h = _layer_norm(x_ref[...], g0_ref[...], b0_ref[...])
    h1 = _layer_norm(DN_ALPHA * h + y, g1_ref[...], b1_ref[...])
    h1_ref[...] = h1
    h1p_ref[...] = _pack_halves(h1[:, :D_MODEL // 2], h1[:, D_MODEL // 2:])


def _mix(x2, u, gp, gg, sg, o_f, o_b, ln0_g, ln0_b, pool_w, pool_scale, norm_g, wbp, wbg, wo,
         ln1_g, ln1_b, seq):
    t = x2.shape[0]
    tiles_per_batch = seq // ROWS
    padded_tiles = tiles_per_batch + FRONT // ROWS
    halo_per_tile = ROWS // HALO
    last_halo = u.shape[0] // HALO - 1

    def pidx(i):
        return (i // tiles_per_batch) * padded_tiles + FRONT // ROWS + i % tiles_per_batch

    padded = lambda width: pl.BlockSpec((ROWS, width), lambda i: (pidx(i), 0))
    plain = lambda width: pl.BlockSpec((ROWS, width), lambda i: (i, 0))
    prev_halo = pl.BlockSpec((HALO, POOL_WIDTH), lambda i: (pidx(i) * halo_per_tile - 1, 0))
    next_halo = pl.BlockSpec(
        (HALO, POOL_WIDTH), lambda i: (jnp.minimum((pidx(i) + 1) * halo_per_tile, last_halo), 0))
    return pl.pallas_call(
        functools.partial(_mix_body, tiles_per_batch=tiles_per_batch, seq=seq),
        grid=(t // ROWS,),
        in_specs=[plain(D_MODEL), prev_halo, padded(POOL_WIDTH), next_halo,
                  padded(D_MODEL), padded(D_MODEL), padded(GLA_DV), padded(GLA_DV), padded(GLA_DV),
                  _const_spec((1, D_MODEL)), _const_spec((1, D_MODEL)),
                  _const_spec((len(POOL_WINDOWS), POOL_GROUP, POOL_GROUP)), _const_spec((1, POOL_WIDTH)),
                  _const_spec((1, GLA_DV)), _const_spec((POOL_WIDTH, D_MODEL)),
                  _const_spec((GLA_DV, D_MODEL)), _const_spec((D_MODEL, D_MODEL)),
                  _const_spec((1, D_MODEL)), _const_spec((1, D_MODEL))],
        out_specs=[plain(D_MODEL), plain(D_MODEL // 2)],
        out_shape=[jax.ShapeDtypeStruct((t, D_MODEL), F32), jax.ShapeDtypeStruct((t, D_MODEL // 2), U32)],
        compiler_params=pltpu.CompilerParams(dimension_semantics=("arbitrary",),
                                             vmem_limit_bytes=VMEM_LIMIT),
        name="mix",
    )(x2, u, u, u, gp, gg, sg, o_f, o_b, ln0_g, ln0_b, pool_w, pool_scale, norm_g, wbp, wbg, wo,
      ln1_g, ln1_b)


def _first_index_of_max(val, idx, size):
    m = jnp.max(val, axis=0, keepdims=True)
    first = jnp.min(jnp.where(val == m, idx, size), axis=0, keepdims=True)
    return m, first


def _route_body(h_ref, wr_ref, bias_ref, eidx_ref, w_ref, rank_ref, cnt_ref, base_ref):
    n = ROUTE_ROWS

    @pl.when(pl.program_id(0) == 0)
    def _():
        base_ref[...] = jnp.zeros_like(base_ref)

    logits = lax.dot_general(wr_ref[...], h_ref[...], (((1,), (1,)), ((), ())),
                             precision=HIGHEST, preferred_element_type=F32)
    scores = jax.nn.sigmoid(logits)
    sel = scores + bias_ref[...]

    gidx = lax.broadcasted_iota(I32, (GROUP_SIZE, n), 0)
    groups = [sel[g * GROUP_SIZE:(g + 1) * GROUP_SIZE, :] for g in range(N_GROUPS)]
    gscore = []
    for grp in groups:
        m1, first = _first_index_of_max(grp, gidx, GROUP_SIZE)
        m2 = jnp.max(jnp.where(gidx == first, -jnp.inf, grp), axis=0, keepdims=True)
        gscore.append(m1 + m2)

    masked = []
    for gi in range(N_GROUPS):
        beaten = jnp.zeros((1, n), F32)
        for gj in range(N_GROUPS):
            if gj != gi:
                wins = (gscore[gj] >= gscore[gi]) if gj < gi else (gscore[gj] > gscore[gi])
                beaten = beaten + wins.astype(F32)
        masked.append(jnp.where(beaten < TOPK_GROUPS, groups[gi], -jnp.inf))
    val = jnp.concatenate(masked, axis=0)

    eidx = lax.broadcasted_iota(I32, (N_EXPERTS, n), 0)
    chosen_f = jnp.zeros((N_EXPERTS, n), F32)
    picks, weights = [], []
    for _ in range(TOP_K):
        _, first = _first_index_of_max(val, eidx, N_EXPERTS)
        hit = eidx == first
        picks.append(first)
        weights.append(jnp.sum(jnp.where(hit, scores, 0.0), axis=0, keepdims=True))
        chosen_f = jnp.where(hit, 1.0, chosen_f)
        val = jnp.where(hit, -jnp.inf, val)
    wsum = weights[0]
    for wk in weights[1:]:
        wsum = wsum + wk
    scale = ROUTED_SCALE / wsum

    r = lax.broadcasted_iota(I32, (n, n), 0)
    c = lax.broadcasted_iota(I32, (n, n), 1)
    earlier = (r < c).astype(BF16)
    before = _dot(chosen_f.astype(BF16), earlier) + base_ref[...]
    base_ref[...] = base_ref[...] + jnp.sum(chosen_f, axis=1, keepdims=True)
    cnt_ref[...] = jnp.broadcast_to(base_ref[...], cnt_ref.shape).astype(I32)

    for kk in range(TOP_K):
        eidx_ref[kk:kk + 1, :] = picks[kk]
        w_ref[kk:kk + 1, :] = weights[kk] * scale
        rank = jnp.sum(jnp.where(eidx == picks[kk], before, 0.0), axis=0, keepdims=True)
        rank_ref[kk:kk + 1, :] = rank.astype(I32)


def _route(h1, wr_t, bias_col):
    t = h1.shape[0]
    col = lambda dtype: jax.ShapeDtypeStruct((TOP_K, t), dtype)
    kspec = pl.BlockSpec((TOP_K, ROUTE_ROWS), lambda i: (0, i))
    return pl.pallas_call(
        _route_body,
        grid=(t // ROUTE_ROWS,),
        in_specs=[pl.BlockSpec((ROUTE_ROWS, D_MODEL), lambda i: (i, 0)),
                  _const_spec((N_EXPERTS, D_MODEL)), _const_spec((N_EXPERTS, 1))],
        out_specs=[kspec, kspec, kspec, _const_spec((N_EXPERTS, LANE))],
        out_shape=[col(I32), col(F32), col(I32), jax.ShapeDtypeStruct((N_EXPERTS, LANE), I32)],
        scratch_shapes=[pltpu.VMEM((N_EXPERTS, 1), F32)],
        compiler_params=pltpu.CompilerParams(dimension_semantics=("arbitrary",),
                                             vmem_limit_bytes=VMEM_LIMIT),
        name="route",
    )(h1, wr_t, bias_col)


def _dest_body(eidx_ref, rank_ref, pstart_ref, dest_ref):
    n = eidx_ref.shape[1]
    eiota = lax.broadcasted_iota(I32, (N_EXPERTS, n), 0)
    pstart = pstart_ref[...]
    for kk in range(TOP_K):
        start = jnp.sum(jnp.where(eiota == eidx_ref[kk:kk + 1, :], pstart, 0.0), axis=0, keepdims=True)
        dest_ref[kk:kk + 1, :] = start.astype(I32) + rank_ref[kk:kk + 1, :]


def _dest(eidx, rank, pstart_col):
    t = eidx.shape[1]
    kspec = pl.BlockSpec((TOP_K, ROUTE_ROWS), lambda i: (0, i))
    return pl.pallas_call(
        _dest_body,
        grid=(t // ROUTE_ROWS,),
        in_specs=[kspec, kspec, _const_spec((N_EXPERTS, 1))],
        out_specs=kspec,
        out_shape=jax.ShapeDtypeStruct((TOP_K, t), I32),
        compiler_params=pltpu.CompilerParams(dimension_semantics=("arbitrary",)),
        name="dest",
    )(eidx, rank, pstart_col)


def _dispatch_body(dest_ref, h1p_ref, zeros_ref, xs_ref, sem):
    del zeros_ref

    def issue(g, carry):
        row0 = pl.multiple_of(g * SUBLANES, SUBLANES)
        for j in range(SUBLANES):
            for kk in range(TOP_K):
                pltpu.make_async_copy(
                    h1p_ref.at[pl.ds(row0 + j, 1)],
                    xs_ref.at[pl.ds(dest_ref[(row0 + j) * TOP_K + kk], 1)], sem).start(priority=kk % 2)
        return carry

    lax.fori_loop(0, ROWS // SUBLANES, issue, 0)
    for _ in range(TOP_K):
        pltpu.make_async_copy(h1p_ref, xs_ref.at[pl.ds(0, ROWS)], sem).wait()


def _dispatch(dest, h1p, n_rows):
    t, width = h1p.shape
    zeros = jnp.zeros((n_rows, width), U32)
    return pl.pallas_call(
        _dispatch_body,
        grid=(t // ROWS,),
        in_specs=[pl.BlockSpec((TOP_K * ROWS,), lambda i: (i,), memory_space=pltpu.SMEM),
                  pl.BlockSpec((ROWS, width), lambda i: (i, 0)), pl.BlockSpec(memory_space=pl.ANY)],
        out_specs=pl.BlockSpec(memory_space=pl.ANY),
        out_shape=jax.ShapeDtypeStruct((n_rows, width), U32),
        scratch_shapes=[pltpu.SemaphoreType.DMA],
        input_output_aliases={2: 0},
        compiler_params=pltpu.CompilerParams(dimension_semantics=("arbitrary",)),
        name="dispatch",
    )(dest, h1p, zeros)


def _experts_body(bs_ref, xs_in_ref, wg_ref, wu_ref, wd_ref, xs_ref,
                  xbuf, ybuf, wgb, wub, wdb, sem_in, sem_out):
    del xs_in_ref
    e = pl.program_id(0)
    b0 = bs_ref[e]
    b1 = bs_ref[e + 1]
    n_total = bs_ref[N_EXPERTS]
    half = D_MODEL // 2

    def fetch(b, slot):
        return pltpu.make_async_copy(xs_ref.at[pl.ds(b * MOE_BLOCK, MOE_BLOCK)], xbuf.at[slot],
                                     sem_in.at[slot])

    def store(b, slot):
        return pltpu.make_async_copy(ybuf.at[slot], xs_ref.at[pl.ds(b * MOE_BLOCK, MOE_BLOCK)],
                                     sem_out.at[slot])

    @pl.when(e == 0)
    def _():
        for b in range(IN_SLOTS - 1):
            @pl.when(b < n_total)
            def _():
                fetch(b, b).start(priority=1)

    @pl.when(b1 > b0)
    def _():
        wgb[...] = wg_ref[0].astype(BF16)
        wub[...] = wu_ref[0].astype(BF16)
        wdb[...] = wd_ref[0].astype(BF16)

    def step(b, carry):
        slot = b % 2
        ahead = b + IN_SLOTS - 1

        @pl.when(ahead < n_total)
        def _():
            fetch(ahead, ahead % IN_SLOTS).start(priority=1)

        fetch(b, b % IN_SLOTS).wait()
        xa, xb = _unpack_halves(xbuf[b % IN_SLOTS])
        xa = xa.astype(BF16)
        xb = xb.astype(BF16)
        gate = _dot(xa, wgb[:half, :]) + _dot(xb, wgb[half:, :])
        up = _dot(xa, wub[:half, :]) + _dot(xb, wub[half:, :])
        hidden = (gate * jax.nn.sigmoid(gate) * up).astype(BF16)
        y = _dot(hidden, wdb[...])

        @pl.when(b >= 2)
        def _():
            store(b - 2, slot).wait()

        ybuf[slot] = _pack_halves(y[:, :half], y[:, half:])
        store(b, slot).start(priority=1)
        return carry

    lax.fori_loop(b0, b1, step, 0)

    @pl.when(e == N_EXPERTS - 1)
    def _():
        @pl.when(n_total >= 2)
        def _():
            store(n_total - 2, n_total % 2).wait()

        @pl.when(n_total >= 1)
        def _():
            store(n_total - 1, (n_total - 1) % 2).wait()


def _experts(block_start, xs, w_gate, w_up, w_down):
    n_rows, width = xs.shape
    wspec = lambda shape: pl.BlockSpec((1,) + shape, lambda e, bs: (e, 0, 0))
    grid_spec = pltpu.PrefetchScalarGridSpec(
        num_scalar_prefetch=1,
        grid=(N_EXPERTS,),
        in_specs=[pl.BlockSpec(memory_space=pl.ANY),
                  wspec((D_MODEL, D_EXPERT)), wspec((D_MODEL, D_EXPERT)), wspec((D_EXPERT, D_MODEL))],
        out_specs=pl.BlockSpec(memory_space=pl.ANY),
        scratch_shapes=[pltpu.VMEM((IN_SLOTS, MOE_BLOCK, width), U32), pltpu.VMEM((2, MOE_BLOCK, width), U32),
                        pltpu.VMEM((D_MODEL, D_EXPERT), BF16), pltpu.VMEM((D_MODEL, D_EXPERT), BF16),
                        pltpu.VMEM((D_EXPERT, D_MODEL), BF16),
                        pltpu.SemaphoreType.DMA((IN_SLOTS,)), pltpu.SemaphoreType.DMA((2,))],
    )
    return pl.pallas_call(
        _experts_body,
        grid_spec=grid_spec,
        out_shape=jax.ShapeDtypeStruct((n_rows, width), U32),
        input_output_aliases={1: 0},
        compiler_params=pltpu.CompilerParams(dimension_semantics=("arbitrary",),
                                             vmem_limit_bytes=VMEM_LIMIT),
        name="experts",
    )(block_start, xs, w_gate, w_up, w_down)


def _combine_body(dest_ref, h1_ref, w_ref, ys_ref, wsg_ref, wsu_ref, wsd_ref, g2_ref, b2_ref,
                  out_ref, buf_ref, sem):
    def issue(g, carry):
        row0 = pl.multiple_of(g * SUBLANES, SUBLANES)
        for j in range(SUBLANES):
            for kk in range(TOP_K):
                pltpu.make_async_copy(
                    ys_ref.at[pl.ds(dest_ref[(row0 + j) * TOP_K + kk], 1)],
                    buf_ref.at[kk, pl.ds(row0 + j, 1)], sem).start(priority=kk % 2)
        return carry

    lax.fori_loop(0, ROWS // SUBLANES, issue, 0)

    h1 = h1_ref[...]
    hb = h1.astype(BF16)
    gate = _dot(hb, wsg_ref[...])
    up = _dot(hb, wsu_ref[...])
    shared = _dot((gate * jax.nn.sigmoid(gate) * up).astype(BF16), wsd_ref[...])

    for kk in range(TOP_K):
        pltpu.make_async_copy(ys_ref.at[pl.ds(0, ROWS)], buf_ref.at[kk], sem).wait()

    half = D_MODEL // 2
    lo = jnp.zeros((ROWS, half), F32)
    hi = jnp.zeros((ROWS, half), F32)
    for kk in range(TOP_K):
        a, b = _unpack_halves(buf_ref[kk])
        wk = w_ref[:, kk:kk + 1]
        lo = lo + wk * a
        hi = hi + wk * b
    z = DN_ALPHA * h1 + shared + jnp.concatenate([lo, hi], axis=1)
    out_ref[...] = _layer_norm(z, g2_ref[...], b2_ref[...])


def _combine(dest, h1, w_rows, ys, wsg, wsu, wsd, ln2_g, ln2_b):
    t = h1.shape[0]
    width = ys.shape[1]
    d_shared = wsg.shape[1]
    return pl.pallas_call(
        _combine_body,
        grid=(t // ROWS,),
        in_specs=[pl.BlockSpec((TOP_K * ROWS,), lambda i: (i,), memory_space=pltpu.SMEM),
                  pl.BlockSpec((ROWS, D_MODEL), lambda i: (i, 0)),
                  pl.BlockSpec((ROWS, TOP_K), lambda i: (i, 0)),
                  pl.BlockSpec(memory_space=pl.ANY),
                  _const_spec((D_MODEL, d_shared)), _const_spec((D_MODEL, d_shared)),
                  _const_spec((d_shared, D_MODEL)),
                  _const_spec((1, D_MODEL)), _const_spec((1, D_MODEL))],
        out_specs=pl.BlockSpec((ROWS, D_MODEL), lambda i: (i, 0)),
        out_shape=jax.ShapeDtypeStruct((t, D_MODEL), F32),
        scratch_shapes=[pltpu.VMEM((TOP_K, ROWS, width), U32), pltpu.SemaphoreType.DMA],
        compiler_params=pltpu.CompilerParams(dimension_semantics=("arbitrary",),
                                             vmem_limit_bytes=VMEM_LIMIT),
        name="combine",
    )(dest, h1, w_rows, ys, wsg, wsu, wsd, ln2_g, ln2_b)


def _pack_in_proj(w_in, b_in, gate_w_f, gate_w_b, gate_b_f, gate_b_b):
    o_pool, o_q, o_k, o_v, o_g = 0, 512, 1024, 1536, 2560
    o_lrf, o_lrb, o_gp, o_gg = 3584, 3600, 3616, 4640
    order = [(o_pool, POOL_WIDTH), (o_q, GLA_DK), (o_k, GLA_DK), (o_v, GLA_DV), (o_g, GLA_DV),
             (o_gp, D_MODEL), (o_gg, D_MODEL), (o_lrf, GATE_RANK), (o_lrb, GATE_RANK)]
    pad = LANE - 2 * GATE_RANK
    w = jnp.concatenate([w_in[:, o:o + n] for o, n in order] + [jnp.zeros((D_MODEL, pad), F32)], axis=1)
    b = jnp.concatenate([b_in[o:o + n] for o, n in order] + [jnp.zeros((pad,), F32)])
    w_gate = jnp.zeros((LANE, 2 * GLA_DK), F32)
    w_gate = w_gate.at[:GATE_RANK, :GLA_DK].set(gate_w_f)
    w_gate = w_gate.at[GATE_RANK:2 * GATE_RANK, GLA_DK:].set(gate_w_b)
    b_gate = jnp.concatenate([gate_b_f, gate_b_b])[None, :]
    return w.astype(BF16), b[None, :], w_gate, b_gate


def kernel(x, meta, ln0_g, ln0_b, w_in, b_in, pool_w, pool_scale, gate_w_f, gate_b_f, gate_w_b, gate_b_b,
           gla_norm_g, w_branch_pool, w_branch_gla, w_out, ln1_g, ln1_b, w_router, router_bias,
           w_exp_gate, w_exp_up, w_exp_down, w_sh_gate, w_sh_up, w_sh_down, ln2_g, ln2_b):
    batch, seq, d = x.shape
    assert d == D_MODEL and seq % ROWS == 0 and w_in.shape[0] == 1
    assert (batch * seq) % ROUTE_ROWS == 0
    t = batch * seq
    row = lambda a: a.reshape(1, -1).astype(F32)

    lead = jnp.concatenate([jnp.zeros((FRONT - N_META, d), x.dtype), meta.astype(x.dtype)], axis=0)
    tiles_per_batch = (seq + FRONT) // ROWS
    x2 = x.reshape(t, d)

    w_packed, b_packed, w_gate, b_gate = _pack_in_proj(
        w_in[0], b_in[0], gate_w_f[0], gate_w_b[0], gate_b_f[0], gate_b_b[0])
    u, q, k, v, sg, gp, gg, bf, cb = _in_proj(
        lead, x2, row(ln0_g), row(ln0_b), w_packed, b_packed, w_gate, b_gate, tiles_per_batch)

    o_f, o_b = _gla(q, k, v, bf, cb, batch)

    h1, h1p = _mix(
        x.reshape(t, d), u, gp, gg, sg, o_f, o_b, row(ln0_g), row(ln0_b),
        pool_w[0].astype(BF16), row(pool_scale[0]), row(jnp.tile(gla_norm_g[0], GLA_HEADS)),
        w_branch_pool[0].astype(BF16), w_branch_gla[0].astype(BF16), w_out[0].astype(BF16),
        row(ln1_g[0]), row(ln1_b[0]), seq)

    eidx, ew, rank, counts = _route(h1, w_router[0].T, router_bias[0].reshape(N_EXPERTS, 1))

    counts = counts[:, 0]
    padded = ((counts + MOE_BLOCK - 1) // MOE_BLOCK) * MOE_BLOCK
    pend = jnp.cumsum(padded)
    pstart = pend - padded
    dest = _dest(eidx, rank, pstart.astype(F32).reshape(N_EXPERTS, 1))
    n_blocks = -(-(t * TOP_K) // MOE_BLOCK) + N_EXPERTS
    block_start = (jnp.concatenate([pstart, pend[-1:]]) // MOE_BLOCK).astype(I32)

    dest = dest.T.reshape(-1)
    xs = _dispatch(dest, h1p, n_blocks * MOE_BLOCK)
    ys = _experts(block_start, xs, w_exp_gate[0], w_exp_up[0], w_exp_down[0])
    out = _combine(dest, h1, ew.T, ys, w_sh_gate[0].astype(BF16), w_sh_up[0].astype(BF16),
                   w_sh_down[0].astype(BF16), row(ln2_g[0]), row(ln2_b[0]))
    return out.reshape(batch, seq, d)
```

```python
import functools

import jax
import jax.numpy as jnp
from jax import lax
from jax.experimental import pallas as pl
from jax.experimental.pallas import tpu as pltpu
from jax.experimental.pallas import tpu_sc as plsc

F32 = jnp.float32
BF16 = jnp.bfloat16
I32 = jnp.int32
U32 = jnp.uint32
HIGHEST = lax.Precision.HIGHEST

D_MODEL = 1024
N_META = 16
POOL_WINDOWS = (2, 4, 8, 16)
POOL_GROUP = 128
POOL_WIDTH = POOL_GROUP * len(POOL_WINDOWS)
GLA_HEADS = 4
GLA_DK = 512
GLA_DV = 1024
HEAD_K = GLA_DK // GLA_HEADS
HEAD_V = GLA_DV // GLA_HEADS
GATE_RANK = 16
GATE_NORMALIZER = 16.0
RMS_EPS = 1e-5
N_EXPERTS = 256
TOP_K = 8
N_GROUPS = 8
GROUP_SIZE = N_EXPERTS // N_GROUPS
TOPK_GROUPS = 4
D_EXPERT = 256
ROUTED_SCALE = 2.5
DN_ALPHA = 2.0 ** 0.25
LN_EPS = 1e-5

LANE = 128
SUBLANES = 8
ROWS = 256
FRONT = ROWS
GLA_CHUNK = 128
HALO = 16
ROUTE_ROWS = 512
MOE_BLOCK = 256
IN_SLOTS = 4
SC_CORES = 2
SC_SUBCORES = 16
SC_CHUNK = 128
VMEM_LIMIT = 56 * 1024 * 1024

C_POOL, C_Q, C_K, C_V, C_G, C_GP, C_GG, C_LR = 0, 512, 1024, 1536, 2560, 3584, 4608, 5632
W_PACKED = C_LR + LANE


def _layer_norm(x, g, b):
    mu = jnp.mean(x, axis=-1, keepdims=True)
    xc = x - mu
    var = jnp.mean(xc * xc, axis=-1, keepdims=True)
    return xc * lax.rsqrt(var + LN_EPS) * g + b


def _dot(a, b):
    return jnp.dot(a, b, preferred_element_type=F32)


def _dot_nt(a, b):
    return lax.dot_general(a, b, (((1,), (1,)), ((), ())), preferred_element_type=F32)


def _split_dot(m, g):
    g_hi = g.astype(BF16)
    g_lo = (g - g_hi.astype(F32)).astype(BF16)
    return _dot(m, g_hi) + _dot(m, g_lo)


def _pack_halves(a, b):
    pa = lax.bitcast_convert_type(a.astype(BF16).astype(F32), U32)
    pb = lax.bitcast_convert_type(b.astype(BF16).astype(F32), U32)
    return pa | (pb >> 16)


def _unpack_halves(p):
    a = lax.bitcast_convert_type(p & jnp.uint32(0xFFFF0000), F32)
    b = lax.bitcast_convert_type(p << 16, F32)
    return a, b


def _const_spec(shape):
    return pl.BlockSpec(shape, lambda *_: (0,) * len(shape))


def _inproj_body(lead_ref, x_ref, g0_ref, b0_ref, w_ref, bias_ref, wgate_ref, bgate_ref,
                 u_ref, q_ref, k_ref, v_ref, sg_ref, gp_ref, gg_ref, bf_ref, cb_ref,
                 *, tiles_per_batch):
    i = pl.program_id(0)
    x_in = jnp.where(i % tiles_per_batch == 0, lead_ref[...], x_ref[...])
    h = _layer_norm(x_in, g0_ref[...], b0_ref[...])
    hb = h.astype(BF16)

    def proj(c0, n):
        return _dot(hb, w_ref[:, c0:c0 + n]) + bias_ref[:, c0:c0 + n]

    row = lax.broadcasted_iota(I32, (ROWS, 1), 0)
    valid = row >= jnp.where(i % tiles_per_batch == 0, FRONT - N_META, 0)

    u_ref[...] = proj(C_POOL, POOL_WIDTH).astype(BF16)
    q_ref[...] = (proj(C_Q, GLA_DK) * (HEAD_K ** -0.5)).astype(BF16)
    k_ref[...] = jnp.where(valid, proj(C_K, GLA_DK), 0.0).astype(BF16)
    v_ref[...] = jnp.where(valid, proj(C_V, GLA_DV), 0.0).astype(BF16)
    g = proj(C_G, GLA_DV)
    sg_ref[...] = (g * jax.nn.sigmoid(g)).astype(BF16)
    gp_ref[...] = jax.nn.sigmoid(proj(C_GP, D_MODEL)).astype(BF16)
    gg_ref[...] = jax.nn.sigmoid(proj(C_GG, D_MODEL)).astype(BF16)

    lr = proj(C_LR, LANE)
    xg = jnp.dot(lr, wgate_ref[...], precision=HIGHEST, preferred_element_type=F32) + bgate_ref[...]
    gk = (jnp.minimum(xg, 0.0) - jnp.log(1.0 + jnp.exp(-jnp.abs(xg)))) * (1.0 / GATE_NORMALIZER)

    r = lax.broadcasted_iota(I32, (GLA_CHUNK, GLA_CHUNK), 0)
    c = lax.broadcasted_iota(I32, (GLA_CHUNK, GLA_CHUNK), 1)
    tril = (r >= c).astype(BF16)
    triu = (r <= c).astype(BF16)
    for ci in range(ROWS // GLA_CHUNK):
        sl = slice(ci * GLA_CHUNK, (ci + 1) * GLA_CHUNK)
        bf_ref[sl, :] = _split_dot(tril, gk[sl, :GLA_DK])
        cb_ref[sl, :] = _split_dot(triu, gk[sl, GLA_DK:])


def _in_proj(lead, x2, ln0_g, ln0_b, w_packed, b_packed, w_gate, b_gate, tiles_per_batch):
    x_tiles = tiles_per_batch - FRONT // ROWS
    n = x2.shape[0] // (x_tiles * ROWS) * tiles_per_batch * ROWS
    row_spec = lambda width: pl.BlockSpec((ROWS, width), lambda i: (i, 0))
    x_spec = pl.BlockSpec(
        (ROWS, D_MODEL),
        lambda i: ((i // tiles_per_batch) * x_tiles + jnp.maximum(i % tiles_per_batch - 1, 0), 0))
    out_widths = (POOL_WIDTH, GLA_DK, GLA_DK, GLA_DV, GLA_DV, D_MODEL, D_MODEL)
    out_shape = [jax.ShapeDtypeStruct((n, w), BF16) for w in out_widths]
    out_shape += [jax.ShapeDtypeStruct((n, GLA_DK), F32)] * 2
    out_specs = [row_spec(w) for w in out_widths] + [row_spec(GLA_DK)] * 2
    return pl.pallas_call(
        functools.partial(_inproj_body, tiles_per_batch=tiles_per_batch),
        grid=(n // ROWS,),
        in_specs=[_const_spec((FRONT, D_MODEL)), x_spec,
                  _const_spec((1, D_MODEL)), _const_spec((1, D_MODEL)),
                  _const_spec((D_MODEL, W_PACKED)), _const_spec((1, W_PACKED)),
                  _const_spec((LANE, 2 * GLA_DK)), _const_spec((1, 2 * GLA_DK))],
        out_specs=out_specs,
        out_shape=out_shape,
        compiler_params=pltpu.CompilerParams(dimension_semantics=("arbitrary",),
                                             vmem_limit_bytes=VMEM_LIMIT),
        name="in_proj",
    )(lead, x2, ln0_g, ln0_b, w_packed, b_packed, w_gate, b_gate)


def _gla_direction(q_ref, k_ref, v_ref, b_ref, o_ref, s_ref, reverse):
    c = GLA_CHUNK
    r = lax.broadcasted_iota(I32, (c, c), 0)
    s = lax.broadcasted_iota(I32, (c, c), 1)
    keep = (r <= s) if reverse else (r >= s)
    eye = lax.broadcasted_iota(I32, (HEAD_K, HEAD_K), 0) == lax.broadcasted_iota(I32, (HEAD_K, HEAD_K), 1)
    edge = 0 if reverse else c - 1
    for h in range(GLA_HEADS):
        ks = slice(h * HEAD_K, (h + 1) * HEAD_K)
        vs = slice(h * HEAD_V, (h + 1) * HEAD_V)
        b = b_ref[:, ks]
        qh = q_ref[:, ks].astype(F32)
        kh = k_ref[:, ks].astype(F32)
        vh = v_ref[:, vs]
        b_tot = b[edge:edge + 1, :]
        b_mid = b[c // 2:c // 2 + 1, :]
        q_state = (qh * jnp.exp(b)).astype(BF16)
        q_in = (qh * jnp.exp(b - b_mid)).astype(BF16)
        k_in = (kh * jnp.exp(b_mid - b)).astype(BF16)
        k_state = (kh * jnp.exp(b_tot - b)).T.astype(BF16)
        att = jnp.where(keep, _dot_nt(q_in, k_in), 0.0).astype(BF16)
        state = s_ref[h]
        o = _dot(q_state, state.astype(BF16)) + _dot(att, vh)
        o_ref[:, vs] = o.astype(o_ref.dtype)
        decay_col = jnp.sum(jnp.where(eye, jnp.exp(b_tot), 0.0), axis=1, keepdims=True)
        s_ref[h] = decay_col * state + _dot(k_state, vh)


def _gla_body(qf_ref, kf_ref, vf_ref, bf_ref, qb_ref, kb_ref, vb_ref, cb_ref,
              of_ref, ob_ref, sf_ref, sb_ref):
    @pl.when(pl.program_id(1) == 0)
    def _():
        sf_ref[...] = jnp.zeros_like(sf_ref)
        sb_ref[...] = jnp.zeros_like(sb_ref)

    _gla_direction(qf_ref, kf_ref, vf_ref, bf_ref, of_ref, sf_ref, reverse=False)
    _gla_direction(qb_ref, kb_ref, vb_ref, cb_ref, ob_ref, sb_ref, reverse=True)


def _gla(q, k, v, bf, cb, batch):
    n = q.shape[0]
    nch = n // batch // GLA_CHUNK
    fwd = lambda width: pl.BlockSpec((GLA_CHUNK, width), lambda b, i: (b * nch + i, 0))
    bwd = lambda width: pl.BlockSpec((GLA_CHUNK, width), lambda b, i: (b * nch + nch - 1 - i, 0))
    state = pltpu.VMEM((GLA_HEADS, HEAD_K, HEAD_V), F32)
    return pl.pallas_call(
        _gla_body,
        grid=(batch, nch),
        in_specs=[fwd(GLA_DK), fwd(GLA_DK), fwd(GLA_DV), fwd(GLA_DK),
                  bwd(GLA_DK), bwd(GLA_DK), bwd(GLA_DV), bwd(GLA_DK)],
        out_specs=[fwd(GLA_DV), bwd(GLA_DV)],
        out_shape=[jax.ShapeDtypeStruct((n, GLA_DV), BF16)] * 2,
        scratch_shapes=[state, state],
        compiler_params=pltpu.CompilerParams(dimension_semantics=("arbitrary", "arbitrary"),
                                             vmem_limit_bytes=VMEM_LIMIT),
        name="gla",
    )(q, k, v, bf, q, k, v, cb)


def _mix_body(x_ref, up_ref, u_ref, un_ref, gp_ref, gg_ref, sg_ref, of_ref, ob_ref,
              g0_ref, b0_ref, pw_ref, ps_ref, gn_ref, wbp_ref, wbg_ref, wo_ref, g1_ref, b1_ref,
              h1_ref, h1p_ref, *, tiles_per_batch, seq):
    j = pl.program_id(0) % tiles_per_batch
    u_main = u_ref[...]
    u_ext = jnp.concatenate([up_ref[...], u_main, un_ref[...]], axis=0)
    r = lax.broadcasted_iota(I32, (ROWS, ROWS + 2 * HALO), 0)
    e = lax.broadcasted_iota(I32, (ROWS, ROWS + 2 * HALO), 1) - HALO
    in_seq = (j * ROWS + e) < seq
    pos = j * ROWS + lax.broadcasted_iota(I32, (ROWS, 1), 0)
    y_pool = []
    for gi, w in enumerate(POOL_WINDOWS):
        cs = slice(gi * POOL_GROUP, (gi + 1) * POOL_GROUP)
        band = jnp.logical_and(jnp.logical_and(e >= r - w // 2, e < r + w - w // 2), in_seq)
        count = w - jnp.maximum(pos + (w - w // 2) - seq, 0)
        mean = _dot(band.astype(BF16), u_ext[:, cs]) / count.astype(F32)
        d = mean - u_main[:, cs].astype(F32)
        y_pool.append(_dot(d.astype(BF16), pw_ref[gi]))
    y_pool = (jnp.concatenate(y_pool, axis=1) * ps_ref[...]).astype(BF16)

    o = of_ref[...].astype(F32) + ob_ref[...].astype(F32)
    y_gla = []
    for h in range(GLA_HEADS):
        oh = o[:, h * HEAD_V:(h + 1) * HEAD_V]
        y_gla.append(oh * lax.rsqrt(jnp.mean(oh * oh, axis=-1, keepdims=True) + RMS_EPS))
    y_gla = (jnp.concatenate(y_gla, axis=1) * gn_ref[...] * sg_ref[...].astype(F32)).astype(BF16)

    merged = (gp_ref[...].astype(F32) * _dot(y_pool, wbp_ref[...])
              + gg_ref[...].astype(F32) * _dot(y_gla, wbg_ref[...]))
    y = _dot(merged.astype(BF16), wo_ref[...])
    h = _layer_norm(x_ref[...], g0_ref[...], b0_ref[...])
    h1 = _layer_norm(DN_ALPHA * h + y, g1_ref[...], b1_ref[...])
    h1_ref[...] = h1
    h1p_ref[...] = _pack_halves(h1[:, :D_MODEL // 2], h1[:, D_MODEL // 2:])


def _mix(x2, u, gp, gg, sg, o_f, o_b, ln0_g, ln0_b, pool_w, pool_scale, norm_g, wbp, wbg, wo,
         ln1_g, ln1_b, seq):
    t = x2.shape[0]
    tiles_per_batch = seq // ROWS
    padded_tiles = tiles_per_batch + FRONT // ROWS
    halo_per_tile = ROWS // HALO
    last_halo = u.shape[0] // HALO - 1

    def pidx(i):
        return (i // tiles_per_batch) * padded_tiles + FRONT // ROWS + i % tiles_per_batch

    padded = lambda width: pl.BlockSpec((ROWS, width), lambda i: (pidx(i), 0))
    plain = lambda width: pl.BlockSpec((ROWS, width), lambda i: (i, 0))
    prev_halo = pl.BlockSpec((HALO, POOL_WIDTH), lambda i: (pidx(i) * halo_per_tile - 1, 0))
    next_halo = pl.BlockSpec(
        (HALO, POOL_WIDTH), lambda i: (jnp.minimum((pidx(i) + 1) * halo_per_tile, last_halo), 0))
    return pl.pallas_call(
        functools.partial(_mix_body, tiles_per_batch=tiles_per_batch, seq=seq),
        grid=(t // ROWS,),
        in_specs=[plain(D_MODEL), prev_halo, padded(POOL_WIDTH), next_halo,
                  padded(D_MODEL), padded(D_MODEL), padded(GLA_DV), padded(GLA_DV), padded(GLA_DV),
                  _const_spec((1, D_MODEL)), _const_spec((1, D_MODEL)),
                  _const_spec((len(POOL_WINDOWS), POOL_GROUP, POOL_GROUP)), _const_spec((1, POOL_WIDTH)),
                  _const_spec((1, GLA_DV)), _const_spec((POOL_WIDTH, D_MODEL)),
                  _const_spec((GLA_DV, D_MODEL)), _const_spec((D_MODEL, D_MODEL)),
                  _const_spec((1, D_MODEL)), _const_spec((1, D_MODEL))],
        out_specs=[plain(D_MODEL), plain(D_MODEL // 2)],
        out_shape=[jax.ShapeDtypeStruct((t, D_MODEL), F32), jax.ShapeDtypeStruct((t, D_MODEL // 2), U32)],
        compiler_params=pltpu.CompilerParams(dimension_semantics=("arbitrary",),
                                             vmem_limit_bytes=VMEM_LIMIT),
        name="mix",
    )(x2, u, u, u, gp, gg, sg, o_f, o_b, ln0_g, ln0_b, pool_w, pool_scale, norm_g, wbp, wbg, wo,
      ln1_g, ln1_b)


def _first_index_of_max(val, idx, size):
    m = jnp.max(val, axis=0, keepdims=True)
    first = jnp.min(jnp.where(val == m, idx, size), axis=0, keepdims=True)
    return m, first


def _route_body(h_ref, wr_ref, bias_ref, eidx_ref, w_ref, rank_ref, cnt_ref, base_ref):
    n = ROUTE_ROWS

    @pl.when(pl.program_id(0) == 0)
    def _():
        base_ref[...] = jnp.zeros_like(base_ref)

    logits = lax.dot_general(wr_ref[...], h_ref[...], (((1,), (1,)), ((), ())),
                             precision=HIGHEST, preferred_element_type=F32)
    scores = jax.nn.sigmoid(logits)
    sel = scores + bias_ref[...]

    gidx = lax.broadcasted_iota(I32, (GROUP_SIZE, n), 0)
    groups = [sel[g * GROUP_SIZE:(g + 1) * GROUP_SIZE, :] for g in range(N_GROUPS)]
    gscore = []
    for grp in groups:
        m1, first = _first_index_of_max(grp, gidx, GROUP_SIZE)
        m2 = jnp.max(jnp.where(gidx == first, -jnp.inf, grp), axis=0, keepdims=True)
        gscore.append(m1 + m2)

    masked = []
    for gi in range(N_GROUPS):
        beaten = jnp.zeros((1, n), F32)
        for gj in range(N_GROUPS):
            if gj != gi:
                wins = (gscore[gj] >= gscore[gi]) if gj < gi else (gscore[gj] > gscore[gi])
                beaten = beaten + wins.astype(F32)
        masked.append(jnp.where(beaten < TOPK_GROUPS, groups[gi], -jnp.inf))
    val = jnp.concatenate(masked, axis=0)

    eidx = lax.broadcasted_iota(I32, (N_EXPERTS, n), 0)
    chosen_f = jnp.zeros((N_EXPERTS, n), F32)
    picks, weights = [], []
    for _ in range(TOP_K):
        _, first = _first_index_of_max(val, eidx, N_EXPERTS)
        hit = eidx == first
        picks.append(first)
        weights.append(jnp.sum(jnp.where(hit, scores, 0.0), axis=0, keepdims=True))
        chosen_f = jnp.where(hit, 1.0, chosen_f)
        val = jnp.where(hit, -jnp.inf, val)
    wsum = weights[0]
    for wk in weights[1:]:
        wsum = wsum + wk
    scale = ROUTED_SCALE / wsum

    r = lax.broadcasted_iota(I32, (n, n), 0)
    c = lax.broadcasted_iota(I32, (n, n), 1)
    earlier = (r < c).astype(BF16)
    before = _dot(chosen_f.astype(BF16), earlier) + base_ref[...]
    base_ref[...] = base_ref[...] + jnp.sum(chosen_f, axis=1, keepdims=True)
    cnt_ref[...] = jnp.broadcast_to(base_ref[...], cnt_ref.shape).astype(I32)

    for kk in range(TOP_K):
        eidx_ref[kk:kk + 1, :] = picks[kk]
        w_ref[kk:kk + 1, :] = weights[kk] * scale
        rank = jnp.sum(jnp.where(eidx == picks[kk], before, 0.0), axis=0, keepdims=True)
        rank_ref[kk:kk + 1, :] = rank.astype(I32)


def _route(h1, wr_t, bias_col):
    t = h1.shape[0]
    col = lambda dtype: jax.ShapeDtypeStruct((TOP_K, t), dtype)
    kspec = pl.BlockSpec((TOP_K, ROUTE_ROWS), lambda i: (0, i))
    return pl.pallas_call(
        _route_body,
        grid=(t // ROUTE_ROWS,),
        in_specs=[pl.BlockSpec((ROUTE_ROWS, D_MODEL), lambda i: (i, 0)),
                  _const_spec((N_EXPERTS, D_MODEL)), _const_spec((N_EXPERTS, 1))],
        out_specs=[kspec, kspec, kspec, _const_spec((N_EXPERTS, LANE))],
        out_shape=[col(I32), col(F32), col(I32), jax.ShapeDtypeStruct((N_EXPERTS, LANE), I32)],
        scratch_shapes=[pltpu.VMEM((N_EXPERTS, 1), F32)],
        compiler_params=pltpu.CompilerParams(dimension_semantics=("arbitrary",),
                                             vmem_limit_bytes=VMEM_LIMIT),
        name="route",
    )(h1, wr_t, bias_col)


def _dest_body(eidx_ref, rank_ref, pstart_ref, dest_ref):
    n = eidx_ref.shape[1]
    eiota = lax.broadcasted_iota(I32, (N_EXPERTS, n), 0)
    pstart = pstart_ref[...]
    for kk in range(TOP_K):
        start = jnp.sum(jnp.where(eiota == eidx_ref[kk:kk + 1, :], pstart, 0.0), axis=0, keepdims=True)
        dest = start.astype(I32) + rank_ref[kk:kk + 1, :]
        for c in range(n // SC_CHUNK):
            dest_ref[c, kk:kk + 1, :] = dest[:, c * SC_CHUNK:(c + 1) * SC_CHUNK]


def _dest(eidx, rank, pstart_col):
    t = eidx.shape[1]
    kspec = pl.BlockSpec((TOP_K, ROUTE_ROWS), lambda i: (0, i))
    chunks = ROUTE_ROWS // SC_CHUNK
    return pl.pallas_call(
        _dest_body,
        grid=(t // ROUTE_ROWS,),
        in_specs=[kspec, kspec, _const_spec((N_EXPERTS, 1))],
        out_specs=pl.BlockSpec((chunks, TOP_K, SC_CHUNK), lambda i: (i, 0, 0)),
        out_shape=jax.ShapeDtypeStruct((t // SC_CHUNK, TOP_K, SC_CHUNK), I32),
        compiler_params=pltpu.CompilerParams(dimension_semantics=("arbitrary",)),
        name="dest",
    )(eidx, rank, pstart_col)


def _sc_mesh():
    return plsc.VectorSubcoreMesh(core_axis_name="c", subcore_axis_name="s",
                                  num_cores=SC_CORES, num_subcores=SC_SUBCORES)


def _sc_worker_chunks(t):
    assert t % (SC_CORES * SC_SUBCORES * SC_CHUNK) == 0
    per_worker = t // (SC_CORES * SC_SUBCORES * SC_CHUNK)
    worker = lax.axis_index("s") * SC_CORES + lax.axis_index("c")
    return worker * per_worker, per_worker


def _dispatch(dest, h1p, n_rows):
    t, width = h1p.shape

    @functools.partial(
        pl.kernel, mesh=_sc_mesh(), out_type=jax.ShapeDtypeStruct((n_rows, width), U32),
        scratch_types=[pltpu.VMEM((SC_CHUNK, width), U32), pltpu.VMEM((TOP_K, SC_CHUNK), I32),
                       pltpu.SemaphoreType.DMA],
        name="dispatch")
    def body(h_hbm, dest_hbm, xs_hbm, rows_v, idx_v, sem):
        first, count = _sc_worker_chunks(t)

        @pl.loop(0, count)
        def _(i):
            chunk = first + i
            pltpu.sync_copy(h_hbm.at[pl.ds(pl.multiple_of(chunk * SC_CHUNK, SC_CHUNK), SC_CHUNK)], rows_v)
            pltpu.sync_copy(dest_hbm.at[chunk], idx_v)
            copies = [pltpu.async_copy(rows_v, xs_hbm.at[idx_v.at[kk]], sem) for kk in range(TOP_K)]
            for c in copies:
                c.wait()

    return body(h1p, dest)


def _gather(dest, ys):
    t = dest.shape[0] * SC_CHUNK
    width = ys.shape[1]

    @functools.partial(
        pl.kernel, mesh=_sc_mesh(), out_type=jax.ShapeDtypeStruct((TOP_K, t, width), U32),
        scratch_types=[pltpu.VMEM((SC_CHUNK, width), U32), pltpu.VMEM((TOP_K, SC_CHUNK), I32),
                       pltpu.SemaphoreType.DMA],
        name="gather")
    def body(ys_hbm, dest_hbm, yg_hbm, rows_v, idx_v, sem):
        first, count = _sc_worker_chunks(t)

        @pl.loop(0, count)
        def _(i):
            chunk = first + i
            pltpu.sync_copy(dest_hbm.at[chunk], idx_v)
            for kk in range(TOP_K):
                pltpu.async_copy(ys_hbm.at[idx_v.at[kk]], rows_v, sem).wait()
                pltpu.sync_copy(
                    rows_v, yg_hbm.at[kk, pl.ds(pl.multiple_of(chunk * SC_CHUNK, SC_CHUNK), SC_CHUNK)])

    return body(ys, dest)


def _experts_body(bs_ref, xs_in_ref, wg_ref, wu_ref, wd_ref, xs_ref,
                  xbuf, ybuf, wgb, wub, wdb, sem_in, sem_out):
    del xs_in_ref
    e = pl.program_id(0)
    b0 = bs_ref[e]
    b1 = bs_ref[e + 1]
    n_total = bs_ref[N_EXPERTS]
    half = D_MODEL // 2

    def fetch(b, slot):
        return pltpu.make_async_copy(xs_ref.at[pl.ds(b * MOE_BLOCK, MOE_BLOCK)], xbuf.at[slot],
                                     sem_in.at[slot])

    def store(b, slot):
        return pltpu.make_async_copy(ybuf.at[slot], xs_ref.at[pl.ds(b * MOE_BLOCK, MOE_BLOCK)],
                                     sem_out.at[slot])

    @pl.when(e == 0)
    def _():
        for b in range(IN_SLOTS - 1):
            @pl.when(b < n_total)
            def _():
                fetch(b, b).start(priority=1)

    @pl.when(b1 > b0)
    def _():
        wgb[...] = wg_ref[0].astype(BF16)
        wub[...] = wu_ref[0].astype(BF16)
        wdb[...] = wd_ref[0].astype(BF16)

    def step(b, carry):
        slot = b % 2
        ahead = b + IN_SLOTS - 1

        @pl.when(ahead < n_total)
        def _():
            fetch(ahead, ahead % IN_SLOTS).start(priority=1)

        fetch(b, b % IN_SLOTS).wait()
        xa, xb = _unpack_halves(xbuf[b % IN_SLOTS])
        xa = xa.astype(BF16)
        xb = xb.astype(BF16)
        gate = _dot(xa, wgb[:half, :]) + _dot(xb, wgb[half:, :])
        up = _dot(xa, wub[:half, :]) + _dot(xb, wub[half:, :])
        hidden = (gate * jax.nn.sigmoid(gate) * up).astype(BF16)
        y = _dot(hidden, wdb[...])

        @pl.when(b >= 2)
        def _():
            store(b - 2, slot).wait()

        ybuf[slot] = _pack_halves(y[:, :half], y[:, half:])
        store(b, slot).start(priority=1)
        return carry

    lax.fori_loop(b0, b1, step, 0)

    @pl.when(e == N_EXPERTS - 1)
    def _():
        @pl.when(n_total >= 2)
        def _():
            store(n_total - 2, n_total % 2).wait()

        @pl.when(n_total >= 1)
        def _():
            store(n_total - 1, (n_total - 1) % 2).wait()


def _experts(block_start, xs, w_gate, w_up, w_down):
    n_rows, width = xs.shape
    wspec = lambda shape: pl.BlockSpec((1,) + shape, lambda e, bs: (e, 0, 0))
    grid_spec = pltpu.PrefetchScalarGridSpec(
        num_scalar_prefetch=1,
        grid=(N_EXPERTS,),
        in_specs=[pl.BlockSpec(memory_space=pl.ANY),
                  wspec((D_MODEL, D_EXPERT)), wspec((D_MODEL, D_EXPERT)), wspec((D_EXPERT, D_MODEL))],
        out_specs=pl.BlockSpec(memory_space=pl.ANY),
        scratch_shapes=[pltpu.VMEM((IN_SLOTS, MOE_BLOCK, width), U32), pltpu.VMEM((2, MOE_BLOCK, width), U32),
                        pltpu.VMEM((D_MODEL, D_EXPERT), BF16), pltpu.VMEM((D_MODEL, D_EXPERT), BF16),
                        pltpu.VMEM((D_EXPERT, D_MODEL), BF16),
                        pltpu.SemaphoreType.DMA((IN_SLOTS,)), pltpu.SemaphoreType.DMA((2,))],
    )
    return pl.pallas_call(
        _experts_body,
        grid_spec=grid_spec,
        out_shape=jax.ShapeDtypeStruct((n_rows, width), U32),
        input_output_aliases={1: 0},
        compiler_params=pltpu.CompilerParams(dimension_semantics=("arbitrary",),
                                             vmem_limit_bytes=VMEM_LIMIT),
        name="experts",
    )(block_start, xs, w_gate, w_up, w_down)


def _combine_body(h1_ref, w_ref, yg_ref, wsg_ref, wsu_ref, wsd_ref, g2_ref, b2_ref, out_ref):
    h1 = h1_ref[...]
    hb = h1.astype(BF16)
    gate = _dot(hb, wsg_ref[...])
    up = _dot(hb, wsu_ref[...])
    shared = _dot((gate * jax.nn.sigmoid(gate) * up).astype(BF16), wsd_ref[...])

    half = D_MODEL // 2
    lo = jnp.zeros((ROWS, half), F32)
    hi = jnp.zeros((ROWS, half), F32)
    for kk in range(TOP_K):
        a, b = _unpack_halves(yg_ref[kk])
        wk = w_ref[:, kk:kk + 1]
        lo = lo + wk * a
        hi = hi + wk * b
    z = DN_ALPHA * h1 + shared + jnp.concatenate([lo, hi], axis=1)
    out_ref[...] = _layer_norm(z, g2_ref[...], b2_ref[...])


def _combine(h1, w_rows, yg, wsg, wsu, wsd, ln2_g, ln2_b):
    t = h1.shape[0]
    width = yg.shape[2]
    d_shared = wsg.shape[1]
    return pl.pallas_call(
        _combine_body,
        grid=(t // ROWS,),
        in_specs=[pl.BlockSpec((ROWS, D_MODEL), lambda i: (i, 0)),
                  pl.BlockSpec((ROWS, TOP_K), lambda i: (i, 0)),
                  pl.BlockSpec((TOP_K, ROWS, width), lambda i: (0, i, 0)),
                  _const_spec((D_MODEL, d_shared)), _const_spec((D_MODEL, d_shared)),
                  _const_spec((d_shared, D_MODEL)),
                  _const_spec((1, D_MODEL)), _const_spec((1, D_MODEL))],
        out_specs=pl.BlockSpec((ROWS, D_MODEL), lambda i: (i, 0)),
        out_shape=jax.ShapeDtypeStruct((t, D_MODEL), F32),
        compiler_params=pltpu.CompilerParams(dimension_semantics=("arbitrary",),
                                             vmem_limit_bytes=VMEM_LIMIT),
        name="combine",
    )(h1, w_rows, yg, wsg, wsu, wsd, ln2_g, ln2_b)


def _pack_in_proj(w_in, b_in, gate_w_f, gate_w_b, gate_b_f, gate_b_b):
    o_pool, o_q, o_k, o_v, o_g = 0, 512, 1024, 1536, 2560
    o_lrf, o_lrb, o_gp, o_gg = 3584, 3600, 3616, 4640
    order = [(o_pool, POOL_WIDTH), (o_q, GLA_DK), (o_k, GLA_DK), (o_v, GLA_DV), (o_g, GLA_DV),
             (o_gp, D_MODEL), (o_gg, D_MODEL), (o_lrf, GATE_RANK), (o_lrb, GATE_RANK)]
    pad = LANE - 2 * GATE_RANK
    w = jnp.concatenate([w_in[:, o:o + n] for o, n in order] + [jnp.zeros((D_MODEL, pad), F32)], axis=1)
    b = jnp.concatenate([b_in[o:o + n] for o, n in order] + [jnp.zeros((pad,), F32)])
    w_gate = jnp.zeros((LANE, 2 * GLA_DK), F32)
    w_gate = w_gate.at[:GATE_RANK, :GLA_DK].set(gate_w_f)
    w_gate = w_gate.at[GATE_RANK:2 * GATE_RANK, GLA_DK:].set(gate_w_b)
    b_gate = jnp.concatenate([gate_b_f, gate_b_b])[None, :]
    return w.astype(BF16), b[None, :], w_gate, b_gate


def kernel(x, meta, ln0_g, ln0_b, w_in, b_in, pool_w, pool_scale, gate_w_f, gate_b_f, gate_w_b, gate_b_b,
           gla_norm_g, w_branch_pool, w_branch_gla, w_out, ln1_g, ln1_b, w_router, router_bias,
           w_exp_gate, w_exp_up, w_exp_down, w_sh_gate, w_sh_up, w_sh_down, ln2_g, ln2_b):
    batch, seq, d = x.shape
    assert d == D_MODEL and seq % ROWS == 0 and w_in.shape[0] == 1
    assert (batch * seq) % ROUTE_ROWS == 0
    t = batch * seq
    row = lambda a: a.reshape(1, -1).astype(F32)

    lead = jnp.concatenate([jnp.zeros((FRONT - N_META, d), x.dtype), meta.astype(x.dtype)], axis=0)
    tiles_per_batch = (seq + FRONT) // ROWS
    x2 = x.reshape(t, d)

    w_packed, b_packed, w_gate, b_gate = _pack_in_proj(
        w_in[0], b_in[0], gate_w_f[0], gate_w_b[0], gate_b_f[0], gate_b_b[0])
    u, q, k, v, sg, gp, gg, bf, cb = _in_proj(
        lead, x2, row(ln0_g), row(ln0_b), w_packed, b_packed, w_gate, b_gate, tiles_per_batch)

    o_f, o_b = _gla(q, k, v, bf, cb, batch)

    h1, h1p = _mix(
        x.reshape(t, d), u, gp, gg, sg, o_f, o_b, row(ln0_g), row(ln0_b),
        pool_w[0].astype(BF16), row(pool_scale[0]), row(jnp.tile(gla_norm_g[0], GLA_HEADS)),
        w_branch_pool[0].astype(BF16), w_branch_gla[0].astype(BF16), w_out[0].astype(BF16),
        row(ln1_g[0]), row(ln1_b[0]), seq)

    eidx, ew, rank, counts = _route(h1, w_router[0].T, router_bias[0].reshape(N_EXPERTS, 1))

    counts = counts[:, 0]
    padded = ((counts + MOE_BLOCK - 1) // MOE_BLOCK) * MOE_BLOCK
    pend = jnp.cumsum(padded)
    pstart = pend - padded
    dest = _dest(eidx, rank, pstart.astype(F32).reshape(N_EXPERTS, 1))
    n_blocks = -(-(t * TOP_K) // MOE_BLOCK) + N_EXPERTS
    block_start = (jnp.concatenate([pstart, pend[-1:]]) // MOE_BLOCK).astype(I32)

    xs = _dispatch(dest, h1p, n_blocks * MOE_BLOCK)
    ys = _experts(block_start, xs, w_exp_gate[0], w_exp_up[0], w_exp_down[0])
    yg = _gather(dest, ys)
    out = _combine(h1, ew.T, yg, w_sh_gate[0].astype(BF16), w_sh_up[0].astype(BF16),
                   w_sh_down[0].astype(BF16), row(ln2_g[0]), row(ln2_b[0]))
    return out.reshape(batch, seq, d)
```

```python
import functools

import jax
import jax.numpy as jnp
from jax import lax
from jax.experimental import pallas as pl
from jax.experimental.pallas import tpu as pltpu
from jax.experimental.pallas import tpu_sc as plsc

F32 = jnp.float32
BF16 = jnp.bfloat16
I32 = jnp.int32
U32 = jnp.uint32
HIGHEST = lax.Precision.HIGHEST

D_MODEL = 1024
N_META = 16
POOL_WINDOWS = (2, 4, 8, 16)
POOL_GROUP = 128
POOL_WIDTH = POOL_GROUP * len(POOL_WINDOWS)
GLA_HEADS = 4
GLA_DK = 512
GLA_DV = 1024
HEAD_K = GLA_DK // GLA_HEADS
HEAD_V = GLA_DV // GLA_HEADS
GATE_RANK = 16
GATE_NORMALIZER = 16.0
RMS_EPS = 1e-5
N_EXPERTS = 256
TOP_K = 8
N_GROUPS = 8
GROUP_SIZE = N_EXPERTS // N_GROUPS
TOPK_GROUPS = 4
D_EXPERT = 256
ROUTED_SCALE = 2.5
DN_ALPHA = 2.0 ** 0.25
LN_EPS = 1e-5

LANE = 128
ROWS = 256
FRONT = ROWS
GLA_CHUNK = 128
HALO = 16
ROUTE_ROWS = 512
MOE_BLOCK = 256
LOOKAHEAD = 3
IN_SLOTS = LOOKAHEAD + 2
OUT_SLOTS = 2
SC_CORES = 2
SC_SUBCORES = 16
SC_CHUNK = 128
COMBINE_CHUNKS = 4
VMEM_LIMIT = 56 * 1024 * 1024

C_POOL, C_Q, C_K, C_V, C_G, C_GP, C_GG, C_LR = 0, 512, 1024, 1536, 2560, 3584, 4608, 5632
W_PACKED = C_LR + LANE


def _layer_norm(x, g, b):
    mu = jnp.mean(x, axis=-1, keepdims=True)
    xc = x - mu
    var = jnp.mean(xc * xc, axis=-1, keepdims=True)
    return xc * lax.rsqrt(var + LN_EPS) * g + b


def _dot(a, b):
    return jnp.dot(a, b, preferred_element_type=F32)


def _dot_nt(a, b):
    return lax.dot_general(a, b, (((1,), (1,)), ((), ())), preferred_element_type=F32)


def _split_dot(m2, g):
    g_hi = g.astype(BF16)
    g_lo = (g - g_hi.astype(F32)).astype(BF16)
    return _dot(m2, jnp.concatenate([g_hi, g_lo], axis=0))


def _pack_halves(a, b):
    pa = lax.bitcast_convert_type(a.astype(BF16).astype(F32), U32)
    pb = lax.bitcast_convert_type(b.astype(BF16).astype(F32), U32)
    return pa | (pb >> 16)


def _unpack_halves(p):
    a = lax.bitcast_convert_type(p & jnp.uint32(0xFFFF0000), F32)
    b = lax.bitcast_convert_type(p << 16, F32)
    return a, b


def _const_spec(shape):
    return pl.BlockSpec(shape, lambda *_: (0,) * len(shape))


def _inproj_body(lead_ref, x_ref, g0_ref, b0_ref, w_ref, bias_ref, wgate_ref, bgate_ref,
                 u_ref, q_ref, k_ref, v_ref, sg_ref, gp_ref, gg_ref, bf_ref, cb_ref,
                 *, tiles_per_batch):
    i = pl.program_id(0)
    x_in = jnp.where(i % tiles_per_batch == 0, lead_ref[...], x_ref[...])
    h = _layer_norm(x_in, g0_ref[...], b0_ref[...])
    hb = h.astype(BF16)

    def proj(c0, n):
        return _dot(hb, w_ref[:, c0:c0 + n]) + bias_ref[:, c0:c0 + n]

    row = lax.broadcasted_iota(I32, (ROWS, 1), 0)
    valid = row >= jnp.where(i % tiles_per_batch == 0, FRONT - N_META, 0)

    u_ref[...] = proj(C_POOL, POOL_WIDTH).astype(BF16)
    q_ref[...] = (proj(C_Q, GLA_DK) * (HEAD_K ** -0.5)).astype(BF16)
    k_ref[...] = jnp.where(valid, proj(C_K, GLA_DK), 0.0).astype(BF16)
    v_ref[...] = jnp.where(valid, proj(C_V, GLA_DV), 0.0).astype(BF16)
    g = proj(C_G, GLA_DV)
    sg_ref[...] = (g * jax.nn.sigmoid(g)).astype(BF16)
    gp_ref[...] = jax.nn.sigmoid(proj(C_GP, D_MODEL)).astype(BF16)
    gg_ref[...] = jax.nn.sigmoid(proj(C_GG, D_MODEL)).astype(BF16)

    lr3 = proj(C_LR, LANE)
    lr_hi = lr3.astype(BF16)
    lr_lo = (lr3 - lr_hi.astype(F32)).astype(BF16)
    lane = lax.broadcasted_iota(I32, (1, LANE), 1)
    second = jnp.logical_and(lane >= 2 * GATE_RANK, lane < 4 * GATE_RANK)
    xg = _dot(jnp.where(second, lr_lo, lr_hi), wgate_ref[...]) + bgate_ref[...]
    gk = (jnp.minimum(xg, 0.0) - jnp.log(1.0 + jnp.exp(-jnp.abs(xg)))) * (1.0 / GATE_NORMALIZER)

    r = lax.broadcasted_iota(I32, (GLA_CHUNK, 2 * GLA_CHUNK), 0)
    c = lax.broadcasted_iota(I32, (GLA_CHUNK, 2 * GLA_CHUNK), 1) % GLA_CHUNK
    tril = (r >= c).astype(BF16)
    triu = (r <= c).astype(BF16)
    for ci in range(ROWS // GLA_CHUNK):
        sl = slice(ci * GLA_CHUNK, (ci + 1) * GLA_CHUNK)
        bf_ref[sl, :] = _split_dot(tril, gk[sl, :GLA_DK])
        cb_ref[sl, :] = _split_dot(triu, gk[sl, GLA_DK:])


def _in_proj(lead, x2, ln0_g, ln0_b, w_packed, b_packed, w_gate, b_gate, tiles_per_batch):
    x_tiles = tiles_per_batch - FRONT // ROWS
    n = x2.shape[0] // (x_tiles * ROWS) * tiles_per_batch * ROWS
    row_spec = lambda width: pl.BlockSpec((ROWS, width), lambda i: (i, 0))
    x_spec = pl.BlockSpec(
        (ROWS, D_MODEL),
        lambda i: ((i // tiles_per_batch) * x_tiles + jnp.maximum(i % tiles_per_batch - 1, 0), 0))
    out_widths = (POOL_WIDTH, GLA_DK, GLA_DK, GLA_DV, GLA_DV, D_MODEL, D_MODEL)
    out_shape = [jax.ShapeDtypeStruct((n, w), BF16) for w in out_widths]
    out_shape += [jax.ShapeDtypeStruct((n, GLA_DK), F32)] * 2
    out_specs = [row_spec(w) for w in out_widths] + [row_spec(GLA_DK)] * 2
    return pl.pallas_call(
        functools.partial(_inproj_body, tiles_per_batch=tiles_per_batch),
        grid=(n // ROWS,),
        in_specs=[_const_spec((FRONT, D_MODEL)), x_spec,
                  _const_spec((1, D_MODEL)), _const_spec((1, D_MODEL)),
                  _const_spec((D_MODEL, W_PACKED)), _const_spec((1, W_PACKED)),
                  _const_spec((LANE, 2 * GLA_DK)), _const_spec((1, 2 * GLA_DK))],
        out_specs=out_specs,
        out_shape=out_shape,
        compiler_params=pltpu.CompilerParams(dimension_semantics=("arbitrary",),
                                             vmem_limit_bytes=VMEM_LIMIT),
        name="in_proj",
    )(lead, x2, ln0_g, ln0_b, w_packed, b_packed, w_gate, b_gate)


def _gla_direction(q_ref, k_ref, v_ref, b_ref, o_ref, s_ref, reverse):
    c = GLA_CHUNK
    r = lax.broadcasted_iota(I32, (c, c), 0)
    s = lax.broadcasted_iota(I32, (c, c), 1)
    keep = (r <= s) if reverse else (r >= s)
    eye = lax.broadcasted_iota(I32, (HEAD_K, HEAD_K), 0) == lax.broadcasted_iota(I32, (HEAD_K, HEAD_K), 1)
    edge = 0 if reverse else c - 1
    for h in range(GLA_HEADS):
        ks = slice(h * HEAD_K, (h + 1) * HEAD_K)
        vs = slice(h * HEAD_V, (h + 1) * HEAD_V)
        b = b_ref[:, ks]
        qh = q_ref[:, ks].astype(F32)
        kh = k_ref[:, ks].astype(F32)
        vh = v_ref[:, vs]
        b_tot = b[edge:edge + 1, :]
        b_mid = b[c // 2:c // 2 + 1, :]
        q_state = (qh * jnp.exp(b)).astype(BF16)
        q_in = (qh * jnp.exp(b - b_mid)).astype(BF16)
        k_in = (kh * jnp.exp(b_mid - b)).astype(BF16)
        k_state = (kh * jnp.exp(b_tot - b)).T.astype(BF16)
        att = jnp.where(keep, _dot_nt(q_in, k_in), 0.0).astype(BF16)
        state = s_ref[h]
        o = _dot(q_state, state.astype(BF16)) + _dot(att, vh)
        o_ref[:, vs] = o.astype(o_ref.dtype)
        decay_col = jnp.sum(jnp.where(eye, jnp.exp(b_tot), 0.0), axis=1, keepdims=True)
        s_ref[h] = decay_col * state + _dot(k_state, vh)


def _gla_body(qf_ref, kf_ref, vf_ref, bf_ref, qb_ref, kb_ref, vb_ref, cb_ref,
              of_ref, ob_ref, sf_ref, sb_ref):
    @pl.when(pl.program_id(1) == 0)
    def _():
        sf_ref[...] = jnp.zeros_like(sf_ref)
        sb_ref[...] = jnp.zeros_like(sb_ref)

    _gla_direction(qf_ref, kf_ref, vf_ref, bf_ref, of_ref, sf_ref, reverse=False)
    _gla_direction(qb_ref, kb_ref, vb_ref, cb_ref, ob_ref, sb_ref, reverse=True)


def _gla(q, k, v, bf, cb, batch):
    n = q.shape[0]
    nch = n // batch // GLA_CHUNK
    fwd = lambda width: pl.BlockSpec((GLA_CHUNK, width), lambda b, i: (b * nch + i, 0))
    bwd = lambda width: pl.BlockSpec((GLA_CHUNK, width), lambda b, i: (b * nch + nch - 1 - i, 0))
    state = pltpu.VMEM((GLA_HEADS, HEAD_K, HEAD_V), F32)
    return pl.pallas_call(
        _gla_body,
        grid=(batch, nch),
        in_specs=[fwd(GLA_DK), fwd(GLA_DK), fwd(GLA_DV), fwd(GLA_DK),
                  bwd(GLA_DK), bwd(GLA_DK), bwd(GLA_DV), bwd(GLA_DK)],
        out_specs=[fwd(GLA_DV), bwd(GLA_DV)],
        out_shape=[jax.ShapeDtypeStruct((n, GLA_DV), BF16)] * 2,
        scratch_shapes=[state, state],
        compiler_params=pltpu.CompilerParams(dimension_semantics=("arbitrary", "arbitrary"),
                                             vmem_limit_bytes=VMEM_LIMIT),
        name="gla",
    )(q, k, v, bf, q, k, v, cb)


def _mix_body(x_ref, up_ref, u_ref, un_ref, gp_ref, gg_ref, sg_ref, of_ref, ob_ref,
              g0_ref, b0_ref, pw_ref, ps_ref, gn_ref, wbp_ref, wbg_ref, wo_ref, g1_ref, b1_ref,
              h1_ref, h1p_ref, *, tiles_per_batch, seq):
    j = pl.program_id(0) % tiles_per_batch
    u_main = u_ref[...]
    u_ext = jnp.concatenate([up_ref[...], u_main, un_ref[...]], axis=0)
    r = lax.broadcasted_iota(I32, (ROWS, ROWS + 2 * HALO), 0)
    e = lax.broadcasted_iota(I32, (ROWS, ROWS + 2 * HALO), 1) - HALO
    in_seq = (j * ROWS + e) < seq
    pos = j * ROWS + lax.broadcasted_iota(I32, (ROWS, 1), 0)
    y_pool = []
    for gi, w in enumerate(POOL_WINDOWS):
        cs = slice(gi * POOL_GROUP, (gi + 1) * POOL_GROUP)
        band = jnp.logical_and(jnp.logical_and(e >= r - w // 2, e < r + w - w // 2), in_seq)
        count = w - jnp.maximum(pos + (w - w // 2) - seq, 0)
        mean = _dot(band.astype(BF16), u_ext[:, cs]) / count.astype(F32)
        d = mean - u_main[:, cs].astype(F32)
        y_pool.append(_dot(d.astype(BF16), pw_ref[gi]))
    y_pool = (jnp.concatenate(y_pool, axis=1) * ps_ref[...]).astype(BF16)

    o = of_ref[...].astype(F32) + ob_ref[...].astype(F32)
    y_gla = []
    for h in range(GLA_HEADS):
        oh = o[:, h * HEAD_V:(h + 1) * HEAD_V]
        y_gla.append(oh * lax.rsqrt(jnp.mean(oh * oh, axis=-1, keepdims=True) + RMS_EPS))
    y_gla = (jnp.concatenate(y_gla, axis=1) * gn_ref[...] * sg_ref[...].astype(F32)).astype(BF16)

    merged = (gp_ref[...].astype(F32) * _dot(y_pool, wbp_ref[...])
              + gg_ref[...].astype(F32) * _dot(y_gla, wbg_ref[...]))
    y = _dot(merged.astype(BF16), wo_ref[...])
    h = _layer_norm(x_ref[...], g0_ref[...], b0_ref[...])
    h1 = _layer_norm(DN_ALPHA * h + y, g1_ref[...], b1_ref[...])
    h1_ref[...] = h1
    h1p_ref[...] = _pack_halves(h1[:, :D_MODEL // 2], h1[:, D_MODEL // 2:])


def _mix(x2, u, gp, gg, sg, o_f, o_b, ln0_g, ln0_b, pool_w, pool_scale, norm_g, wbp, wbg, wo,
         ln1_g, ln1_b, seq):
    t = x2.shape[0]
    tiles_per_batch = seq // ROWS
    padded_tiles = tiles_per_batch + FRONT // ROWS
    halo_per_tile = ROWS // HALO
    last_halo = u.shape[0] // HALO - 1

    def pidx(i):
        return (i // tiles_per_batch) * padded_tiles + FRONT // ROWS + i % tiles_per_batch

    padded = lambda width: pl.BlockSpec((ROWS, width), lambda i: (pidx(i), 0))
    plain = lambda width: pl.BlockSpec((ROWS, width), lambda i: (i, 0))
    prev_halo = pl.BlockSpec((HALO, POOL_WIDTH), lambda i: (pidx(i) * halo_per_tile - 1, 0))
    next_halo = pl.BlockSpec(
        (HALO, POOL_WIDTH), lambda i: (jnp.minimum((pidx(i) + 1) * halo_per_tile, last_halo), 0))
    return pl.pallas_call(
        functools.partial(_mix_body, tiles_per_batch=tiles_per_batch, seq=seq),
        grid=(t // ROWS,),
        in_specs=[plain(D_MODEL), prev_halo, padded(POOL_WIDTH), next_halo,
                  padded(D_MODEL), padded(D_MODEL), padded(GLA_DV), padded(GLA_DV), padded(GLA_DV),
                  _const_spec((1, D_MODEL)), _const_spec((1, D_MODEL)),
                  _const_spec((len(POOL_WINDOWS), POOL_GROUP, POOL_GROUP)), _const_spec((1, POOL_WIDTH)),
                  _const_spec((1, GLA_DV)), _const_spec((POOL_WIDTH, D_MODEL)),
                  _const_spec((GLA_DV, D_MODEL)), _const_spec((D_MODEL, D_MODEL)),
                  _const_spec((1, D_MODEL)), _const_spec((1, D_MODEL))],
        out_specs=[plain(D_MODEL), plain(D_MODEL // 2)],
        out_shape=[jax.ShapeDtypeStruct((t, D_MODEL), F32), jax.ShapeDtypeStruct((t, D_MODEL // 2), U32)],
        compiler_params=pltpu.CompilerParams(dimension_semantics=("arbitrary",),
                                             vmem_limit_bytes=VMEM_LIMIT),
        name="mix",
    )(x2, u, u, u, gp, gg, sg, o_f, o_b, ln0_g, ln0_b, pool_w, pool_scale, norm_g, wbp, wbg, wo,
      ln1_g, ln1_b)


def _first_index_of_max(val, idx, size):
    m = jnp.max(val, axis=0, keepdims=True)
    first = jnp.min(jnp.where(val == m, idx, size), axis=0, keepdims=True)
    return m, first


def _route_body(h_ref, wr_ref, bias_ref, eidx_ref, w_ref, rank_ref, cnt_ref, base_ref):
    n = ROUTE_ROWS

    @pl.when(pl.program_id(0) == 0)
    def _():
        base_ref[...] = jnp.zeros_like(base_ref)

    logits = lax.dot_general(wr_ref[...], h_ref[...], (((1,), (1,)), ((), ())),
                             precision=HIGHEST, preferred_element_type=F32)
    scores = jax.nn.sigmoid(logits)
    sel = scores + bias_ref[...]

    gidx = lax.broadcasted_iota(I32, (GROUP_SIZE, n), 0)
    groups = [sel[g * GROUP_SIZE:(g + 1) * GROUP_SIZE, :] for g in range(N_GROUPS)]
    gscore = []
    for grp in groups:
        m1, first = _first_index_of_max(grp, gidx, GROUP_SIZE)
        m2 = jnp.max(jnp.where(gidx == first, -jnp.inf, grp), axis=0, keepdims=True)
        gscore.append(m1 + m2)

    masked = []
    for gi in range(N_GROUPS):
        beaten = jnp.zeros((1, n), F32)
        for gj in range(N_GROUPS):
            if gj != gi:
                wins = (gscore[gj] >= gscore[gi]) if gj < gi else (gscore[gj] > gscore[gi])
                beaten = beaten + wins.astype(F32)
        masked.append(jnp.where(beaten < TOPK_GROUPS, groups[gi], -jnp.inf))
    val = jnp.concatenate(masked, axis=0)

    eidx = lax.broadcasted_iota(I32, (N_EXPERTS, n), 0)
    chosen_f = jnp.zeros((N_EXPERTS, n), F32)
    picks, weights = [], []
    for _ in range(TOP_K):
        _, first = _first_index_of_max(val, eidx, N_EXPERTS)
        hit = eidx == first
        picks.append(first)
        weights.append(jnp.sum(jnp.where(hit, scores, 0.0), axis=0, keepdims=True))
        chosen_f = jnp.where(hit, 1.0, chosen_f)
        val = jnp.where(hit, -jnp.inf, val)
    wsum = weights[0]
    for wk in weights[1:]:
        wsum = wsum + wk
    scale = ROUTED_SCALE / wsum

    r = lax.broadcasted_iota(I32, (n, n), 0)
    c = lax.broadcasted_iota(I32, (n, n), 1)
    earlier = (r < c).astype(BF16)
    before = _dot(chosen_f.astype(BF16), earlier) + base_ref[...]
    base_ref[...] = base_ref[...] + jnp.sum(chosen_f, axis=1, keepdims=True)
    cnt_ref[...] = jnp.broadcast_to(base_ref[...], cnt_ref.shape).astype(I32)

    for kk in range(TOP_K):
        eidx_ref[kk:kk + 1, :] = picks[kk]
        w_ref[kk:kk + 1, :] = weights[kk] * scale
        rank = jnp.sum(jnp.where(eidx == picks[kk], before, 0.0), axis=0, keepdims=True)
        rank_ref[kk:kk + 1, :] = rank.astype(I32)


def _route(h1, wr_t, bias_col):
    t = h1.shape[0]
    col = lambda dtype: jax.ShapeDtypeStruct((TOP_K, t), dtype)
    kspec = pl.BlockSpec((TOP_K, ROUTE_ROWS), lambda i: (0, i))
    return pl.pallas_call(
        _route_body,
        grid=(t // ROUTE_ROWS,),
        in_specs=[pl.BlockSpec((ROUTE_ROWS, D_MODEL), lambda i: (i, 0)),
                  _const_spec((N_EXPERTS, D_MODEL)), _const_spec((N_EXPERTS, 1))],
        out_specs=[kspec, kspec, kspec, _const_spec((N_EXPERTS, LANE))],
        out_shape=[col(I32), col(F32), col(I32), jax.ShapeDtypeStruct((N_EXPERTS, LANE), I32)],
        scratch_shapes=[pltpu.VMEM((N_EXPERTS, 1), F32)],
        compiler_params=pltpu.CompilerParams(dimension_semantics=("arbitrary",),
                                             vmem_limit_bytes=VMEM_LIMIT),
        name="route",
    )(h1, wr_t, bias_col)


def _dest_body(eidx_ref, rank_ref, pstart_ref, dest_ref):
    n = eidx_ref.shape[1]
    eiota = lax.broadcasted_iota(I32, (N_EXPERTS, n), 0)
    pstart = pstart_ref[...]
    for kk in range(TOP_K):
        start = jnp.sum(jnp.where(eiota == eidx_ref[kk:kk + 1, :], pstart, 0.0), axis=0, keepdims=True)
        dest = start.astype(I32) + rank_ref[kk:kk + 1, :]
        for c in range(n // SC_CHUNK):
            dest_ref[c, kk:kk + 1, :] = dest[:, c * SC_CHUNK:(c + 1) * SC_CHUNK]


def _dest(eidx, rank, pstart_col):
    t = eidx.shape[1]
    kspec = pl.BlockSpec((TOP_K, ROUTE_ROWS), lambda i: (0, i))
    chunks = ROUTE_ROWS // SC_CHUNK
    return pl.pallas_call(
        _dest_body,
        grid=(t // ROUTE_ROWS,),
        in_specs=[kspec, kspec, _const_spec((N_EXPERTS, 1))],
        out_specs=pl.BlockSpec((chunks, TOP_K, SC_CHUNK), lambda i: (i, 0, 0)),
        out_shape=jax.ShapeDtypeStruct((t // SC_CHUNK, TOP_K, SC_CHUNK), I32),
        compiler_params=pltpu.CompilerParams(dimension_semantics=("arbitrary",)),
        name="dest",
    )(eidx, rank, pstart_col)


def _sc_mesh():
    return plsc.VectorSubcoreMesh(core_axis_name="c", subcore_axis_name="s",
                                  num_cores=SC_CORES, num_subcores=SC_SUBCORES)


def _sc_worker_chunks(t):
    assert t % (SC_CORES * SC_SUBCORES * SC_CHUNK) == 0
    per_worker = t // (SC_CORES * SC_SUBCORES * SC_CHUNK)
    worker = lax.axis_index("s") * SC_CORES + lax.axis_index("c")
    return worker * per_worker, per_worker


def _dispatch(dest, h1p, n_rows):
    t, width = h1p.shape

    @functools.partial(
        pl.kernel, mesh=_sc_mesh(), out_type=jax.ShapeDtypeStruct((n_rows, width), U32),
        scratch_types=[pltpu.VMEM((SC_CHUNK, width), U32), pltpu.VMEM((TOP_K, SC_CHUNK), I32),
                       pltpu.SemaphoreType.DMA],
        name="dispatch")
    def body(h_hbm, dest_hbm, xs_hbm, rows_v, idx_v, sem):
        first, count = _sc_worker_chunks(t)

        @pl.loop(0, count)
        def _(i):
            chunk = first + i
            pltpu.sync_copy(h_hbm.at[pl.ds(pl.multiple_of(chunk * SC_CHUNK, SC_CHUNK), SC_CHUNK)], rows_v)
            pltpu.sync_copy(dest_hbm.at[chunk], idx_v)
            copies = [pltpu.async_copy(rows_v, xs_hbm.at[idx_v.at[kk]], sem) for kk in range(TOP_K)]
            for c in copies:
                c.wait()

    return body(h1p, dest)


def _gather(dest, ys):
    t = dest.shape[0] * SC_CHUNK
    width = ys.shape[1]

    @functools.partial(
        pl.kernel, mesh=_sc_mesh(), out_type=jax.ShapeDtypeStruct((TOP_K, t, width), U32),
        scratch_types=[pltpu.VMEM((SC_CHUNK, width), U32), pltpu.VMEM((TOP_K, SC_CHUNK), I32),
                       pltpu.SemaphoreType.DMA],
        name="gather")
    def body(ys_hbm, dest_hbm, yg_hbm, rows_v, idx_v, sem):
        first, count = _sc_worker_chunks(t)

        @pl.loop(0, count)
        def _(i):
            chunk = first + i
            pltpu.sync_copy(dest_hbm.at[chunk], idx_v)
            for kk in range(TOP_K):
                pltpu.async_copy(ys_hbm.at[idx_v.at[kk]], rows_v, sem).wait()
                pltpu.sync_copy(
                    rows_v, yg_hbm.at[kk, pl.ds(pl.multiple_of(chunk * SC_CHUNK, SC_CHUNK), SC_CHUNK)])

    return body(ys, dest)


def _experts_body(bs_ref, xs_in_ref, wg_ref, wu_ref, wd_ref, xs_ref,
                  xbuf, ybuf, wgb, wub, wdb, sem_in, sem_out):
    del xs_in_ref
    e = pl.program_id(0)
    b0 = bs_ref[e]
    b1 = bs_ref[e + 1]
    n_total = bs_ref[N_EXPERTS]
    half = D_MODEL // 2

    def fetch(b):
        slot = b % IN_SLOTS
        return pltpu.make_async_copy(xs_ref.at[pl.ds(b * MOE_BLOCK, MOE_BLOCK)], xbuf.at[slot],
                                     sem_in.at[slot])

    def store(b):
        slot = b % OUT_SLOTS
        return pltpu.make_async_copy(ybuf.at[slot], xs_ref.at[pl.ds(b * MOE_BLOCK, MOE_BLOCK)],
                                     sem_out.at[slot])

    @pl.when(e == 0)
    def _():
        for b in range(LOOKAHEAD):
            @pl.when(b < n_total)
            def _():
                fetch(b).start(priority=1)

    @pl.when(b1 > b0)
    def _():
        wgb[...] = wg_ref[0].astype(BF16)
        wub[...] = wu_ref[0].astype(BF16)
        wdb[...] = wd_ref[0].astype(BF16)

    def run(blocks):
        for b in blocks:
            @pl.when(b + LOOKAHEAD < n_total)
            def _():
                fetch(b + LOOKAHEAD).start(priority=1)
        for b in blocks:
            fetch(b).wait()
        xa, xb = _unpack_halves(jnp.concatenate([xbuf[b % IN_SLOTS] for b in blocks], axis=0))
        xa = xa.astype(BF16)
        xb = xb.astype(BF16)
        gate = _dot(xa, wgb[:half, :]) + _dot(xb, wgb[half:, :])
        up = _dot(xa, wub[:half, :]) + _dot(xb, wub[half:, :])
        hidden = (gate * jax.nn.sigmoid(gate) * up).astype(BF16)
        y = _dot(hidden, wdb[...])
        packed = _pack_halves(y[:, :half], y[:, half:])
        for i, b in enumerate(blocks):
            @pl.when(b >= OUT_SLOTS)
            def _():
                store(b - OUT_SLOTS).wait()

            ybuf[b % OUT_SLOTS] = packed[i * MOE_BLOCK:(i + 1) * MOE_BLOCK]
            store(b).start(priority=1)

    def pair(i, carry):
        run([b0 + 2 * i, b0 + 2 * i + 1])
        return carry

    lax.fori_loop(0, (b1 - b0) // 2, pair, 0)

    @pl.when((b1 - b0) % 2 == 1)
    def _():
        run([b1 - 1])

    @pl.when(e == N_EXPERTS - 1)
    def _():
        for back in range(OUT_SLOTS, 0, -1):
            @pl.when(n_total >= back)
            def _():
                store(n_total - back).wait()


def _experts(block_start, xs, w_gate, w_up, w_down):
    n_rows, width = xs.shape
    wspec = lambda shape: pl.BlockSpec((1,) + shape, lambda e, bs: (e, 0, 0))
    grid_spec = pltpu.PrefetchScalarGridSpec(
        num_scalar_prefetch=1,
        grid=(N_EXPERTS,),
        in_specs=[pl.BlockSpec(memory_space=pl.ANY),
                  wspec((D_MODEL, D_EXPERT)), wspec((D_MODEL, D_EXPERT)), wspec((D_EXPERT, D_MODEL))],
        out_specs=pl.BlockSpec(memory_space=pl.ANY),
        scratch_shapes=[pltpu.VMEM((IN_SLOTS, MOE_BLOCK, width), U32),
                        pltpu.VMEM((OUT_SLOTS, MOE_BLOCK, width), U32),
                        pltpu.VMEM((D_MODEL, D_EXPERT), BF16), pltpu.VMEM((D_MODEL, D_EXPERT), BF16),
                        pltpu.VMEM((D_EXPERT, D_MODEL), BF16),
                        pltpu.SemaphoreType.DMA((IN_SLOTS,)), pltpu.SemaphoreType.DMA((OUT_SLOTS,))],
    )
    return pl.pallas_call(
        _experts_body,
        grid_spec=grid_spec,
        out_shape=jax.ShapeDtypeStruct((n_rows, width), U32),
        input_output_aliases={1: 0},
        compiler_params=pltpu.CompilerParams(dimension_semantics=("arbitrary",),
                                             vmem_limit_bytes=VMEM_LIMIT),
        name="experts",
    )(block_start, xs, w_gate, w_up, w_down)


def _combine_body(h1_ref, w_ref, yg_ref, wsg_ref, wsu_ref, wsd_ref, g2_ref, b2_ref, *refs):
    out_ref = refs[-1]
    h1 = h1_ref[...]
    hb = h1.astype(BF16)
    gate = _dot(hb, wsg_ref[...])
    up = _dot(hb, wsu_ref[...])
    shared = _dot((gate * jax.nn.sigmoid(gate) * up).astype(BF16), wsd_ref[...])

    half = D_MODEL // 2
    lo = jnp.zeros((ROWS, half), F32)
    hi = jnp.zeros((ROWS, half), F32)
    for kk in range(TOP_K):
        a, b = _unpack_halves(yg_ref[kk])
        wk = w_ref[:, kk:kk + 1]
        lo = lo + wk * a
        hi = hi + wk * b
    z = DN_ALPHA * h1 + shared + jnp.concatenate([lo, hi], axis=1)
    out_ref[...] = _layer_norm(z, g2_ref[...], b2_ref[...])


def _combine(h1, w_rows, yg, first_tile, out_prev, wsg, wsu, wsd, ln2_g, ln2_b):
    width = yg.shape[2]
    d_shared = wsg.shape[1]
    rows = lambda cols: pl.BlockSpec((ROWS, cols), lambda i: (first_tile + i, 0))
    args = [h1, w_rows, yg, wsg, wsu, wsd, ln2_g, ln2_b]
    in_specs = [rows(D_MODEL), rows(TOP_K),
                pl.BlockSpec((TOP_K, ROWS, width), lambda i: (0, i, 0)),
                _const_spec((D_MODEL, d_shared)), _const_spec((D_MODEL, d_shared)),
                _const_spec((d_shared, D_MODEL)),
                _const_spec((1, D_MODEL)), _const_spec((1, D_MODEL))]
    aliases = {}
    if out_prev is not None:
        aliases = {len(args): 0}
        args.append(out_prev)
        in_specs.append(pl.BlockSpec(memory_space=pl.ANY))
    return pl.pallas_call(
        _combine_body,
        grid=(yg.shape[1] // ROWS,),
        in_specs=in_specs,
        out_specs=rows(D_MODEL),
        out_shape=jax.ShapeDtypeStruct(h1.shape, F32),
        input_output_aliases=aliases,
        compiler_params=pltpu.CompilerParams(dimension_semantics=("arbitrary",),
                                             vmem_limit_bytes=VMEM_LIMIT),
        name="combine",
    )(*args)


def _pack_in_proj(w_in, b_in, gate_w_f, gate_w_b, gate_b_f, gate_b_b):
    o_pool, o_q, o_k, o_v, o_g = 0, 512, 1024, 1536, 2560
    o_lr, o_gp, o_gg = 3584, 3616, 4640
    order = [(o_pool, POOL_WIDTH), (o_q, GLA_DK), (o_k, GLA_DK), (o_v, GLA_DV), (o_g, GLA_DV),
             (o_gp, D_MODEL), (o_gg, D_MODEL)] + [(o_lr, 2 * GATE_RANK)] * 3
    pad = LANE - 6 * GATE_RANK
    w = jnp.concatenate([w_in[:, o:o + n] for o, n in order] + [jnp.zeros((D_MODEL, pad), F32)], axis=1)
    b = jnp.concatenate([b_in[o:o + n] for o, n in order] + [jnp.zeros((pad,), F32)])
    w_gate = jnp.zeros((2 * GATE_RANK, 2 * GLA_DK), F32)
    w_gate = w_gate.at[:GATE_RANK, :GLA_DK].set(gate_w_f)
    w_gate = w_gate.at[GATE_RANK:, GLA_DK:].set(gate_w_b)
    w_hi = w_gate.astype(BF16)
    w_lo = (w_gate - w_hi.astype(F32)).astype(BF16)
    w_gate3 = jnp.concatenate([w_hi, w_hi, w_lo, jnp.zeros((pad, 2 * GLA_DK), BF16)], axis=0)
    b_gate = jnp.concatenate([gate_b_f, gate_b_b])[None, :]
    return w.astype(BF16), b[None, :], w_gate3, b_gate


def kernel(x, meta, ln0_g, ln0_b, w_in, b_in, pool_w, pool_scale, gate_w_f, gate_b_f, gate_w_b, gate_b_b,
           gla_norm_g, w_branch_pool, w_branch_gla, w_out, ln1_g, ln1_b, w_router, router_bias,
           w_exp_gate, w_exp_up, w_exp_down, w_sh_gate, w_sh_up, w_sh_down, ln2_g, ln2_b):
    batch, seq, d = x.shape
    assert d == D_MODEL and seq % ROWS == 0 and w_in.shape[0] == 1
    assert (batch * seq) % ROUTE_ROWS == 0
    t = batch * seq
    row = lambda a: a.reshape(1, -1).astype(F32)

    lead = jnp.concatenate([jnp.zeros((FRONT - N_META, d), x.dtype), meta.astype(x.dtype)], axis=0)
    tiles_per_batch = (seq + FRONT) // ROWS
    x2 = x.reshape(t, d)

    w_packed, b_packed, w_gate, b_gate = _pack_in_proj(
        w_in[0], b_in[0], gate_w_f[0], gate_w_b[0], gate_b_f[0], gate_b_b[0])
    u, q, k, v, sg, gp, gg, bf, cb = _in_proj(
        lead, x2, row(ln0_g), row(ln0_b), w_packed, b_packed, w_gate, b_gate, tiles_per_batch)

    o_f, o_b = _gla(q, k, v, bf, cb, batch)

    h1, h1p = _mix(
        x.reshape(t, d), u, gp, gg, sg, o_f, o_b, row(ln0_g), row(ln0_b),
        pool_w[0].astype(BF16), row(pool_scale[0]), row(jnp.tile(gla_norm_g[0], GLA_HEADS)),
        w_branch_pool[0].astype(BF16), w_branch_gla[0].astype(BF16), w_out[0].astype(BF16),
        row(ln1_g[0]), row(ln1_b[0]), seq)

    eidx, ew, rank, counts = _route(h1, w_router[0].T, router_bias[0].reshape(N_EXPERTS, 1))

    counts = counts[:, 0]
    padded = ((counts + MOE_BLOCK - 1) // MOE_BLOCK) * MOE_BLOCK
    pend = jnp.cumsum(padded)
    pstart = pend - padded
    dest = _dest(eidx, rank, pstart.astype(F32).reshape(N_EXPERTS, 1))
    n_blocks = -(-(t * TOP_K) // MOE_BLOCK) + N_EXPERTS
    block_start = (jnp.concatenate([pstart, pend[-1:]]) // MOE_BLOCK).astype(I32)

    xs = _dispatch(dest, h1p, n_blocks * MOE_BLOCK)
    ys = _experts(block_start, xs, w_exp_gate[0], w_exp_up[0], w_exp_down[0])
    shared_w = (w_sh_gate[0].astype(BF16), w_sh_up[0].astype(BF16), w_sh_down[0].astype(BF16))
    ew_rows = ew.T
    windows = t // SC_CHUNK // COMBINE_CHUNKS
    out = None
    for c in range(COMBINE_CHUNKS):
        yg = _gather(dest[c * windows:(c + 1) * windows], ys)
        out = _combine(h1, ew_rows, yg, c * windows * SC_CHUNK // ROWS, out, *shared_w,
                       row(ln2_g[0]), row(ln2_b[0]))
    return out.reshape(batch, seq, d)
```

```python
import functools

import jax
import jax.numpy as jnp
from jax import lax
from jax.experimental import pallas as pl
from jax.experimental.pallas import tpu as pltpu
from jax.experimental.pallas import tpu_sc as plsc

F32 = jnp.float32
BF16 = jnp.bfloat16
I32 = jnp.int32
U32 = jnp.uint32
HIGHEST = lax.Precision.HIGHEST

D_MODEL = 1024
N_META = 16
POOL_WINDOWS = (2, 4, 8, 16)
POOL_GROUP = 128
POOL_WIDTH = POOL_GROUP * len(POOL_WINDOWS)
GLA_HEADS = 4
GLA_DK = 512
GLA_DV = 1024
HEAD_K = GLA_DK // GLA_HEADS
HEAD_V = GLA_DV // GLA_HEADS
GATE_RANK = 16
GATE_NORMALIZER = 16.0
RMS_EPS = 1e-5
N_EXPERTS = 256
TOP_K = 8
N_GROUPS = 8
GROUP_SIZE = N_EXPERTS // N_GROUPS
TOPK_GROUPS = 4
D_EXPERT = 256
ROUTED_SCALE = 2.5
DN_ALPHA = 2.0 ** 0.25
LN_EPS = 1e-5

LANE = 128
ROWS = 256
FRONT = ROWS
GLA_CHUNK = 256
HALO = 16
ROUTE_ROWS = 512
MOE_BLOCK = 256
LOOKAHEAD = 6
IN_SLOTS = LOOKAHEAD + 2
OUT_SLOTS = 2
SC_CORES = 2
SC_SUBCORES = 16
SC_CHUNK = 128
COMBINE_CHUNKS = 4
VMEM_LIMIT = 56 * 1024 * 1024

C_POOL, C_Q, C_K, C_V, C_G, C_GP, C_GG, C_LR = 0, 512, 1024, 1536, 2560, 3584, 4608, 5632
W_PACKED = C_LR + LANE


def _layer_norm(x, g, b):
    mu = jnp.mean(x, axis=-1, keepdims=True)
    xc = x - mu
    var = jnp.mean(xc * xc, axis=-1, keepdims=True)
    return xc * lax.rsqrt(var + LN_EPS) * g + b


def _dot(a, b):
    return jnp.dot(a, b, preferred_element_type=F32)


def _dot_nt(a, b):
    return lax.dot_general(a, b, (((1,), (1,)), ((), ())), preferred_element_type=F32)


def _split_dot(m2, g):
    g_hi = g.astype(BF16)
    g_lo = (g - g_hi.astype(F32)).astype(BF16)
    return _dot(m2, jnp.concatenate([g_hi, g_lo], axis=0))


def _pack_halves(a, b):
    pa = lax.bitcast_convert_type(a.astype(BF16).astype(F32), U32)
    pb = lax.bitcast_convert_type(b.astype(BF16).astype(F32), U32)
    return pa | (pb >> 16)


def _unpack_halves(p):
    a = lax.bitcast_convert_type(p & jnp.uint32(0xFFFF0000), F32)
    b = lax.bitcast_convert_type(p << 16, F32)
    return a, b


def _const_spec(shape):
    return pl.BlockSpec(shape, lambda *_: (0,) * len(shape))


def _inproj_body(lead_ref, x_ref, g0_ref, b0_ref, w_ref, bias_ref, wgate_ref, bgate_ref,
                 u_ref, q_ref, k_ref, v_ref, sg_ref, gp_ref, gg_ref, bf_ref, cb_ref,
                 *, tiles_per_batch):
    i = pl.program_id(0)
    x_in = jnp.where(i % tiles_per_batch == 0, lead_ref[...], x_ref[...])
    h = _layer_norm(x_in, g0_ref[...], b0_ref[...])
    hb = h.astype(BF16)

    def proj(c0, n):
        return _dot(hb, w_ref[:, c0:c0 + n]) + bias_ref[:, c0:c0 + n]

    row = lax.broadcasted_iota(I32, (ROWS, 1), 0)
    valid = row >= jnp.where(i % tiles_per_batch == 0, FRONT - N_META, 0)

    u_ref[...] = proj(C_POOL, POOL_WIDTH).astype(BF16)
    q_ref[...] = (proj(C_Q, GLA_DK) * (HEAD_K ** -0.5)).astype(BF16)
    k_ref[...] = jnp.where(valid, proj(C_K, GLA_DK), 0.0).astype(BF16)
    v_ref[...] = jnp.where(valid, proj(C_V, GLA_DV), 0.0).astype(BF16)
    g = proj(C_G, GLA_DV)
    sg_ref[...] = (g * jax.nn.sigmoid(g)).astype(BF16)
    gp_ref[...] = jax.nn.sigmoid(proj(C_GP, D_MODEL)).astype(BF16)
    gg_ref[...] = jax.nn.sigmoid(proj(C_GG, D_MODEL)).astype(BF16)

    lr3 = proj(C_LR, LANE)
    lr_hi = lr3.astype(BF16)
    lr_lo = (lr3 - lr_hi.astype(F32)).astype(BF16)
    lane = lax.broadcasted_iota(I32, (1, LANE), 1)
    second = jnp.logical_and(lane >= 2 * GATE_RANK, lane < 4 * GATE_RANK)
    xg = _dot(jnp.where(second, lr_lo, lr_hi), wgate_ref[...]) + bgate_ref[...]
    gk = (jnp.minimum(xg, 0.0) - jnp.log(1.0 + jnp.exp(-jnp.abs(xg)))) * (1.0 / GATE_NORMALIZER)

    r = lax.broadcasted_iota(I32, (GLA_CHUNK, 2 * GLA_CHUNK), 0)
    c = lax.broadcasted_iota(I32, (GLA_CHUNK, 2 * GLA_CHUNK), 1) % GLA_CHUNK
    tril = (r >= c).astype(BF16)
    triu = (r <= c).astype(BF16)
    for ci in range(ROWS // GLA_CHUNK):
        sl = slice(ci * GLA_CHUNK, (ci + 1) * GLA_CHUNK)
        bf_ref[sl, :] = _split_dot(tril, gk[sl, :GLA_DK])
        cb_ref[sl, :] = _split_dot(triu, gk[sl, GLA_DK:])


def _in_proj(lead, x2, ln0_g, ln0_b, w_packed, b_packed, w_gate, b_gate, tiles_per_batch):
    x_tiles = tiles_per_batch - FRONT // ROWS
    n = x2.shape[0] // (x_tiles * ROWS) * tiles_per_batch * ROWS
    row_spec = lambda width: pl.BlockSpec((ROWS, width), lambda i: (i, 0))
    x_spec = pl.BlockSpec(
        (ROWS, D_MODEL),
        lambda i: ((i // tiles_per_batch) * x_tiles + jnp.maximum(i % tiles_per_batch - 1, 0), 0))
    out_widths = (POOL_WIDTH, GLA_DK, GLA_DK, GLA_DV, GLA_DV, D_MODEL, D_MODEL)
    out_shape = [jax.ShapeDtypeStruct((n, w), BF16) for w in out_widths]
    out_shape += [jax.ShapeDtypeStruct((n, GLA_DK), F32)] * 2
    out_specs = [row_spec(w) for w in out_widths] + [row_spec(GLA_DK)] * 2
    return pl.pallas_call(
        functools.partial(_inproj_body, tiles_per_batch=tiles_per_batch),
        grid=(n // ROWS,),
        in_specs=[_const_spec((FRONT, D_MODEL)), x_spec,
                  _const_spec((1, D_MODEL)), _const_spec((1, D_MODEL)),
                  _const_spec((D_MODEL, W_PACKED)), _const_spec((1, W_PACKED)),
                  _const_spec((LANE, 2 * GLA_DK)), _const_spec((1, 2 * GLA_DK))],
        out_specs=out_specs,
        out_shape=out_shape,
        compiler_params=pltpu.CompilerParams(dimension_semantics=("arbitrary",),
                                             vmem_limit_bytes=VMEM_LIMIT),
        name="in_proj",
    )(lead, x2, ln0_g, ln0_b, w_packed, b_packed, w_gate, b_gate)


def _gla_direction(q_ref, k_ref, v_ref, b_ref, o_ref, s_ref, reverse):
    c = GLA_CHUNK
    r = lax.broadcasted_iota(I32, (c, c), 0)
    s = lax.broadcasted_iota(I32, (c, c), 1)
    keep = (r <= s) if reverse else (r >= s)
    eye = lax.broadcasted_iota(I32, (HEAD_K, HEAD_K), 0) == lax.broadcasted_iota(I32, (HEAD_K, HEAD_K), 1)
    edge = 0 if reverse else c - 1
    for h in range(GLA_HEADS):
        ks = slice(h * HEAD_K, (h + 1) * HEAD_K)
        vs = slice(h * HEAD_V, (h + 1) * HEAD_V)
        b = b_ref[:, ks]
        qh = q_ref[:, ks].astype(F32)
        kh = k_ref[:, ks].astype(F32)
        vh = v_ref[:, vs]
        b_tot = b[edge:edge + 1, :]
        b_mid = b[c // 2:c // 2 + 1, :]
        q_state = (qh * jnp.exp(b)).astype(BF16)
        q_in = (qh * jnp.exp(b - b_mid)).astype(BF16)
        k_in = (kh * jnp.exp(b_mid - b)).astype(BF16)
        k_state = (kh * jnp.exp(b_tot - b)).T.astype(BF16)
        att = jnp.where(keep, _dot_nt(q_in, k_in), 0.0).astype(BF16)
        state = s_ref[h]
        o = _dot(q_state, state.astype(BF16)) + _dot(att, vh)
        o_ref[:, vs] = o.astype(o_ref.dtype)
        decay_col = jnp.sum(jnp.where(eye, jnp.exp(b_tot), 0.0), axis=1, keepdims=True)
        s_ref[h] = decay_col * state + _dot(k_state, vh)


def _gla_body(qf_ref, kf_ref, vf_ref, bf_ref, qb_ref, kb_ref, vb_ref, cb_ref,
              of_ref, ob_ref, sf_ref, sb_ref):
    @pl.when(pl.program_id(1) == 0)
    def _():
        sf_ref[...] = jnp.zeros_like(sf_ref)
        sb_ref[...] = jnp.zeros_like(sb_ref)

    _gla_direction(qf_ref, kf_ref, vf_ref, bf_ref, of_ref, sf_ref, reverse=False)
    _gla_direction(qb_ref, kb_ref, vb_ref, cb_ref, ob_ref, sb_ref, reverse=True)


def _gla(q, k, v, bf, cb, batch):
    n = q.shape[0]
    nch = n // batch // GLA_CHUNK
    fwd = lambda width: pl.BlockSpec((GLA_CHUNK, width), lambda b, i: (b * nch + i, 0))
    bwd = lambda width: pl.BlockSpec((GLA_CHUNK, width), lambda b, i: (b * nch + nch - 1 - i, 0))
    state = pltpu.VMEM((GLA_HEADS, HEAD_K, HEAD_V), F32)
    return pl.pallas_call(
        _gla_body,
        grid=(batch, nch),
        in_specs=[fwd(GLA_DK), fwd(GLA_DK), fwd(GLA_DV), fwd(GLA_DK),
                  bwd(GLA_DK), bwd(GLA_DK), bwd(GLA_DV), bwd(GLA_DK)],
        out_specs=[fwd(GLA_DV), bwd(GLA_DV)],
        out_shape=[jax.ShapeDtypeStruct((n, GLA_DV), BF16)] * 2,
        scratch_shapes=[state, state],
        compiler_params=pltpu.CompilerParams(dimension_semantics=("arbitrary", "arbitrary"),
                                             vmem_limit_bytes=VMEM_LIMIT),
        name="gla",
    )(q, k, v, bf, q, k, v, cb)


def _mix_body(x_ref, up_ref, u_ref, un_ref, gp_ref, gg_ref, sg_ref, of_ref, ob_ref,
              g0_ref, b0_ref, pw_ref, ps_ref, gn_ref, wbp_ref, wbg_ref, wo_ref, g1_ref, b1_ref,
              h1_ref, h1p_ref, *, tiles_per_batch, seq):
    j = pl.program_id(0) % tiles_per_batch
    u_main = u_ref[...]
    u_ext = jnp.concatenate([up_ref[...], u_main, un_ref[...]], axis=0)
    r = lax.broadcasted_iota(I32, (ROWS, ROWS + 2 * HALO), 0)
    e = lax.broadcasted_iota(I32, (ROWS, ROWS + 2 * HALO), 1) - HALO
    in_seq = (j * ROWS + e) < seq
    pos = j * ROWS + lax.broadcasted_iota(I32, (ROWS, 1), 0)
    y_pool = []
    for gi, w in enumerate(POOL_WINDOWS):
        cs = slice(gi * POOL_GROUP, (gi + 1) * POOL_GROUP)
        band = jnp.logical_and(jnp.logical_and(e >= r - w // 2, e < r + w - w // 2), in_seq)
        count = w - jnp.maximum(pos + (w - w // 2) - seq, 0)
        mean = _dot(band.astype(BF16), u_ext[:, cs]) / count.astype(F32)
        d = mean - u_main[:, cs].astype(F32)
        y_pool.append(_dot(d.astype(BF16), pw_ref[gi]))
    y_pool = (jnp.concatenate(y_pool, axis=1) * ps_ref[...]).astype(BF16)

    o = of_ref[...].astype(F32) + ob_ref[...].astype(F32)
    y_gla = []
    for h in range(GLA_HEADS):
        oh = o[:, h * HEAD_V:(h + 1) * HEAD_V]
        y_gla.append(oh * lax.rsqrt(jnp.mean(oh * oh, axis=-1, keepdims=True) + RMS_EPS))
    y_gla = (jnp.concatenate(y_gla, axis=1) * gn_ref[...] * sg_ref[...].astype(F32)).astype(BF16)

    merged = (gp_ref[...].astype(F32) * _dot(y_pool, wbp_ref[...])
              + gg_ref[...].astype(F32) * _dot(y_gla, wbg_ref[...]))
    y = _dot(merged.astype(BF16), wo_ref[...])
    h = _layer_norm(x_ref[...], g0_ref[...], b0_ref[...])
    h1 = _layer_norm(DN_ALPHA * h + y, g1_ref[...], b1_ref[...])
    h1_ref[...] = h1
    h1p_ref[...] = _pack_halves(h1[:, :D_MODEL // 2], h1[:, D_MODEL // 2:])


def _mix(x2, u, gp, gg, sg, o_f, o_b, ln0_g, ln0_b, pool_w, pool_scale, norm_g, wbp, wbg, wo,
         ln1_g, ln1_b, seq):
    t = x2.shape[0]
    tiles_per_batch = seq // ROWS
    padded_tiles = tiles_per_batch + FRONT // ROWS
    halo_per_tile = ROWS // HALO
    last_halo = u.shape[0] // HALO - 1

    def pidx(i):
        return (i // tiles_per_batch) * padded_tiles + FRONT // ROWS + i % tiles_per_batch

    padded = lambda width: pl.BlockSpec((ROWS, width), lambda i: (pidx(i), 0))
    plain = lambda width: pl.BlockSpec((ROWS, width), lambda i: (i, 0))
    prev_halo = pl.BlockSpec((HALO, POOL_WIDTH), lambda i: (pidx(i) * halo_per_tile - 1, 0))
    next_halo = pl.BlockSpec(
        (HALO, POOL_WIDTH), lambda i: (jnp.minimum((pidx(i) + 1) * halo_per_tile, last_halo), 0))
    return pl.pallas_call(
        functools.partial(_mix_body, tiles_per_batch=tiles_per_batch, seq=seq),
        grid=(t // ROWS,),
        in_specs=[plain(D_MODEL), prev_halo, padded(POOL_WIDTH), next_halo,
                  padded(D_MODEL), padded(D_MODEL), padded(GLA_DV), padded(GLA_DV), padded(GLA_DV),
                  _const_spec((1, D_MODEL)), _const_spec((1, D_MODEL)),
                  _const_spec((len(POOL_WINDOWS), POOL_GROUP, POOL_GROUP)), _const_spec((1, POOL_WIDTH)),
                  _const_spec((1, GLA_DV)), _const_spec((POOL_WIDTH, D_MODEL)),
                  _const_spec((GLA_DV, D_MODEL)), _const_spec((D_MODEL, D_MODEL)),
                  _const_spec((1, D_MODEL)), _const_spec((1, D_MODEL))],
        out_specs=[plain(D_MODEL), plain(D_MODEL // 2)],
        out_shape=[jax.ShapeDtypeStruct((t, D_MODEL), F32), jax.ShapeDtypeStruct((t, D_MODEL // 2), U32)],
        compiler_params=pltpu.CompilerParams(dimension_semantics=("arbitrary",),
                                             vmem_limit_bytes=VMEM_LIMIT),
        name="mix",
    )(x2, u, u, u, gp, gg, sg, o_f, o_b, ln0_g, ln0_b, pool_w, pool_scale, norm_g, wbp, wbg, wo,
      ln1_g, ln1_b)


def _first_index_of_max(val, idx, size):
    m = jnp.max(val, axis=0, keepdims=True)
    first = jnp.min(jnp.where(val == m, idx, size), axis=0, keepdims=True)
    return m, first


def _route_body(h_ref, wr_ref, bias_ref, eidx_ref, w_ref, rank_ref, cnt_ref, base_ref):
    n = ROUTE_ROWS

    @pl.when(pl.program_id(0) == 0)
    def _():
        base_ref[...] = jnp.zeros_like(base_ref)

    logits = lax.dot_general(wr_ref[...], h_ref[...], (((1,), (1,)), ((), ())),
                             precision=HIGHEST, preferred_element_type=F32)
    scores = jax.nn.sigmoid(logits)
    sel = scores + bias_ref[...]

    gidx = lax.broadcasted_iota(I32, (GROUP_SIZE, n), 0)
    groups = [sel[g * GROUP_SIZE:(g + 1) * GROUP_SIZE, :] for g in range(N_GROUPS)]
    gscore = []
    for grp in groups:
        m1, first = _first_index_of_max(grp, gidx, GROUP_SIZE)
        m2 = jnp.max(jnp.where(gidx == first, -jnp.inf, grp), axis=0, keepdims=True)
        gscore.append(m1 + m2)

    masked = []
    for gi in range(N_GROUPS):
        beaten = jnp.zeros((1, n), F32)
        for gj in range(N_GROUPS):
            if gj != gi:
                wins = (gscore[gj] >= gscore[gi]) if gj < gi else (gscore[gj] > gscore[gi])
                beaten = beaten + wins.astype(F32)
        masked.append(jnp.where(beaten < TOPK_GROUPS, groups[gi], -jnp.inf))
    val = jnp.concatenate(masked, axis=0)

    eidx = lax.broadcasted_iota(I32, (N_EXPERTS, n), 0)
    chosen_f = jnp.zeros((N_EXPERTS, n), F32)
    picks, weights = [], []
    for _ in range(TOP_K):
        _, first = _first_index_of_max(val, eidx, N_EXPERTS)
        hit = eidx == first
        picks.append(first)
        weights.append(jnp.sum(jnp.where(hit, scores, 0.0), axis=0, keepdims=True))
        chosen_f = jnp.where(hit, 1.0, chosen_f)
        val = jnp.where(hit, -jnp.inf, val)
    wsum = weights[0]
    for wk in weights[1:]:
        wsum = wsum + wk
    scale = ROUTED_SCALE / wsum

    r = lax.broadcasted_iota(I32, (n, n), 0)
    c = lax.broadcasted_iota(I32, (n, n), 1)
    earlier = (r < c).astype(BF16)
    before = _dot(chosen_f.astype(BF16), earlier) + base_ref[...]
    base_ref[...] = base_ref[...] + jnp.sum(chosen_f, axis=1, keepdims=True)
    cnt_ref[...] = jnp.broadcast_to(base_ref[...], cnt_ref.shape).astype(I32)

    for kk in range(TOP_K):
        eidx_ref[kk:kk + 1, :] = picks[kk]
        w_ref[kk:kk + 1, :] = weights[kk] * scale
        rank = jnp.sum(jnp.where(eidx == picks[kk], before, 0.0), axis=0, keepdims=True)
        rank_ref[kk:kk + 1, :] = rank.astype(I32)


def _route(h1, wr_t, bias_col):
    t = h1.shape[0]
    col = lambda dtype: jax.ShapeDtypeStruct((TOP_K, t), dtype)
    kspec = pl.BlockSpec((TOP_K, ROUTE_ROWS), lambda i: (0, i))
    return pl.pallas_call(
        _route_body,
        grid=(t // ROUTE_ROWS,),
        in_specs=[pl.BlockSpec((ROUTE_ROWS, D_MODEL), lambda i: (i, 0)),
                  _const_spec((N_EXPERTS, D_MODEL)), _const_spec((N_EXPERTS, 1))],
        out_specs=[kspec, kspec, kspec, _const_spec((N_EXPERTS, LANE))],
        out_shape=[col(I32), col(F32), col(I32), jax.ShapeDtypeStruct((N_EXPERTS, LANE), I32)],
        scratch_shapes=[pltpu.VMEM((N_EXPERTS, 1), F32)],
        compiler_params=pltpu.CompilerParams(dimension_semantics=("arbitrary",),
                                             vmem_limit_bytes=VMEM_LIMIT),
        name="route",
    )(h1, wr_t, bias_col)


def _dest_body(eidx_ref, rank_ref, pstart_ref, dest_ref):
    n = eidx_ref.shape[1]
    eiota = lax.broadcasted_iota(I32, (N_EXPERTS, n), 0)
    pstart = pstart_ref[...]
    for kk in range(TOP_K):
        start = jnp.sum(jnp.where(eiota == eidx_ref[kk:kk + 1, :], pstart, 0.0), axis=0, keepdims=True)
        dest = start.astype(I32) + rank_ref[kk:kk + 1, :]
        for c in range(n // SC_CHUNK):
            dest_ref[c, kk:kk + 1, :] = dest[:, c * SC_CHUNK:(c + 1) * SC_CHUNK]


def _dest(eidx, rank, pstart_col):
    t = eidx.shape[1]
    kspec = pl.BlockSpec((TOP_K, ROUTE_ROWS), lambda i: (0, i))
    chunks = ROUTE_ROWS // SC_CHUNK
    return pl.pallas_call(
        _dest_body,
        grid=(t // ROUTE_ROWS,),
        in_specs=[kspec, kspec, _const_spec((N_EXPERTS, 1))],
        out_specs=pl.BlockSpec((chunks, TOP_K, SC_CHUNK), lambda i: (i, 0, 0)),
        out_shape=jax.ShapeDtypeStruct((t // SC_CHUNK, TOP_K, SC_CHUNK), I32),
        compiler_params=pltpu.CompilerParams(dimension_semantics=("arbitrary",)),
        name="dest",
    )(eidx, rank, pstart_col)


def _sc_mesh():
    return plsc.VectorSubcoreMesh(core_axis_name="c", subcore_axis_name="s",
                                  num_cores=SC_CORES, num_subcores=SC_SUBCORES)


def _sc_worker_chunks(t):
    assert t % (SC_CORES * SC_SUBCORES * SC_CHUNK) == 0
    per_worker = t // (SC_CORES * SC_SUBCORES * SC_CHUNK)
    worker = lax.axis_index("s") * SC_CORES + lax.axis_index("c")
    return worker * per_worker, per_worker


def _dispatch(dest, h1p, n_rows):
    t, width = h1p.shape

    @functools.partial(
        pl.kernel, mesh=_sc_mesh(), out_type=jax.ShapeDtypeStruct((n_rows, width), U32),
        scratch_types=[pltpu.VMEM((SC_CHUNK, width), U32), pltpu.VMEM((TOP_K, SC_CHUNK), I32),
                       pltpu.SemaphoreType.DMA],
        name="dispatch")
    def body(h_hbm, dest_hbm, xs_hbm, rows_v, idx_v, sem):
        first, count = _sc_worker_chunks(t)

        @pl.loop(0, count)
        def _(i):
            chunk = first + i
            pltpu.sync_copy(h_hbm.at[pl.ds(pl.multiple_of(chunk * SC_CHUNK, SC_CHUNK), SC_CHUNK)], rows_v)
            pltpu.sync_copy(dest_hbm.at[chunk], idx_v)
            copies = [pltpu.async_copy(rows_v, xs_hbm.at[idx_v.at[kk]], sem) for kk in range(TOP_K)]
            for c in copies:
                c.wait()

    return body(h1p, dest)


def _gather(dest, ys):
    t = dest.shape[0] * SC_CHUNK
    width = ys.shape[1]

    @functools.partial(
        pl.kernel, mesh=_sc_mesh(), out_type=jax.ShapeDtypeStruct((TOP_K, t, width), U32),
        scratch_types=[pltpu.VMEM((SC_CHUNK, width), U32), pltpu.VMEM((TOP_K, SC_CHUNK), I32),
                       pltpu.SemaphoreType.DMA],
        name="gather")
    def body(ys_hbm, dest_hbm, yg_hbm, rows_v, idx_v, sem):
        first, count = _sc_worker_chunks(t)

        @pl.loop(0, count)
        def _(i):
            chunk = first + i
            pltpu.sync_copy(dest_hbm.at[chunk], idx_v)
            for kk in range(TOP_K):
                pltpu.async_copy(ys_hbm.at[idx_v.at[kk]], rows_v, sem).wait()
                pltpu.sync_copy(
                    rows_v, yg_hbm.at[kk, pl.ds(pl.multiple_of(chunk * SC_CHUNK, SC_CHUNK), SC_CHUNK)])

    return body(ys, dest)


def _experts_body(bs_ref, xs_in_ref, wg_ref, wu_ref, wd_ref, xs_ref,
                  xbuf, ybuf, wgb, wub, wdb, sem_in, sem_out):
    del xs_in_ref
    e = pl.program_id(0)
    b0 = bs_ref[e]
    b1 = bs_ref[e + 1]
    n_total = bs_ref[N_EXPERTS]
    half = D_MODEL // 2

    def fetch(b):
        slot = b % IN_SLOTS
        return pltpu.make_async_copy(xs_ref.at[pl.ds(b * MOE_BLOCK, MOE_BLOCK)], xbuf.at[slot],
                                     sem_in.at[slot])

    def store(b):
        slot = b % OUT_SLOTS
        return pltpu.make_async_copy(ybuf.at[slot], xs_ref.at[pl.ds(b * MOE_BLOCK, MOE_BLOCK)],
                                     sem_out.at[slot])

    @pl.when(e == 0)
    def _():
        for b in range(LOOKAHEAD):
            @pl.when(b < n_total)
            def _():
                fetch(b).start(priority=1)

    @pl.when(b1 > b0)
    def _():
        wgb[...] = wg_ref[0].astype(BF16)
        wub[...] = wu_ref[0].astype(BF16)
        wdb[...] = wd_ref[0].astype(BF16)

    def run(blocks):
        for b in blocks:
            @pl.when(b + LOOKAHEAD < n_total)
            def _():
                fetch(b + LOOKAHEAD).start(priority=1)
        for b in blocks:
            fetch(b).wait()
        xa, xb = _unpack_halves(jnp.concatenate([xbuf[b % IN_SLOTS] for b in blocks], axis=0))
        xa = xa.astype(BF16)
        xb = xb.astype(BF16)
        gate = _dot(xa, wgb[:half, :]) + _dot(xb, wgb[half:, :])
        up = _dot(xa, wub[:half, :]) + _dot(xb, wub[half:, :])
        hidden = (gate * jax.nn.sigmoid(gate) * up).astype(BF16)
        y = _dot(hidden, wdb[...])
        packed = _pack_halves(y[:, :half], y[:, half:])
        for i, b in enumerate(blocks):
            @pl.when(b >= OUT_SLOTS)
            def _():
                store(b - OUT_SLOTS).wait()

            ybuf[b % OUT_SLOTS] = packed[i * MOE_BLOCK:(i + 1) * MOE_BLOCK]
            store(b).start(priority=1)

    def pair(i, carry):
        run([b0 + 2 * i, b0 + 2 * i + 1])
        return carry

    lax.fori_loop(0, (b1 - b0) // 2, pair, 0)

    @pl.when((b1 - b0) % 2 == 1)
    def _():
        run([b1 - 1])

    @pl.when(e == N_EXPERTS - 1)
    def _():
        for back in range(OUT_SLOTS, 0, -1):
            @pl.when(n_total >= back)
            def _():
                store(n_total - back).wait()


def _experts(block_start, xs, w_gate, w_up, w_down):
    n_rows, width = xs.shape
    wspec = lambda shape: pl.BlockSpec((1,) + shape, lambda e, bs: (e, 0, 0))
    grid_spec = pltpu.PrefetchScalarGridSpec(
        num_scalar_prefetch=1,
        grid=(N_EXPERTS,),
        in_specs=[pl.BlockSpec(memory_space=pl.ANY),
                  wspec((D_MODEL, D_EXPERT)), wspec((D_MODEL, D_EXPERT)), wspec((D_EXPERT, D_MODEL))],
        out_specs=pl.BlockSpec(memory_space=pl.ANY),
        scratch_shapes=[pltpu.VMEM((IN_SLOTS, MOE_BLOCK, width), U32),
                        pltpu.VMEM((OUT_SLOTS, MOE_BLOCK, width), U32),
                        pltpu.VMEM((D_MODEL, D_EXPERT), BF16), pltpu.VMEM((D_MODEL, D_EXPERT), BF16),
                        pltpu.VMEM((D_EXPERT, D_MODEL), BF16),
                        pltpu.SemaphoreType.DMA((IN_SLOTS,)), pltpu.SemaphoreType.DMA((OUT_SLOTS,))],
    )
    return pl.pallas_call(
        _experts_body,
        grid_spec=grid_spec,
        out_shape=jax.ShapeDtypeStruct((n_rows, width), U32),
        input_output_aliases={1: 0},
        compiler_params=pltpu.CompilerParams(dimension_semantics=("arbitrary",),
                                             vmem_limit_bytes=VMEM_LIMIT),
        name="experts",
    )(block_start, xs, w_gate, w_up, w_down)


def _combine_body(h1_ref, w_ref, yg_ref, wsg_ref, wsu_ref, wsd_ref, g2_ref, b2_ref, *refs):
    out_ref = refs[-1]
    h1 = h1_ref[...]
    hb = h1.astype(BF16)
    gate = _dot(hb, wsg_ref[...])
    up = _dot(hb, wsu_ref[...])
    shared = _dot((gate * jax.nn.sigmoid(gate) * up).astype(BF16), wsd_ref[...])

    half = D_MODEL // 2
    lo = jnp.zeros((ROWS, half), F32)
    hi = jnp.zeros((ROWS, half), F32)
    for kk in range(TOP_K):
        a, b = _unpack_halves(yg_ref[kk])
        wk = w_ref[:, kk:kk + 1]
        lo = lo + wk * a
        hi = hi + wk * b
    z = DN_ALPHA * h1 + shared + jnp.concatenate([lo, hi], axis=1)
    out_ref[...] = _layer_norm(z, g2_ref[...], b2_ref[...])


def _combine(h1, w_rows, yg, first_tile, out_prev, wsg, wsu, wsd, ln2_g, ln2_b):
    width = yg.shape[2]
    d_shared = wsg.shape[1]
    rows = lambda cols: pl.BlockSpec((ROWS, cols), lambda i: (first_tile + i, 0))
    args = [h1, w_rows, yg, wsg, wsu, wsd, ln2_g, ln2_b]
    in_specs = [rows(D_MODEL), rows(TOP_K),
                pl.BlockSpec((TOP_K, ROWS, width), lambda i: (0, i, 0)),
                _const_spec((D_MODEL, d_shared)), _const_spec((D_MODEL, d_shared)),
                _const_spec((d_shared, D_MODEL)),
                _const_spec((1, D_MODEL)), _const_spec((1, D_MODEL))]
    aliases = {}
    if out_prev is not None:
        aliases = {len(args): 0}
        args.append(out_prev)
        in_specs.append(pl.BlockSpec(memory_space=pl.ANY))
    return pl.pallas_call(
        _combine_body,
        grid=(yg.shape[1] // ROWS,),
        in_specs=in_specs,
        out_specs=rows(D_MODEL),
        out_shape=jax.ShapeDtypeStruct(h1.shape, F32),
        input_output_aliases=aliases,
        compiler_params=pltpu.CompilerParams(dimension_semantics=("arbitrary",),
                                             vmem_limit_bytes=VMEM_LIMIT),
        name="combine",
    )(*args)


def _pack_in_proj(w_in, b_in, gate_w_f, gate_w_b, gate_b_f, gate_b_b):
    o_pool, o_q, o_k, o_v, o_g = 0, 512, 1024, 1536, 2560
    o_lr, o_gp, o_gg = 3584, 3616, 4640
    order = [(o_pool, POOL_WIDTH), (o_q, GLA_DK), (o_k, GLA_DK), (o_v, GLA_DV), (o_g, GLA_DV),
             (o_gp, D_MODEL), (o_gg, D_MODEL)] + [(o_lr, 2 * GATE_RANK)] * 3
    pad = LANE - 6 * GATE_RANK
    w = jnp.concatenate([w_in[:, o:o + n] for o, n in order] + [jnp.zeros((D_MODEL, pad), F32)], axis=1)
    b = jnp.concatenate([b_in[o:o + n] for o, n in order] + [jnp.zeros((pad,), F32)])
    w_gate = jnp.zeros((2 * GATE_RANK, 2 * GLA_DK), F32)
    w_gate = w_gate.at[:GATE_RANK, :GLA_DK].set(gate_w_f)
    w_gate = w_gate.at[GATE_RANK:, GLA_DK:].set(gate_w_b)
    w_hi = w_gate.astype(BF16)
    w_lo = (w_gate - w_hi.astype(F32)).astype(BF16)
    w_gate3 = jnp.concatenate([w_hi, w_hi, w_lo, jnp.zeros((pad, 2 * GLA_DK), BF16)], axis=0)
    b_gate = jnp.concatenate([gate_b_f, gate_b_b])[None, :]
    return w.astype(BF16), b[None, :], w_gate3, b_gate


def kernel(x, meta, ln0_g, ln0_b, w_in, b_in, pool_w, pool_scale, gate_w_f, gate_b_f, gate_w_b, gate_b_b,
           gla_norm_g, w_branch_pool, w_branch_gla, w_out, ln1_g, ln1_b, w_router, router_bias,
           w_exp_gate, w_exp_up, w_exp_down, w_sh_gate, w_sh_up, w_sh_down, ln2_g, ln2_b):
    batch, seq, d = x.shape
    assert d == D_MODEL and seq % ROWS == 0 and w_in.shape[0] == 1
    assert (batch * seq) % ROUTE_ROWS == 0
    t = batch * seq
    row = lambda a: a.reshape(1, -1).astype(F32)

    lead = jnp.concatenate([jnp.zeros((FRONT - N_META, d), x.dtype), meta.astype(x.dtype)], axis=0)
    tiles_per_batch = (seq + FRONT) // ROWS
    x2 = x.reshape(t, d)

    w_packed, b_packed, w_gate, b_gate = _pack_in_proj(
        w_in[0], b_in[0], gate_w_f[0], gate_w_b[0], gate_b_f[0], gate_b_b[0])
    u, q, k, v, sg, gp, gg, bf, cb = _in_proj(
        lead, x2, row(ln0_g), row(ln0_b), w_packed, b_packed, w_gate, b_gate, tiles_per_batch)

    o_f, o_b = _gla(q, k, v, bf, cb, batch)

    h1, h1p = _mix(
        x.reshape(t, d), u, gp, gg, sg, o_f, o_b, row(ln0_g), row(ln0_b),
        pool_w[0].astype(BF16), row(pool_scale[0]), row(jnp.tile(gla_norm_g[0], GLA_HEADS)),
        w_branch_pool[0].astype(BF16), w_branch_gla[0].astype(BF16), w_out[0].astype(BF16),
        row(ln1_g[0]), row(ln1_b[0]), seq)

    eidx, ew, rank, counts = _route(h1, w_router[0].T, router_bias[0].reshape(N_EXPERTS, 1))

    counts = counts[:, 0]
    padded = ((counts + MOE_BLOCK - 1) // MOE_BLOCK) * MOE_BLOCK
    pend = jnp.cumsum(padded)
    pstart = pend - padded
    dest = _dest(eidx, rank, pstart.astype(F32).reshape(N_EXPERTS, 1))
    n_blocks = -(-(t * TOP_K) // MOE_BLOCK) + N_EXPERTS
    block_start = (jnp.concatenate([pstart, pend[-1:]]) // MOE_BLOCK).astype(I32)

    xs = _dispatch(dest, h1p, n_blocks * MOE_BLOCK)
    ys = _experts(block_start, xs, w_exp_gate[0], w_exp_up[0], w_exp_down[0])
    shared_w = (w_sh_gate[0].astype(BF16), w_sh_up[0].astype(BF16), w_sh_down[0].astype(BF16))
    ew_rows = ew.T
    windows = t // SC_CHUNK // COMBINE_CHUNKS
    out = None
    for c in range(COMBINE_CHUNKS):
        yg = _gather(dest[c * windows:(c + 1) * windows], ys)
        out = _combine(h1, ew_rows, yg, c * windows * SC_CHUNK // ROWS, out, *shared_w,
                       row(ln2_g[0]), row(ln2_b[0]))
    return out.reshape(batch, seq, d)
```

```python
import functools

import jax
import jax.numpy as jnp
from jax import lax
from jax.experimental import pallas as pl
from jax.experimental.pallas import tpu as pltpu
from jax.experimental.pallas import tpu_sc as plsc

F32 = jnp.float32
BF16 = jnp.bfloat16
I32 = jnp.int32
U32 = jnp.uint32

D_MODEL = 1024
N_META = 16
POOL_WINDOWS = (2, 4, 8, 16)
POOL_GROUP = 128
POOL_WIDTH = POOL_GROUP * len(POOL_WINDOWS)
GLA_HEADS = 4
GLA_DK = 512
GLA_DV = 1024
HEAD_K = GLA_DK // GLA_HEADS
HEAD_V = GLA_DV // GLA_HEADS
GATE_RANK = 16
GATE_NORMALIZER = 16.0
RMS_EPS = 1e-5
N_EXPERTS = 256
TOP_K = 8
N_GROUPS = 8
GROUP_SIZE = N_EXPERTS // N_GROUPS
TOPK_GROUPS = 4
D_EXPERT = 256
ROUTED_SCALE = 2.5
DN_ALPHA = 2.0 ** 0.25
LN_EPS = 1e-5

LANE = 128
ROWS = 256
FRONT = ROWS
GLA_CHUNK = 256
HALO = 16
ROUTE_ROWS = 512
MOE_BLOCK = 256
LOOKAHEAD = 6
GROUP = 4
IN_SLOTS = LOOKAHEAD + GROUP
OUT_SLOTS = GROUP
SC_CORES = 2
SC_SUBCORES = 16
SC_CHUNK = 128
COMBINE_CHUNKS = 4
VMEM_LIMIT = 56 * 1024 * 1024

C_POOL, C_Q, C_K, C_V, C_G, C_GP, C_GG, C_LR = 0, 512, 1024, 1536, 2560, 3584, 4608, 5632
W_PACKED = C_LR + LANE


def _layer_norm(x, g, b):
    mu = jnp.mean(x, axis=-1, keepdims=True)
    xc = x - mu
    var = jnp.mean(xc * xc, axis=-1, keepdims=True)
    return xc * lax.rsqrt(var + LN_EPS) * g + b


def _dot(a, b):
    return jnp.dot(a, b, preferred_element_type=F32)


def _dot_nt(a, b):
    return lax.dot_general(a, b, (((1,), (1,)), ((), ())), preferred_element_type=F32)


def _split_dot(m2, g):
    g_hi = g.astype(BF16)
    g_lo = (g - g_hi.astype(F32)).astype(BF16)
    return _dot(m2, jnp.concatenate([g_hi, g_lo], axis=0))


def _pack_halves(a, b):
    pa = lax.bitcast_convert_type(a.astype(BF16).astype(F32), U32)
    pb = lax.bitcast_convert_type(b.astype(BF16).astype(F32), U32)
    return pa | (pb >> 16)


def _unpack_halves(p):
    a = lax.bitcast_convert_type(p & jnp.uint32(0xFFFF0000), F32)
    b = lax.bitcast_convert_type(p << 16, F32)
    return a, b


def _const_spec(shape):
    return pl.BlockSpec(shape, lambda *_: (0,) * len(shape))


def _inproj_body(lead_ref, x_ref, g0_ref, b0_ref, w_ref, bias_ref, wgate_ref, bgate_ref,
                 u_ref, q_ref, k_ref, v_ref, sg_ref, gp_ref, gg_ref, bf_ref, cb_ref,
                 *, tiles_per_batch):
    i = pl.program_id(0)
    x_in = jnp.where(i % tiles_per_batch == 0, lead_ref[...], x_ref[...])
    h = _layer_norm(x_in, g0_ref[...], b0_ref[...])
    hb = h.astype(BF16)

    def proj(c0, n):
        return _dot(hb, w_ref[:, c0:c0 + n]) + bias_ref[:, c0:c0 + n]

    row = lax.broadcasted_iota(I32, (ROWS, 1), 0)
    valid = row >= jnp.where(i % tiles_per_batch == 0, FRONT - N_META, 0)

    u_ref[...] = proj(C_POOL, POOL_WIDTH).astype(BF16)
    q_ref[...] = (proj(C_Q, GLA_DK) * (HEAD_K ** -0.5)).astype(BF16)
    k_ref[...] = jnp.where(valid, proj(C_K, GLA_DK), 0.0).astype(BF16)
    v_ref[...] = jnp.where(valid, proj(C_V, GLA_DV), 0.0).astype(BF16)
    g = proj(C_G, GLA_DV)
    sg_ref[...] = (g * jax.nn.sigmoid(g)).astype(BF16)
    gp_ref[...] = jax.nn.sigmoid(proj(C_GP, D_MODEL)).astype(BF16)
    gg_ref[...] = jax.nn.sigmoid(proj(C_GG, D_MODEL)).astype(BF16)

    lr3 = proj(C_LR, LANE)
    lr_hi = lr3.astype(BF16)
    lr_lo = (lr3 - lr_hi.astype(F32)).astype(BF16)
    lane = lax.broadcasted_iota(I32, (1, LANE), 1)
    second = jnp.logical_and(lane >= 2 * GATE_RANK, lane < 4 * GATE_RANK)
    xg = _dot(jnp.where(second, lr_lo, lr_hi), wgate_ref[...]) + bgate_ref[...]
    gk = (jnp.minimum(xg, 0.0) - jnp.log(1.0 + jnp.exp(-jnp.abs(xg)))) * (1.0 / GATE_NORMALIZER)

    r = lax.broadcasted_iota(I32, (GLA_CHUNK, 2 * GLA_CHUNK), 0)
    c = lax.broadcasted_iota(I32, (GLA_CHUNK, 2 * GLA_CHUNK), 1) % GLA_CHUNK
    tril = (r >= c).astype(BF16)
    triu = (r <= c).astype(BF16)
    for ci in range(ROWS // GLA_CHUNK):
        sl = slice(ci * GLA_CHUNK, (ci + 1) * GLA_CHUNK)
        bf_ref[sl, :] = _split_dot(tril, gk[sl, :GLA_DK])
        cb_ref[sl, :] = _split_dot(triu, gk[sl, GLA_DK:])


def _in_proj(lead, x2, ln0_g, ln0_b, w_packed, b_packed, w_gate, b_gate, tiles_per_batch):
    x_tiles = tiles_per_batch - FRONT // ROWS
    n = x2.shape[0] // (x_tiles * ROWS) * tiles_per_batch * ROWS
    row_spec = lambda width: pl.BlockSpec((ROWS, width), lambda i: (i, 0))
    x_spec = pl.BlockSpec(
        (ROWS, D_MODEL),
        lambda i: ((i // tiles_per_batch) * x_tiles + jnp.maximum(i % tiles_per_batch - 1, 0), 0))
    out_widths = (POOL_WIDTH, GLA_DK, GLA_DK, GLA_DV, GLA_DV, D_MODEL, D_MODEL)
    out_shape = [jax.ShapeDtypeStruct((n, w), BF16) for w in out_widths]
    out_shape += [jax.ShapeDtypeStruct((n, GLA_DK), F32)] * 2
    out_specs = [row_spec(w) for w in out_widths] + [row_spec(GLA_DK)] * 2
    return pl.pallas_call(
        functools.partial(_inproj_body, tiles_per_batch=tiles_per_batch),
        grid=(n // ROWS,),
        in_specs=[_const_spec((FRONT, D_MODEL)), x_spec,
                  _const_spec((1, D_MODEL)), _const_spec((1, D_MODEL)),
                  _const_spec((D_MODEL, W_PACKED)), _const_spec((1, W_PACKED)),
                  _const_spec((LANE, 2 * GLA_DK)), _const_spec((1, 2 * GLA_DK))],
        out_specs=out_specs,
        out_shape=out_shape,
        compiler_params=pltpu.CompilerParams(dimension_semantics=("arbitrary",),
                                             vmem_limit_bytes=VMEM_LIMIT),
        name="in_proj",
    )(lead, x2, ln0_g, ln0_b, w_packed, b_packed, w_gate, b_gate)


def _gla_direction(q_ref, k_ref, v_ref, b_ref, o_ref, s_ref, reverse):
    c = GLA_CHUNK
    r = lax.broadcasted_iota(I32, (c, c), 0)
    s = lax.broadcasted_iota(I32, (c, c), 1)
    keep = (r <= s) if reverse else (r >= s)
    eye = lax.broadcasted_iota(I32, (HEAD_K, HEAD_K), 0) == lax.broadcasted_iota(I32, (HEAD_K, HEAD_K), 1)
    edge = 0 if reverse else c - 1
    for h in range(GLA_HEADS):
        ks = slice(h * HEAD_K, (h + 1) * HEAD_K)
        vs = slice(h * HEAD_V, (h + 1) * HEAD_V)
        b = b_ref[:, ks]
        qh = q_ref[:, ks].astype(F32)
        kh = k_ref[:, ks].astype(F32)
        vh = v_ref[:, vs]
        b_tot = b[edge:edge + 1, :]
        b_mid = b[c // 2:c // 2 + 1, :]
        q_state = (qh * jnp.exp(b)).astype(BF16)
        q_in = (qh * jnp.exp(b - b_mid)).astype(BF16)
        k_in = (kh * jnp.exp(b_mid - b)).astype(BF16)
        k_state = (kh * jnp.exp(b_tot - b)).T.astype(BF16)
        att = jnp.where(keep, _dot_nt(q_in, k_in), 0.0).astype(BF16)
        state = s_ref[h]
        o = _dot(q_state, state.astype(BF16)) + _dot(att, vh)
        o_ref[:, vs] = o.astype(o_ref.dtype)
        decay_col = jnp.sum(jnp.where(eye, jnp.exp(b_tot), 0.0), axis=1, keepdims=True)
        s_ref[h] = decay_col * state + _dot(k_state, vh)


def _gla_body(qf_ref, kf_ref, vf_ref, bf_ref, qb_ref, kb_ref, vb_ref, cb_ref,
              of_ref, ob_ref, sf_ref, sb_ref):
    @pl.when(pl.program_id(1) == 0)
    def _():
        sf_ref[...] = jnp.zeros_like(sf_ref)
        sb_ref[...] = jnp.zeros_like(sb_ref)

    _gla_direction(qf_ref, kf_ref, vf_ref, bf_ref, of_ref, sf_ref, reverse=False)
    _gla_direction(qb_ref, kb_ref, vb_ref, cb_ref, ob_ref, sb_ref, reverse=True)


def _gla(q, k, v, bf, cb, batch):
    n = q.shape[0]
    nch = n // batch // GLA_CHUNK
    fwd = lambda width: pl.BlockSpec((GLA_CHUNK, width), lambda b, i: (b * nch + i, 0))
    bwd = lambda width: pl.BlockSpec((GLA_CHUNK, width), lambda b, i: (b * nch + nch - 1 - i, 0))
    state = pltpu.VMEM((GLA_HEADS, HEAD_K, HEAD_V), F32)
    return pl.pallas_call(
        _gla_body,
        grid=(batch, nch),
        in_specs=[fwd(GLA_DK), fwd(GLA_DK), fwd(GLA_DV), fwd(GLA_DK),
                  bwd(GLA_DK), bwd(GLA_DK), bwd(GLA_DV), bwd(GLA_DK)],
        out_specs=[fwd(GLA_DV), bwd(GLA_DV)],
        out_shape=[jax.ShapeDtypeStruct((n, GLA_DV), BF16)] * 2,
        scratch_shapes=[state, state],
        compiler_params=pltpu.CompilerParams(dimension_semantics=("arbitrary", "arbitrary"),
                                             vmem_limit_bytes=VMEM_LIMIT),
        name="gla",
    )(q, k, v, bf, q, k, v, cb)


def _mix_body(x_ref, up_ref, u_ref, un_ref, gp_ref, gg_ref, sg_ref, of_ref, ob_ref,
              g0_ref, b0_ref, pw_ref, ps_ref, gn_ref, wbp_ref, wbg_ref, wo_ref, g1_ref, b1_ref,
              h1_ref, h1p_ref, *, tiles_per_batch, seq):
    j = pl.program_id(0) % tiles_per_batch
    u_main = u_ref[...]
    u_ext = jnp.concatenate([up_ref[...], u_main, un_ref[...]], axis=0)
    r = lax.broadcasted_iota(I32, (ROWS, ROWS + 2 * HALO), 0)
    e = lax.broadcasted_iota(I32, (ROWS, ROWS + 2 * HALO), 1) - HALO
    in_seq = (j * ROWS + e) < seq
    pos = j * ROWS + lax.broadcasted_iota(I32, (ROWS, 1), 0)
    y_pool = []
    for gi, w in enumerate(POOL_WINDOWS):
        cs = slice(gi * POOL_GROUP, (gi + 1) * POOL_GROUP)
        band = jnp.logical_and(jnp.logical_and(e >= r - w // 2, e < r + w - w // 2), in_seq)
        count = w - jnp.maximum(pos + (w - w // 2) - seq, 0)
        mean = _dot(band.astype(BF16), u_ext[:, cs]) / count.astype(F32)
        d = mean - u_main[:, cs].astype(F32)
        y_pool.append(_dot(d.astype(BF16), pw_ref[gi]))
    y_pool = (jnp.concatenate(y_pool, axis=1) * ps_ref[...]).astype(BF16)

    o = of_ref[...].astype(F32) + ob_ref[...].astype(F32)
    y_gla = []
    for h in range(GLA_HEADS):
        oh = o[:, h * HEAD_V:(h + 1) * HEAD_V]
        y_gla.append(oh * lax.rsqrt(jnp.mean(oh * oh, axis=-1, keepdims=True) + RMS_EPS))
    y_gla = (jnp.concatenate(y_gla, axis=1) * gn_ref[...] * sg_ref[...].astype(F32)).astype(BF16)

    merged = (gp_ref[...].astype(F32) * _dot(y_pool, wbp_ref[...])
              + gg_ref[...].astype(F32) * _dot(y_gla, wbg_ref[...]))
    y = _dot(merged.astype(BF16), wo_ref[...])
    h = _layer_norm(x_ref[...], g0_ref[...], b0_ref[...])
    h1 = _layer_norm(DN_ALPHA * h + y, g1_ref[...], b1_ref[...])
    h1_ref[...] = h1
    h1p_ref[...] = _pack_halves(h1[:, :D_MODEL // 2], h1[:, D_MODEL // 2:])


def _mix(x2, u, gp, gg, sg, o_f, o_b, ln0_g, ln0_b, pool_w, pool_scale, norm_g, wbp, wbg, wo,
         ln1_g, ln1_b, seq):
    t = x2.shape[0]
    tiles_per_batch = seq // ROWS
    padded_tiles = tiles_per_batch + FRONT // ROWS
    halo_per_tile = ROWS // HALO
    last_halo = u.shape[0] // HALO - 1

    def pidx(i):
        return (i // tiles_per_batch) * padded_tiles + FRONT // ROWS + i % tiles_per_batch

    padded = lambda width: pl.BlockSpec((ROWS, width), lambda i: (pidx(i), 0))
    plain = lambda width: pl.BlockSpec((ROWS, width), lambda i: (i, 0))
    prev_halo = pl.BlockSpec((HALO, POOL_WIDTH), lambda i: (pidx(i) * halo_per_tile - 1, 0))
    next_halo = pl.BlockSpec(
        (HALO, POOL_WIDTH), lambda i: (jnp.minimum((pidx(i) + 1) * halo_per_tile, last_halo), 0))
    return pl.pallas_call(
        functools.partial(_mix_body, tiles_per_batch=tiles_per_batch, seq=seq),
        grid=(t // ROWS,),
        in_specs=[plain(D_MODEL), prev_halo, padded(POOL_WIDTH), next_halo,
                  padded(D_MODEL), padded(D_MODEL), padded(GLA_DV), padded(GLA_DV), padded(GLA_DV),
                  _const_spec((1, D_MODEL)), _const_spec((1, D_MODEL)),
                  _const_spec((len(POOL_WINDOWS), POOL_GROUP, POOL_GROUP)), _const_spec((1, POOL_WIDTH)),
                  _const_spec((1, GLA_DV)), _const_spec((POOL_WIDTH, D_MODEL)),
                  _const_spec((GLA_DV, D_MODEL)), _const_spec((D_MODEL, D_MODEL)),
                  _const_spec((1, D_MODEL)), _const_spec((1, D_MODEL))],
        out_specs=[plain(D_MODEL), plain(D_MODEL // 2)],
        out_shape=[jax.ShapeDtypeStruct((t, D_MODEL), F32), jax.ShapeDtypeStruct((t, D_MODEL // 2), U32)],
        compiler_params=pltpu.CompilerParams(dimension_semantics=("arbitrary",),
                                             vmem_limit_bytes=VMEM_LIMIT),
        name="mix",
    )(x2, u, u, u, gp, gg, sg, o_f, o_b, ln0_g, ln0_b, pool_w, pool_scale, norm_g, wbp, wbg, wo,
      ln1_g, ln1_b)


def _first_index_of_max(val, idx, size):
    m = jnp.max(val, axis=0, keepdims=True)
    first = jnp.min(jnp.where(val == m, idx, size), axis=0, keepdims=True)
    return m, first


def _route_body(h_ref, wr_ref, bias_ref, eidx_ref, w_ref, rank_ref, cnt_ref, base_ref):
    n = ROUTE_ROWS

    @pl.when(pl.program_id(0) == 0)
    def _():
        base_ref[...] = jnp.zeros_like(base_ref)

    h = h_ref[...]
    h_hi = h.astype(BF16)
    h_mid = (h - h_hi.astype(F32)).astype(BF16)
    logits = _dot_nt(wr_ref[...], jnp.concatenate([h_hi, h_mid, h_hi], axis=1))
    scores = jax.nn.sigmoid(logits)
    sel = scores + bias_ref[...]

    gidx = lax.broadcasted_iota(I32, (GROUP_SIZE, n), 0)
    groups = [sel[g * GROUP_SIZE:(g + 1) * GROUP_SIZE, :] for g in range(N_GROUPS)]
    gscore = []
    for grp in groups:
        m1, first = _first_index_of_max(grp, gidx, GROUP_SIZE)
        m2 = jnp.max(jnp.where(gidx == first, -jnp.inf, grp), axis=0, keepdims=True)
        gscore.append(m1 + m2)

    masked = []
    for gi in range(N_GROUPS):
        beaten = jnp.zeros((1, n), F32)
        for gj in range(N_GROUPS):
            if gj != gi:
                wins = (gscore[gj] >= gscore[gi]) if gj < gi else (gscore[gj] > gscore[gi])
                beaten = beaten + wins.astype(F32)
        masked.append(jnp.where(beaten < TOPK_GROUPS, groups[gi], -jnp.inf))
    val = jnp.concatenate(masked, axis=0)

    eidx = lax.broadcasted_iota(I32, (N_EXPERTS, n), 0)
    chosen_f = jnp.zeros((N_EXPERTS, n), F32)
    picks, weights = [], []
    for _ in range(TOP_K):
        _, first = _first_index_of_max(val, eidx, N_EXPERTS)
        hit = eidx == first
        picks.append(first)
        weights.append(jnp.sum(jnp.where(hit, scores, 0.0), axis=0, keepdims=True))
        chosen_f = jnp.where(hit, 1.0, chosen_f)
        val = jnp.where(hit, -jnp.inf, val)
    wsum = weights[0]
    for wk in weights[1:]:
        wsum = wsum + wk
    scale = ROUTED_SCALE / wsum

    r = lax.broadcasted_iota(I32, (n, n), 0)
    c = lax.broadcasted_iota(I32, (n, n), 1)
    earlier = (r < c).astype(BF16)
    before = _dot(chosen_f.astype(BF16), earlier) + base_ref[...]
    base_ref[...] = base_ref[...] + jnp.sum(chosen_f, axis=1, keepdims=True)
    cnt_ref[...] = jnp.broadcast_to(base_ref[...], cnt_ref.shape).astype(I32)

    for kk in range(TOP_K):
        eidx_ref[kk:kk + 1, :] = picks[kk]
        w_ref[kk:kk + 1, :] = weights[kk] * scale
        rank = jnp.sum(jnp.where(eidx == picks[kk], before, 0.0), axis=0, keepdims=True)
        rank_ref[kk:kk + 1, :] = rank.astype(I32)


def _route(h1, wr_t, bias_col):
    t = h1.shape[0]
    col = lambda dtype: jax.ShapeDtypeStruct((TOP_K, t), dtype)
    kspec = pl.BlockSpec((TOP_K, ROUTE_ROWS), lambda i: (0, i))
    return pl.pallas_call(
        _route_body,
        grid=(t // ROUTE_ROWS,),
        in_specs=[pl.BlockSpec((ROUTE_ROWS, D_MODEL), lambda i: (i, 0)),
                  _const_spec((N_EXPERTS, 3 * D_MODEL)), _const_spec((N_EXPERTS, 1))],
        out_specs=[kspec, kspec, kspec, _const_spec((N_EXPERTS, LANE))],
        out_shape=[col(I32), col(F32), col(I32), jax.ShapeDtypeStruct((N_EXPERTS, LANE), I32)],
        scratch_shapes=[pltpu.VMEM((N_EXPERTS, 1), F32)],
        compiler_params=pltpu.CompilerParams(dimension_semantics=("arbitrary",),
                                             vmem_limit_bytes=VMEM_LIMIT),
        name="route",
    )(h1, wr_t, bias_col)


def _dest_body(eidx_ref, rank_ref, pstart_ref, dest_ref):
    n = eidx_ref.shape[1]
    eiota = lax.broadcasted_iota(I32, (N_EXPERTS, n), 0)
    pstart = pstart_ref[...]
    for kk in range(TOP_K):
        start = jnp.sum(jnp.where(eiota == eidx_ref[kk:kk + 1, :], pstart, 0.0), axis=0, keepdims=True)
        dest = start.astype(I32) + rank_ref[kk:kk + 1, :]
        for c in range(n // SC_CHUNK):
            dest_ref[c, kk:kk + 1, :] = dest[:, c * SC_CHUNK:(c + 1) * SC_CHUNK]


def _dest(eidx, rank, pstart_col):
    t = eidx.shape[1]
    kspec = pl.BlockSpec((TOP_K, ROUTE_ROWS), lambda i: (0, i))
    chunks = ROUTE_ROWS // SC_CHUNK
    return pl.pallas_call(
        _dest_body,
        grid=(t // ROUTE_ROWS,),
        in_specs=[kspec, kspec, _const_spec((N_EXPERTS, 1))],
        out_specs=pl.BlockSpec((chunks, TOP_K, SC_CHUNK), lambda i: (i, 0, 0)),
        out_shape=jax.ShapeDtypeStruct((t // SC_CHUNK, TOP_K, SC_CHUNK), I32),
        compiler_params=pltpu.CompilerParams(dimension_semantics=("arbitrary",)),
        name="dest",
    )(eidx, rank, pstart_col)


def _sc_mesh():
    return plsc.VectorSubcoreMesh(core_axis_name="c", subcore_axis_name="s",
                                  num_cores=SC_CORES, num_subcores=SC_SUBCORES)


def _sc_worker_chunks(t):
    assert t % (SC_CORES * SC_SUBCORES * SC_CHUNK) == 0
    per_worker = t // (SC_CORES * SC_SUBCORES * SC_CHUNK)
    worker = lax.axis_index("s") * SC_CORES + lax.axis_index("c")
    return worker * per_worker, per_worker


def _dispatch(dest, h1p, n_rows):
    t, width = h1p.shape

    @functools.partial(
        pl.kernel, mesh=_sc_mesh(), out_type=jax.ShapeDtypeStruct((n_rows, width), U32),
        scratch_types=[pltpu.VMEM((SC_CHUNK, width), U32), pltpu.VMEM((TOP_K, SC_CHUNK), I32),
                       pltpu.SemaphoreType.DMA],
        name="dispatch")
    def body(h_hbm, dest_hbm, xs_hbm, rows_v, idx_v, sem):
        first, count = _sc_worker_chunks(t)

        @pl.loop(0, count)
        def _(i):
            chunk = first + i
            pltpu.sync_copy(h_hbm.at[pl.ds(pl.multiple_of(chunk * SC_CHUNK, SC_CHUNK), SC_CHUNK)], rows_v)
            pltpu.sync_copy(dest_hbm.at[chunk], idx_v)
            copies = [pltpu.async_copy(rows_v, xs_hbm.at[idx_v.at[kk]], sem) for kk in range(TOP_K)]
            for c in copies:
                c.wait()

    return body(h1p, dest)


def _gather(dest, ys):
    t = dest.shape[0] * SC_CHUNK
    width = ys.shape[1]

    @functools.partial(
        pl.kernel, mesh=_sc_mesh(), out_type=jax.ShapeDtypeStruct((TOP_K, t, width), U32),
        scratch_types=[pltpu.VMEM((SC_CHUNK, width), U32), pltpu.VMEM((TOP_K, SC_CHUNK), I32),
                       pltpu.SemaphoreType.DMA],
        name="gather")
    def body(ys_hbm, dest_hbm, yg_hbm, rows_v, idx_v, sem):
        first, count = _sc_worker_chunks(t)

        @pl.loop(0, count)
        def _(i):
            chunk = first + i
            pltpu.sync_copy(dest_hbm.at[chunk], idx_v)
            for kk in range(TOP_K):
                pltpu.async_copy(ys_hbm.at[idx_v.at[kk]], rows_v, sem).wait()
                pltpu.sync_copy(
                    rows_v, yg_hbm.at[kk, pl.ds(pl.multiple_of(chunk * SC_CHUNK, SC_CHUNK), SC_CHUNK)])

    return body(ys, dest)


def _experts_body(bs_ref, xs_in_ref, wg_ref, wu_ref, wd_ref, xs_ref,
                  xbuf, ybuf, wgb, wub, wdb, sem_in, sem_out):
    del xs_in_ref
    e = pl.program_id(0)
    b0 = bs_ref[e]
    b1 = bs_ref[e + 1]
    n_total = bs_ref[N_EXPERTS]
    half = D_MODEL // 2

    def fetch(b):
        slot = b % IN_SLOTS
        return pltpu.make_async_copy(xs_ref.at[pl.ds(b * MOE_BLOCK, MOE_BLOCK)], xbuf.at[slot],
                                     sem_in.at[slot])

    def store(b):
        slot = b % OUT_SLOTS
        return pltpu.make_async_copy(ybuf.at[slot], xs_ref.at[pl.ds(b * MOE_BLOCK, MOE_BLOCK)],
                                     sem_out.at[slot])

    @pl.when(e == 0)
    def _():
        for b in range(LOOKAHEAD):
            @pl.when(b < n_total)
            def _():
                fetch(b).start(priority=1)

    @pl.when(b1 > b0)
    def _():
        wgb[...] = wg_ref[0].astype(BF16)
        wub[...] = wu_ref[0].astype(BF16)
        wdb[...] = wd_ref[0].astype(BF16)

    def run(blocks):
        for b in blocks:
            @pl.when(b + LOOKAHEAD < n_total)
            def _():
                fetch(b + LOOKAHEAD).start(priority=1)
        for b in blocks:
            fetch(b).wait()
        xa, xb = _unpack_halves(jnp.concatenate([xbuf[b % IN_SLOTS] for b in blocks], axis=0))
        xa = xa.astype(BF16)
        xb = xb.astype(BF16)
        gate = _dot(xa, wgb[:half, :]) + _dot(xb, wgb[half:, :])
        up = _dot(xa, wub[:half, :]) + _dot(xb, wub[half:, :])
        hidden = (gate * jax.nn.sigmoid(gate) * up).astype(BF16)
        y = _dot(hidden, wdb[...])
        packed = _pack_halves(y[:, :half], y[:, half:])
        for i, b in enumerate(blocks):
            @pl.when(b >= OUT_SLOTS)
            def _():
                store(b - OUT_SLOTS).wait()

            ybuf[b % OUT_SLOTS] = packed[i * MOE_BLOCK:(i + 1) * MOE_BLOCK]
            store(b).start(priority=1)

    n = b1 - b0

    def group(i, carry):
        run([b0 + GROUP * i + j for j in range(GROUP)])
        return carry

    lax.fori_loop(0, n // GROUP, group, 0)
    rest = b0 + (n // GROUP) * GROUP

    @pl.when(n % GROUP >= 2)
    def _():
        run([rest, rest + 1])

    @pl.when(n % 2 == 1)
    def _():
        run([b1 - 1])

    @pl.when(e == N_EXPERTS - 1)
    def _():
        for back in range(OUT_SLOTS, 0, -1):
            @pl.when(n_total >= back)
            def _():
                store(n_total - back).wait()


def _experts(block_start, xs, w_gate, w_up, w_down):
    n_rows, width = xs.shape
    wspec = lambda shape: pl.BlockSpec((1,) + shape, lambda e, bs: (e, 0, 0))
    grid_spec = pltpu.PrefetchScalarGridSpec(
        num_scalar_prefetch=1,
        grid=(N_EXPERTS,),
        in_specs=[pl.BlockSpec(memory_space=pl.ANY),
                  wspec((D_MODEL, D_EXPERT)), wspec((D_MODEL, D_EXPERT)), wspec((D_EXPERT, D_MODEL))],
        out_specs=pl.BlockSpec(memory_space=pl.ANY),
        scratch_shapes=[pltpu.VMEM((IN_SLOTS, MOE_BLOCK, width), U32),
                        pltpu.VMEM((OUT_SLOTS, MOE_BLOCK, width), U32),
                        pltpu.VMEM((D_MODEL, D_EXPERT), BF16), pltpu.VMEM((D_MODEL, D_EXPERT), BF16),
                        pltpu.VMEM((D_EXPERT, D_MODEL), BF16),
                        pltpu.SemaphoreType.DMA((IN_SLOTS,)), pltpu.SemaphoreType.DMA((OUT_SLOTS,))],
    )
    return pl.pallas_call(
        _experts_body,
        grid_spec=grid_spec,
        out_shape=jax.ShapeDtypeStruct((n_rows, width), U32),
        input_output_aliases={1: 0},
        compiler_params=pltpu.CompilerParams(dimension_semantics=("arbitrary",),
                                             vmem_limit_bytes=VMEM_LIMIT),
        name="experts",
    )(block_start, xs, w_gate, w_up, w_down)


def _combine_body(h1_ref, w_ref, yg_ref, wsg_ref, wsu_ref, wsd_ref, g2_ref, b2_ref, *refs):
    out_ref = refs[-1]
    h1 = h1_ref[...]
    hb = h1.astype(BF16)
    gate = _dot(hb, wsg_ref[...])
    up = _dot(hb, wsu_ref[...])
    shared = _dot((gate * jax.nn.sigmoid(gate) * up).astype(BF16), wsd_ref[...])

    half = D_MODEL // 2
    lo = jnp.zeros((ROWS, half), F32)
    hi = jnp.zeros((ROWS, half), F32)
    for kk in range(TOP_K):
        a, b = _unpack_halves(yg_ref[kk])
        wk = w_ref[:, kk:kk + 1]
        lo = lo + wk * a
        hi = hi + wk * b
    z = DN_ALPHA * h1 + shared + jnp.concatenate([lo, hi], axis=1)
    out_ref[...] = _layer_norm(z, g2_ref[...], b2_ref[...])


def _combine(h1, w_rows, yg, first_tile, out_prev, wsg, wsu, wsd, ln2_g, ln2_b):
    width = yg.shape[2]
    d_shared = wsg.shape[1]
    rows = lambda cols: pl.BlockSpec((ROWS, cols), lambda i: (first_tile + i, 0))
    args = [h1, w_rows, yg, wsg, wsu, wsd, ln2_g, ln2_b]
    in_specs = [rows(D_MODEL), rows(TOP_K),
                pl.BlockSpec((TOP_K, ROWS, width), lambda i: (0, i, 0)),
                _const_spec((D_MODEL, d_shared)), _const_spec((D_MODEL, d_shared)),
                _const_spec((d_shared, D_MODEL)),
                _const_spec((1, D_MODEL)), _const_spec((1, D_MODEL))]
    aliases = {}
    if out_prev is not None:
        aliases = {len(args): 0}
        args.append(out_prev)
        in_specs.append(pl.BlockSpec(memory_space=pl.ANY))
    return pl.pallas_call(
        _combine_body,
        grid=(yg.shape[1] // ROWS,),
        in_specs=in_specs,
        out_specs=rows(D_MODEL),
        out_shape=jax.ShapeDtypeStruct(h1.shape, F32),
        input_output_aliases=aliases,
        compiler_params=pltpu.CompilerParams(dimension_semantics=("arbitrary",),
                                             vmem_limit_bytes=VMEM_LIMIT),
        name="combine",
    )(*args)


def _pack_in_proj(w_in, b_in, gate_w_f, gate_w_b, gate_b_f, gate_b_b):
    o_pool, o_q, o_k, o_v, o_g = 0, 512, 1024, 1536, 2560
    o_lr, o_gp, o_gg = 3584, 3616, 4640
    order = [(o_pool, POOL_WIDTH), (o_q, GLA_DK), (o_k, GLA_DK), (o_v, GLA_DV), (o_g, GLA_DV),
             (o_gp, D_MODEL), (o_gg, D_MODEL)] + [(o_lr, 2 * GATE_RANK)] * 3
    pad = LANE - 6 * GATE_RANK
    w = jnp.concatenate([w_in[:, o:o + n] for o, n in order] + [jnp.zeros((D_MODEL, pad), F32)], axis=1)
    b = jnp.concatenate([b_in[o:o + n] for o, n in order] + [jnp.zeros((pad,), F32)])
    w_gate = jnp.zeros((2 * GATE_RANK, 2 * GLA_DK), F32)
    w_gate = w_gate.at[:GATE_RANK, :GLA_DK].set(gate_w_f)
    w_gate = w_gate.at[GATE_RANK:, GLA_DK:].set(gate_w_b)
    w_hi = w_gate.astype(BF16)
    w_lo = (w_gate - w_hi.astype(F32)).astype(BF16)
    w_gate3 = jnp.concatenate([w_hi, w_hi, w_lo, jnp.zeros((pad, 2 * GLA_DK), BF16)], axis=0)
    b_gate = jnp.concatenate([gate_b_f, gate_b_b])[None, :]
    return w.astype(BF16), b[None, :], w_gate3, b_gate


def kernel(x, meta, ln0_g, ln0_b, w_in, b_in, pool_w, pool_scale, gate_w_f, gate_b_f, gate_w_b, gate_b_b,
           gla_norm_g, w_branch_pool, w_branch_gla, w_out, ln1_g, ln1_b, w_router, router_bias,
           w_exp_gate, w_exp_up, w_exp_down, w_sh_gate, w_sh_up, w_sh_down, ln2_g, ln2_b):
    batch, seq, d = x.shape
    assert d == D_MODEL and seq % ROWS == 0 and w_in.shape[0] == 1
    assert (batch * seq) % ROUTE_ROWS == 0
    t = batch * seq
    row = lambda a: a.reshape(1, -1).astype(F32)

    lead = jnp.concatenate([jnp.zeros((FRONT - N_META, d), x.dtype), meta.astype(x.dtype)], axis=0)
    tiles_per_batch = (seq + FRONT) // ROWS
    x2 = x.reshape(t, d)

    w_packed, b_packed, w_gate, b_gate = _pack_in_proj(
        w_in[0], b_in[0], gate_w_f[0], gate_w_b[0], gate_b_f[0], gate_b_b[0])
    u, q, k, v, sg, gp, gg, bf, cb = _in_proj(
        lead, x2, row(ln0_g), row(ln0_b), w_packed, b_packed, w_gate, b_gate, tiles_per_batch)

    o_f, o_b = _gla(q, k, v, bf, cb, batch)

    h1, h1p = _mix(
        x.reshape(t, d), u, gp, gg, sg, o_f, o_b, row(ln0_g), row(ln0_b),
        pool_w[0].astype(BF16), row(pool_scale[0]), row(jnp.tile(gla_norm_g[0], GLA_HEADS)),
        w_branch_pool[0].astype(BF16), w_branch_gla[0].astype(BF16), w_out[0].astype(BF16),
        row(ln1_g[0]), row(ln1_b[0]), seq)

    wr = w_router[0].T.astype(F32)
    wr_hi = wr.astype(BF16)
    wr_mid = (wr - wr_hi.astype(F32)).astype(BF16)
    wr_terms = jnp.concatenate([wr_hi, wr_hi, wr_mid], axis=1)
    eidx, ew, rank, counts = _route(h1, wr_terms, router_bias[0].reshape(N_EXPERTS, 1))

    counts = counts[:, 0]
    padded = ((counts + MOE_BLOCK - 1) // MOE_BLOCK) * MOE_BLOCK
    pend = jnp.cumsum(padded)
    pstart = pend - padded
    dest = _dest(eidx, rank, pstart.astype(F32).reshape(N_EXPERTS, 1))
    n_blocks = -(-(t * TOP_K) // MOE_BLOCK) + N_EXPERTS
    block_start = (jnp.concatenate([pstart, pend[-1:]]) // MOE_BLOCK).astype(I32)

    xs = _dispatch(dest, h1p, n_blocks * MOE_BLOCK)
    ys = _experts(block_start, xs, w_exp_gate[0], w_exp_up[0], w_exp_down[0])
    shared_w = (w_sh_gate[0].astype(BF16), w_sh_up[0].astype(BF16), w_sh_down[0].astype(BF16))
    ew_rows = ew.T
    windows = t // SC_CHUNK // COMBINE_CHUNKS
    out = None
    for c in range(COMBINE_CHUNKS):
        yg = _gather(dest[c * windows:(c + 1) * windows], ys)
        out = _combine(h1, ew_rows, yg, c * windows * SC_CHUNK // ROWS, out, *shared_w,
                       row(ln2_g[0]), row(ln2_b[0]))
    return out.reshape(batch, seq, d)
```

```python
import functools

import jax
import jax.numpy as jnp
from jax import lax
from jax.experimental import pallas as pl
from jax.experimental.pallas import tpu as pltpu
from jax.experimental.pallas import tpu_sc as plsc

F32 = jnp.float32
BF16 = jnp.bfloat16
I32 = jnp.int32
U32 = jnp.uint32

D_MODEL = 1024
N_META = 16
POOL_WINDOWS = (2, 4, 8, 16)
POOL_GROUP = 128
POOL_WIDTH = POOL_GROUP * len(POOL_WINDOWS)
GLA_HEADS = 4
GLA_DK = 512
GLA_DV = 1024
HEAD_K = GLA_DK // GLA_HEADS
HEAD_V = GLA_DV // GLA_HEADS
GATE_RANK = 16
GATE_NORMALIZER = 16.0
RMS_EPS = 1e-5
N_EXPERTS = 256
TOP_K = 8
N_GROUPS = 8
GROUP_SIZE = N_EXPERTS // N_GROUPS
TOPK_GROUPS = 4
D_EXPERT = 256
ROUTED_SCALE = 2.5
DN_ALPHA = 2.0 ** 0.25
LN_EPS = 1e-5

LANE = 128
ROWS = 256
FRONT = ROWS
GLA_CHUNK = 256
HALO = 16
ROUTE_ROWS = 512
MOE_BLOCK = 256
LOOKAHEAD = 6
GROUP = 4
IN_SLOTS = LOOKAHEAD + GROUP
OUT_SLOTS = GROUP
SC_CORES = 2
SC_SUBCORES = 16
SC_CHUNK = 128
COMBINE_CHUNKS = 8
VMEM_LIMIT = 56 * 1024 * 1024

C_POOL, C_Q, C_K, C_V, C_G, C_GP, C_GG, C_LR = 0, 512, 1024, 1536, 2560, 3584, 4608, 5632
W_PACKED = C_LR + LANE


def _layer_norm(x, g, b):
    mu = jnp.mean(x, axis=-1, keepdims=True)
    xc = x - mu
    var = jnp.mean(xc * xc, axis=-1, keepdims=True)
    return xc * lax.rsqrt(var + LN_EPS) * g + b


def _dot(a, b):
    return jnp.dot(a, b, preferred_element_type=F32)


def _dot_nt(a, b):
    return lax.dot_general(a, b, (((1,), (1,)), ((), ())), preferred_element_type=F32)


def _split_dot(m2, g):
    g_hi = g.astype(BF16)
    g_lo = (g - g_hi.astype(F32)).astype(BF16)
    return _dot(m2, jnp.concatenate([g_hi, g_lo], axis=0))


def _pack_halves(a, b):
    pa = lax.bitcast_convert_type(a.astype(BF16).astype(F32), U32)
    pb = lax.bitcast_convert_type(b.astype(BF16).astype(F32), U32)
    return pa | (pb >> 16)


def _unpack_halves(p):
    a = lax.bitcast_convert_type(p & jnp.uint32(0xFFFF0000), F32)
    b = lax.bitcast_convert_type(p << 16, F32)
    return a, b


def _const_spec(shape):
    return pl.BlockSpec(shape, lambda *_: (0,) * len(shape))


def _inproj_body(lead_ref, x_ref, g0_ref, b0_ref, w_ref, bias_ref, wgate_ref, bgate_ref,
                 u_ref, q_ref, k_ref, v_ref, sg_ref, gp_ref, gg_ref, bf_ref, cb_ref,
                 *, tiles_per_batch):
    i = pl.program_id(0)
    x_in = jnp.where(i % tiles_per_batch == 0, lead_ref[...], x_ref[...])
    h = _layer_norm(x_in, g0_ref[...], b0_ref[...])
    hb = h.astype(BF16)

    def proj(c0, n):
        return _dot(hb, w_ref[:, c0:c0 + n]) + bias_ref[:, c0:c0 + n]

    row = lax.broadcasted_iota(I32, (ROWS, 1), 0)
    valid = row >= jnp.where(i % tiles_per_batch == 0, FRONT - N_META, 0)

    u_ref[...] = proj(C_POOL, POOL_WIDTH).astype(BF16)
    q_ref[...] = (proj(C_Q, GLA_DK) * (HEAD_K ** -0.5)).astype(BF16)
    k_ref[...] = jnp.where(valid, proj(C_K, GLA_DK), 0.0).astype(BF16)
    v_ref[...] = jnp.where(valid, proj(C_V, GLA_DV), 0.0).astype(BF16)
    g = proj(C_G, GLA_DV)
    sg_ref[...] = (g * jax.nn.sigmoid(g)).astype(BF16)
    gp_ref[...] = jax.nn.sigmoid(proj(C_GP, D_MODEL)).astype(BF16)
    gg_ref[...] = jax.nn.sigmoid(proj(C_GG, D_MODEL)).astype(BF16)

    lr3 = proj(C_LR, LANE)
    lr_hi = lr3.astype(BF16)
    lr_lo = (lr3 - lr_hi.astype(F32)).astype(BF16)
    lane = lax.broadcasted_iota(I32, (1, LANE), 1)
    second = jnp.logical_and(lane >= 2 * GATE_RANK, lane < 4 * GATE_RANK)
    xg = _dot(jnp.where(second, lr_lo, lr_hi), wgate_ref[...]) + bgate_ref[...]
    gk = (jnp.minimum(xg, 0.0) - jnp.log(1.0 + jnp.exp(-jnp.abs(xg)))) * (1.0 / GATE_NORMALIZER)

    r = lax.broadcasted_iota(I32, (GLA_CHUNK, 2 * GLA_CHUNK), 0)
    c = lax.broadcasted_iota(I32, (GLA_CHUNK, 2 * GLA_CHUNK), 1) % GLA_CHUNK
    tril = (r >= c).astype(BF16)
    triu = (r <= c).astype(BF16)
    for ci in range(ROWS // GLA_CHUNK):
        sl = slice(ci * GLA_CHUNK, (ci + 1) * GLA_CHUNK)
        bf_ref[sl, :] = _split_dot(tril, gk[sl, :GLA_DK])
        cb_ref[sl, :] = _split_dot(triu, gk[sl, GLA_DK:])


def _in_proj(lead, x2, ln0_g, ln0_b, w_packed, b_packed, w_gate, b_gate, tiles_per_batch):
    x_tiles = tiles_per_batch - FRONT // ROWS
    n = x2.shape[0] // (x_tiles * ROWS) * tiles_per_batch * ROWS
    row_spec = lambda width: pl.BlockSpec((ROWS, width), lambda i: (i, 0))
    x_spec = pl.BlockSpec(
        (ROWS, D_MODEL),
        lambda i: ((i // tiles_per_batch) * x_tiles + jnp.maximum(i % tiles_per_batch - 1, 0), 0))
    out_widths = (POOL_WIDTH, GLA_DK, GLA_DK, GLA_DV, GLA_DV, D_MODEL, D_MODEL)
    out_shape = [jax.ShapeDtypeStruct((n, w), BF16) for w in out_widths]
    out_shape += [jax.ShapeDtypeStruct((n, GLA_DK), F32)] * 2
    out_specs = [row_spec(w) for w in out_widths] + [row_spec(GLA_DK)] * 2
    return pl.pallas_call(
        functools.partial(_inproj_body, tiles_per_batch=tiles_per_batch),
        grid=(n // ROWS,),
        in_specs=[_const_spec((FRONT, D_MODEL)), x_spec,
                  _const_spec((1, D_MODEL)), _const_spec((1, D_MODEL)),
                  _const_spec((D_MODEL, W_PACKED)), _const_spec((1, W_PACKED)),
                  _const_spec((LANE, 2 * GLA_DK)), _const_spec((1, 2 * GLA_DK))],
        out_specs=out_specs,
        out_shape=out_shape,
        compiler_params=pltpu.CompilerParams(dimension_semantics=("arbitrary",),
                                             vmem_limit_bytes=VMEM_LIMIT),
        name="in_proj",
    )(lead, x2, ln0_g, ln0_b, w_packed, b_packed, w_gate, b_gate)


def _gla_direction(q_ref, k_ref, v_ref, b_ref, o_ref, s_ref, reverse):
    c = GLA_CHUNK
    r = lax.broadcasted_iota(I32, (c, c), 0)
    s = lax.broadcasted_iota(I32, (c, c), 1)
    keep = (r <= s) if reverse else (r >= s)
    eye = lax.broadcasted_iota(I32, (HEAD_K, HEAD_K), 0) == lax.broadcasted_iota(I32, (HEAD_K, HEAD_K), 1)
    edge = 0 if reverse else c - 1
    for h in range(GLA_HEADS):
        ks = slice(h * HEAD_K, (h + 1) * HEAD_K)
        vs = slice(h * HEAD_V, (h + 1) * HEAD_V)
        b = b_ref[:, ks]
        qh = q_ref[:, ks].astype(F32)
        kh = k_ref[:, ks].astype(F32)
        vh = v_ref[:, vs]
        b_tot = b[edge:edge + 1, :]
        b_mid = b[c // 2:c // 2 + 1, :]
        q_state = (qh * jnp.exp(b)).astype(BF16)
        q_in = (qh * jnp.exp(b - b_mid)).astype(BF16)
        k_in = (kh * jnp.exp(b_mid - b)).astype(BF16)
        k_state = (kh * jnp.exp(b_tot - b)).T.astype(BF16)
        att = jnp.where(keep, _dot_nt(q_in, k_in), 0.0).astype(BF16)
        state = s_ref[h]
        o = _dot(q_state, state.astype(BF16)) + _dot(att, vh)
        o_ref[:, vs] = o.astype(o_ref.dtype)
        decay_col = jnp.sum(jnp.where(eye, jnp.exp(b_tot), 0.0), axis=1, keepdims=True)
        s_ref[h] = decay_col * state + _dot(k_state, vh)


def _gla_body(qf_ref, kf_ref, vf_ref, bf_ref, qb_ref, kb_ref, vb_ref, cb_ref,
              of_ref, ob_ref, sf_ref, sb_ref):
    @pl.when(pl.program_id(1) == 0)
    def _():
        sf_ref[...] = jnp.zeros_like(sf_ref)
        sb_ref[...] = jnp.zeros_like(sb_ref)

    _gla_direction(qf_ref, kf_ref, vf_ref, bf_ref, of_ref, sf_ref, reverse=False)
    _gla_direction(qb_ref, kb_ref, vb_ref, cb_ref, ob_ref, sb_ref, reverse=True)


def _gla(q, k, v, bf, cb, batch):
    n = q.shape[0]
    nch = n // batch // GLA_CHUNK
    fwd = lambda width: pl.BlockSpec((GLA_CHUNK, width), lambda b, i: (b * nch + i, 0))
    bwd = lambda width: pl.BlockSpec((GLA_CHUNK, width), lambda b, i: (b * nch + nch - 1 - i, 0))
    state = pltpu.VMEM((GLA_HEADS, HEAD_K, HEAD_V), F32)
    return pl.pallas_call(
        _gla_body,
        grid=(batch, nch),
        in_specs=[fwd(GLA_DK), fwd(GLA_DK), fwd(GLA_DV), fwd(GLA_DK),
                  bwd(GLA_DK), bwd(GLA_DK), bwd(GLA_DV), bwd(GLA_DK)],
        out_specs=[fwd(GLA_DV), bwd(GLA_DV)],
        out_shape=[jax.ShapeDtypeStruct((n, GLA_DV), BF16)] * 2,
        scratch_shapes=[state, state],
        compiler_params=pltpu.CompilerParams(dimension_semantics=("arbitrary", "arbitrary"),
                                             vmem_limit_bytes=VMEM_LIMIT),
        name="gla",
    )(q, k, v, bf, q, k, v, cb)


def _mix_body(x_ref, up_ref, u_ref, un_ref, gp_ref, gg_ref, sg_ref, of_ref, ob_ref,
              g0_ref, b0_ref, pw_ref, ps_ref, gn_ref, wbp_ref, wbg_ref, wo_ref, g1_ref, b1_ref,
              h1_ref, h1p_ref, *, tiles_per_batch, seq):
    j = pl.program_id(0) % tiles_per_batch
    u_main = u_ref[...]
    u_ext = jnp.concatenate([up_ref[...], u_main, un_ref[...]], axis=0)
    r = lax.broadcasted_iota(I32, (ROWS, ROWS + 2 * HALO), 0)
    e = lax.broadcasted_iota(I32, (ROWS, ROWS + 2 * HALO), 1) - HALO
    in_seq = (j * ROWS + e) < seq
    pos = j * ROWS + lax.broadcasted_iota(I32, (ROWS, 1), 0)
    y_pool = []
    for gi, w in enumerate(POOL_WINDOWS):
        cs = slice(gi * POOL_GROUP, (gi + 1) * POOL_GROUP)
        band = jnp.logical_and(jnp.logical_and(e >= r - w // 2, e < r + w - w // 2), in_seq)
        count = w - jnp.maximum(pos + (w - w // 2) - seq, 0)
        mean = _dot(band.astype(BF16), u_ext[:, cs]) / count.astype(F32)
        d = mean - u_main[:, cs].astype(F32)
        y_pool.append(_dot(d.astype(BF16), pw_ref[gi]))
    y_pool = (jnp.concatenate(y_pool, axis=1) * ps_ref[...]).astype(BF16)

    o = of_ref[...].astype(F32) + ob_ref[...].astype(F32)
    y_gla = []
    for hd in range(GLA_HEADS):
        oh = o[:, hd * HEAD_V:(hd + 1) * HEAD_V]
        y_gla.append(oh * lax.rsqrt(jnp.mean(oh * oh, axis=-1, keepdims=True) + RMS_EPS))
    y_gla = (jnp.concatenate(y_gla, axis=1) * gn_ref[...] * sg_ref[...].astype(F32)).astype(BF16)

    merged = (gp_ref[...].astype(F32) * _dot(y_pool, wbp_ref[...])
              + gg_ref[...].astype(F32) * _dot(y_gla, wbg_ref[...]))
    y = _dot(merged.astype(BF16), wo_ref[...])
    h = _layer_norm(x_ref[...], g0_ref[...], b0_ref[...])
    h1 = _layer_norm(DN_ALPHA * h + y, g1_ref[...], b1_ref[...])
    h1_ref[...] = h1
    h1p_ref[...] = _pack_halves(h1[:, :D_MODEL // 2], h1[:, D_MODEL // 2:])


def _mix(x2, u, gp, gg, sg, o_f, o_b, ln0_g, ln0_b, pool_w, pool_scale, norm_g, wbp, wbg, wo,
         ln1_g, ln1_b, seq):
    t = x2.shape[0]
    tiles_per_batch = seq // ROWS
    padded_tiles = tiles_per_batch + FRONT // ROWS
    halo_per_tile = ROWS // HALO
    last_halo = u.shape[0] // HALO - 1

    def pidx(i):
        return (i // tiles_per_batch) * padded_tiles + FRONT // ROWS + i % tiles_per_batch

    padded = lambda width: pl.BlockSpec((ROWS, width), lambda i: (pidx(i), 0))
    plain = lambda width: pl.BlockSpec((ROWS, width), lambda i: (i, 0))
    prev_halo = pl.BlockSpec((HALO, POOL_WIDTH), lambda i: (pidx(i) * halo_per_tile - 1, 0))
    next_halo = pl.BlockSpec(
        (HALO, POOL_WIDTH), lambda i: (jnp.minimum((pidx(i) + 1) * halo_per_tile, last_halo), 0))
    return pl.pallas_call(
        functools.partial(_mix_body, tiles_per_batch=tiles_per_batch, seq=seq),
        grid=(t // ROWS,),
        in_specs=[plain(D_MODEL), prev_halo, padded(POOL_WIDTH), next_halo,
                  padded(D_MODEL), padded(D_MODEL), padded(GLA_DV), padded(GLA_DV), padded(GLA_DV),
                  _const_spec((1, D_MODEL)), _const_spec((1, D_MODEL)),
                  _const_spec((len(POOL_WINDOWS), POOL_GROUP, POOL_GROUP)), _const_spec((1, POOL_WIDTH)),
                  _const_spec((1, GLA_DV)), _const_spec((POOL_WIDTH, D_MODEL)),
                  _const_spec((GLA_DV, D_MODEL)), _const_spec((D_MODEL, D_MODEL)),
                  _const_spec((1, D_MODEL)), _const_spec((1, D_MODEL))],
        out_specs=[plain(D_MODEL), plain(D_MODEL // 2)],
        out_shape=[jax.ShapeDtypeStruct((t, D_MODEL), F32), jax.ShapeDtypeStruct((t, D_MODEL // 2), U32)],
        compiler_params=pltpu.CompilerParams(dimension_semantics=("arbitrary",),
                                             vmem_limit_bytes=VMEM_LIMIT),
        name="mix",
    )(x2, u, u, u, gp, gg, sg, o_f, o_b, ln0_g, ln0_b, pool_w, pool_scale, norm_g, wbp, wbg, wo,
      ln1_g, ln1_b)


def _first_index_of_max(val, idx, size):
    m = jnp.max(val, axis=0, keepdims=True)
    first = jnp.min(jnp.where(val == m, idx, size), axis=0, keepdims=True)
    return m, first


def _route_body(h_ref, wr_ref, bias_ref, eidx_ref, w_ref, before_ref, cnt_ref, base_ref):
    n = ROUTE_ROWS

    @pl.when(pl.program_id(0) == 0)
    def _():
        base_ref[...] = jnp.zeros_like(base_ref)

    h = h_ref[...]
    h_hi = h.astype(BF16)
    h_mid = (h - h_hi.astype(F32)).astype(BF16)
    logits = _dot_nt(wr_ref[...], jnp.concatenate([h_hi, h_mid, h_hi], axis=1))
    scores = jax.nn.sigmoid(logits)
    sel = scores + bias_ref[...]

    gidx = lax.broadcasted_iota(I32, (GROUP_SIZE, n), 0)
    groups = [sel[g * GROUP_SIZE:(g + 1) * GROUP_SIZE, :] for g in range(N_GROUPS)]
    gscore = []
    for grp in groups:
        m1, first = _first_index_of_max(grp, gidx, GROUP_SIZE)
        m2 = jnp.max(jnp.where(gidx == first, -jnp.inf, grp), axis=0, keepdims=True)
        gscore.append(m1 + m2)

    masked = []
    for gi in range(N_GROUPS):
        beaten = jnp.zeros((1, n), F32)
        for gj in range(N_GROUPS):
            if gj != gi:
                wins = (gscore[gj] >= gscore[gi]) if gj < gi else (gscore[gj] > gscore[gi])
                beaten = beaten + wins.astype(F32)
        masked.append(jnp.where(beaten < TOPK_GROUPS, groups[gi], -jnp.inf))
    val = jnp.concatenate(masked, axis=0)

    eidx = lax.broadcasted_iota(I32, (N_EXPERTS, n), 0)
    candidate = val != -jnp.inf
    picks, weights = [], []
    for _ in range(TOP_K):
        _, first = _first_index_of_max(val, eidx, N_EXPERTS)
        hit = eidx == first
        picks.append(first)
        weights.append(jnp.sum(jnp.where(hit, scores, 0.0), axis=0, keepdims=True))
        val = jnp.where(hit, -jnp.inf, val)
    chosen_f = jnp.where(jnp.logical_and(candidate, val == -jnp.inf), 1.0, 0.0)
    wsum = weights[0]
    for wk in weights[1:]:
        wsum = wsum + wk
    scale = ROUTED_SCALE / wsum

    r = lax.broadcasted_iota(I32, (n, n), 0)
    c = lax.broadcasted_iota(I32, (n, n), 1)
    earlier = (r < c).astype(BF16)
    before_ref[...] = _dot(chosen_f.astype(BF16), earlier) + base_ref[...]
    base_ref[...] = base_ref[...] + jnp.sum(chosen_f, axis=1, keepdims=True)
    cnt_ref[...] = jnp.broadcast_to(base_ref[...], cnt_ref.shape).astype(I32)

    for kk in range(TOP_K):
        eidx_ref[kk:kk + 1, :] = picks[kk]
        w_ref[kk:kk + 1, :] = weights[kk] * scale


def _route(h1, wr_t, bias_col):
    t = h1.shape[0]
    col = lambda dtype: jax.ShapeDtypeStruct((TOP_K, t), dtype)
    kspec = pl.BlockSpec((TOP_K, ROUTE_ROWS), lambda i: (0, i))
    return pl.pallas_call(
        _route_body,
        grid=(t // ROUTE_ROWS,),
        in_specs=[pl.BlockSpec((ROUTE_ROWS, D_MODEL), lambda i: (i, 0)),
                  _const_spec((N_EXPERTS, 3 * D_MODEL)), _const_spec((N_EXPERTS, 1))],
        out_specs=[kspec, kspec, pl.BlockSpec((N_EXPERTS, ROUTE_ROWS), lambda i: (0, i)),
                   _const_spec((N_EXPERTS, LANE))],
        out_shape=[col(I32), col(F32), jax.ShapeDtypeStruct((N_EXPERTS, t), F32),
                   jax.ShapeDtypeStruct((N_EXPERTS, LANE), I32)],
        scratch_shapes=[pltpu.VMEM((N_EXPERTS, 1), F32)],
        compiler_params=pltpu.CompilerParams(dimension_semantics=("arbitrary",),
                                             vmem_limit_bytes=VMEM_LIMIT),
        name="route",
    )(h1, wr_t, bias_col)


def _dest_body(eidx_ref, before_ref, pstart_ref, dest_ref):
    n = eidx_ref.shape[1]
    eiota = lax.broadcasted_iota(I32, (N_EXPERTS, n), 0)
    row_of = pstart_ref[...] + before_ref[...]
    for kk in range(TOP_K):
        dest = jnp.sum(jnp.where(eiota == eidx_ref[kk:kk + 1, :], row_of, 0.0), axis=0, keepdims=True)
        dest = dest.astype(I32)
        for c in range(n // SC_CHUNK):
            dest_ref[c, kk:kk + 1, :] = dest[:, c * SC_CHUNK:(c + 1) * SC_CHUNK]


def _dest(eidx, before, pstart_col):
    t = eidx.shape[1]
    kspec = pl.BlockSpec((TOP_K, ROUTE_ROWS), lambda i: (0, i))
    chunks = ROUTE_ROWS // SC_CHUNK
    return pl.pallas_call(
        _dest_body,
        grid=(t // ROUTE_ROWS,),
        in_specs=[kspec, pl.BlockSpec((N_EXPERTS, ROUTE_ROWS), lambda i: (0, i)),
                  _const_spec((N_EXPERTS, 1))],
        out_specs=pl.BlockSpec((chunks, TOP_K, SC_CHUNK), lambda i: (i, 0, 0)),
        out_shape=jax.ShapeDtypeStruct((t // SC_CHUNK, TOP_K, SC_CHUNK), I32),
        compiler_params=pltpu.CompilerParams(dimension_semantics=("arbitrary",)),
        name="dest",
    )(eidx, before, pstart_col)


def _sc_mesh():
    return plsc.VectorSubcoreMesh(core_axis_name="c", subcore_axis_name="s",
                                  num_cores=SC_CORES, num_subcores=SC_SUBCORES)


def _sc_worker_chunks(t):
    assert t % (SC_CORES * SC_SUBCORES * SC_CHUNK) == 0
    per_worker = t // (SC_CORES * SC_SUBCORES * SC_CHUNK)
    worker = lax.axis_index("s") * SC_CORES + lax.axis_index("c")
    return worker * per_worker, per_worker


def _dispatch(dest, h1p, n_rows):
    t, width = h1p.shape

    @functools.partial(
        pl.kernel, mesh=_sc_mesh(), out_type=jax.ShapeDtypeStruct((n_rows, width), U32),
        scratch_types=[pltpu.VMEM((SC_CHUNK, width), U32), pltpu.VMEM((TOP_K, SC_CHUNK), I32),
                       pltpu.SemaphoreType.DMA],
        name="dispatch")
    def body(h_hbm, dest_hbm, xs_hbm, rows_v, idx_v, sem):
        first, count = _sc_worker_chunks(t)

        @pl.loop(0, count)
        def _(i):
            chunk = first + i
            pltpu.sync_copy(h_hbm.at[pl.ds(pl.multiple_of(chunk * SC_CHUNK, SC_CHUNK), SC_CHUNK)], rows_v)
            pltpu.sync_copy(dest_hbm.at[chunk], idx_v)
            copies = [pltpu.async_copy(rows_v, xs_hbm.at[idx_v.at[kk]], sem) for kk in range(TOP_K)]
            for c in copies:
                c.wait()

    return body(h1p, dest)


def _gather(dest, ys):
    t = dest.shape[0] * SC_CHUNK
    width = ys.shape[1]

    @functools.partial(
        pl.kernel, mesh=_sc_mesh(), out_type=jax.ShapeDtypeStruct((TOP_K, t, width), U32),
        scratch_types=[pltpu.VMEM((SC_CHUNK, width), U32), pltpu.VMEM((TOP_K, SC_CHUNK), I32),
                       pltpu.SemaphoreType.DMA],
        name="gather")
    def body(ys_hbm, dest_hbm, yg_hbm, rows_v, idx_v, sem):
        first, count = _sc_worker_chunks(t)

        @pl.loop(0, count)
        def _(i):
            chunk = first + i
            pltpu.sync_copy(dest_hbm.at[chunk], idx_v)
            for kk in range(TOP_K):
                pltpu.async_copy(ys_hbm.at[idx_v.at[kk]], rows_v, sem).wait()
                pltpu.sync_copy(
                    rows_v, yg_hbm.at[kk, pl.ds(pl.multiple_of(chunk * SC_CHUNK, SC_CHUNK), SC_CHUNK)])

    return body(ys, dest)


def _experts_body(bs_ref, xs_in_ref, wg_ref, wu_ref, wd_ref, xs_ref,
                  xbuf, ybuf, wgb, wub, wdb, sem_in, sem_out):
    del xs_in_ref
    e = pl.program_id(0)
    b0 = bs_ref[e]
    b1 = bs_ref[e + 1]
    n_total = bs_ref[N_EXPERTS]
    half = D_MODEL // 2

    def fetch(b):
        slot = b % IN_SLOTS
        return pltpu.make_async_copy(xs_ref.at[pl.ds(b * MOE_BLOCK, MOE_BLOCK)], xbuf.at[slot],
                                     sem_in.at[slot])

    def store(b):
        slot = b % OUT_SLOTS
        return pltpu.make_async_copy(ybuf.at[slot], xs_ref.at[pl.ds(b * MOE_BLOCK, MOE_BLOCK)],
                                     sem_out.at[slot])

    @pl.when(e == 0)
    def _():
        for b in range(LOOKAHEAD):
            @pl.when(b < n_total)
            def _():
                fetch(b).start(priority=1)

    @pl.when(b1 > b0)
    def _():
        wgb[...] = wg_ref[0].astype(BF16)
        wub[...] = wu_ref[0].astype(BF16)
        wdb[...] = wd_ref[0].astype(BF16)

    def run(blocks):
        for b in blocks:
            @pl.when(b + LOOKAHEAD < n_total)
            def _():
                fetch(b + LOOKAHEAD).start(priority=1)
        for b in blocks:
            fetch(b).wait()
        xa, xb = _unpack_halves(jnp.concatenate([xbuf[b % IN_SLOTS] for b in blocks], axis=0))
        xa = xa.astype(BF16)
        xb = xb.astype(BF16)
        gate = _dot(xa, wgb[:half, :]) + _dot(xb, wgb[half:, :])
        up = _dot(xa, wub[:half, :]) + _dot(xb, wub[half:, :])
        hidden = (gate * jax.nn.sigmoid(gate) * up).astype(BF16)
        y = _dot(hidden, wdb[...])
        packed = _pack_halves(y[:, :half], y[:, half:])
        for i, b in enumerate(blocks):
            @pl.when(b >= OUT_SLOTS)
            def _():
                store(b - OUT_SLOTS).wait()

            ybuf[b % OUT_SLOTS] = packed[i * MOE_BLOCK:(i + 1) * MOE_BLOCK]
            store(b).start(priority=1)

    n = b1 - b0

    def group(i, carry):
        run([b0 + GROUP * i + j for j in range(GROUP)])
        return carry

    lax.fori_loop(0, n // GROUP, group, 0)
    rest = b0 + (n // GROUP) * GROUP

    @pl.when(n % GROUP >= 2)
    def _():
        run([rest, rest + 1])

    @pl.when(n % 2 == 1)
    def _():
        run([b1 - 1])

    @pl.when(e == N_EXPERTS - 1)
    def _():
        for back in range(OUT_SLOTS, 0, -1):
            @pl.when(n_total >= back)
            def _():
                store(n_total - back).wait()


def _experts(block_start, xs, w_gate, w_up, w_down):
    n_rows, width = xs.shape
    wspec = lambda shape: pl.BlockSpec((1,) + shape, lambda e, bs: (e, 0, 0))
    grid_spec = pltpu.PrefetchScalarGridSpec(
        num_scalar_prefetch=1,
        grid=(N_EXPERTS,),
        in_specs=[pl.BlockSpec(memory_space=pl.ANY),
                  wspec((D_MODEL, D_EXPERT)), wspec((D_MODEL, D_EXPERT)), wspec((D_EXPERT, D_MODEL))],
        out_specs=pl.BlockSpec(memory_space=pl.ANY),
        scratch_shapes=[pltpu.VMEM((IN_SLOTS, MOE_BLOCK, width), U32),
                        pltpu.VMEM((OUT_SLOTS, MOE_BLOCK, width), U32),
                        pltpu.VMEM((D_MODEL, D_EXPERT), BF16), pltpu.VMEM((D_MODEL, D_EXPERT), BF16),
                        pltpu.VMEM((D_EXPERT, D_MODEL), BF16),
                        pltpu.SemaphoreType.DMA((IN_SLOTS,)), pltpu.SemaphoreType.DMA((OUT_SLOTS,))],
    )
    return pl.pallas_call(
        _experts_body,
        grid_spec=grid_spec,
        out_shape=jax.ShapeDtypeStruct((n_rows, width), U32),
        input_output_aliases={1: 0},
        compiler_params=pltpu.CompilerParams(dimension_semantics=("arbitrary",),
                                             vmem_limit_bytes=VMEM_LIMIT),
        name="experts",
    )(block_start, xs, w_gate, w_up, w_down)


def _combine_body(h1_ref, w_ref, yg_ref, wsg_ref, wsu_ref, wsd_ref, g2_ref, b2_ref, *refs):
    out_ref = refs[-1]
    h1 = h1_ref[...]
    hb = h1.astype(BF16)
    gate = _dot(hb, wsg_ref[...])
    up = _dot(hb, wsu_ref[...])
    shared = _dot((gate * jax.nn.sigmoid(gate) * up).astype(BF16), wsd_ref[...])

    half = D_MODEL // 2
    lo = jnp.zeros((ROWS, half), F32)
    hi = jnp.zeros((ROWS, half), F32)
    for kk in range(TOP_K):
        a, b = _unpack_halves(yg_ref[kk])
        wk = w_ref[:, kk:kk + 1]
        lo = lo + wk * a
        hi = hi + wk * b
    z = DN_ALPHA * h1 + shared + jnp.concatenate([lo, hi], axis=1)
    out_ref[...] = _layer_norm(z, g2_ref[...], b2_ref[...])


def _combine(h1, w_rows, yg, first_tile, out_prev, wsg, wsu, wsd, ln2_g, ln2_b):
    width = yg.shape[2]
    d_shared = wsg.shape[1]
    rows = lambda cols: pl.BlockSpec((ROWS, cols), lambda i: (first_tile + i, 0))
    args = [h1, w_rows, yg, wsg, wsu, wsd, ln2_g, ln2_b]
    in_specs = [rows(D_MODEL), rows(TOP_K),
                pl.BlockSpec((TOP_K, ROWS, width), lambda i: (0, i, 0)),
                _const_spec((D_MODEL, d_shared)), _const_spec((D_MODEL, d_shared)),
                _const_spec((d_shared, D_MODEL)),
                _const_spec((1, D_MODEL)), _const_spec((1, D_MODEL))]
    aliases = {}
    if out_prev is not None:
        aliases = {len(args): 0}
        args.append(out_prev)
        in_specs.append(pl.BlockSpec(memory_space=pl.ANY))
    return pl.pallas_call(
        _combine_body,
        grid=(yg.shape[1] // ROWS,),
        in_specs=in_specs,
        out_specs=rows(D_MODEL),
        out_shape=jax.ShapeDtypeStruct(h1.shape, F32),
        input_output_aliases=aliases,
        compiler_params=pltpu.CompilerParams(dimension_semantics=("arbitrary",),
                                             vmem_limit_bytes=VMEM_LIMIT),
        name="combine",
    )(*args)


def _pack_in_proj(w_in, b_in, gate_w_f, gate_w_b, gate_b_f, gate_b_b):
    o_pool, o_q, o_k, o_v, o_g = 0, 512, 1024, 1536, 2560
    o_lr, o_gp, o_gg = 3584, 3616, 4640
    order = [(o_pool, POOL_WIDTH), (o_q, GLA_DK), (o_k, GLA_DK), (o_v, GLA_DV), (o_g, GLA_DV),
             (o_gp, D_MODEL), (o_gg, D_MODEL)] + [(o_lr, 2 * GATE_RANK)] * 3
    pad = LANE - 6 * GATE_RANK
    w = jnp.concatenate([w_in[:, o:o + n] for o, n in order] + [jnp.zeros((D_MODEL, pad), F32)], axis=1)
    b = jnp.concatenate([b_in[o:o + n] for o, n in order] + [jnp.zeros((pad,), F32)])
    w_gate = jnp.zeros((2 * GATE_RANK, 2 * GLA_DK), F32)
    w_gate = w_gate.at[:GATE_RANK, :GLA_DK].set(gate_w_f)
    w_gate = w_gate.at[GATE_RANK:, GLA_DK:].set(gate_w_b)
    w_hi = w_gate.astype(BF16)
    w_lo = (w_gate - w_hi.astype(F32)).astype(BF16)
    w_gate3 = jnp.concatenate([w_hi, w_hi, w_lo, jnp.zeros((pad, 2 * GLA_DK), BF16)], axis=0)
    b_gate = jnp.concatenate([gate_b_f, gate_b_b])[None, :]
    return w.astype(BF16), b[None, :], w_gate3, b_gate


def kernel(x, meta, ln0_g, ln0_b, w_in, b_in, pool_w, pool_scale, gate_w_f, gate_b_f, gate_w_b, gate_b_b,
           gla_norm_g, w_branch_pool, w_branch_gla, w_out, ln1_g, ln1_b, w_router, router_bias,
           w_exp_gate, w_exp_up, w_exp_down, w_sh_gate, w_sh_up, w_sh_down, ln2_g, ln2_b):
    batch, seq, d = x.shape
    assert d == D_MODEL and seq % ROWS == 0 and w_in.shape[0] == 1
    assert (batch * seq) % ROUTE_ROWS == 0
    t = batch * seq
    row = lambda a: a.reshape(1, -1).astype(F32)

    lead = jnp.concatenate([jnp.zeros((FRONT - N_META, d), x.dtype), meta.astype(x.dtype)], axis=0)
    tiles_per_batch = (seq + FRONT) // ROWS
    x2 = x.reshape(t, d)

    w_packed, b_packed, w_gate, b_gate = _pack_in_proj(
        w_in[0], b_in[0], gate_w_f[0], gate_w_b[0], gate_b_f[0], gate_b_b[0])
    u, q, k, v, sg, gp, gg, bf, cb = _in_proj(
        lead, x2, row(ln0_g), row(ln0_b), w_packed, b_packed, w_gate, b_gate, tiles_per_batch)

    o_f, o_b = _gla(q, k, v, bf, cb, batch)

    h1, h1p = _mix(
        x.reshape(t, d), u, gp, gg, sg, o_f, o_b, row(ln0_g), row(ln0_b),
        pool_w[0].astype(BF16), row(pool_scale[0]), row(jnp.tile(gla_norm_g[0], GLA_HEADS)),
        w_branch_pool[0].astype(BF16), w_branch_gla[0].astype(BF16), w_out[0].astype(BF16),
        row(ln1_g[0]), row(ln1_b[0]), seq)

    wr = w_router[0].T.astype(F32)
    wr_hi = wr.astype(BF16)
    wr_mid = (wr - wr_hi.astype(F32)).astype(BF16)
    wr_terms = jnp.concatenate([wr_hi, wr_hi, wr_mid], axis=1)
    eidx, ew, before, counts = _route(h1, wr_terms, router_bias[0].reshape(N_EXPERTS, 1))

    counts = counts[:, 0]
    padded = ((counts + MOE_BLOCK - 1) // MOE_BLOCK) * MOE_BLOCK
    pend = jnp.cumsum(padded)
    pstart = pend - padded
    dest = _dest(eidx, before, pstart.astype(F32).reshape(N_EXPERTS, 1))
    n_blocks = -(-(t * TOP_K) // MOE_BLOCK) + N_EXPERTS
    block_start = (jnp.concatenate([pstart, pend[-1:]]) // MOE_BLOCK).astype(I32)

    xs = _dispatch(dest, h1p, n_blocks * MOE_BLOCK)
    ys = _experts(block_start, xs, w_exp_gate[0], w_exp_up[0], w_exp_down[0])
    shared_w = (w_sh_gate[0].astype(BF16), w_sh_up[0].astype(BF16), w_sh_down[0].astype(BF16))
    ew_rows = ew.T
    windows = t // SC_CHUNK // COMBINE_CHUNKS
    out = None
    for c in range(COMBINE_CHUNKS):
        yg = _gather(dest[c * windows:(c + 1) * windows], ys)
        out = _combine(h1, ew_rows, yg, c * windows * SC_CHUNK // ROWS, out, *shared_w,
                       row(ln2_g[0]), row(ln2_b[0]))
    return out.reshape(batch, seq, d)
```

```python
import functools

import jax
import jax.numpy as jnp
from jax import lax
from jax.experimental import pallas as pl
from jax.experimental.pallas import tpu as pltpu
from jax.experimental.pallas import tpu_sc as plsc

F32 = jnp.float32
BF16 = jnp.bfloat16
I32 = jnp.int32
U32 = jnp.uint32

D_MODEL = 1024
N_META = 16
POOL_WINDOWS = (2, 4, 8, 16)
POOL_GROUP = 128
POOL_WIDTH = POOL_GROUP * len(POOL_WINDOWS)
GLA_HEADS = 4
GLA_DK = 512
GLA_DV = 1024
HEAD_K = GLA_DK // GLA_HEADS
HEAD_V = GLA_DV // GLA_HEADS
GATE_RANK = 16
GATE_NORMALIZER = 16.0
RMS_EPS = 1e-5
N_EXPERTS = 256
TOP_K = 8
N_GROUPS = 8
GROUP_SIZE = N_EXPERTS // N_GROUPS
TOPK_GROUPS = 4
D_EXPERT = 256
ROUTED_SCALE = 2.5
DN_ALPHA = 2.0 ** 0.25
LN_EPS = 1e-5

LANE = 128
ROWS = 256
FRONT = ROWS
GLA_CHUNK = 256
HALO = 16
ROUTE_ROWS = 512
MOE_BLOCK = 256
LOOKAHEAD = 6
GROUP = 4
IN_SLOTS = LOOKAHEAD + GROUP
OUT_SLOTS = GROUP
SC_CORES = 2
SC_SUBCORES = 16
SC_LANES = 16
SC_CHUNK = 128
CAST_WORDS = 64 * 256
COMBINE_CHUNKS = 8
VMEM_LIMIT = 56 * 1024 * 1024

C_POOL, C_Q, C_K, C_V, C_G, C_GP, C_GG, C_LR = 0, 512, 1024, 1536, 2560, 3584, 4608, 5632
W_PACKED = C_LR + LANE


def _layer_norm(x, g, b):
    mu = jnp.mean(x, axis=-1, keepdims=True)
    xc = x - mu
    var = jnp.mean(xc * xc, axis=-1, keepdims=True)
    return xc * lax.rsqrt(var + LN_EPS) * g + b


def _dot(a, b):
    return jnp.dot(a, b, preferred_element_type=F32)


def _dot_nt(a, b):
    return lax.dot_general(a, b, (((1,), (1,)), ((), ())), preferred_element_type=F32)


def _split_dot(m2, g):
    g_hi = g.astype(BF16)
    g_lo = (g - g_hi.astype(F32)).astype(BF16)
    return _dot(m2, jnp.concatenate([g_hi, g_lo], axis=0))


def _pack_halves(a, b):
    pa = lax.bitcast_convert_type(a.astype(BF16).astype(F32), U32)
    pb = lax.bitcast_convert_type(b.astype(BF16).astype(F32), U32)
    return pa | (pb >> 16)


def _unpack_halves(p):
    a = lax.bitcast_convert_type(p & jnp.uint32(0xFFFF0000), F32)
    b = lax.bitcast_convert_type(p << 16, F32)
    return a, b


def _const_spec(shape):
    return pl.BlockSpec(shape, lambda *_: (0,) * len(shape))


def _inproj_body(lead_ref, x_ref, g0_ref, b0_ref, w_ref, bias_ref, wgate_ref, bgate_ref,
                 u_ref, q_ref, k_ref, v_ref, sg_ref, gp_ref, gg_ref, bf_ref, cb_ref,
                 *, tiles_per_batch):
    i = pl.program_id(0)
    x_in = jnp.where(i % tiles_per_batch == 0, lead_ref[...], x_ref[...])
    h = _layer_norm(x_in, g0_ref[...], b0_ref[...])
    hb = h.astype(BF16)

    def proj(c0, n):
        return _dot(hb, w_ref[:, c0:c0 + n]) + bias_ref[:, c0:c0 + n]

    row = lax.broadcasted_iota(I32, (ROWS, 1), 0)
    valid = row >= jnp.where(i % tiles_per_batch == 0, FRONT - N_META, 0)

    u_ref[...] = proj(C_POOL, POOL_WIDTH).astype(BF16)
    q_ref[...] = (proj(C_Q, GLA_DK) * (HEAD_K ** -0.5)).astype(BF16)
    k_ref[...] = jnp.where(valid, proj(C_K, GLA_DK), 0.0).astype(BF16)
    v_ref[...] = jnp.where(valid, proj(C_V, GLA_DV), 0.0).astype(BF16)
    g = proj(C_G, GLA_DV)
    sg_ref[...] = (g * jax.nn.sigmoid(g)).astype(BF16)
    gp_ref[...] = jax.nn.sigmoid(proj(C_GP, D_MODEL)).astype(BF16)
    gg_ref[...] = jax.nn.sigmoid(proj(C_GG, D_MODEL)).astype(BF16)

    lr3 = proj(C_LR, LANE)
    lr_hi = lr3.astype(BF16)
    lr_lo = (lr3 - lr_hi.astype(F32)).astype(BF16)
    lane = lax.broadcasted_iota(I32, (1, LANE), 1)
    second = jnp.logical_and(lane >= 2 * GATE_RANK, lane < 4 * GATE_RANK)
    xg = _dot(jnp.where(second, lr_lo, lr_hi), wgate_ref[...]) + bgate_ref[...]
    gk = (jnp.minimum(xg, 0.0) - jnp.log(1.0 + jnp.exp(-jnp.abs(xg)))) * (1.0 / GATE_NORMALIZER)

    r = lax.broadcasted_iota(I32, (GLA_CHUNK, 2 * GLA_CHUNK), 0)
    c = lax.broadcasted_iota(I32, (GLA_CHUNK, 2 * GLA_CHUNK), 1) % GLA_CHUNK
    tril = (r >= c).astype(BF16)
    triu = (r <= c).astype(BF16)
    for ci in range(ROWS // GLA_CHUNK):
        sl = slice(ci * GLA_CHUNK, (ci + 1) * GLA_CHUNK)
        bf_ref[sl, :] = _split_dot(tril, gk[sl, :GLA_DK])
        cb_ref[sl, :] = _split_dot(triu, gk[sl, GLA_DK:])


def _in_proj(lead, x2, ln0_g, ln0_b, w_packed, b_packed, w_gate, b_gate, tiles_per_batch):
    x_tiles = tiles_per_batch - FRONT // ROWS
    n = x2.shape[0] // (x_tiles * ROWS) * tiles_per_batch * ROWS
    row_spec = lambda width: pl.BlockSpec((ROWS, width), lambda i: (i, 0))
    x_spec = pl.BlockSpec(
        (ROWS, D_MODEL),
        lambda i: ((i // tiles_per_batch) * x_tiles + jnp.maximum(i % tiles_per_batch - 1, 0), 0))
    out_widths = (POOL_WIDTH, GLA_DK, GLA_DK, GLA_DV, GLA_DV, D_MODEL, D_MODEL)
    out_shape = [jax.ShapeDtypeStruct((n, w), BF16) for w in out_widths]
    out_shape += [jax.ShapeDtypeStruct((n, GLA_DK), F32)] * 2
    out_specs = [row_spec(w) for w in out_widths] + [row_spec(GLA_DK)] * 2
    return pl.pallas_call(
        functools.partial(_inproj_body, tiles_per_batch=tiles_per_batch),
        grid=(n // ROWS,),
        in_specs=[_const_spec((FRONT, D_MODEL)), x_spec,
                  _const_spec((1, D_MODEL)), _const_spec((1, D_MODEL)),
                  _const_spec((D_MODEL, W_PACKED)), _const_spec((1, W_PACKED)),
                  _const_spec((LANE, 2 * GLA_DK)), _const_spec((1, 2 * GLA_DK))],
        out_specs=out_specs,
        out_shape=out_shape,
        compiler_params=pltpu.CompilerParams(dimension_semantics=("arbitrary",),
                                             vmem_limit_bytes=VMEM_LIMIT),
        name="in_proj",
    )(lead, x2, ln0_g, ln0_b, w_packed, b_packed, w_gate, b_gate)


def _gla_direction(q_ref, k_ref, v_ref, b_ref, o_ref, s_ref, reverse):
    c = GLA_CHUNK
    r = lax.broadcasted_iota(I32, (c, c), 0)
    s = lax.broadcasted_iota(I32, (c, c), 1)
    keep = (r <= s) if reverse else (r >= s)
    eye = lax.broadcasted_iota(I32, (HEAD_K, HEAD_K), 0) == lax.broadcasted_iota(I32, (HEAD_K, HEAD_K), 1)
    edge = 0 if reverse else c - 1
    for h in range(GLA_HEADS):
        ks = slice(h * HEAD_K, (h + 1) * HEAD_K)
        vs = slice(h * HEAD_V, (h + 1) * HEAD_V)
        b = b_ref[:, ks]
        qh = q_ref[:, ks].astype(F32)
        kh = k_ref[:, ks].astype(F32)
        vh = v_ref[:, vs]
        b_tot = b[edge:edge + 1, :]
        b_mid = b[c // 2:c // 2 + 1, :]
        q_state = (qh * jnp.exp(b)).astype(BF16)
        q_in = (qh * jnp.exp(b - b_mid)).astype(BF16)
        k_in = (kh * jnp.exp(b_mid - b)).astype(BF16)
        k_state = (kh * jnp.exp(b_tot - b)).T.astype(BF16)
        att = jnp.where(keep, _dot_nt(q_in, k_in), 0.0).astype(BF16)
        state = s_ref[h]
        o = _dot(q_state, state.astype(BF16)) + _dot(att, vh)
        o_ref[:, vs] = o.astype(o_ref.dtype)
        decay_col = jnp.sum(jnp.where(eye, jnp.exp(b_tot), 0.0), axis=1, keepdims=True)
        s_ref[h] = decay_col * state + _dot(k_state, vh)


def _gla_body(qf_ref, kf_ref, vf_ref, bf_ref, qb_ref, kb_ref, vb_ref, cb_ref,
              of_ref, ob_ref, sf_ref, sb_ref):
    @pl.when(pl.program_id(1) == 0)
    def _():
        sf_ref[...] = jnp.zeros_like(sf_ref)
        sb_ref[...] = jnp.zeros_like(sb_ref)

    _gla_direction(qf_ref, kf_ref, vf_ref, bf_ref, of_ref, sf_ref, reverse=False)
    _gla_direction(qb_ref, kb_ref, vb_ref, cb_ref, ob_ref, sb_ref, reverse=True)


def _gla(q, k, v, bf, cb, batch):
    n = q.shape[0]
    nch = n // batch // GLA_CHUNK
    fwd = lambda width: pl.BlockSpec((GLA_CHUNK, width), lambda b, i: (b * nch + i, 0))
    bwd = lambda width: pl.BlockSpec((GLA_CHUNK, width), lambda b, i: (b * nch + nch - 1 - i, 0))
    state = pltpu.VMEM((GLA_HEADS, HEAD_K, HEAD_V), F32)
    return pl.pallas_call(
        _gla_body,
        grid=(batch, nch),
        in_specs=[fwd(GLA_DK), fwd(GLA_DK), fwd(GLA_DV), fwd(GLA_DK),
                  bwd(GLA_DK), bwd(GLA_DK), bwd(GLA_DV), bwd(GLA_DK)],
        out_specs=[fwd(GLA_DV), bwd(GLA_DV)],
        out_shape=[jax.ShapeDtypeStruct((n, GLA_DV), BF16)] * 2,
        scratch_shapes=[state, state],
        compiler_params=pltpu.CompilerParams(dimension_semantics=("arbitrary", "arbitrary"),
                                             vmem_limit_bytes=VMEM_LIMIT),
        name="gla",
    )(q, k, v, bf, q, k, v, cb)


def _mix_body(x_ref, up_ref, u_ref, un_ref, gp_ref, gg_ref, sg_ref, of_ref, ob_ref,
              g0_ref, b0_ref, pw_ref, ps_ref, gn_ref, wbp_ref, wbg_ref, wo_ref, g1_ref, b1_ref,
              h1_ref, h1p_ref, *, tiles_per_batch, seq):
    j = pl.program_id(0) % tiles_per_batch
    u_main = u_ref[...]
    u_ext = jnp.concatenate([up_ref[...], u_main, un_ref[...]], axis=0)
    r = lax.broadcasted_iota(I32, (ROWS, ROWS + 2 * HALO), 0)
    e = lax.broadcasted_iota(I32, (ROWS, ROWS + 2 * HALO), 1) - HALO
    in_seq = (j * ROWS + e) < seq
    pos = j * ROWS + lax.broadcasted_iota(I32, (ROWS, 1), 0)
    y_pool = []
    for gi, w in enumerate(POOL_WINDOWS):
        cs = slice(gi * POOL_GROUP, (gi + 1) * POOL_GROUP)
        band = jnp.logical_and(jnp.logical_and(e >= r - w // 2, e < r + w - w // 2), in_seq)
        count = w - jnp.maximum(pos + (w - w // 2) - seq, 0)
        mean = _dot(band.astype(BF16), u_ext[:, cs]) / count.astype(F32)
        d = mean - u_main[:, cs].astype(F32)
        y_pool.append(_dot(d.astype(BF16), pw_ref[gi]))
    y_pool = (jnp.concatenate(y_pool, axis=1) * ps_ref[...]).astype(BF16)

    o = of_ref[...].astype(F32) + ob_ref[...].astype(F32)
    y_gla = []
    for hd in range(GLA_HEADS):
        oh = o[:, hd * HEAD_V:(hd + 1) * HEAD_V]
        y_gla.append(oh * lax.rsqrt(jnp.mean(oh * oh, axis=-1, keepdims=True) + RMS_EPS))
    y_gla = (jnp.concatenate(y_gla, axis=1) * gn_ref[...] * sg_ref[...].astype(F32)).astype(BF16)

    merged = (gp_ref[...].astype(F32) * _dot(y_pool, wbp_ref[...])
              + gg_ref[...].astype(F32) * _dot(y_gla, wbg_ref[...]))
    y = _dot(merged.astype(BF16), wo_ref[...])
    h = _layer_norm(x_ref[...], g0_ref[...], b0_ref[...])
    h1 = _layer_norm(DN_ALPHA * h + y, g1_ref[...], b1_ref[...])
    h1_ref[...] = h1
    h1p_ref[...] = _pack_halves(h1[:, :D_MODEL // 2], h1[:, D_MODEL // 2:])


def _mix(x2, u, gp, gg, sg, o_f, o_b, ln0_g, ln0_b, pool_w, pool_scale, norm_g, wbp, wbg, wo,
         ln1_g, ln1_b, seq):
    t = x2.shape[0]
    tiles_per_batch = seq // ROWS
    padded_tiles = tiles_per_batch + FRONT // ROWS
    halo_per_tile = ROWS // HALO
    last_halo = u.shape[0] // HALO - 1

    def pidx(i):
        return (i // tiles_per_batch) * padded_tiles + FRONT // ROWS + i % tiles_per_batch

    padded = lambda width: pl.BlockSpec((ROWS, width), lambda i: (pidx(i), 0))
    plain = lambda width: pl.BlockSpec((ROWS, width), lambda i: (i, 0))
    prev_halo = pl.BlockSpec((HALO, POOL_WIDTH), lambda i: (pidx(i) * halo_per_tile - 1, 0))
    next_halo = pl.BlockSpec(
        (HALO, POOL_WIDTH), lambda i: (jnp.minimum((pidx(i) + 1) * halo_per_tile, last_halo), 0))
    return pl.pallas_call(
        functools.partial(_mix_body, tiles_per_batch=tiles_per_batch, seq=seq),
        grid=(t // ROWS,),
        in_specs=[plain(D_MODEL), prev_halo, padded(POOL_WIDTH), next_halo,
                  padded(D_MODEL), padded(D_MODEL), padded(GLA_DV), padded(GLA_DV), padded(GLA_DV),
                  _const_spec((1, D_MODEL)), _const_spec((1, D_MODEL)),
                  _const_spec((len(POOL_WINDOWS), POOL_GROUP, POOL_GROUP)), _const_spec((1, POOL_WIDTH)),
                  _const_spec((1, GLA_DV)), _const_spec((POOL_WIDTH, D_MODEL)),
                  _const_spec((GLA_DV, D_MODEL)), _const_spec((D_MODEL, D_MODEL)),
                  _const_spec((1, D_MODEL)), _const_spec((1, D_MODEL))],
        out_specs=[plain(D_MODEL), plain(D_MODEL // 2)],
        out_shape=[jax.ShapeDtypeStruct((t, D_MODEL), F32), jax.ShapeDtypeStruct((t, D_MODEL // 2), U32)],
        compiler_params=pltpu.CompilerParams(dimension_semantics=("arbitrary",),
                                             vmem_limit_bytes=VMEM_LIMIT),
        name="mix",
    )(x2, u, u, u, gp, gg, sg, o_f, o_b, ln0_g, ln0_b, pool_w, pool_scale, norm_g, wbp, wbg, wo,
      ln1_g, ln1_b)


def _first_index_of_max(val, idx, size):
    m = jnp.max(val, axis=0, keepdims=True)
    first = jnp.min(jnp.where(val == m, idx, size), axis=0, keepdims=True)
    return m, first


def _route_body(h_ref, wr_ref, bias_ref, eidx_ref, w_ref, before_ref, cnt_ref, base_ref):
    n = ROUTE_ROWS

    @pl.when(pl.program_id(0) == 0)
    def _():
        base_ref[...] = jnp.zeros_like(base_ref)

    h = h_ref[...]
    h_hi = h.astype(BF16)
    h_mid = (h - h_hi.astype(F32)).astype(BF16)
    logits = _dot_nt(wr_ref[...], jnp.concatenate([h_hi, h_mid, h_hi], axis=1))
    scores = jax.nn.sigmoid(logits)
    sel = scores + bias_ref[...]

    gidx = lax.broadcasted_iota(I32, (GROUP_SIZE, n), 0)
    groups = [sel[g * GROUP_SIZE:(g + 1) * GROUP_SIZE, :] for g in range(N_GROUPS)]
    gscore = []
    for grp in groups:
        m1, first = _first_index_of_max(grp, gidx, GROUP_SIZE)
        m2 = jnp.max(jnp.where(gidx == first, -jnp.inf, grp), axis=0, keepdims=True)
        gscore.append(m1 + m2)

    masked = []
    for gi in range(N_GROUPS):
        beaten = jnp.zeros((1, n), F32)
        for gj in range(N_GROUPS):
            if gj != gi:
                wins = (gscore[gj] >= gscore[gi]) if gj < gi else (gscore[gj] > gscore[gi])
                beaten = beaten + wins.astype(F32)
        masked.append(jnp.where(beaten < TOPK_GROUPS, groups[gi], -jnp.inf))
    val = jnp.concatenate(masked, axis=0)

    eidx = lax.broadcasted_iota(I32, (N_EXPERTS, n), 0)
    candidate = val != -jnp.inf
    picks, weights = [], []
    for _ in range(TOP_K):
        _, first = _first_index_of_max(val, eidx, N_EXPERTS)
        hit = eidx == first
        picks.append(first)
        weights.append(jnp.sum(jnp.where(hit, scores, 0.0), axis=0, keepdims=True))
        val = jnp.where(hit, -jnp.inf, val)
    chosen_f = jnp.where(jnp.logical_and(candidate, val == -jnp.inf), 1.0, 0.0)
    wsum = weights[0]
    for wk in weights[1:]:
        wsum = wsum + wk
    scale = ROUTED_SCALE / wsum

    r = lax.broadcasted_iota(I32, (n, n), 0)
    c = lax.broadcasted_iota(I32, (n, n), 1)
    earlier = (r < c).astype(BF16)
    before_ref[...] = _dot(chosen_f.astype(BF16), earlier) + base_ref[...]
    base_ref[...] = base_ref[...] + jnp.sum(chosen_f, axis=1, keepdims=True)
    cnt_ref[...] = jnp.broadcast_to(base_ref[...], cnt_ref.shape).astype(I32)

    for kk in range(TOP_K):
        eidx_ref[kk:kk + 1, :] = picks[kk]
        w_ref[kk:kk + 1, :] = weights[kk] * scale


def _route(h1, wr_t, bias_col):
    t = h1.shape[0]
    col = lambda dtype: jax.ShapeDtypeStruct((TOP_K, t), dtype)
    kspec = pl.BlockSpec((TOP_K, ROUTE_ROWS), lambda i: (0, i))
    return pl.pallas_call(
        _route_body,
        grid=(t // ROUTE_ROWS,),
        in_specs=[pl.BlockSpec((ROUTE_ROWS, D_MODEL), lambda i: (i, 0)),
                  _const_spec((N_EXPERTS, 3 * D_MODEL)), _const_spec((N_EXPERTS, 1))],
        out_specs=[kspec, kspec, pl.BlockSpec((N_EXPERTS, ROUTE_ROWS), lambda i: (0, i)),
                   _const_spec((N_EXPERTS, LANE))],
        out_shape=[col(I32), col(F32), jax.ShapeDtypeStruct((N_EXPERTS, t), F32),
                   jax.ShapeDtypeStruct((N_EXPERTS, LANE), I32)],
        scratch_shapes=[pltpu.VMEM((N_EXPERTS, 1), F32)],
        compiler_params=pltpu.CompilerParams(dimension_semantics=("arbitrary",),
                                             vmem_limit_bytes=VMEM_LIMIT),
        name="route",
    )(h1, wr_t, bias_col)


def _dest_body(eidx_ref, before_ref, pstart_ref, dest_ref):
    n = eidx_ref.shape[1]
    eiota = lax.broadcasted_iota(I32, (N_EXPERTS, n), 0)
    row_of = pstart_ref[...] + before_ref[...]
    for kk in range(TOP_K):
        dest = jnp.sum(jnp.where(eiota == eidx_ref[kk:kk + 1, :], row_of, 0.0), axis=0, keepdims=True)
        dest = dest.astype(I32)
        for c in range(n // SC_CHUNK):
            dest_ref[c, kk:kk + 1, :] = dest[:, c * SC_CHUNK:(c + 1) * SC_CHUNK]


def _dest(eidx, before, pstart_col):
    t = eidx.shape[1]
    kspec = pl.BlockSpec((TOP_K, ROUTE_ROWS), lambda i: (0, i))
    chunks = ROUTE_ROWS // SC_CHUNK
    return pl.pallas_call(
        _dest_body,
        grid=(t // ROUTE_ROWS,),
        in_specs=[kspec, pl.BlockSpec((N_EXPERTS, ROUTE_ROWS), lambda i: (0, i)),
                  _const_spec((N_EXPERTS, 1))],
        out_specs=pl.BlockSpec((chunks, TOP_K, SC_CHUNK), lambda i: (i, 0, 0)),
        out_shape=jax.ShapeDtypeStruct((t // SC_CHUNK, TOP_K, SC_CHUNK), I32),
        compiler_params=pltpu.CompilerParams(dimension_semantics=("arbitrary",)),
        name="dest",
    )(eidx, before, pstart_col)


def _sc_mesh():
    return plsc.VectorSubcoreMesh(core_axis_name="c", subcore_axis_name="s",
                                  num_cores=SC_CORES, num_subcores=SC_SUBCORES)


def _sc_worker_chunks(t):
    assert t % (SC_CORES * SC_SUBCORES * SC_CHUNK) == 0
    per_worker = t // (SC_CORES * SC_SUBCORES * SC_CHUNK)
    worker = lax.axis_index("s") * SC_CORES + lax.axis_index("c")
    return worker * per_worker, per_worker


def _dispatch(dest, h1p, n_rows):
    t, width = h1p.shape

    @functools.partial(
        pl.kernel, mesh=_sc_mesh(), out_type=jax.ShapeDtypeStruct((n_rows, width), U32),
        scratch_types=[pltpu.VMEM((SC_CHUNK, width), U32), pltpu.VMEM((TOP_K, SC_CHUNK), I32),
                       pltpu.SemaphoreType.DMA],
        name="dispatch")
    def body(h_hbm, dest_hbm, xs_hbm, rows_v, idx_v, sem):
        first, count = _sc_worker_chunks(t)

        @pl.loop(0, count)
        def _(i):
            chunk = first + i
            pltpu.sync_copy(h_hbm.at[pl.ds(pl.multiple_of(chunk * SC_CHUNK, SC_CHUNK), SC_CHUNK)], rows_v)
            pltpu.sync_copy(dest_hbm.at[chunk], idx_v)
            copies = [pltpu.async_copy(rows_v, xs_hbm.at[idx_v.at[kk]], sem) for kk in range(TOP_K)]
            for c in copies:
                c.wait()

    return body(h1p, dest)


def _gather(dest, ys):
    t = dest.shape[0] * SC_CHUNK
    width = ys.shape[1]

    @functools.partial(
        pl.kernel, mesh=_sc_mesh(), out_type=jax.ShapeDtypeStruct((TOP_K, t, width), U32),
        scratch_types=[pltpu.VMEM((SC_CHUNK, width), U32), pltpu.VMEM((TOP_K, SC_CHUNK), I32),
                       pltpu.SemaphoreType.DMA],
        name="gather")
    def body(ys_hbm, dest_hbm, yg_hbm, rows_v, idx_v, sem):
        first, count = _sc_worker_chunks(t)

        @pl.loop(0, count)
        def _(i):
            chunk = first + i
            pltpu.sync_copy(dest_hbm.at[chunk], idx_v)
            for kk in range(TOP_K):
                pltpu.async_copy(ys_hbm.at[idx_v.at[kk]], rows_v, sem).wait()
                pltpu.sync_copy(
                    rows_v, yg_hbm.at[kk, pl.ds(pl.multiple_of(chunk * SC_CHUNK, SC_CHUNK), SC_CHUNK)])

    return body(ys, dest)


def _cast_weights(*weights):
    flat = [w.reshape(-1, w.shape[-1]) for w in weights]
    workers = SC_CORES * SC_SUBCORES
    scratch = []
    for w in flat:
        rows = CAST_WORDS // w.shape[1]
        assert w.shape[0] % (workers * rows) == 0 and rows % 2 == 0
        scratch += [pltpu.VMEM((rows, w.shape[1]), F32), pltpu.VMEM((rows // 2, w.shape[1]), U32)]

    @functools.partial(
        pl.kernel, mesh=_sc_mesh(),
        out_type=[jax.ShapeDtypeStruct((w.shape[0] // 2, w.shape[1]), U32) for w in flat],
        scratch_types=scratch,
        compiler_params=pltpu.CompilerParams(needs_layout_passes=False),
        name="cast_weights")
    def body(*refs):
        n = len(flat)
        worker = lax.axis_index("s") * SC_CORES + lax.axis_index("c")
        for j in range(n):
            w_hbm, o_hbm, in_v, out_v = refs[j], refs[n + j], refs[2 * n + 2 * j], refs[2 * n + 2 * j + 1]
            rows, cols = in_v.shape
            per_worker = w_hbm.shape[0] // (workers * rows)

            @pl.loop(0, per_worker)
            def _(i):
                chunk = worker * per_worker + i
                pltpu.sync_copy(w_hbm.at[pl.ds(pl.multiple_of(chunk * rows, rows), rows)], in_v)

                @plsc.parallel_loop(0, rows // 2)
                def _(r2):
                    for c in range(0, cols, SC_LANES):
                        pair = plsc.pack(in_v[2 * r2, pl.ds(c, SC_LANES)], in_v[2 * r2 + 1, pl.ds(c, SC_LANES)],
                                         format=plsc.PackFormat.INTERLEAVED)
                        out_v[r2, pl.ds(c, SC_LANES)] = plsc.bitcast(pair, U32)

                pltpu.sync_copy(
                    out_v, o_hbm.at[pl.ds(pl.multiple_of(chunk * (rows // 2), rows // 2), rows // 2)])

    packed = body(*flat)
    return [p.reshape(w.shape[:-2] + (w.shape[-2] // 2, w.shape[-1])) for p, w in zip(packed, weights)]


def _experts_body(bs_ref, xs_in_ref, wg_ref, wu_ref, wd_ref, xs_ref, xbuf, ybuf, sem_in, sem_out):
    del xs_in_ref
    e = pl.program_id(0)
    b0 = bs_ref[e]
    b1 = bs_ref[e + 1]
    n_total = bs_ref[N_EXPERTS]
    half = D_MODEL // 2

    def fetch(b):
        slot = b % IN_SLOTS
        return pltpu.make_async_copy(xs_ref.at[pl.ds(b * MOE_BLOCK, MOE_BLOCK)], xbuf.at[slot],
                                     sem_in.at[slot])

    def store(b):
        slot = b % OUT_SLOTS
        return pltpu.make_async_copy(ybuf.at[slot], xs_ref.at[pl.ds(b * MOE_BLOCK, MOE_BLOCK)],
                                     sem_out.at[slot])

    @pl.when(e == 0)
    def _():
        for b in range(LOOKAHEAD):
            @pl.when(b < n_total)
            def _():
                fetch(b).start(priority=1)

    def run(blocks):
        for b in blocks:
            @pl.when(b + LOOKAHEAD < n_total)
            def _():
                fetch(b + LOOKAHEAD).start(priority=1)
        for b in blocks:
            fetch(b).wait()
        xa, xb = _unpack_halves(jnp.concatenate([xbuf[b % IN_SLOTS] for b in blocks], axis=0))
        xa = xa.astype(BF16)
        xb = xb.astype(BF16)
        wg_a = pltpu.bitcast(wg_ref[0, :half // 2, :], BF16)
        wg_b = pltpu.bitcast(wg_ref[0, half // 2:, :], BF16)
        wu_a = pltpu.bitcast(wu_ref[0, :half // 2, :], BF16)
        wu_b = pltpu.bitcast(wu_ref[0, half // 2:, :], BF16)
        gate = _dot(xa, wg_a) + _dot(xb, wg_b)
        up = _dot(xa, wu_a) + _dot(xb, wu_b)
        hidden = (gate * jax.nn.sigmoid(gate) * up).astype(BF16)
        y = _dot(hidden, pltpu.bitcast(wd_ref[0], BF16))
        packed = _pack_halves(y[:, :half], y[:, half:])
        for i, b in enumerate(blocks):
            @pl.when(b >= OUT_SLOTS)
            def _():
                store(b - OUT_SLOTS).wait()

            ybuf[b % OUT_SLOTS] = packed[i * MOE_BLOCK:(i + 1) * MOE_BLOCK]
            store(b).start(priority=1)

    n = b1 - b0

    def group(i, carry):
        run([b0 + GROUP * i + j for j in range(GROUP)])
        return carry

    lax.fori_loop(0, n // GROUP, group, 0)
    rest = b0 + (n // GROUP) * GROUP

    @pl.when(n % GROUP >= 2)
    def _():
        run([rest, rest + 1])

    @pl.when(n % 2 == 1)
    def _():
        run([b1 - 1])

    @pl.when(e == N_EXPERTS - 1)
    def _():
        for back in range(OUT_SLOTS, 0, -1):
            @pl.when(n_total >= back)
            def _():
                store(n_total - back).wait()


def _experts(block_start, xs, w_gate, w_up, w_down):
    n_rows, width = xs.shape
    wspec = lambda shape: pl.BlockSpec((1,) + shape, lambda e, bs: (e, 0, 0))
    grid_spec = pltpu.PrefetchScalarGridSpec(
        num_scalar_prefetch=1,
        grid=(N_EXPERTS,),
        in_specs=[pl.BlockSpec(memory_space=pl.ANY), wspec((D_MODEL // 2, D_EXPERT)),
                  wspec((D_MODEL // 2, D_EXPERT)), wspec((D_EXPERT // 2, D_MODEL))],
        out_specs=pl.BlockSpec(memory_space=pl.ANY),
        scratch_shapes=[pltpu.VMEM((IN_SLOTS, MOE_BLOCK, width), U32),
                        pltpu.VMEM((OUT_SLOTS, MOE_BLOCK, width), U32),
                        pltpu.SemaphoreType.DMA((IN_SLOTS,)), pltpu.SemaphoreType.DMA((OUT_SLOTS,))],
    )
    return pl.pallas_call(
        _experts_body,
        grid_spec=grid_spec,
        out_shape=jax.ShapeDtypeStruct((n_rows, width), U32),
        input_output_aliases={1: 0},
        compiler_params=pltpu.CompilerParams(dimension_semantics=("arbitrary",),
                                             vmem_limit_bytes=VMEM_LIMIT),
        name="experts",
    )(block_start, xs, w_gate, w_up, w_down)


def _combine_body(h1_ref, w_ref, yg_ref, wsg_ref, wsu_ref, wsd_ref, g2_ref, b2_ref, *refs):
    out_ref = refs[-1]
    h1 = h1_ref[...]
    hb = h1.astype(BF16)
    gate = _dot(hb, wsg_ref[...])
    up = _dot(hb, wsu_ref[...])
    shared = _dot((gate * jax.nn.sigmoid(gate) * up).astype(BF16), wsd_ref[...])

    half = D_MODEL // 2
    lo = jnp.zeros((ROWS, half), F32)
    hi = jnp.zeros((ROWS, half), F32)
    for kk in range(TOP_K):
        a, b = _unpack_halves(yg_ref[kk])
        wk = w_ref[:, kk:kk + 1]
        lo = lo + wk * a
        hi = hi + wk * b
    z = DN_ALPHA * h1 + shared + jnp.concatenate([lo, hi], axis=1)
    out_ref[...] = _layer_norm(z, g2_ref[...], b2_ref[...])


def _combine(h1, w_rows, yg, first_tile, out_prev, wsg, wsu, wsd, ln2_g, ln2_b):
    width = yg.shape[2]
    d_shared = wsg.shape[1]
    rows = lambda cols: pl.BlockSpec((ROWS, cols), lambda i: (first_tile + i, 0))
    args = [h1, w_rows, yg, wsg, wsu, wsd, ln2_g, ln2_b]
    in_specs = [rows(D_MODEL), rows(TOP_K),
                pl.BlockSpec((TOP_K, ROWS, width), lambda i: (0, i, 0)),
                _const_spec((D_MODEL, d_shared)), _const_spec((D_MODEL, d_shared)),
                _const_spec((d_shared, D_MODEL)),
                _const_spec((1, D_MODEL)), _const_spec((1, D_MODEL))]
    aliases = {}
    if out_prev is not None:
        aliases = {len(args): 0}
        args.append(out_prev)
        in_specs.append(pl.BlockSpec(memory_space=pl.ANY))
    return pl.pallas_call(
        _combine_body,
        grid=(yg.shape[1] // ROWS,),
        in_specs=in_specs,
        out_specs=rows(D_MODEL),
        out_shape=jax.ShapeDtypeStruct(h1.shape, F32),
        input_output_aliases=aliases,
        compiler_params=pltpu.CompilerParams(dimension_semantics=("arbitrary",),
                                             vmem_limit_bytes=VMEM_LIMIT),
        name="combine",
    )(*args)


def _pack_in_proj(w_in, b_in, gate_w_f, gate_w_b, gate_b_f, gate_b_b):
    o_pool, o_q, o_k, o_v, o_g = 0, 512, 1024, 1536, 2560
    o_lr, o_gp, o_gg = 3584, 3616, 4640
    order = [(o_pool, POOL_WIDTH), (o_q, GLA_DK), (o_k, GLA_DK), (o_v, GLA_DV), (o_g, GLA_DV),
             (o_gp, D_MODEL), (o_gg, D_MODEL)] + [(o_lr, 2 * GATE_RANK)] * 3
    pad = LANE - 6 * GATE_RANK
    w = jnp.concatenate([w_in[:, o:o + n] for o, n in order] + [jnp.zeros((D_MODEL, pad), F32)], axis=1)
    b = jnp.concatenate([b_in[o:o + n] for o, n in order] + [jnp.zeros((pad,), F32)])
    w_gate = jnp.zeros((2 * GATE_RANK, 2 * GLA_DK), F32)
    w_gate = w_gate.at[:GATE_RANK, :GLA_DK].set(gate_w_f)
    w_gate = w_gate.at[GATE_RANK:, GLA_DK:].set(gate_w_b)
    w_hi = w_gate.astype(BF16)
    w_lo = (w_gate - w_hi.astype(F32)).astype(BF16)
    w_gate3 = jnp.concatenate([w_hi, w_hi, w_lo, jnp.zeros((pad, 2 * GLA_DK), BF16)], axis=0)
    b_gate = jnp.concatenate([gate_b_f, gate_b_b])[None, :]
    return w.astype(BF16), b[None, :], w_gate3, b_gate


def kernel(x, meta, ln0_g, ln0_b, w_in, b_in, pool_w, pool_scale, gate_w_f, gate_b_f, gate_w_b, gate_b_b,
           gla_norm_g, w_branch_pool, w_branch_gla, w_out, ln1_g, ln1_b, w_router, router_bias,
           w_exp_gate, w_exp_up, w_exp_down, w_sh_gate, w_sh_up, w_sh_down, ln2_g, ln2_b):
    batch, seq, d = x.shape
    assert d == D_MODEL and seq % ROWS == 0 and w_in.shape[0] == 1
    assert (batch * seq) % ROUTE_ROWS == 0
    t = batch * seq
    row = lambda a: a.reshape(1, -1).astype(F32)

    expert_w = _cast_weights(w_exp_gate[0], w_exp_up[0], w_exp_down[0])

    lead = jnp.concatenate([jnp.zeros((FRONT - N_META, d), x.dtype), meta.astype(x.dtype)], axis=0)
    tiles_per_batch = (seq + FRONT) // ROWS
    x2 = x.reshape(t, d)

    w_packed, b_packed, w_gate, b_gate = _pack_in_proj(
        w_in[0], b_in[0], gate_w_f[0], gate_w_b[0], gate_b_f[0], gate_b_b[0])
    u, q, k, v, sg, gp, gg, bf, cb = _in_proj(
        lead, x2, row(ln0_g), row(ln0_b), w_packed, b_packed, w_gate, b_gate, tiles_per_batch)

    o_f, o_b = _gla(q, k, v, bf, cb, batch)

    h1, h1p = _mix(
        x.reshape(t, d), u, gp, gg, sg, o_f, o_b, row(ln0_g), row(ln0_b),
        pool_w[0].astype(BF16), row(pool_scale[0]), row(jnp.tile(gla_norm_g[0], GLA_HEADS)),
        w_branch_pool[0].astype(BF16), w_branch_gla[0].astype(BF16), w_out[0].astype(BF16),
        row(ln1_g[0]), row(ln1_b[0]), seq)

    wr = w_router[0].T.astype(F32)
    wr_hi = wr.astype(BF16)
    wr_mid = (wr - wr_hi.astype(F32)).astype(BF16)
    wr_terms = jnp.concatenate([wr_hi, wr_hi, wr_mid], axis=1)
    eidx, ew, before, counts = _route(h1, wr_terms, router_bias[0].reshape(N_EXPERTS, 1))

    counts = counts[:, 0]
    padded = ((counts + MOE_BLOCK - 1) // MOE_BLOCK) * MOE_BLOCK
    pend = jnp.cumsum(padded)
    pstart = pend - padded
    dest = _dest(eidx, before, pstart.astype(F32).reshape(N_EXPERTS, 1))
    n_blocks = -(-(t * TOP_K) // MOE_BLOCK) + N_EXPERTS
    block_start = (jnp.concatenate([pstart, pend[-1:]]) // MOE_BLOCK).astype(I32)

    xs = _dispatch(dest, h1p, n_blocks * MOE_BLOCK)
    ys = _experts(block_start, xs, *expert_w)
    shared_w = (w_sh_gate[0].astype(BF16), w_sh_up[0].astype(BF16), w_sh_down[0].astype(BF16))
    ew_rows = ew.T
    windows = t // SC_CHUNK // COMBINE_CHUNKS
    out = None
    for c in range(COMBINE_CHUNKS):
        yg = _gather(dest[c * windows:(c + 1) * windows], ys)
        out = _combine(h1, ew_rows, yg, c * windows * SC_CHUNK // ROWS, out, *shared_w,
                       row(ln2_g[0]), row(ln2_b[0]))
    return out.reshape(batch, seq, d)
```

```python
import functools

import jax
import jax.numpy as jnp
from jax import lax
from jax.experimental import pallas as pl
from jax.experimental.pallas import tpu as pltpu
from jax.experimental.pallas import tpu_sc as plsc

F32 = jnp.float32
BF16 = jnp.bfloat16
I32 = jnp.int32
U32 = jnp.uint32

D_MODEL = 1024
N_META = 16
POOL_WINDOWS = (2, 4, 8, 16)
POOL_GROUP = 128
POOL_WIDTH = POOL_GROUP * len(POOL_WINDOWS)
GLA_HEADS = 4
GLA_DK = 512
GLA_DV = 1024
HEAD_K = GLA_DK // GLA_HEADS
HEAD_V = GLA_DV // GLA_HEADS
GATE_RANK = 16
GATE_NORMALIZER = 16.0
RMS_EPS = 1e-5
N_EXPERTS = 256
TOP_K = 8
N_GROUPS = 8
GROUP_SIZE = N_EXPERTS // N_GROUPS
TOPK_GROUPS = 4
D_EXPERT = 256
ROUTED_SCALE = 2.5
DN_ALPHA = 2.0 ** 0.25
LN_EPS = 1e-5

LANE = 128
ROWS = 256
FRONT = ROWS
GLA_CHUNK = 256
HALO = 16
ROUTE_ROWS = 512
MOE_BLOCK = 256
LOOKAHEAD = 6
GROUP = 4
CHAIN = 2
IN_SLOTS = LOOKAHEAD + GROUP
OUT_SLOTS = GROUP
SC_CORES = 2
SC_SUBCORES = 16
SC_CHUNK = 128
COMBINE_CHUNKS = 8
VMEM_LIMIT = 56 * 1024 * 1024

C_POOL, C_Q, C_K, C_V, C_G, C_GP, C_GG, C_LR = 0, 512, 1024, 1536, 2560, 3584, 4608, 5632
W_PACKED = C_LR + LANE


def _layer_norm(x, g, b):
    mu = jnp.mean(x, axis=-1, keepdims=True)
    xc = x - mu
    var = jnp.mean(xc * xc, axis=-1, keepdims=True)
    return xc * lax.rsqrt(var + LN_EPS) * g + b


def _dot(a, b):
    return jnp.dot(a, b, preferred_element_type=F32)


def _dot_nt(a, b):
    return lax.dot_general(a, b, (((1,), (1,)), ((), ())), preferred_element_type=F32)


def _split_dot(m2, g):
    g_hi = g.astype(BF16)
    g_lo = (g - g_hi.astype(F32)).astype(BF16)
    return _dot(m2, jnp.concatenate([g_hi, g_lo], axis=0))


def _pack_halves(a, b):
    pa = lax.bitcast_convert_type(a.astype(BF16).astype(F32), U32)
    pb = lax.bitcast_convert_type(b.astype(BF16).astype(F32), U32)
    return pa | (pb >> 16)


def _unpack_halves(p):
    a = lax.bitcast_convert_type(p & jnp.uint32(0xFFFF0000), F32)
    b = lax.bitcast_convert_type(p << 16, F32)
    return a, b


def _const_spec(shape):
    return pl.BlockSpec(shape, lambda *_: (0,) * len(shape))


def _inproj_body(lead_ref, x_ref, g0_ref, b0_ref, w_ref, bias_ref, wgate_ref, bgate_ref,
                 u_ref, q_ref, k_ref, v_ref, sg_ref, gp_ref, gg_ref, bf_ref, cb_ref,
                 *, tiles_per_batch):
    i = pl.program_id(0)
    x_in = jnp.where(i % tiles_per_batch == 0, lead_ref[...], x_ref[...])
    h = _layer_norm(x_in, g0_ref[...], b0_ref[...])
    hb = h.astype(BF16)

    def proj(c0, n):
        return _dot(hb, w_ref[:, c0:c0 + n]) + bias_ref[:, c0:c0 + n]

    row = lax.broadcasted_iota(I32, (ROWS, 1), 0)
    valid = row >= jnp.where(i % tiles_per_batch == 0, FRONT - N_META, 0)

    lr3 = proj(C_LR, LANE)
    g = proj(C_G, GLA_DV)
    sg_ref[...] = (g * jax.nn.sigmoid(g)).astype(BF16)
    lr_hi = lr3.astype(BF16)
    lr_lo = (lr3 - lr_hi.astype(F32)).astype(BF16)
    lane = lax.broadcasted_iota(I32, (1, LANE), 1)
    second = jnp.logical_and(lane >= 2 * GATE_RANK, lane < 4 * GATE_RANK)
    xg = _dot(jnp.where(second, lr_lo, lr_hi), wgate_ref[...]) + bgate_ref[...]
    gp_ref[...] = jax.nn.sigmoid(proj(C_GP, D_MODEL)).astype(BF16)
    gk = (jnp.minimum(xg, 0.0) - jnp.log(1.0 + jnp.exp(-jnp.abs(xg)))) * (1.0 / GATE_NORMALIZER)
    gg_ref[...] = jax.nn.sigmoid(proj(C_GG, D_MODEL)).astype(BF16)

    r = lax.broadcasted_iota(I32, (GLA_CHUNK, 2 * GLA_CHUNK), 0)
    c = lax.broadcasted_iota(I32, (GLA_CHUNK, 2 * GLA_CHUNK), 1) % GLA_CHUNK
    tril = (r >= c).astype(BF16)
    triu = (r <= c).astype(BF16)
    for ci in range(ROWS // GLA_CHUNK):
        sl = slice(ci * GLA_CHUNK, (ci + 1) * GLA_CHUNK)
        bf_ref[sl, :] = _split_dot(tril, gk[sl, :GLA_DK])
        cb_ref[sl, :] = _split_dot(triu, gk[sl, GLA_DK:])

    v_ref[...] = jnp.where(valid, proj(C_V, GLA_DV), 0.0).astype(BF16)
    k_ref[...] = jnp.where(valid, proj(C_K, GLA_DK), 0.0).astype(BF16)
    q_ref[...] = (proj(C_Q, GLA_DK) * (HEAD_K ** -0.5)).astype(BF16)
    u_ref[...] = proj(C_POOL, POOL_WIDTH).astype(BF16)


def _in_proj(lead, x2, ln0_g, ln0_b, w_packed, b_packed, w_gate, b_gate, tiles_per_batch):
    x_tiles = tiles_per_batch - FRONT // ROWS
    n = x2.shape[0] // (x_tiles * ROWS) * tiles_per_batch * ROWS
    row_spec = lambda width: pl.BlockSpec((ROWS, width), lambda i: (i, 0))
    x_spec = pl.BlockSpec(
        (ROWS, D_MODEL),
        lambda i: ((i // tiles_per_batch) * x_tiles + jnp.maximum(i % tiles_per_batch - 1, 0), 0))
    out_widths = (POOL_WIDTH, GLA_DK, GLA_DK, GLA_DV, GLA_DV, D_MODEL, D_MODEL)
    out_shape = [jax.ShapeDtypeStruct((n, w), BF16) for w in out_widths]
    out_shape += [jax.ShapeDtypeStruct((n, GLA_DK), F32)] * 2
    out_specs = [row_spec(w) for w in out_widths] + [row_spec(GLA_DK)] * 2
    return pl.pallas_call(
        functools.partial(_inproj_body, tiles_per_batch=tiles_per_batch),
        grid=(n // ROWS,),
        in_specs=[_const_spec((FRONT, D_MODEL)), x_spec,
                  _const_spec((1, D_MODEL)), _const_spec((1, D_MODEL)),
                  _const_spec((D_MODEL, W_PACKED)), _const_spec((1, W_PACKED)),
                  _const_spec((LANE, 2 * GLA_DK)), _const_spec((1, 2 * GLA_DK))],
        out_specs=out_specs,
        out_shape=out_shape,
        compiler_params=pltpu.CompilerParams(dimension_semantics=("arbitrary",),
                                             vmem_limit_bytes=VMEM_LIMIT),
        name="in_proj",
    )(lead, x2, ln0_g, ln0_b, w_packed, b_packed, w_gate, b_gate)


def _gla_direction(q_ref, k_ref, v_ref, b_ref, o_ref, s_ref, reverse):
    c = GLA_CHUNK
    r = lax.broadcasted_iota(I32, (c, c), 0)
    s = lax.broadcasted_iota(I32, (c, c), 1)
    keep = (r <= s) if reverse else (r >= s)
    eye = lax.broadcasted_iota(I32, (HEAD_K, HEAD_K), 0) == lax.broadcasted_iota(I32, (HEAD_K, HEAD_K), 1)
    edge = 0 if reverse else c - 1
    for h in range(GLA_HEADS):
        ks = slice(h * HEAD_K, (h + 1) * HEAD_K)
        vs = slice(h * HEAD_V, (h + 1) * HEAD_V)
        b = b_ref[:, ks]
        qh = q_ref[:, ks].astype(F32)
        kh = k_ref[:, ks].astype(F32)
        vh = v_ref[:, vs]
        b_tot = b[edge:edge + 1, :]
        b_mid = b[c // 2:c // 2 + 1, :]
        q_state = (qh * jnp.exp(b)).astype(BF16)
        q_in = (qh * jnp.exp(b - b_mid)).astype(BF16)
        k_in = (kh * jnp.exp(b_mid - b)).astype(BF16)
        k_state = (kh * jnp.exp(b_tot - b)).T.astype(BF16)
        att = jnp.where(keep, _dot_nt(q_in, k_in), 0.0).astype(BF16)
        state = s_ref[h]
        o = _dot(q_state, state.astype(BF16)) + _dot(att, vh)
        o_ref[:, vs] = o.astype(o_ref.dtype)
        decay_col = jnp.sum(jnp.where(eye, jnp.exp(b_tot), 0.0), axis=1, keepdims=True)
        s_ref[h] = decay_col * state + _dot(k_state, vh)


def _gla_body(qf_ref, kf_ref, vf_ref, bf_ref, qb_ref, kb_ref, vb_ref, cb_ref,
              of_ref, ob_ref, sf_ref, sb_ref):
    @pl.when(pl.program_id(1) == 0)
    def _():
        sf_ref[...] = jnp.zeros_like(sf_ref)
        sb_ref[...] = jnp.zeros_like(sb_ref)

    _gla_direction(qf_ref, kf_ref, vf_ref, bf_ref, of_ref, sf_ref, reverse=False)
    _gla_direction(qb_ref, kb_ref, vb_ref, cb_ref, ob_ref, sb_ref, reverse=True)


def _gla(q, k, v, bf, cb, batch):
    n = q.shape[0]
    nch = n // batch // GLA_CHUNK
    fwd = lambda width: pl.BlockSpec((GLA_CHUNK, width), lambda b, i: (b * nch + i, 0))
    bwd = lambda width: pl.BlockSpec((GLA_CHUNK, width), lambda b, i: (b * nch + nch - 1 - i, 0))
    state = pltpu.VMEM((GLA_HEADS, HEAD_K, HEAD_V), F32)
    return pl.pallas_call(
        _gla_body,
        grid=(batch, nch),
        in_specs=[fwd(GLA_DK), fwd(GLA_DK), fwd(GLA_DV), fwd(GLA_DK),
                  bwd(GLA_DK), bwd(GLA_DK), bwd(GLA_DV), bwd(GLA_DK)],
        out_specs=[fwd(GLA_DV), bwd(GLA_DV)],
        out_shape=[jax.ShapeDtypeStruct((n, GLA_DV), BF16)] * 2,
        scratch_shapes=[state, state],
        compiler_params=pltpu.CompilerParams(dimension_semantics=("arbitrary", "arbitrary"),
                                             vmem_limit_bytes=VMEM_LIMIT),
        name="gla",
    )(q, k, v, bf, q, k, v, cb)


def _mix_body(x_ref, up_ref, u_ref, un_ref, gp_ref, gg_ref, sg_ref, of_ref, ob_ref,
              g0_ref, b0_ref, pw_ref, ps_ref, gn_ref, wbp_ref, wbg_ref, wo_ref, g1_ref, b1_ref,
              h1_ref, h1p_ref, *, tiles_per_batch, seq):
    j = pl.program_id(0) % tiles_per_batch
    u_main = u_ref[...]
    u_ext = jnp.concatenate([up_ref[...], u_main, un_ref[...]], axis=0)
    r = lax.broadcasted_iota(I32, (ROWS, ROWS + 2 * HALO), 0)
    e = lax.broadcasted_iota(I32, (ROWS, ROWS + 2 * HALO), 1) - HALO
    in_seq = (j * ROWS + e) < seq
    pos = j * ROWS + lax.broadcasted_iota(I32, (ROWS, 1), 0)
    y_pool = []
    for gi, w in enumerate(POOL_WINDOWS):
        cs = slice(gi * POOL_GROUP, (gi + 1) * POOL_GROUP)
        band = jnp.logical_and(jnp.logical_and(e >= r - w // 2, e < r + w - w // 2), in_seq)
        count = w - jnp.maximum(pos + (w - w // 2) - seq, 0)
        mean = _dot(band.astype(BF16), u_ext[:, cs]) / count.astype(F32)
        d = mean - u_main[:, cs].astype(F32)
        y_pool.append(_dot(d.astype(BF16), pw_ref[gi]))
    y_pool = (jnp.concatenate(y_pool, axis=1) * ps_ref[...]).astype(BF16)

    o = of_ref[...].astype(F32) + ob_ref[...].astype(F32)
    y_gla = []
    for hd in range(GLA_HEADS):
        oh = o[:, hd * HEAD_V:(hd + 1) * HEAD_V]
        y_gla.append(oh * lax.rsqrt(jnp.mean(oh * oh, axis=-1, keepdims=True) + RMS_EPS))
    y_gla = (jnp.concatenate(y_gla, axis=1) * gn_ref[...] * sg_ref[...].astype(F32)).astype(BF16)

    merged = (gp_ref[...].astype(F32) * _dot(y_pool, wbp_ref[...])
              + gg_ref[...].astype(F32) * _dot(y_gla, wbg_ref[...]))
    y = _dot(merged.astype(BF16), wo_ref[...])
    h = _layer_norm(x_ref[...], g0_ref[...], b0_ref[...])
    h1 = _layer_norm(DN_ALPHA * h + y, g1_ref[...], b1_ref[...])
    h1_ref[...] = h1
    h1p_ref[...] = _pack_halves(h1[:, :D_MODEL // 2], h1[:, D_MODEL // 2:])


def _mix(x2, u, gp, gg, sg, o_f, o_b, ln0_g, ln0_b, pool_w, pool_scale, norm_g, wbp, wbg, wo,
         ln1_g, ln1_b, seq):
    t = x2.shape[0]
    tiles_per_batch = seq // ROWS
    padded_tiles = tiles_per_batch + FRONT // ROWS
    halo_per_tile = ROWS // HALO
    last_halo = u.shape[0] // HALO - 1

    def pidx(i):
        return (i // tiles_per_batch) * padded_tiles + FRONT // ROWS + i % tiles_per_batch

    padded = lambda width: pl.BlockSpec((ROWS, width), lambda i: (pidx(i), 0))
    plain = lambda width: pl.BlockSpec((ROWS, width), lambda i: (i, 0))
    prev_halo = pl.BlockSpec((HALO, POOL_WIDTH), lambda i: (pidx(i) * halo_per_tile - 1, 0))
    next_halo = pl.BlockSpec(
        (HALO, POOL_WIDTH), lambda i: (jnp.minimum((pidx(i) + 1) * halo_per_tile, last_halo), 0))
    return pl.pallas_call(
        functools.partial(_mix_body, tiles_per_batch=tiles_per_batch, seq=seq),
        grid=(t // ROWS,),
        in_specs=[plain(D_MODEL), prev_halo, padded(POOL_WIDTH), next_halo,
                  padded(D_MODEL), padded(D_MODEL), padded(GLA_DV), padded(GLA_DV), padded(GLA_DV),
                  _const_spec((1, D_MODEL)), _const_spec((1, D_MODEL)),
                  _const_spec((len(POOL_WINDOWS), POOL_GROUP, POOL_GROUP)), _const_spec((1, POOL_WIDTH)),
                  _const_spec((1, GLA_DV)), _const_spec((POOL_WIDTH, D_MODEL)),
                  _const_spec((GLA_DV, D_MODEL)), _const_spec((D_MODEL, D_MODEL)),
                  _const_spec((1, D_MODEL)), _const_spec((1, D_MODEL))],
        out_specs=[plain(D_MODEL), plain(D_MODEL // 2)],
        out_shape=[jax.ShapeDtypeStruct((t, D_MODEL), F32), jax.ShapeDtypeStruct((t, D_MODEL // 2), U32)],
        compiler_params=pltpu.CompilerParams(dimension_semantics=("arbitrary",),
                                             vmem_limit_bytes=VMEM_LIMIT),
        name="mix",
    )(x2, u, u, u, gp, gg, sg, o_f, o_b, ln0_g, ln0_b, pool_w, pool_scale, norm_g, wbp, wbg, wo,
      ln1_g, ln1_b)


def _first_index_of_max(val, idx, size):
    m = jnp.max(val, axis=0, keepdims=True)
    first = jnp.min(jnp.where(val == m, idx, size), axis=0, keepdims=True)
    return m, first


def _route_body(h_ref, wr_ref, bias_ref, eidx_ref, w_ref, before_ref, cnt_ref, base_ref):
    n = ROUTE_ROWS

    @pl.when(pl.program_id(0) == 0)
    def _():
        base_ref[...] = jnp.zeros_like(base_ref)

    h = h_ref[...]
    h_hi = h.astype(BF16)
    h_mid = (h - h_hi.astype(F32)).astype(BF16)
    logits = _dot_nt(wr_ref[...], jnp.concatenate([h_hi, h_mid, h_hi], axis=1))
    scores = jax.nn.sigmoid(logits)
    sel = scores + bias_ref[...]

    gidx = lax.broadcasted_iota(I32, (GROUP_SIZE, n), 0)
    groups = [sel[g * GROUP_SIZE:(g + 1) * GROUP_SIZE, :] for g in range(N_GROUPS)]
    gscore = []
    for grp in groups:
        m1, first = _first_index_of_max(grp, gidx, GROUP_SIZE)
        m2 = jnp.max(jnp.where(gidx == first, -jnp.inf, grp), axis=0, keepdims=True)
        gscore.append(m1 + m2)

    masked = []
    for gi in range(N_GROUPS):
        beaten = jnp.zeros((1, n), F32)
        for gj in range(N_GROUPS):
            if gj != gi:
                wins = (gscore[gj] >= gscore[gi]) if gj < gi else (gscore[gj] > gscore[gi])
                beaten = beaten + wins.astype(F32)
        masked.append(jnp.where(beaten < TOPK_GROUPS, groups[gi], -jnp.inf))
    val = jnp.concatenate(masked, axis=0)

    eidx = lax.broadcasted_iota(I32, (N_EXPERTS, n), 0)
    candidate = val != -jnp.inf
    picks, weights = [], []
    for _ in range(TOP_K):
        _, first = _first_index_of_max(val, eidx, N_EXPERTS)
        hit = eidx == first
        picks.append(first)
        weights.append(jnp.sum(jnp.where(hit, scores, 0.0), axis=0, keepdims=True))
        val = jnp.where(hit, -jnp.inf, val)
    chosen_f = jnp.where(jnp.logical_and(candidate, val == -jnp.inf), 1.0, 0.0)
    wsum = weights[0]
    for wk in weights[1:]:
        wsum = wsum + wk
    scale = ROUTED_SCALE / wsum

    r = lax.broadcasted_iota(I32, (n, n), 0)
    c = lax.broadcasted_iota(I32, (n, n), 1)
    earlier = (r < c).astype(BF16)
    before_ref[...] = _dot(chosen_f.astype(BF16), earlier) + base_ref[...]
    base_ref[...] = base_ref[...] + jnp.sum(chosen_f, axis=1, keepdims=True)
    cnt_ref[...] = jnp.broadcast_to(base_ref[...], cnt_ref.shape).astype(I32)

    for kk in range(TOP_K):
        eidx_ref[kk:kk + 1, :] = picks[kk]
        w_ref[kk:kk + 1, :] = weights[kk] * scale


def _route(h1, wr_t, bias_col):
    t = h1.shape[0]
    col = lambda dtype: jax.ShapeDtypeStruct((TOP_K, t), dtype)
    kspec = pl.BlockSpec((TOP_K, ROUTE_ROWS), lambda i: (0, i))
    return pl.pallas_call(
        _route_body,
        grid=(t // ROUTE_ROWS,),
        in_specs=[pl.BlockSpec((ROUTE_ROWS, D_MODEL), lambda i: (i, 0)),
                  _const_spec((N_EXPERTS, 3 * D_MODEL)), _const_spec((N_EXPERTS, 1))],
        out_specs=[kspec, kspec, pl.BlockSpec((N_EXPERTS, ROUTE_ROWS), lambda i: (0, i)),
                   _const_spec((N_EXPERTS, LANE))],
        out_shape=[col(I32), col(F32), jax.ShapeDtypeStruct((N_EXPERTS, t), F32),
                   jax.ShapeDtypeStruct((N_EXPERTS, LANE), I32)],
        scratch_shapes=[pltpu.VMEM((N_EXPERTS, 1), F32)],
        compiler_params=pltpu.CompilerParams(dimension_semantics=("arbitrary",),
                                             vmem_limit_bytes=VMEM_LIMIT),
        name="route",
    )(h1, wr_t, bias_col)


def _dest_body(eidx_ref, before_ref, pstart_ref, dest_ref):
    n = eidx_ref.shape[1]
    eiota = lax.broadcasted_iota(I32, (N_EXPERTS, n), 0)
    row_of = pstart_ref[...] + before_ref[...]
    for kk in range(TOP_K):
        dest = jnp.sum(jnp.where(eiota == eidx_ref[kk:kk + 1, :], row_of, 0.0), axis=0, keepdims=True)
        dest = dest.astype(I32)
        for c in range(n // SC_CHUNK):
            dest_ref[c, kk:kk + 1, :] = dest[:, c * SC_CHUNK:(c + 1) * SC_CHUNK]


def _dest(eidx, before, pstart_col):
    t = eidx.shape[1]
    kspec = pl.BlockSpec((TOP_K, ROUTE_ROWS), lambda i: (0, i))
    chunks = ROUTE_ROWS // SC_CHUNK
    return pl.pallas_call(
        _dest_body,
        grid=(t // ROUTE_ROWS,),
        in_specs=[kspec, pl.BlockSpec((N_EXPERTS, ROUTE_ROWS), lambda i: (0, i)),
                  _const_spec((N_EXPERTS, 1))],
        out_specs=pl.BlockSpec((chunks, TOP_K, SC_CHUNK), lambda i: (i, 0, 0)),
        out_shape=jax.ShapeDtypeStruct((t // SC_CHUNK, TOP_K, SC_CHUNK), I32),
        compiler_params=pltpu.CompilerParams(dimension_semantics=("arbitrary",)),
        name="dest",
    )(eidx, before, pstart_col)


def _sc_mesh():
    return plsc.VectorSubcoreMesh(core_axis_name="c", subcore_axis_name="s",
                                  num_cores=SC_CORES, num_subcores=SC_SUBCORES)


def _sc_worker_chunks(t):
    assert t % (SC_CORES * SC_SUBCORES * SC_CHUNK) == 0
    per_worker = t // (SC_CORES * SC_SUBCORES * SC_CHUNK)
    worker = lax.axis_index("s") * SC_CORES + lax.axis_index("c")
    return worker * per_worker, per_worker


def _dispatch(dest, h1p, n_rows):
    t, width = h1p.shape

    @functools.partial(
        pl.kernel, mesh=_sc_mesh(), out_type=jax.ShapeDtypeStruct((n_rows, width), U32),
        scratch_types=[pltpu.VMEM((SC_CHUNK, width), U32), pltpu.VMEM((TOP_K, SC_CHUNK), I32),
                       pltpu.SemaphoreType.DMA],
        name="dispatch")
    def body(h_hbm, dest_hbm, xs_hbm, rows_v, idx_v, sem):
        first, count = _sc_worker_chunks(t)

        @pl.loop(0, count)
        def _(i):
            chunk = first + i
            pltpu.sync_copy(h_hbm.at[pl.ds(pl.multiple_of(chunk * SC_CHUNK, SC_CHUNK), SC_CHUNK)], rows_v)
            pltpu.sync_copy(dest_hbm.at[chunk], idx_v)
            copies = [pltpu.async_copy(rows_v, xs_hbm.at[idx_v.at[kk]], sem) for kk in range(TOP_K)]
            for c in copies:
                c.wait()

    return body(h1p, dest)


def _gather(dest, ys):
    t = dest.shape[0] * SC_CHUNK
    width = ys.shape[1]

    @functools.partial(
        pl.kernel, mesh=_sc_mesh(), out_type=jax.ShapeDtypeStruct((TOP_K, t, width), U32),
        scratch_types=[pltpu.VMEM((SC_CHUNK, width), U32), pltpu.VMEM((TOP_K, SC_CHUNK), I32),
                       pltpu.SemaphoreType.DMA],
        name="gather")
    def body(ys_hbm, dest_hbm, yg_hbm, rows_v, idx_v, sem):
        first, count = _sc_worker_chunks(t)

        @pl.loop(0, count)
        def _(i):
            chunk = first + i
            pltpu.sync_copy(dest_hbm.at[chunk], idx_v)
            for kk in range(TOP_K):
                pltpu.async_copy(ys_hbm.at[idx_v.at[kk]], rows_v, sem).wait()
                pltpu.sync_copy(
                    rows_v, yg_hbm.at[kk, pl.ds(pl.multiple_of(chunk * SC_CHUNK, SC_CHUNK), SC_CHUNK)])

    return body(ys, dest)


def _experts_body(bs_ref, xs_in_ref, wg_ref, wu_ref, wd_ref, xs_ref,
                  xbuf, ybuf, wgb, wub, wdb, sem_in, sem_out):
    del xs_in_ref
    e = pl.program_id(0)
    b0 = bs_ref[e]
    b1 = bs_ref[e + 1]
    n_total = bs_ref[N_EXPERTS]
    half = D_MODEL // 2

    def fetch(b):
        slot = b % IN_SLOTS
        return pltpu.make_async_copy(xs_ref.at[pl.ds(b * MOE_BLOCK, MOE_BLOCK)], xbuf.at[slot],
                                     sem_in.at[slot])

    def store(b):
        slot = b % OUT_SLOTS
        return pltpu.make_async_copy(ybuf.at[slot], xs_ref.at[pl.ds(b * MOE_BLOCK, MOE_BLOCK)],
                                     sem_out.at[slot])

    @pl.when(e == 0)
    def _():
        for b in range(LOOKAHEAD):
            @pl.when(b < n_total)
            def _():
                fetch(b).start(priority=1)

    @pl.when(b1 > b0)
    def _():
        wgb[...] = wg_ref[0].astype(BF16)
        wub[...] = wu_ref[0].astype(BF16)
        wdb[...] = wd_ref[0].astype(BF16)

    def run(blocks):
        for b in blocks:
            @pl.when(b + LOOKAHEAD < n_total)
            def _():
                fetch(b + LOOKAHEAD).start(priority=1)
        for b in blocks:
            fetch(b).wait()
        for b in blocks:
            @pl.when(b >= OUT_SLOTS)
            def _():
                store(b - OUT_SLOTS).wait()

        for c0 in range(0, len(blocks), CHAIN):
            chain = blocks[c0:c0 + CHAIN]
            xa, xb = _unpack_halves(jnp.concatenate([xbuf[b % IN_SLOTS] for b in chain], axis=0))
            xa = xa.astype(BF16)
            xb = xb.astype(BF16)
            gate = _dot(xa, wgb[:half, :]) + _dot(xb, wgb[half:, :])
            up = _dot(xa, wub[:half, :]) + _dot(xb, wub[half:, :])
            hidden = (gate * jax.nn.sigmoid(gate) * up).astype(BF16)
            y = _dot(hidden, wdb[...])
            packed = _pack_halves(y[:, :half], y[:, half:])
            for i, b in enumerate(chain):
                ybuf[b % OUT_SLOTS] = packed[i * MOE_BLOCK:(i + 1) * MOE_BLOCK]
                store(b).start(priority=1)

    n = b1 - b0

    def group(i, carry):
        run([b0 + GROUP * i + j for j in range(GROUP)])
        return carry

    lax.fori_loop(0, n // GROUP, group, 0)
    rest = b0 + (n // GROUP) * GROUP

    @pl.when(n % GROUP >= 2)
    def _():
        run([rest, rest + 1])

    @pl.when(n % 2 == 1)
    def _():
        run([b1 - 1])

    @pl.when(e == N_EXPERTS - 1)
    def _():
        for back in range(OUT_SLOTS, 0, -1):
            @pl.when(n_total >= back)
            def _():
                store(n_total - back).wait()


def _experts(block_start, xs, w_gate, w_up, w_down):
    n_rows, width = xs.shape
    wspec = lambda shape: pl.BlockSpec((1,) + shape, lambda e, bs: (e, 0, 0))
    grid_spec = pltpu.PrefetchScalarGridSpec(
        num_scalar_prefetch=1,
        grid=(N_EXPERTS,),
        in_specs=[pl.BlockSpec(memory_space=pl.ANY),
                  wspec((D_MODEL, D_EXPERT)), wspec((D_MODEL, D_EXPERT)), wspec((D_EXPERT, D_MODEL))],
        out_specs=pl.BlockSpec(memory_space=pl.ANY),
        scratch_shapes=[pltpu.VMEM((IN_SLOTS, MOE_BLOCK, width), U32),
                        pltpu.VMEM((OUT_SLOTS, MOE_BLOCK, width), U32),
                        pltpu.VMEM((D_MODEL, D_EXPERT), BF16), pltpu.VMEM((D_MODEL, D_EXPERT), BF16),
                        pltpu.VMEM((D_EXPERT, D_MODEL), BF16),
                        pltpu.SemaphoreType.DMA((IN_SLOTS,)), pltpu.SemaphoreType.DMA((OUT_SLOTS,))],
    )
    return pl.pallas_call(
        _experts_body,
        grid_spec=grid_spec,
        out_shape=jax.ShapeDtypeStruct((n_rows, width), U32),
        input_output_aliases={1: 0},
        compiler_params=pltpu.CompilerParams(dimension_semantics=("arbitrary",),
                                             vmem_limit_bytes=VMEM_LIMIT),
        name="experts",
    )(block_start, xs, w_gate, w_up, w_down)


def _combine_body(h1_ref, w_ref, yg_ref, wsg_ref, wsu_ref, wsd_ref, g2_ref, b2_ref, *refs):
    out_ref = refs[-1]
    h1 = h1_ref[...]
    hb = h1.astype(BF16)
    gate = _dot(hb, wsg_ref[...])
    up = _dot(hb, wsu_ref[...])
    shared = _dot((gate * jax.nn.sigmoid(gate) * up).astype(BF16), wsd_ref[...])

    half = D_MODEL // 2
    lo = jnp.zeros((ROWS, half), F32)
    hi = jnp.zeros((ROWS, half), F32)
    for kk in range(TOP_K):
        a, b = _unpack_halves(yg_ref[kk])
        wk = w_ref[:, kk:kk + 1]
        lo = lo + wk * a
        hi = hi + wk * b
    z = DN_ALPHA * h1 + shared + jnp.concatenate([lo, hi], axis=1)
    out_ref[...] = _layer_norm(z, g2_ref[...], b2_ref[...])


def _combine(h1, w_rows, yg, first_tile, out_prev, wsg, wsu, wsd, ln2_g, ln2_b):
    width = yg.shape[2]
    d_shared = wsg.shape[1]
    rows = lambda cols: pl.BlockSpec((ROWS, cols), lambda i: (first_tile + i, 0))
    args = [h1, w_rows, yg, wsg, wsu, wsd, ln2_g, ln2_b]
    in_specs = [rows(D_MODEL), rows(TOP_K),
                pl.BlockSpec((TOP_K, ROWS, width), lambda i: (0, i, 0)),
                _const_spec((D_MODEL, d_shared)), _const_spec((D_MODEL, d_shared)),
                _const_spec((d_shared, D_MODEL)),
                _const_spec((1, D_MODEL)), _const_spec((1, D_MODEL))]
    aliases = {}
    if out_prev is not None:
        aliases = {len(args): 0}
        args.append(out_prev)
        in_specs.append(pl.BlockSpec(memory_space=pl.ANY))
    return pl.pallas_call(
        _combine_body,
        grid=(yg.shape[1] // ROWS,),
        in_specs=in_specs,
        out_specs=rows(D_MODEL),
        out_shape=jax.ShapeDtypeStruct(h1.shape, F32),
        input_output_aliases=aliases,
        compiler_params=pltpu.CompilerParams(dimension_semantics=("arbitrary",),
                                             vmem_limit_bytes=VMEM_LIMIT),
        name="combine",
    )(*args)


def _pack_in_proj(w_in, b_in, gate_w_f, gate_w_b, gate_b_f, gate_b_b):
    o_pool, o_q, o_k, o_v, o_g = 0, 512, 1024, 1536, 2560
    o_lr, o_gp, o_gg = 3584, 3616, 4640
    order = [(o_pool, POOL_WIDTH), (o_q, GLA_DK), (o_k, GLA_DK), (o_v, GLA_DV), (o_g, GLA_DV),
             (o_gp, D_MODEL), (o_gg, D_MODEL)] + [(o_lr, 2 * GATE_RANK)] * 3
    pad = LANE - 6 * GATE_RANK
    w = jnp.concatenate([w_in[:, o:o + n] for o, n in order] + [jnp.zeros((D_MODEL, pad), F32)], axis=1)
    b = jnp.concatenate([b_in[o:o + n] for o, n in order] + [jnp.zeros((pad,), F32)])
    w_gate = jnp.zeros((2 * GATE_RANK, 2 * GLA_DK), F32)
    w_gate = w_gate.at[:GATE_RANK, :GLA_DK].set(gate_w_f)
    w_gate = w_gate.at[GATE_RANK:, GLA_DK:].set(gate_w_b)
    w_hi = w_gate.astype(BF16)
    w_lo = (w_gate - w_hi.astype(F32)).astype(BF16)
    w_gate3 = jnp.concatenate([w_hi, w_hi, w_lo, jnp.zeros((pad, 2 * GLA_DK), BF16)], axis=0)
    b_gate = jnp.concatenate([gate_b_f, gate_b_b])[None, :]
    return w.astype(BF16), b[None, :], w_gate3, b_gate


def kernel(x, meta, ln0_g, ln0_b, w_in, b_in, pool_w, pool_scale, gate_w_f, gate_b_f, gate_w_b, gate_b_b,
           gla_norm_g, w_branch_pool, w_branch_gla, w_out, ln1_g, ln1_b, w_router, router_bias,
           w_exp_gate, w_exp_up, w_exp_down, w_sh_gate, w_sh_up, w_sh_down, ln2_g, ln2_b):
    batch, seq, d = x.shape
    assert d == D_MODEL and seq % ROWS == 0 and w_in.shape[0] == 1
    assert (batch * seq) % ROUTE_ROWS == 0
    t = batch * seq
    row = lambda a: a.reshape(1, -1).astype(F32)

    lead = jnp.concatenate([jnp.zeros((FRONT - N_META, d), x.dtype), meta.astype(x.dtype)], axis=0)
    tiles_per_batch = (seq + FRONT) // ROWS
    x2 = x.reshape(t, d)

    w_packed, b_packed, w_gate, b_gate = _pack_in_proj(
        w_in[0], b_in[0], gate_w_f[0], gate_w_b[0], gate_b_f[0], gate_b_b[0])
    u, q, k, v, sg, gp, gg, bf, cb = _in_proj(
        lead, x2, row(ln0_g), row(ln0_b), w_packed, b_packed, w_gate, b_gate, tiles_per_batch)

    o_f, o_b = _gla(q, k, v, bf, cb, batch)

    h1, h1p = _mix(
        x.reshape(t, d), u, gp, gg, sg, o_f, o_b, row(ln0_g), row(ln0_b),
        pool_w[0].astype(BF16), row(pool_scale[0]), row(jnp.tile(gla_norm_g[0], GLA_HEADS)),
        w_branch_pool[0].astype(BF16), w_branch_gla[0].astype(BF16), w_out[0].astype(BF16),
        row(ln1_g[0]), row(ln1_b[0]), seq)

    wr = w_router[0].T.astype(F32)
    wr_hi = wr.astype(BF16)
    wr_mid = (wr - wr_hi.astype(F32)).astype(BF16)
    wr_terms = jnp.concatenate([wr_hi, wr_hi, wr_mid], axis=1)
    eidx, ew, before, counts = _route(h1, wr_terms, router_bias[0].reshape(N_EXPERTS, 1))

    counts = counts[:, 0]
    padded = ((counts + MOE_BLOCK - 1) // MOE_BLOCK) * MOE_BLOCK
    pend = jnp.cumsum(padded)
    pstart = pend - padded
    dest = _dest(eidx, before, pstart.astype(F32).reshape(N_EXPERTS, 1))
    n_blocks = -(-(t * TOP_K) // MOE_BLOCK) + N_EXPERTS
    block_start = (jnp.concatenate([pstart, pend[-1:]]) // MOE_BLOCK).astype(I32)

    xs = _dispatch(dest, h1p, n_blocks * MOE_BLOCK)
    ys = _experts(block_start, xs, w_exp_gate[0], w_exp_up[0], w_exp_down[0])
    shared_w = (w_sh_gate[0].astype(BF16), w_sh_up[0].astype(BF16), w_sh_down[0].astype(BF16))
    ew_rows = ew.T
    windows = t // SC_CHUNK // COMBINE_CHUNKS
    out = None
    for c in range(COMBINE_CHUNKS):
        yg = _gather(dest[c * windows:(c + 1) * windows], ys)
        out = _combine(h1, ew_rows, yg, c * windows * SC_CHUNK // ROWS, out, *shared_w,
                       row(ln2_g[0]), row(ln2_b[0]))
    return out.reshape(batch, seq, d)
```

```python
import functools

import jax
import jax.numpy as jnp
from jax import lax
from jax.experimental import pallas as pl
from jax.experimental.pallas import tpu as pltpu
from jax.experimental.pallas import tpu_sc as plsc

F32 = jnp.float32
BF16 = jnp.bfloat16
I32 = jnp.int32
U32 = jnp.uint32

D_MODEL = 1024
N_META = 16
POOL_WINDOWS = (2, 4, 8, 16)
POOL_GROUP = 128
POOL_WIDTH = POOL_GROUP * len(POOL_WINDOWS)
GLA_HEADS = 4
GLA_DK = 512
GLA_DV = 1024
HEAD_K = GLA_DK // GLA_HEADS
HEAD_V = GLA_DV // GLA_HEADS
GATE_RANK = 16
GATE_NORMALIZER = 16.0
RMS_EPS = 1e-5
N_EXPERTS = 256
TOP_K = 8
N_GROUPS = 8
GROUP_SIZE = N_EXPERTS // N_GROUPS
TOPK_GROUPS = 4
D_EXPERT = 256
ROUTED_SCALE = 2.5
DN_ALPHA = 2.0 ** 0.25
LN_EPS = 1e-5

LANE = 128
ROWS = 256
FRONT = ROWS
GLA_CHUNK = 256
HALO = 16
ROUTE_ROWS = 512
MOE_BLOCK = 256
LOOKAHEAD = 6
GROUP = 4
IN_SLOTS = LOOKAHEAD + GROUP
OUT_SLOTS = 2 * GROUP
SC_CORES = 2
SC_SUBCORES = 16
SC_CHUNK = 128
COMBINE_CHUNKS = 8
VMEM_LIMIT = 56 * 1024 * 1024

C_POOL, C_Q, C_K, C_V, C_G, C_GP, C_GG, C_LR = 0, 512, 1024, 1536, 2560, 3584, 4608, 5632
W_PACKED = C_LR + LANE


def _layer_norm(x, g, b):
    mu = jnp.mean(x, axis=-1, keepdims=True)
    xc = x - mu
    var = jnp.mean(xc * xc, axis=-1, keepdims=True)
    return xc * lax.rsqrt(var + LN_EPS) * g + b


def _dot(a, b):
    return jnp.dot(a, b, preferred_element_type=F32)


def _dot_nt(a, b):
    return lax.dot_general(a, b, (((1,), (1,)), ((), ())), preferred_element_type=F32)


def _split_dot(m2, g):
    g_hi = g.astype(BF16)
    g_lo = (g - g_hi.astype(F32)).astype(BF16)
    return _dot(m2, jnp.concatenate([g_hi, g_lo], axis=0))


def _pack_halves(a, b):
    pa = lax.bitcast_convert_type(a.astype(BF16).astype(F32), U32)
    pb = lax.bitcast_convert_type(b.astype(BF16).astype(F32), U32)
    return pa | (pb >> 16)


def _unpack_halves(p):
    a = lax.bitcast_convert_type(p & jnp.uint32(0xFFFF0000), F32)
    b = lax.bitcast_convert_type(p << 16, F32)
    return a, b


def _const_spec(shape):
    return pl.BlockSpec(shape, lambda *_: (0,) * len(shape))


def _inproj_body(lead_ref, x_ref, g0_ref, b0_ref, w_ref, bias_ref, wgate_ref, bgate_ref,
                 u_ref, q_ref, k_ref, v_ref, sg_ref, gp_ref, gg_ref, bf_ref, cb_ref,
                 *, tiles_per_batch):
    i = pl.program_id(0)
    x_in = jnp.where(i % tiles_per_batch == 0, lead_ref[...], x_ref[...])
    h = _layer_norm(x_in, g0_ref[...], b0_ref[...])
    hb = h.astype(BF16)

    def proj(c0, n):
        return _dot(hb, w_ref[:, c0:c0 + n]) + bias_ref[:, c0:c0 + n]

    row = lax.broadcasted_iota(I32, (ROWS, 1), 0)
    valid = row >= jnp.where(i % tiles_per_batch == 0, FRONT - N_META, 0)

    lr3 = proj(C_LR, LANE)
    g = proj(C_G, GLA_DV)
    sg_ref[...] = (g * jax.nn.sigmoid(g)).astype(BF16)
    lr_hi = lr3.astype(BF16)
    lr_lo = (lr3 - lr_hi.astype(F32)).astype(BF16)
    lane = lax.broadcasted_iota(I32, (1, LANE), 1)
    second = jnp.logical_and(lane >= 2 * GATE_RANK, lane < 4 * GATE_RANK)
    xg = _dot(jnp.where(second, lr_lo, lr_hi), wgate_ref[...]) + bgate_ref[...]
    gp_ref[...] = jax.nn.sigmoid(proj(C_GP, D_MODEL)).astype(BF16)
    gk = (jnp.minimum(xg, 0.0) - jnp.log(1.0 + jnp.exp(-jnp.abs(xg)))) * (1.0 / GATE_NORMALIZER)
    gg_ref[...] = jax.nn.sigmoid(proj(C_GG, D_MODEL)).astype(BF16)

    r = lax.broadcasted_iota(I32, (GLA_CHUNK, 2 * GLA_CHUNK), 0)
    c = lax.broadcasted_iota(I32, (GLA_CHUNK, 2 * GLA_CHUNK), 1) % GLA_CHUNK
    tril = (r >= c).astype(BF16)
    triu = (r <= c).astype(BF16)
    for ci in range(ROWS // GLA_CHUNK):
        sl = slice(ci * GLA_CHUNK, (ci + 1) * GLA_CHUNK)
        bf_ref[sl, :] = _split_dot(tril, gk[sl, :GLA_DK])
        cb_ref[sl, :] = _split_dot(triu, gk[sl, GLA_DK:])

    v_ref[...] = jnp.where(valid, proj(C_V, GLA_DV), 0.0).astype(BF16)
    k_ref[...] = jnp.where(valid, proj(C_K, GLA_DK), 0.0).astype(BF16)
    q_ref[...] = (proj(C_Q, GLA_DK) * (HEAD_K ** -0.5)).astype(BF16)
    u_ref[...] = proj(C_POOL, POOL_WIDTH).astype(BF16)


def _in_proj(lead, x2, ln0_g, ln0_b, w_packed, b_packed, w_gate, b_gate, tiles_per_batch):
    x_tiles = tiles_per_batch - FRONT // ROWS
    n = x2.shape[0] // (x_tiles * ROWS) * tiles_per_batch * ROWS
    row_spec = lambda width: pl.BlockSpec((ROWS, width), lambda i: (i, 0))
    x_spec = pl.BlockSpec(
        (ROWS, D_MODEL),
        lambda i: ((i // tiles_per_batch) * x_tiles + jnp.maximum(i % tiles_per_batch - 1, 0), 0))
    out_widths = (POOL_WIDTH, GLA_DK, GLA_DK, GLA_DV, GLA_DV, D_MODEL, D_MODEL)
    out_shape = [jax.ShapeDtypeStruct((n, w), BF16) for w in out_widths]
    out_shape += [jax.ShapeDtypeStruct((n, GLA_DK), F32)] * 2
    out_specs = [row_spec(w) for w in out_widths] + [row_spec(GLA_DK)] * 2
    return pl.pallas_call(
        functools.partial(_inproj_body, tiles_per_batch=tiles_per_batch),
        grid=(n // ROWS,),
        in_specs=[_const_spec((FRONT, D_MODEL)), x_spec,
                  _const_spec((1, D_MODEL)), _const_spec((1, D_MODEL)),
                  _const_spec((D_MODEL, W_PACKED)), _const_spec((1, W_PACKED)),
                  _const_spec((LANE, 2 * GLA_DK)), _const_spec((1, 2 * GLA_DK))],
        out_specs=out_specs,
        out_shape=out_shape,
        compiler_params=pltpu.CompilerParams(dimension_semantics=("arbitrary",),
                                             vmem_limit_bytes=VMEM_LIMIT),
        name="in_proj",
    )(lead, x2, ln0_g, ln0_b, w_packed, b_packed, w_gate, b_gate)


def _gla_direction(q_ref, k_ref, v_ref, b_ref, o_ref, s_ref, reverse):
    c = GLA_CHUNK
    r = lax.broadcasted_iota(I32, (c, c), 0)
    s = lax.broadcasted_iota(I32, (c, c), 1)
    keep = (r <= s) if reverse else (r >= s)
    eye = lax.broadcasted_iota(I32, (HEAD_K, HEAD_K), 0) == lax.broadcasted_iota(I32, (HEAD_K, HEAD_K), 1)
    edge = 0 if reverse else c - 1
    for h in range(GLA_HEADS):
        ks = slice(h * HEAD_K, (h + 1) * HEAD_K)
        vs = slice(h * HEAD_V, (h + 1) * HEAD_V)
        b = b_ref[:, ks]
        qh = q_ref[:, ks].astype(F32)
        kh = k_ref[:, ks].astype(F32)
        vh = v_ref[:, vs]
        b_tot = b[edge:edge + 1, :]
        b_mid = b[c // 2:c // 2 + 1, :]
        q_state = (qh * jnp.exp(b)).astype(BF16)
        q_in = (qh * jnp.exp(b - b_mid)).astype(BF16)
        k_in = (kh * jnp.exp(b_mid - b)).astype(BF16)
        k_state = (kh * jnp.exp(b_tot - b)).T.astype(BF16)
        att = jnp.where(keep, _dot_nt(q_in, k_in), 0.0).astype(BF16)
        state = s_ref[h]
        o = _dot(q_state, state.astype(BF16)) + _dot(att, vh)
        o_ref[:, vs] = o.astype(o_ref.dtype)
        decay_col = jnp.sum(jnp.where(eye, jnp.exp(b_tot), 0.0), axis=1, keepdims=True)
        s_ref[h] = decay_col * state + _dot(k_state, vh)


def _gla_body(qf_ref, kf_ref, vf_ref, bf_ref, qb_ref, kb_ref, vb_ref, cb_ref,
              of_ref, ob_ref, sf_ref, sb_ref):
    @pl.when(pl.program_id(1) == 0)
    def _():
        sf_ref[...] = jnp.zeros_like(sf_ref)
        sb_ref[...] = jnp.zeros_like(sb_ref)

    _gla_direction(qf_ref, kf_ref, vf_ref, bf_ref, of_ref, sf_ref, reverse=False)
    _gla_direction(qb_ref, kb_ref, vb_ref, cb_ref, ob_ref, sb_ref, reverse=True)


def _gla(q, k, v, bf, cb, batch):
    n = q.shape[0]
    nch = n // batch // GLA_CHUNK
    fwd = lambda width: pl.BlockSpec((GLA_CHUNK, width), lambda b, i: (b * nch + i, 0))
    bwd = lambda width: pl.BlockSpec((GLA_CHUNK, width), lambda b, i: (b * nch + nch - 1 - i, 0))
    state = pltpu.VMEM((GLA_HEADS, HEAD_K, HEAD_V), F32)
    return pl.pallas_call(
        _gla_body,
        grid=(batch, nch),
        in_specs=[fwd(GLA_DK), fwd(GLA_DK), fwd(GLA_DV), fwd(GLA_DK),
                  bwd(GLA_DK), bwd(GLA_DK), bwd(GLA_DV), bwd(GLA_DK)],
        out_specs=[fwd(GLA_DV), bwd(GLA_DV)],
        out_shape=[jax.ShapeDtypeStruct((n, GLA_DV), BF16)] * 2,
        scratch_shapes=[state, state],
        compiler_params=pltpu.CompilerParams(dimension_semantics=("arbitrary", "arbitrary"),
                                             vmem_limit_bytes=VMEM_LIMIT),
        name="gla",
    )(q, k, v, bf, q, k, v, cb)


def _mix_body(x_ref, up_ref, u_ref, un_ref, gp_ref, gg_ref, sg_ref, of_ref, ob_ref,
              g0_ref, b0_ref, pw_ref, ps_ref, gn_ref, wbp_ref, wbg_ref, wo_ref, g1_ref, b1_ref,
              h1_ref, h1p_ref, *, tiles_per_batch, seq):
    j = pl.program_id(0) % tiles_per_batch
    u_main = u_ref[...]
    u_ext = jnp.concatenate([up_ref[...], u_main, un_ref[...]], axis=0)
    r = lax.broadcasted_iota(I32, (ROWS, ROWS + 2 * HALO), 0)
    e = lax.broadcasted_iota(I32, (ROWS, ROWS + 2 * HALO), 1) - HALO
    in_seq = (j * ROWS + e) < seq
    pos = j * ROWS + lax.broadcasted_iota(I32, (ROWS, 1), 0)
    y_pool = []
    for gi, w in enumerate(POOL_WINDOWS):
        cs = slice(gi * POOL_GROUP, (gi + 1) * POOL_GROUP)
        band = jnp.logical_and(jnp.logical_and(e >= r - w // 2, e < r + w - w // 2), in_seq)
        count = w - jnp.maximum(pos + (w - w // 2) - seq, 0)
        mean = _dot(band.astype(BF16), u_ext[:, cs]) / count.astype(F32)
        d = mean - u_main[:, cs].astype(F32)
        y_pool.append(_dot(d.astype(BF16), pw_ref[gi]))
    y_pool = (jnp.concatenate(y_pool, axis=1) * ps_ref[...]).astype(BF16)

    o = of_ref[...].astype(F32) + ob_ref[...].astype(F32)
    y_gla = []
    for hd in range(GLA_HEADS):
        oh = o[:, hd * HEAD_V:(hd + 1) * HEAD_V]
        y_gla.append(oh * lax.rsqrt(jnp.mean(oh * oh, axis=-1, keepdims=True) + RMS_EPS))
    y_gla = (jnp.concatenate(y_gla, axis=1) * gn_ref[...] * sg_ref[...].astype(F32)).astype(BF16)

    merged = (gp_ref[...].astype(F32) * _dot(y_pool, wbp_ref[...])
              + gg_ref[...].astype(F32) * _dot(y_gla, wbg_ref[...]))
    y = _dot(merged.astype(BF16), wo_ref[...])
    h = _layer_norm(x_ref[...], g0_ref[...], b0_ref[...])
    h1 = _layer_norm(DN_ALPHA * h + y, g1_ref[...], b1_ref[...])
    h1_ref[...] = h1
    h1p_ref[...] = _pack_halves(h1[:, :D_MODEL // 2], h1[:, D_MODEL // 2:])


def _mix(x2, u, gp, gg, sg, o_f, o_b, ln0_g, ln0_b, pool_w, pool_scale, norm_g, wbp, wbg, wo,
         ln1_g, ln1_b, seq):
    t = x2.shape[0]
    tiles_per_batch = seq // ROWS
    padded_tiles = tiles_per_batch + FRONT // ROWS
    halo_per_tile = ROWS // HALO
    last_halo = u.shape[0] // HALO - 1

    def pidx(i):
        return (i // tiles_per_batch) * padded_tiles + FRONT // ROWS + i % tiles_per_batch

    padded = lambda width: pl.BlockSpec((ROWS, width), lambda i: (pidx(i), 0))
    plain = lambda width: pl.BlockSpec((ROWS, width), lambda i: (i, 0))
    prev_halo = pl.BlockSpec((HALO, POOL_WIDTH), lambda i: (pidx(i) * halo_per_tile - 1, 0))
    next_halo = pl.BlockSpec(
        (HALO, POOL_WIDTH), lambda i: (jnp.minimum((pidx(i) + 1) * halo_per_tile, last_halo), 0))
    return pl.pallas_call(
        functools.partial(_mix_body, tiles_per_batch=tiles_per_batch, seq=seq),
        grid=(t // ROWS,),
        in_specs=[plain(D_MODEL), prev_halo, padded(POOL_WIDTH), next_halo,
                  padded(D_MODEL), padded(D_MODEL), padded(GLA_DV), padded(GLA_DV), padded(GLA_DV),
                  _const_spec((1, D_MODEL)), _const_spec((1, D_MODEL)),
                  _const_spec((len(POOL_WINDOWS), POOL_GROUP, POOL_GROUP)), _const_spec((1, POOL_WIDTH)),
                  _const_spec((1, GLA_DV)), _const_spec((POOL_WIDTH, D_MODEL)),
                  _const_spec((GLA_DV, D_MODEL)), _const_spec((D_MODEL, D_MODEL)),
                  _const_spec((1, D_MODEL)), _const_spec((1, D_MODEL))],
        out_specs=[plain(D_MODEL), plain(D_MODEL // 2)],
        out_shape=[jax.ShapeDtypeStruct((t, D_MODEL), F32), jax.ShapeDtypeStruct((t, D_MODEL // 2), U32)],
        compiler_params=pltpu.CompilerParams(dimension_semantics=("arbitrary",),
                                             vmem_limit_bytes=VMEM_LIMIT),
        name="mix",
    )(x2, u, u, u, gp, gg, sg, o_f, o_b, ln0_g, ln0_b, pool_w, pool_scale, norm_g, wbp, wbg, wo,
      ln1_g, ln1_b)


def _first_index_of_max(val, idx, size):
    m = jnp.max(val, axis=0, keepdims=True)
    first = jnp.min(jnp.where(val == m, idx, size), axis=0, keepdims=True)
    return m, first


def _route_body(h_ref, wr_ref, bias_ref, eidx_ref, w_ref, before_ref, cnt_ref, base_ref):
    n = ROUTE_ROWS

    @pl.when(pl.program_id(0) == 0)
    def _():
        base_ref[...] = jnp.zeros_like(base_ref)

    h = h_ref[...]
    h_hi = h.astype(BF16)
    h_mid = (h - h_hi.astype(F32)).astype(BF16)
    logits = _dot_nt(wr_ref[...], jnp.concatenate([h_hi, h_mid, h_hi], axis=1))
    scores = jax.nn.sigmoid(logits)
    sel = scores + bias_ref[...]

    gidx = lax.broadcasted_iota(I32, (GROUP_SIZE, n), 0)
    groups = [sel[g * GROUP_SIZE:(g + 1) * GROUP_SIZE, :] for g in range(N_GROUPS)]
    gscore = []
    for grp in groups:
        m1, first = _first_index_of_max(grp, gidx, GROUP_SIZE)
        m2 = jnp.max(jnp.where(gidx == first, -jnp.inf, grp), axis=0, keepdims=True)
        gscore.append(m1 + m2)

    masked = []
    for gi in range(N_GROUPS):
        beaten = jnp.zeros((1, n), F32)
        for gj in range(N_GROUPS):
            if gj != gi:
                wins = (gscore[gj] >= gscore[gi]) if gj < gi else (gscore[gj] > gscore[gi])
                beaten = beaten + wins.astype(F32)
        masked.append(jnp.where(beaten < TOPK_GROUPS, groups[gi], -jnp.inf))
    val = jnp.concatenate(masked, axis=0)

    eidx = lax.broadcasted_iota(I32, (N_EXPERTS, n), 0)
    candidate = val != -jnp.inf
    picks, weights = [], []
    for _ in range(TOP_K):
        _, first = _first_index_of_max(val, eidx, N_EXPERTS)
        hit = eidx == first
        picks.append(first)
        weights.append(jnp.sum(jnp.where(hit, scores, 0.0), axis=0, keepdims=True))
        val = jnp.where(hit, -jnp.inf, val)
    chosen_f = jnp.where(jnp.logical_and(candidate, val == -jnp.inf), 1.0, 0.0)
    wsum = weights[0]
    for wk in weights[1:]:
        wsum = wsum + wk
    scale = ROUTED_SCALE / wsum

    r = lax.broadcasted_iota(I32, (n, n), 0)
    c = lax.broadcasted_iota(I32, (n, n), 1)
    earlier = (r < c).astype(BF16)
    before_ref[...] = _dot(chosen_f.astype(BF16), earlier) + base_ref[...]
    base_ref[...] = base_ref[...] + jnp.sum(chosen_f, axis=1, keepdims=True)
    cnt_ref[...] = jnp.broadcast_to(base_ref[...], cnt_ref.shape).astype(I32)

    for kk in range(TOP_K):
        eidx_ref[kk:kk + 1, :] = picks[kk]
        w_ref[kk:kk + 1, :] = weights[kk] * scale


def _route(h1, wr_t, bias_col):
    t = h1.shape[0]
    col = lambda dtype: jax.ShapeDtypeStruct((TOP_K, t), dtype)
    kspec = pl.BlockSpec((TOP_K, ROUTE_ROWS), lambda i: (0, i))
    return pl.pallas_call(
        _route_body,
        grid=(t // ROUTE_ROWS,),
        in_specs=[pl.BlockSpec((ROUTE_ROWS, D_MODEL), lambda i: (i, 0)),
                  _const_spec((N_EXPERTS, 3 * D_MODEL)), _const_spec((N_EXPERTS, 1))],
        out_specs=[kspec, kspec, pl.BlockSpec((N_EXPERTS, ROUTE_ROWS), lambda i: (0, i)),
                   _const_spec((N_EXPERTS, LANE))],
        out_shape=[col(I32), col(F32), jax.ShapeDtypeStruct((N_EXPERTS, t), F32),
                   jax.ShapeDtypeStruct((N_EXPERTS, LANE), I32)],
        scratch_shapes=[pltpu.VMEM((N_EXPERTS, 1), F32)],
        compiler_params=pltpu.CompilerParams(dimension_semantics=("arbitrary",),
                                             vmem_limit_bytes=VMEM_LIMIT),
        name="route",
    )(h1, wr_t, bias_col)


def _dest_body(eidx_ref, before_ref, pstart_ref, dest_ref):
    n = eidx_ref.shape[1]
    eiota = lax.broadcasted_iota(I32, (N_EXPERTS, n), 0)
    row_of = pstart_ref[...] + before_ref[...]
    for kk in range(TOP_K):
        dest = jnp.sum(jnp.where(eiota == eidx_ref[kk:kk + 1, :], row_of, 0.0), axis=0, keepdims=True)
        dest = dest.astype(I32)
        for c in range(n // SC_CHUNK):
            dest_ref[c, kk:kk + 1, :] = dest[:, c * SC_CHUNK:(c + 1) * SC_CHUNK]


def _dest(eidx, before, pstart_col):
    t = eidx.shape[1]
    kspec = pl.BlockSpec((TOP_K, ROUTE_ROWS), lambda i: (0, i))
    chunks = ROUTE_ROWS // SC_CHUNK
    return pl.pallas_call(
        _dest_body,
        grid=(t // ROUTE_ROWS,),
        in_specs=[kspec, pl.BlockSpec((N_EXPERTS, ROUTE_ROWS), lambda i: (0, i)),
                  _const_spec((N_EXPERTS, 1))],
        out_specs=pl.BlockSpec((chunks, TOP_K, SC_CHUNK), lambda i: (i, 0, 0)),
        out_shape=jax.ShapeDtypeStruct((t // SC_CHUNK, TOP_K, SC_CHUNK), I32),
        compiler_params=pltpu.CompilerParams(dimension_semantics=("arbitrary",)),
        name="dest",
    )(eidx, before, pstart_col)


def _sc_mesh():
    return plsc.VectorSubcoreMesh(core_axis_name="c", subcore_axis_name="s",
                                  num_cores=SC_CORES, num_subcores=SC_SUBCORES)


def _sc_worker_chunks(t):
    assert t % (SC_CORES * SC_SUBCORES * SC_CHUNK) == 0
    per_worker = t // (SC_CORES * SC_SUBCORES * SC_CHUNK)
    worker = lax.axis_index("s") * SC_CORES + lax.axis_index("c")
    return worker * per_worker, per_worker


def _dispatch(dest, h1p, n_rows):
    t, width = h1p.shape

    @functools.partial(
        pl.kernel, mesh=_sc_mesh(), out_type=jax.ShapeDtypeStruct((n_rows, width), U32),
        scratch_types=[pltpu.VMEM((SC_CHUNK, width), U32), pltpu.VMEM((TOP_K, SC_CHUNK), I32),
                       pltpu.SemaphoreType.DMA],
        name="dispatch")
    def body(h_hbm, dest_hbm, xs_hbm, rows_v, idx_v, sem):
        first, count = _sc_worker_chunks(t)

        @pl.loop(0, count)
        def _(i):
            chunk = first + i
            pltpu.sync_copy(h_hbm.at[pl.ds(pl.multiple_of(chunk * SC_CHUNK, SC_CHUNK), SC_CHUNK)], rows_v)
            pltpu.sync_copy(dest_hbm.at[chunk], idx_v)
            copies = [pltpu.async_copy(rows_v, xs_hbm.at[idx_v.at[kk]], sem) for kk in range(TOP_K)]
            for c in copies:
                c.wait()

    return body(h1p, dest)


def _gather(dest, ys):
    t = dest.shape[0] * SC_CHUNK
    width = ys.shape[1]

    @functools.partial(
        pl.kernel, mesh=_sc_mesh(), out_type=jax.ShapeDtypeStruct((TOP_K, t, width), U32),
        scratch_types=[pltpu.VMEM((SC_CHUNK, width), U32), pltpu.VMEM((TOP_K, SC_CHUNK), I32),
                       pltpu.SemaphoreType.DMA],
        name="gather")
    def body(ys_hbm, dest_hbm, yg_hbm, rows_v, idx_v, sem):
        first, count = _sc_worker_chunks(t)

        @pl.loop(0, count)
        def _(i):
            chunk = first + i
            pltpu.sync_copy(dest_hbm.at[chunk], idx_v)
            for kk in range(TOP_K):
                pltpu.async_copy(ys_hbm.at[idx_v.at[kk]], rows_v, sem).wait()
                pltpu.sync_copy(
                    rows_v, yg_hbm.at[kk, pl.ds(pl.multiple_of(chunk * SC_CHUNK, SC_CHUNK), SC_CHUNK)])

    return body(ys, dest)


def _experts_body(bs_ref, xs_in_ref, wg_ref, wu_ref, wd_ref, xs_ref,
                  xbuf, ybuf, wgb, wub, wdb, sem_in, sem_out):
    del xs_in_ref
    e = pl.program_id(0)
    b0 = bs_ref[e]
    b1 = bs_ref[e + 1]
    n_total = bs_ref[N_EXPERTS]
    half = D_MODEL // 2

    def fetch(b):
        slot = b % IN_SLOTS
        return pltpu.make_async_copy(xs_ref.at[pl.ds(b * MOE_BLOCK, MOE_BLOCK)], xbuf.at[slot],
                                     sem_in.at[slot])

    def store(b):
        slot = b % OUT_SLOTS
        return pltpu.make_async_copy(ybuf.at[slot], xs_ref.at[pl.ds(b * MOE_BLOCK, MOE_BLOCK)],
                                     sem_out.at[slot])

    @pl.when(e == 0)
    def _():
        for b in range(LOOKAHEAD):
            @pl.when(b < n_total)
            def _():
                fetch(b).start(priority=1)

    @pl.when(b1 > b0)
    def _():
        wgb[...] = wg_ref[0].astype(BF16)
        wub[...] = wu_ref[0].astype(BF16)
        wdb[...] = wd_ref[0].astype(BF16)

    def run(blocks):
        for b in blocks:
            @pl.when(b + LOOKAHEAD < n_total)
            def _():
                fetch(b + LOOKAHEAD).start(priority=1)
        for b in blocks:
            fetch(b).wait()
        for b in blocks:
            @pl.when(b >= OUT_SLOTS)
            def _():
                store(b - OUT_SLOTS).wait()

        xa, xb = _unpack_halves(jnp.concatenate([xbuf[b % IN_SLOTS] for b in blocks], axis=0))
        xa = xa.astype(BF16)
        xb = xb.astype(BF16)
        gate = _dot(xa, wgb[:half, :]) + _dot(xb, wgb[half:, :])
        up = _dot(xa, wub[:half, :]) + _dot(xb, wub[half:, :])
        hidden = (gate * jax.nn.sigmoid(gate) * up).astype(BF16)
        y = _dot(hidden, wdb[...])
        packed = _pack_halves(y[:, :half], y[:, half:])
        for i, b in enumerate(blocks):
            ybuf[b % OUT_SLOTS] = packed[i * MOE_BLOCK:(i + 1) * MOE_BLOCK]
            store(b).start(priority=1)

    n = b1 - b0

    def group(i, carry):
        run([b0 + GROUP * i + j for j in range(GROUP)])
        return carry

    lax.fori_loop(0, n // GROUP, group, 0)
    rest = b0 + (n // GROUP) * GROUP

    @pl.when(n % GROUP >= 2)
    def _():
        run([rest, rest + 1])

    @pl.when(n % 2 == 1)
    def _():
        run([b1 - 1])

    @pl.when(e == N_EXPERTS - 1)
    def _():
        for back in range(OUT_SLOTS, 0, -1):
            @pl.when(n_total >= back)
            def _():
                store(n_total - back).wait()


def _experts(block_start, xs, w_gate, w_up, w_down):
    n_rows, width = xs.shape
    wspec = lambda shape: pl.BlockSpec((1,) + shape, lambda e, bs: (e, 0, 0))
    grid_spec = pltpu.PrefetchScalarGridSpec(
        num_scalar_prefetch=1,
        grid=(N_EXPERTS,),
        in_specs=[pl.BlockSpec(memory_space=pl.ANY),
                  wspec((D_MODEL, D_EXPERT)), wspec((D_MODEL, D_EXPERT)), wspec((D_EXPERT, D_MODEL))],
        out_specs=pl.BlockSpec(memory_space=pl.ANY),
        scratch_shapes=[pltpu.VMEM((IN_SLOTS, MOE_BLOCK, width), U32),
                        pltpu.VMEM((OUT_SLOTS, MOE_BLOCK, width), U32),
                        pltpu.VMEM((D_MODEL, D_EXPERT), BF16), pltpu.VMEM((D_MODEL, D_EXPERT), BF16),
                        pltpu.VMEM((D_EXPERT, D_MODEL), BF16),
                        pltpu.SemaphoreType.DMA((IN_SLOTS,)), pltpu.SemaphoreType.DMA((OUT_SLOTS,))],
    )
    return pl.pallas_call(
        _experts_body,
        grid_spec=grid_spec,
        out_shape=jax.ShapeDtypeStruct((n_rows, width), U32),
        input_output_aliases={1: 0},
        compiler_params=pltpu.CompilerParams(dimension_semantics=("arbitrary",),
                                             vmem_limit_bytes=VMEM_LIMIT),
        name="experts",
    )(block_start, xs, w_gate, w_up, w_down)


def _combine_body(h1_ref, w_ref, yg_ref, wsg_ref, wsu_ref, wsd_ref, g2_ref, b2_ref, *refs):
    out_ref = refs[-1]
    h1 = h1_ref[...]
    hb = h1.astype(BF16)
    gate = _dot(hb, wsg_ref[...])
    up = _dot(hb, wsu_ref[...])
    shared = _dot((gate * jax.nn.sigmoid(gate) * up).astype(BF16), wsd_ref[...])

    half = D_MODEL // 2
    lo = jnp.zeros((ROWS, half), F32)
    hi = jnp.zeros((ROWS, half), F32)
    for kk in range(TOP_K):
        a, b = _unpack_halves(yg_ref[kk])
        wk = w_ref[:, kk:kk + 1]
        lo = lo + wk * a
        hi = hi + wk * b
    z = DN_ALPHA * h1 + shared + jnp.concatenate([lo, hi], axis=1)
    out_ref[...] = _layer_norm(z, g2_ref[...], b2_ref[...])


def _combine(h1, w_rows, yg, first_tile, out_prev, wsg, wsu, wsd, ln2_g, ln2_b):
    width = yg.shape[2]
    d_shared = wsg.shape[1]
    rows = lambda cols: pl.BlockSpec((ROWS, cols), lambda i: (first_tile + i, 0))
    args = [h1, w_rows, yg, wsg, wsu, wsd, ln2_g, ln2_b]
    in_specs = [rows(D_MODEL), rows(TOP_K),
                pl.BlockSpec((TOP_K, ROWS, width), lambda i: (0, i, 0)),
                _const_spec((D_MODEL, d_shared)), _const_spec((D_MODEL, d_shared)),
                _const_spec((d_shared, D_MODEL)),
                _const_spec((1, D_MODEL)), _const_spec((1, D_MODEL))]
    aliases = {}
    if out_prev is not None:
        aliases = {len(args): 0}
        args.append(out_prev)
        in_specs.append(pl.BlockSpec(memory_space=pl.ANY))
    return pl.pallas_call(
        _combine_body,
        grid=(yg.shape[1] // ROWS,),
        in_specs=in_specs,
        out_specs=rows(D_MODEL),
        out_shape=jax.ShapeDtypeStruct(h1.shape, F32),
        input_output_aliases=aliases,
        compiler_params=pltpu.CompilerParams(dimension_semantics=("arbitrary",),
                                             vmem_limit_bytes=VMEM_LIMIT),
        name="combine",
    )(*args)


def _pack_in_proj(w_in, b_in, gate_w_f, gate_w_b, gate_b_f, gate_b_b):
    o_pool, o_q, o_k, o_v, o_g = 0, 512, 1024, 1536, 2560
    o_lr, o_gp, o_gg = 3584, 3616, 4640
    order = [(o_pool, POOL_WIDTH), (o_q, GLA_DK), (o_k, GLA_DK), (o_v, GLA_DV), (o_g, GLA_DV),
             (o_gp, D_MODEL), (o_gg, D_MODEL)] + [(o_lr, 2 * GATE_RANK)] * 3
    pad = LANE - 6 * GATE_RANK
    w = jnp.concatenate([w_in[:, o:o + n] for o, n in order] + [jnp.zeros((D_MODEL, pad), F32)], axis=1)
    b = jnp.concatenate([b_in[o:o + n] for o, n in order] + [jnp.zeros((pad,), F32)])
    w_gate = jnp.zeros((2 * GATE_RANK, 2 * GLA_DK), F32)
    w_gate = w_gate.at[:GATE_RANK, :GLA_DK].set(gate_w_f)
    w_gate = w_gate.at[GATE_RANK:, GLA_DK:].set(gate_w_b)
    w_hi = w_gate.astype(BF16)
    w_lo = (w_gate - w_hi.astype(F32)).astype(BF16)
    w_gate3 = jnp.concatenate([w_hi, w_hi, w_lo, jnp.zeros((pad, 2 * GLA_DK), BF16)], axis=0)
    b_gate = jnp.concatenate([gate_b_f, gate_b_b])[None, :]
    return w.astype(BF16), b[None, :], w_gate3, b_gate


def kernel(x, meta, ln0_g, ln0_b, w_in, b_in, pool_w, pool_scale, gate_w_f, gate_b_f, gate_w_b, gate_b_b,
           gla_norm_g, w_branch_pool, w_branch_gla, w_out, ln1_g, ln1_b, w_router, router_bias,
           w_exp_gate, w_exp_up, w_exp_down, w_sh_gate, w_sh_up, w_sh_down, ln2_g, ln2_b):
    batch, seq, d = x.shape
    assert d == D_MODEL and seq % ROWS == 0 and w_in.shape[0] == 1
    assert (batch * seq) % ROUTE_ROWS == 0
    t = batch * seq
    row = lambda a: a.reshape(1, -1).astype(F32)

    lead = jnp.concatenate([jnp.zeros((FRONT - N_META, d), x.dtype), meta.astype(x.dtype)], axis=0)
    tiles_per_batch = (seq + FRONT) // ROWS
    x2 = x.reshape(t, d)

    w_packed, b_packed, w_gate, b_gate = _pack_in_proj(
        w_in[0], b_in[0], gate_w_f[0], gate_w_b[0], gate_b_f[0], gate_b_b[0])
    u, q, k, v, sg, gp, gg, bf, cb = _in_proj(
        lead, x2, row(ln0_g), row(ln0_b), w_packed, b_packed, w_gate, b_gate, tiles_per_batch)

    o_f, o_b = _gla(q, k, v, bf, cb, batch)

    h1, h1p = _mix(
        x.reshape(t, d), u, gp, gg, sg, o_f, o_b, row(ln0_g), row(ln0_b),
        pool_w[0].astype(BF16), row(pool_scale[0]), row(jnp.tile(gla_norm_g[0], GLA_HEADS)),
        w_branch_pool[0].astype(BF16), w_branch_gla[0].astype(BF16), w_out[0].astype(BF16),
        row(ln1_g[0]), row(ln1_b[0]), seq)

    wr = w_router[0].T.astype(F32)
    wr_hi = wr.astype(BF16)
    wr_mid = (wr - wr_hi.astype(F32)).astype(BF16)
    wr_terms = jnp.concatenate([wr_hi, wr_hi, wr_mid], axis=1)
    eidx, ew, before, counts = _route(h1, wr_terms, router_bias[0].reshape(N_EXPERTS, 1))

    counts = counts[:, 0]
    padded = ((counts + MOE_BLOCK - 1) // MOE_BLOCK) * MOE_BLOCK
    pend = jnp.cumsum(padded)
    pstart = pend - padded
    dest = _dest(eidx, before, pstart.astype(F32).reshape(N_EXPERTS, 1))
    n_blocks = -(-(t * TOP_K) // MOE_BLOCK) + N_EXPERTS
    block_start = (jnp.concatenate([pstart, pend[-1:]]) // MOE_BLOCK).astype(I32)

    xs = _dispatch(dest, h1p, n_blocks * MOE_BLOCK)
    ys = _experts(block_start, xs, w_exp_gate[0], w_exp_up[0], w_exp_down[0])
    shared_w = (w_sh_gate[0].astype(BF16), w_sh_up[0].astype(BF16), w_sh_down[0].astype(BF16))
    ew_rows = ew.T
    windows = t // SC_CHUNK // COMBINE_CHUNKS
    out = None
    for c in range(COMBINE_CHUNKS):
        yg = _gather(dest[c * windows:(c + 1) * windows], ys)
        out = _combine(h1, ew_rows, yg, c * windows * SC_CHUNK // ROWS, out, *shared_w,
                       row(ln2_g[0]), row(ln2_b[0]))
    return out.reshape(batch, seq, d)
```

```python
import functools

import jax
import jax.numpy as jnp
from jax import lax
from jax.experimental import pallas as pl
from jax.experimental.pallas import tpu as pltpu
from jax.experimental.pallas import tpu_sc as plsc

F32 = jnp.float32
BF16 = jnp.bfloat16
I32 = jnp.int32
U32 = jnp.uint32

D_MODEL = 1024
N_META = 16
POOL_WINDOWS = (2, 4, 8, 16)
POOL_GROUP = 128
POOL_WIDTH = POOL_GROUP * len(POOL_WINDOWS)
GLA_HEADS = 4
GLA_DK = 512
GLA_DV = 1024
HEAD_K = GLA_DK // GLA_HEADS
HEAD_V = GLA_DV // GLA_HEADS
GATE_RANK = 16
GATE_NORMALIZER = 16.0
RMS_EPS = 1e-5
N_EXPERTS = 256
TOP_K = 8
N_GROUPS = 8
GROUP_SIZE = N_EXPERTS // N_GROUPS
TOPK_GROUPS = 4
D_EXPERT = 256
ROUTED_SCALE = 2.5
DN_ALPHA = 2.0 ** 0.25
LN_EPS = 1e-5

LANE = 128
ROWS = 256
FRONT = ROWS
GLA_CHUNK = 256
HALO = 16
ROUTE_ROWS = 512
MOE_BLOCK = 256
LOOKAHEAD = 6
GROUP = 4
assert GROUP & (GROUP - 1) == 0
IN_SLOTS = LOOKAHEAD + GROUP
OUT_SLOTS = 2 * GROUP
SC_CORES = 2
SC_SUBCORES = 16
SC_CHUNK = 128
COMBINE_CHUNKS = 8
VMEM_LIMIT = 56 * 1024 * 1024

C_POOL, C_Q, C_K, C_V, C_G, C_GP, C_GG, C_LR = 0, 512, 1024, 1536, 2560, 3584, 4608, 5632
W_PACKED = C_LR + LANE


def _layer_norm(x, g, b):
    mu = jnp.mean(x, axis=-1, keepdims=True)
    xc = x - mu
    var = jnp.mean(xc * xc, axis=-1, keepdims=True)
    return xc * lax.rsqrt(var + LN_EPS) * g + b


def _dot(a, b):
    return jnp.dot(a, b, preferred_element_type=F32)


def _dot_nt(a, b):
    return lax.dot_general(a, b, (((1,), (1,)), ((), ())), preferred_element_type=F32)


def _split_dot(m2, g):
    g_hi = g.astype(BF16)
    g_lo = (g - g_hi.astype(F32)).astype(BF16)
    return _dot(m2, jnp.concatenate([g_hi, g_lo], axis=0))


def _pack_halves(a, b):
    pa = lax.bitcast_convert_type(a.astype(BF16).astype(F32), U32)
    pb = lax.bitcast_convert_type(b.astype(BF16).astype(F32), U32)
    return pa | (pb >> 16)


def _unpack_halves(p):
    a = lax.bitcast_convert_type(p & jnp.uint32(0xFFFF0000), F32)
    b = lax.bitcast_convert_type(p << 16, F32)
    return a, b


def _const_spec(shape):
    return pl.BlockSpec(shape, lambda *_: (0,) * len(shape))


def _inproj_body(lead_ref, x_ref, g0_ref, b0_ref, w_ref, bias_ref, wgate_ref, bgate_ref,
                 u_ref, q_ref, k_ref, v_ref, sg_ref, gp_ref, gg_ref, bf_ref, cb_ref, h_even, h_odd,
                 *, tiles_per_batch):
    i = pl.program_id(0)

    def normalised(is_lead):
        x_in = jnp.where(is_lead, lead_ref[...], x_ref[...])
        return _layer_norm(x_in, g0_ref[...], b0_ref[...]).astype(BF16)

    @pl.when(i == 0)
    def _():
        h_even[...] = normalised(True)

    for parity, (h_ref, h_next_ref) in enumerate(((h_even, h_odd), (h_odd, h_even))):
        @pl.when(i % 2 == parity)
        def _():
            h_next_ref[...] = normalised((i + 1) % tiles_per_batch == 0)
            _inproj_tile(h_ref[...], i % tiles_per_batch == 0, w_ref, bias_ref, wgate_ref, bgate_ref,
                         u_ref, q_ref, k_ref, v_ref, sg_ref, gp_ref, gg_ref, bf_ref, cb_ref)


def _inproj_tile(hb, is_lead, w_ref, bias_ref, wgate_ref, bgate_ref,
                 u_ref, q_ref, k_ref, v_ref, sg_ref, gp_ref, gg_ref, bf_ref, cb_ref):
    def proj(c0, n):
        return _dot(hb, w_ref[:, c0:c0 + n]) + bias_ref[:, c0:c0 + n]

    row = lax.broadcasted_iota(I32, (ROWS, 1), 0)
    valid = row >= jnp.where(is_lead, FRONT - N_META, 0)

    lr3 = proj(C_LR, LANE)
    g = proj(C_G, GLA_DV)
    sg_ref[...] = (g * jax.nn.sigmoid(g)).astype(BF16)
    lr_hi = lr3.astype(BF16)
    lr_lo = (lr3 - lr_hi.astype(F32)).astype(BF16)
    lane = lax.broadcasted_iota(I32, (1, LANE), 1)
    second = jnp.logical_and(lane >= 2 * GATE_RANK, lane < 4 * GATE_RANK)
    xg = _dot(jnp.where(second, lr_lo, lr_hi), wgate_ref[...]) + bgate_ref[...]
    gp_ref[...] = jax.nn.sigmoid(proj(C_GP, D_MODEL)).astype(BF16)
    gk = (jnp.minimum(xg, 0.0) - jnp.log(1.0 + jnp.exp(-jnp.abs(xg)))) * (1.0 / GATE_NORMALIZER)
    gg_ref[...] = jax.nn.sigmoid(proj(C_GG, D_MODEL)).astype(BF16)

    r = lax.broadcasted_iota(I32, (GLA_CHUNK, 2 * GLA_CHUNK), 0)
    c = lax.broadcasted_iota(I32, (GLA_CHUNK, 2 * GLA_CHUNK), 1) % GLA_CHUNK
    tril = (r >= c).astype(BF16)
    triu = (r <= c).astype(BF16)
    for ci in range(ROWS // GLA_CHUNK):
        sl = slice(ci * GLA_CHUNK, (ci + 1) * GLA_CHUNK)
        bf_ref[sl, :] = _split_dot(tril, gk[sl, :GLA_DK])
        cb_ref[sl, :] = _split_dot(triu, gk[sl, GLA_DK:])

    v_ref[...] = jnp.where(valid, proj(C_V, GLA_DV), 0.0).astype(BF16)
    k_ref[...] = jnp.where(valid, proj(C_K, GLA_DK), 0.0).astype(BF16)
    q_ref[...] = (proj(C_Q, GLA_DK) * (HEAD_K ** -0.5)).astype(BF16)
    u_ref[...] = proj(C_POOL, POOL_WIDTH).astype(BF16)


def _in_proj(lead, x2, ln0_g, ln0_b, w_packed, b_packed, w_gate, b_gate, tiles_per_batch):
    x_tiles = tiles_per_batch - FRONT // ROWS
    n = x2.shape[0] // (x_tiles * ROWS) * tiles_per_batch * ROWS
    row_spec = lambda width: pl.BlockSpec((ROWS, width), lambda i: (i, 0))
    n_steps = n // ROWS

    def next_x_tile(i):
        j = jnp.minimum(i + 1, n_steps - 1)
        return (j // tiles_per_batch) * x_tiles + jnp.maximum(j % tiles_per_batch - 1, 0), 0

    x_spec = pl.BlockSpec((ROWS, D_MODEL), next_x_tile)
    out_widths = (POOL_WIDTH, GLA_DK, GLA_DK, GLA_DV, GLA_DV, D_MODEL, D_MODEL)
    out_shape = [jax.ShapeDtypeStruct((n, w), BF16) for w in out_widths]
    out_shape += [jax.ShapeDtypeStruct((n, GLA_DK), F32)] * 2
    out_specs = [row_spec(w) for w in out_widths] + [row_spec(GLA_DK)] * 2
    return pl.pallas_call(
        functools.partial(_inproj_body, tiles_per_batch=tiles_per_batch),
        grid=(n // ROWS,),
        in_specs=[_const_spec((FRONT, D_MODEL)), x_spec,
                  _const_spec((1, D_MODEL)), _const_spec((1, D_MODEL)),
                  _const_spec((D_MODEL, W_PACKED)), _const_spec((1, W_PACKED)),
                  _const_spec((LANE, 2 * GLA_DK)), _const_spec((1, 2 * GLA_DK))],
        out_specs=out_specs,
        out_shape=out_shape,
        scratch_shapes=[pltpu.VMEM((ROWS, D_MODEL), BF16)] * 2,
        compiler_params=pltpu.CompilerParams(dimension_semantics=("arbitrary",),
                                             vmem_limit_bytes=VMEM_LIMIT),
        name="in_proj",
    )(lead, x2, ln0_g, ln0_b, w_packed, b_packed, w_gate, b_gate)


def _gla_direction(q_ref, k_ref, v_ref, b_ref, o_ref, s_ref, reverse):
    c = GLA_CHUNK
    r = lax.broadcasted_iota(I32, (c, c), 0)
    s = lax.broadcasted_iota(I32, (c, c), 1)
    keep = (r <= s) if reverse else (r >= s)
    eye = lax.broadcasted_iota(I32, (HEAD_K, HEAD_K), 0) == lax.broadcasted_iota(I32, (HEAD_K, HEAD_K), 1)
    edge = 0 if reverse else c - 1
    for h in range(GLA_HEADS):
        ks = slice(h * HEAD_K, (h + 1) * HEAD_K)
        vs = slice(h * HEAD_V, (h + 1) * HEAD_V)
        b = b_ref[:, ks]
        qh = q_ref[:, ks].astype(F32)
        kh = k_ref[:, ks].astype(F32)
        vh = v_ref[:, vs]
        b_tot = b[edge:edge + 1, :]
        b_mid = b[c // 2:c // 2 + 1, :]
        q_state = (qh * jnp.exp(b)).astype(BF16)
        q_in = (qh * jnp.exp(b - b_mid)).astype(BF16)
        k_in = (kh * jnp.exp(b_mid - b)).astype(BF16)
        k_state = (kh * jnp.exp(b_tot - b)).T.astype(BF16)
        att = jnp.where(keep, _dot_nt(q_in, k_in), 0.0).astype(BF16)
        state = s_ref[h]
        o = _dot(q_state, state.astype(BF16)) + _dot(att, vh)
        o_ref[:, vs] = o.astype(o_ref.dtype)
        decay_col = jnp.sum(jnp.where(eye, jnp.exp(b_tot), 0.0), axis=1, keepdims=True)
        s_ref[h] = decay_col * state + _dot(k_state, vh)


def _gla_body(qf_ref, kf_ref, vf_ref, bf_ref, qb_ref, kb_ref, vb_ref, cb_ref,
              of_ref, ob_ref, sf_ref, sb_ref):
    @pl.when(pl.program_id(1) == 0)
    def _():
        sf_ref[...] = jnp.zeros_like(sf_ref)
        sb_ref[...] = jnp.zeros_like(sb_ref)

    _gla_direction(qf_ref, kf_ref, vf_ref, bf_ref, of_ref, sf_ref, reverse=False)
    _gla_direction(qb_ref, kb_ref, vb_ref, cb_ref, ob_ref, sb_ref, reverse=True)


def _gla(q, k, v, bf, cb, batch):
    n = q.shape[0]
    nch = n // batch // GLA_CHUNK
    fwd = lambda width: pl.BlockSpec((GLA_CHUNK, width), lambda b, i: (b * nch + i, 0))
    bwd = lambda width: pl.BlockSpec((GLA_CHUNK, width), lambda b, i: (b * nch + nch - 1 - i, 0))
    state = pltpu.VMEM((GLA_HEADS, HEAD_K, HEAD_V), F32)
    return pl.pallas_call(
        _gla_body,
        grid=(batch, nch),
        in_specs=[fwd(GLA_DK), fwd(GLA_DK), fwd(GLA_DV), fwd(GLA_DK),
                  bwd(GLA_DK), bwd(GLA_DK), bwd(GLA_DV), bwd(GLA_DK)],
        out_specs=[fwd(GLA_DV), bwd(GLA_DV)],
        out_shape=[jax.ShapeDtypeStruct((n, GLA_DV), BF16)] * 2,
        scratch_shapes=[state, state],
        compiler_params=pltpu.CompilerParams(dimension_semantics=("arbitrary", "arbitrary"),
                                             vmem_limit_bytes=VMEM_LIMIT),
        name="gla",
    )(q, k, v, bf, q, k, v, cb)


def _mix_body(x_ref, up_ref, u_ref, un_ref, gp_ref, gg_ref, sg_ref, of_ref, ob_ref,
              g0_ref, b0_ref, pw_ref, ps_ref, gn_ref, wbp_ref, wbg_ref, wo_ref, g1_ref, b1_ref,
              h1_ref, h1p_ref, *, tiles_per_batch, seq):
    j = pl.program_id(0) % tiles_per_batch
    u_main = u_ref[...]
    u_ext = jnp.concatenate([up_ref[...], u_main, un_ref[...]], axis=0)
    r = lax.broadcasted_iota(I32, (ROWS, ROWS + 2 * HALO), 0)
    e = lax.broadcasted_iota(I32, (ROWS, ROWS + 2 * HALO), 1) - HALO
    in_seq = (j * ROWS + e) < seq
    pos = j * ROWS + lax.broadcasted_iota(I32, (ROWS, 1), 0)
    y_pool = []
    for gi, w in enumerate(POOL_WINDOWS):
        cs = slice(gi * POOL_GROUP, (gi + 1) * POOL_GROUP)
        band = jnp.logical_and(jnp.logical_and(e >= r - w // 2, e < r + w - w // 2), in_seq)
        count = w - jnp.maximum(pos + (w - w // 2) - seq, 0)
        mean = _dot(band.astype(BF16), u_ext[:, cs]) / count.astype(F32)
        d = mean - u_main[:, cs].astype(F32)
        y_pool.append(_dot(d.astype(BF16), pw_ref[gi]))
    y_pool = (jnp.concatenate(y_pool, axis=1) * ps_ref[...]).astype(BF16)

    o = of_ref[...].astype(F32) + ob_ref[...].astype(F32)
    y_gla = []
    for hd in range(GLA_HEADS):
        oh = o[:, hd * HEAD_V:(hd + 1) * HEAD_V]
        y_gla.append(oh * lax.rsqrt(jnp.mean(oh * oh, axis=-1, keepdims=True) + RMS_EPS))
    y_gla = (jnp.concatenate(y_gla, axis=1) * gn_ref[...] * sg_ref[...].astype(F32)).astype(BF16)

    merged = (gp_ref[...].astype(F32) * _dot(y_pool, wbp_ref[...])
              + gg_ref[...].astype(F32) * _dot(y_gla, wbg_ref[...]))
    y = _dot(merged.astype(BF16), wo_ref[...])
    h = _layer_norm(x_ref[...], g0_ref[...], b0_ref[...])
    h1 = _layer_norm(DN_ALPHA * h + y, g1_ref[...], b1_ref[...])
    h1_ref[...] = h1
    h1p_ref[...] = _pack_halves(h1[:, :D_MODEL // 2], h1[:, D_MODEL // 2:])


def _mix(x2, u, gp, gg, sg, o_f, o_b, ln0_g, ln0_b, pool_w, pool_scale, norm_g, wbp, wbg, wo,
         ln1_g, ln1_b, seq):
    t = x2.shape[0]
    tiles_per_batch = seq // ROWS
    padded_tiles = tiles_per_batch + FRONT // ROWS
    halo_per_tile = ROWS // HALO
    last_halo = u.shape[0] // HALO - 1

    def pidx(i):
        return (i // tiles_per_batch) * padded_tiles + FRONT // ROWS + i % tiles_per_batch

    padded = lambda width: pl.BlockSpec((ROWS, width), lambda i: (pidx(i), 0))
    plain = lambda width: pl.BlockSpec((ROWS, width), lambda i: (i, 0))
    prev_halo = pl.BlockSpec((HALO, POOL_WIDTH), lambda i: (pidx(i) * halo_per_tile - 1, 0))
    next_halo = pl.BlockSpec(
        (HALO, POOL_WIDTH), lambda i: (jnp.minimum((pidx(i) + 1) * halo_per_tile, last_halo), 0))
    return pl.pallas_call(
        functools.partial(_mix_body, tiles_per_batch=tiles_per_batch, seq=seq),
        grid=(t // ROWS,),
        in_specs=[plain(D_MODEL), prev_halo, padded(POOL_WIDTH), next_halo,
                  padded(D_MODEL), padded(D_MODEL), padded(GLA_DV), padded(GLA_DV), padded(GLA_DV),
                  _const_spec((1, D_MODEL)), _const_spec((1, D_MODEL)),
                  _const_spec((len(POOL_WINDOWS), POOL_GROUP, POOL_GROUP)), _const_spec((1, POOL_WIDTH)),
                  _const_spec((1, GLA_DV)), _const_spec((POOL_WIDTH, D_MODEL)),
                  _const_spec((GLA_DV, D_MODEL)), _const_spec((D_MODEL, D_MODEL)),
                  _const_spec((1, D_MODEL)), _const_spec((1, D_MODEL))],
        out_specs=[plain(D_MODEL), plain(D_MODEL // 2)],
        out_shape=[jax.ShapeDtypeStruct((t, D_MODEL), F32), jax.ShapeDtypeStruct((t, D_MODEL // 2), U32)],
        compiler_params=pltpu.CompilerParams(dimension_semantics=("arbitrary",),
                                             vmem_limit_bytes=VMEM_LIMIT),
        name="mix",
    )(x2, u, u, u, gp, gg, sg, o_f, o_b, ln0_g, ln0_b, pool_w, pool_scale, norm_g, wbp, wbg, wo,
      ln1_g, ln1_b)


def _first_index_of_max(val, idx, size):
    m = jnp.max(val, axis=0, keepdims=True)
    first = jnp.min(jnp.where(val == m, idx, size), axis=0, keepdims=True)
    return m, first


def _route_body(h_ref, wr_ref, bias_ref, eidx_ref, w_ref, before_ref, cnt_ref, base_ref):
    n = ROUTE_ROWS

    @pl.when(pl.program_id(0) == 0)
    def _():
        base_ref[...] = jnp.zeros_like(base_ref)

    h = h_ref[...]
    h_hi = h.astype(BF16)
    h_mid = (h - h_hi.astype(F32)).astype(BF16)
    logits = _dot_nt(wr_ref[...], jnp.concatenate([h_hi, h_mid, h_hi], axis=1))
    scores = jax.nn.sigmoid(logits)
    sel = scores + bias_ref[...]

    gidx = lax.broadcasted_iota(I32, (GROUP_SIZE, n), 0)
    groups = [sel[g * GROUP_SIZE:(g + 1) * GROUP_SIZE, :] for g in range(N_GROUPS)]
    gscore = []
    for grp in groups:
        m1, first = _first_index_of_max(grp, gidx, GROUP_SIZE)
        m2 = jnp.max(jnp.where(gidx == first, -jnp.inf, grp), axis=0, keepdims=True)
        gscore.append(m1 + m2)

    masked = []
    for gi in range(N_GROUPS):
        beaten = jnp.zeros((1, n), F32)
        for gj in range(N_GROUPS):
            if gj != gi:
                wins = (gscore[gj] >= gscore[gi]) if gj < gi else (gscore[gj] > gscore[gi])
                beaten = beaten + wins.astype(F32)
        masked.append(jnp.where(beaten < TOPK_GROUPS, groups[gi], -jnp.inf))
    val = jnp.concatenate(masked, axis=0)

    eidx = lax.broadcasted_iota(I32, (N_EXPERTS, n), 0)
    candidate = val != -jnp.inf
    picks, weights = [], []
    for _ in range(TOP_K):
        _, first = _first_index_of_max(val, eidx, N_EXPERTS)
        hit = eidx == first
        picks.append(first)
        weights.append(jnp.sum(jnp.where(hit, scores, 0.0), axis=0, keepdims=True))
        val = jnp.where(hit, -jnp.inf, val)
    chosen_f = jnp.where(jnp.logical_and(candidate, val == -jnp.inf), 1.0, 0.0)
    wsum = weights[0]
    for wk in weights[1:]:
        wsum = wsum + wk
    scale = ROUTED_SCALE / wsum

    r = lax.broadcasted_iota(I32, (n, n), 0)
    c = lax.broadcasted_iota(I32, (n, n), 1)
    earlier = (r < c).astype(BF16)
    before_ref[...] = _dot(chosen_f.astype(BF16), earlier) + base_ref[...]
    base_ref[...] = base_ref[...] + jnp.sum(chosen_f, axis=1, keepdims=True)
    cnt_ref[...] = jnp.broadcast_to(base_ref[...], cnt_ref.shape).astype(I32)

    for kk in range(TOP_K):
        eidx_ref[kk:kk + 1, :] = picks[kk]
        w_ref[kk:kk + 1, :] = weights[kk] * scale


def _route(h1, wr_t, bias_col):
    t = h1.shape[0]
    col = lambda dtype: jax.ShapeDtypeStruct((TOP_K, t), dtype)
    kspec = pl.BlockSpec((TOP_K, ROUTE_ROWS), lambda i: (0, i))
    return pl.pallas_call(
        _route_body,
        grid=(t // ROUTE_ROWS,),
        in_specs=[pl.BlockSpec((ROUTE_ROWS, D_MODEL), lambda i: (i, 0)),
                  _const_spec((N_EXPERTS, 3 * D_MODEL)), _const_spec((N_EXPERTS, 1))],
        out_specs=[kspec, kspec, pl.BlockSpec((N_EXPERTS, ROUTE_ROWS), lambda i: (0, i)),
                   _const_spec((N_EXPERTS, LANE))],
        out_shape=[col(I32), col(F32), jax.ShapeDtypeStruct((N_EXPERTS, t), F32),
                   jax.ShapeDtypeStruct((N_EXPERTS, LANE), I32)],
        scratch_shapes=[pltpu.VMEM((N_EXPERTS, 1), F32)],
        compiler_params=pltpu.CompilerParams(dimension_semantics=("arbitrary",),
                                             vmem_limit_bytes=VMEM_LIMIT),
        name="route",
    )(h1, wr_t, bias_col)


def _dest_body(eidx_ref, before_ref, pstart_ref, dest_ref):
    n = eidx_ref.shape[1]
    eiota = lax.broadcasted_iota(I32, (N_EXPERTS, n), 0)
    row_of = pstart_ref[...] + before_ref[...]
    for kk in range(TOP_K):
        dest = jnp.sum(jnp.where(eiota == eidx_ref[kk:kk + 1, :], row_of, 0.0), axis=0, keepdims=True)
        dest = dest.astype(I32)
        for c in range(n // SC_CHUNK):
            dest_ref[c, kk:kk + 1, :] = dest[:, c * SC_CHUNK:(c + 1) * SC_CHUNK]


def _dest(eidx, before, pstart_col):
    t = eidx.shape[1]
    kspec = pl.BlockSpec((TOP_K, ROUTE_ROWS), lambda i: (0, i))
    chunks = ROUTE_ROWS // SC_CHUNK
    return pl.pallas_call(
        _dest_body,
        grid=(t // ROUTE_ROWS,),
        in_specs=[kspec, pl.BlockSpec((N_EXPERTS, ROUTE_ROWS), lambda i: (0, i)),
                  _const_spec((N_EXPERTS, 1))],
        out_specs=pl.BlockSpec((chunks, TOP_K, SC_CHUNK), lambda i: (i, 0, 0)),
        out_shape=jax.ShapeDtypeStruct((t // SC_CHUNK, TOP_K, SC_CHUNK), I32),
        compiler_params=pltpu.CompilerParams(dimension_semantics=("arbitrary",)),
        name="dest",
    )(eidx, before, pstart_col)


def _sc_mesh():
    return plsc.VectorSubcoreMesh(core_axis_name="c", subcore_axis_name="s",
                                  num_cores=SC_CORES, num_subcores=SC_SUBCORES)


def _sc_worker_chunks(t):
    assert t % (SC_CORES * SC_SUBCORES * SC_CHUNK) == 0
    per_worker = t // (SC_CORES * SC_SUBCORES * SC_CHUNK)
    worker = lax.axis_index("s") * SC_CORES + lax.axis_index("c")
    return worker * per_worker, per_worker


def _dispatch(dest, h1p, n_rows):
    t, width = h1p.shape

    @functools.partial(
        pl.kernel, mesh=_sc_mesh(), out_type=jax.ShapeDtypeStruct((n_rows, width), U32),
        scratch_types=[pltpu.VMEM((SC_CHUNK, width), U32), pltpu.VMEM((TOP_K, SC_CHUNK), I32),
                       pltpu.SemaphoreType.DMA],
        name="dispatch")
    def body(h_hbm, dest_hbm, xs_hbm, rows_v, idx_v, sem):
        first, count = _sc_worker_chunks(t)

        @pl.loop(0, count)
        def _(i):
            chunk = first + i
            pltpu.sync_copy(h_hbm.at[pl.ds(pl.multiple_of(chunk * SC_CHUNK, SC_CHUNK), SC_CHUNK)], rows_v)
            pltpu.sync_copy(dest_hbm.at[chunk], idx_v)
            copies = [pltpu.async_copy(rows_v, xs_hbm.at[idx_v.at[kk]], sem) for kk in range(TOP_K)]
            for c in copies:
                c.wait()

    return body(h1p, dest)


def _gather(dest, ys):
    t = dest.shape[0] * SC_CHUNK
    width = ys.shape[1]

    @functools.partial(
        pl.kernel, mesh=_sc_mesh(), out_type=jax.ShapeDtypeStruct((TOP_K, t, width), U32),
        scratch_types=[pltpu.VMEM((SC_CHUNK, width), U32), pltpu.VMEM((TOP_K, SC_CHUNK), I32),
                       pltpu.SemaphoreType.DMA],
        name="gather")
    def body(ys_hbm, dest_hbm, yg_hbm, rows_v, idx_v, sem):
        first, count = _sc_worker_chunks(t)

        @pl.loop(0, count)
        def _(i):
            chunk = first + i
            pltpu.sync_copy(dest_hbm.at[chunk], idx_v)
            for kk in range(TOP_K):
                pltpu.async_copy(ys_hbm.at[idx_v.at[kk]], rows_v, sem).wait()
                pltpu.sync_copy(
                    rows_v, yg_hbm.at[kk, pl.ds(pl.multiple_of(chunk * SC_CHUNK, SC_CHUNK), SC_CHUNK)])

    return body(ys, dest)


def _experts_body(bs_ref, xs_in_ref, wg_ref, wu_ref, wd_ref, xs_ref,
                  xbuf, ybuf, wgb, wub, wdb, sem_in, sem_out):
    del xs_in_ref
    e = pl.program_id(0)
    b0 = bs_ref[e]
    b1 = bs_ref[e + 1]
    n_total = bs_ref[N_EXPERTS]
    half = D_MODEL // 2

    def fetch(b):
        slot = b % IN_SLOTS
        return pltpu.make_async_copy(xs_ref.at[pl.ds(b * MOE_BLOCK, MOE_BLOCK)], xbuf.at[slot],
                                     sem_in.at[slot])

    def store(b):
        slot = b % OUT_SLOTS
        return pltpu.make_async_copy(ybuf.at[slot], xs_ref.at[pl.ds(b * MOE_BLOCK, MOE_BLOCK)],
                                     sem_out.at[slot])

    @pl.when(e == 0)
    def _():
        for b in range(LOOKAHEAD):
            @pl.when(b < n_total)
            def _():
                fetch(b).start(priority=1)

    @pl.when(b1 > b0)
    def _():
        wgb[...] = wg_ref[0].astype(BF16)
        wub[...] = wu_ref[0].astype(BF16)
        wdb[...] = wd_ref[0].astype(BF16)

    def run(blocks):
        for b in blocks:
            @pl.when(b + LOOKAHEAD < n_total)
            def _():
                fetch(b + LOOKAHEAD).start(priority=1)
        for b in blocks:
            fetch(b).wait()
        for b in blocks:
            @pl.when(b >= OUT_SLOTS)
            def _():
                store(b - OUT_SLOTS).wait()

        xa, xb = _unpack_halves(jnp.concatenate([xbuf[b % IN_SLOTS] for b in blocks], axis=0))
        xa = xa.astype(BF16)
        xb = xb.astype(BF16)
        gate = _dot(xa, wgb[:half, :]) + _dot(xb, wgb[half:, :])
        up = _dot(xa, wub[:half, :]) + _dot(xb, wub[half:, :])
        hidden = (gate * jax.nn.sigmoid(gate) * up).astype(BF16)
        y = _dot(hidden, wdb[...])
        packed = _pack_halves(y[:, :half], y[:, half:])
        for i, b in enumerate(blocks):
            ybuf[b % OUT_SLOTS] = packed[i * MOE_BLOCK:(i + 1) * MOE_BLOCK]
            store(b).start(priority=1)

    n = b1 - b0

    def group(i, carry):
        run([b0 + GROUP * i + j for j in range(GROUP)])
        return carry

    lax.fori_loop(0, n // GROUP, group, 0)
    rest = b0 + (n // GROUP) * GROUP
    left = n % GROUP
    size = GROUP // 2
    while size >= 1:
        @pl.when(left & size != 0)
        def _():
            first = rest + (left & ~(2 * size - 1))
            run([first + j for j in range(size)])
        size //= 2

    @pl.when(e == N_EXPERTS - 1)
    def _():
        for back in range(OUT_SLOTS, 0, -1):
            @pl.when(n_total >= back)
            def _():
                store(n_total - back).wait()


def _experts(block_start, xs, w_gate, w_up, w_down):
    n_rows, width = xs.shape
    wspec = lambda shape: pl.BlockSpec((1,) + shape, lambda e, bs: (e, 0, 0))
    grid_spec = pltpu.PrefetchScalarGridSpec(
        num_scalar_prefetch=1,
        grid=(N_EXPERTS,),
        in_specs=[pl.BlockSpec(memory_space=pl.ANY),
                  wspec((D_MODEL, D_EXPERT)), wspec((D_MODEL, D_EXPERT)), wspec((D_EXPERT, D_MODEL))],
        out_specs=pl.BlockSpec(memory_space=pl.ANY),
        scratch_shapes=[pltpu.VMEM((IN_SLOTS, MOE_BLOCK, width), U32),
                        pltpu.VMEM((OUT_SLOTS, MOE_BLOCK, width), U32),
                        pltpu.VMEM((D_MODEL, D_EXPERT), BF16), pltpu.VMEM((D_MODEL, D_EXPERT), BF16),
                        pltpu.VMEM((D_EXPERT, D_MODEL), BF16),
                        pltpu.SemaphoreType.DMA((IN_SLOTS,)), pltpu.SemaphoreType.DMA((OUT_SLOTS,))],
    )
    return pl.pallas_call(
        _experts_body,
        grid_spec=grid_spec,
        out_shape=jax.ShapeDtypeStruct((n_rows, width), U32),
        input_output_aliases={1: 0},
        compiler_params=pltpu.CompilerParams(dimension_semantics=("arbitrary",),
                                             vmem_limit_bytes=VMEM_LIMIT),
        name="experts",
    )(block_start, xs, w_gate, w_up, w_down)


def _combine_body(h1_ref, w_ref, yg_ref, wsg_ref, wsu_ref, wsd_ref, g2_ref, b2_ref, *refs):
    out_ref = refs[-1]
    h1 = h1_ref[...]
    hb = h1.astype(BF16)
    gate = _dot(hb, wsg_ref[...])
    up = _dot(hb, wsu_ref[...])
    shared = _dot((gate * jax.nn.sigmoid(gate) * up).astype(BF16), wsd_ref[...])

    half = D_MODEL // 2
    lo = jnp.zeros((ROWS, half), F32)
    hi = jnp.zeros((ROWS, half), F32)
    for kk in range(TOP_K):
        a, b = _unpack_halves(yg_ref[kk])
        wk = w_ref[:, kk:kk + 1]
        lo = lo + wk * a
        hi = hi + wk * b
    z = DN_ALPHA * h1 + shared + jnp.concatenate([lo, hi], axis=1)
    out_ref[...] = _layer_norm(z, g2_ref[...], b2_ref[...])


def _combine(h1, w_rows, yg, first_tile, out_prev, wsg, wsu, wsd, ln2_g, ln2_b):
    width = yg.shape[2]
    d_shared = wsg.shape[1]
    rows = lambda cols: pl.BlockSpec((ROWS, cols), lambda i: (first_tile + i, 0))
    args = [h1, w_rows, yg, wsg, wsu, wsd, ln2_g, ln2_b]
    in_specs = [rows(D_MODEL), rows(TOP_K),
                pl.BlockSpec((TOP_K, ROWS, width), lambda i: (0, i, 0)),
                _const_spec((D_MODEL, d_shared)), _const_spec((D_MODEL, d_shared)),
                _const_spec((d_shared, D_MODEL)),
                _const_spec((1, D_MODEL)), _const_spec((1, D_MODEL))]
    aliases = {}
    if out_prev is not None:
        aliases = {len(args): 0}
        args.append(out_prev)
        in_specs.append(pl.BlockSpec(memory_space=pl.ANY))
    return pl.pallas_call(
        _combine_body,
        grid=(yg.shape[1] // ROWS,),
        in_specs=in_specs,
        out_specs=rows(D_MODEL),
        out_shape=jax.ShapeDtypeStruct(h1.shape, F32),
        input_output_aliases=aliases,
        compiler_params=pltpu.CompilerParams(dimension_semantics=("arbitrary",),
                                             vmem_limit_bytes=VMEM_LIMIT),
        name="combine",
    )(*args)


def _pack_in_proj(w_in, b_in, gate_w_f, gate_w_b, gate_b_f, gate_b_b):
    o_pool, o_q, o_k, o_v, o_g = 0, 512, 1024, 1536, 2560
    o_lr, o_gp, o_gg = 3584, 3616, 4640
    order = [(o_pool, POOL_WIDTH), (o_q, GLA_DK), (o_k, GLA_DK), (o_v, GLA_DV), (o_g, GLA_DV),
             (o_gp, D_MODEL), (o_gg, D_MODEL)] + [(o_lr, 2 * GATE_RANK)] * 3
    pad = LANE - 6 * GATE_RANK
    w = jnp.concatenate([w_in[:, o:o + n] for o, n in order] + [jnp.zeros((D_MODEL, pad), F32)], axis=1)
    b = jnp.concatenate([b_in[o:o + n] for o, n in order] + [jnp.zeros((pad,), F32)])
    w_gate = jnp.zeros((2 * GATE_RANK, 2 * GLA_DK), F32)
    w_gate = w_gate.at[:GATE_RANK, :GLA_DK].set(gate_w_f)
    w_gate = w_gate.at[GATE_RANK:, GLA_DK:].set(gate_w_b)
    w_hi = w_gate.astype(BF16)
    w_lo = (w_gate - w_hi.astype(F32)).astype(BF16)
    w_gate3 = jnp.concatenate([w_hi, w_hi, w_lo, jnp.zeros((pad, 2 * GLA_DK), BF16)], axis=0)
    b_gate = jnp.concatenate([gate_b_f, gate_b_b])[None, :]
    return w.astype(BF16), b[None, :], w_gate3, b_gate


def kernel(x, meta, ln0_g, ln0_b, w_in, b_in, pool_w, pool_scale, gate_w_f, gate_b_f, gate_w_b, gate_b_b,
           gla_norm_g, w_branch_pool, w_branch_gla, w_out, ln1_g, ln1_b, w_router, router_bias,
           w_exp_gate, w_exp_up, w_exp_down, w_sh_gate, w_sh_up, w_sh_down, ln2_g, ln2_b):
    batch, seq, d = x.shape
    assert d == D_MODEL and seq % ROWS == 0 and w_in.shape[0] == 1
    assert (batch * seq) % ROUTE_ROWS == 0
    t = batch * seq
    row = lambda a: a.reshape(1, -1).astype(F32)

    lead = jnp.concatenate([jnp.zeros((FRONT - N_META, d), x.dtype), meta.astype(x.dtype)], axis=0)
    tiles_per_batch = (seq + FRONT) // ROWS
    x2 = x.reshape(t, d)

    w_packed, b_packed, w_gate, b_gate = _pack_in_proj(
        w_in[0], b_in[0], gate_w_f[0], gate_w_b[0], gate_b_f[0], gate_b_b[0])
    u, q, k, v, sg, gp, gg, bf, cb = _in_proj(
        lead, x2, row(ln0_g), row(ln0_b), w_packed, b_packed, w_gate, b_gate, tiles_per_batch)

    o_f, o_b = _gla(q, k, v, bf, cb, batch)

    h1, h1p = _mix(
        x.reshape(t, d), u, gp, gg, sg, o_f, o_b, row(ln0_g), row(ln0_b),
        pool_w[0].astype(BF16), row(pool_scale[0]), row(jnp.tile(gla_norm_g[0], GLA_HEADS)),
        w_branch_pool[0].astype(BF16), w_branch_gla[0].astype(BF16), w_out[0].astype(BF16),
        row(ln1_g[0]), row(ln1_b[0]), seq)

    wr = w_router[0].T.astype(F32)
    wr_hi = wr.astype(BF16)
    wr_mid = (wr - wr_hi.astype(F32)).astype(BF16)
    wr_terms = jnp.concatenate([wr_hi, wr_hi, wr_mid], axis=1)
    eidx, ew, before, counts = _route(h1, wr_terms, router_bias[0].reshape(N_EXPERTS, 1))

    counts = counts[:, 0]
    padded = ((counts + MOE_BLOCK - 1) // MOE_BLOCK) * MOE_BLOCK
    pend = jnp.cumsum(padded)
    pstart = pend - padded
    dest = _dest(eidx, before, pstart.astype(F32).reshape(N_EXPERTS, 1))
    n_blocks = -(-(t * TOP_K) // MOE_BLOCK) + N_EXPERTS
    block_start = (jnp.concatenate([pstart, pend[-1:]]) // MOE_BLOCK).astype(I32)

    xs = _dispatch(dest, h1p, n_blocks * MOE_BLOCK)
    ys = _experts(block_start, xs, w_exp_gate[0], w_exp_up[0], w_exp_down[0])
    shared_w = (w_sh_gate[0].astype(BF16), w_sh_up[0].astype(BF16), w_sh_down[0].astype(BF16))
    ew_rows = ew.T
    windows = t // SC_CHUNK // COMBINE_CHUNKS
    out = None
    for c in range(COMBINE_CHUNKS):
        yg = _gather(dest[c * windows:(c + 1) * windows], ys)
        out = _combine(h1, ew_rows, yg, c * windows * SC_CHUNK // ROWS, out, *shared_w,
                       row(ln2_g[0]), row(ln2_b[0]))
    return out.reshape(batch, seq, d)
```

```python
import functools

import jax
import jax.numpy as jnp
from jax import lax
from jax.experimental import pallas as pl
from jax.experimental.pallas import tpu as pltpu
from jax.experimental.pallas import tpu_sc as plsc

F32 = jnp.float32
BF16 = jnp.bfloat16
I32 = jnp.int32
U32 = jnp.uint32

D_MODEL = 1024
N_META = 16
POOL_WINDOWS = (2, 4, 8, 16)
POOL_GROUP = 128
POOL_WIDTH = POOL_GROUP * len(POOL_WINDOWS)
GLA_HEADS = 4
GLA_DK = 512
GLA_DV = 1024
HEAD_K = GLA_DK // GLA_HEADS
HEAD_V = GLA_DV // GLA_HEADS
GATE_RANK = 16
GATE_NORMALIZER = 16.0
RMS_EPS = 1e-5
N_EXPERTS = 256
TOP_K = 8
N_GROUPS = 8
GROUP_SIZE = N_EXPERTS // N_GROUPS
TOPK_GROUPS = 4
D_EXPERT = 256
ROUTED_SCALE = 2.5
DN_ALPHA = 2.0 ** 0.25
LN_EPS = 1e-5

LANE = 128
ROWS = 256
FRONT = ROWS
GLA_CHUNK = 256
HALO = 16
ROUTE_ROWS = 1024
MOE_BLOCK = 256
LOOKAHEAD = 6
GROUP = 4
assert GROUP & (GROUP - 1) == 0
IN_SLOTS = LOOKAHEAD + GROUP
OUT_SLOTS = 2 * GROUP
SC_CORES = 2
SC_SUBCORES = 16
SC_CHUNK = 128
COMBINE_CHUNKS = 8
VMEM_LIMIT = 56 * 1024 * 1024

C_POOL, C_Q, C_K, C_V, C_G, C_GP, C_GG, C_LR = 0, 512, 1024, 1536, 2560, 3584, 4608, 5632
W_PACKED = C_LR + LANE


def _layer_norm(x, g, b):
    mu = jnp.mean(x, axis=-1, keepdims=True)
    xc = x - mu
    var = jnp.mean(xc * xc, axis=-1, keepdims=True)
    return xc * lax.rsqrt(var + LN_EPS) * g + b


def _dot(a, b):
    return jnp.dot(a, b, preferred_element_type=F32)


def _dot_nt(a, b):
    return lax.dot_general(a, b, (((1,), (1,)), ((), ())), preferred_element_type=F32)


def _split_dot(m2, g):
    g_hi = g.astype(BF16)
    g_lo = (g - g_hi.astype(F32)).astype(BF16)
    return _dot(m2, jnp.concatenate([g_hi, g_lo], axis=0))


def _pack_halves(a, b):
    pa = lax.bitcast_convert_type(a.astype(BF16).astype(F32), U32)
    pb = lax.bitcast_convert_type(b.astype(BF16).astype(F32), U32)
    return pa | (pb >> 16)


def _unpack_halves(p):
    a = lax.bitcast_convert_type(p & jnp.uint32(0xFFFF0000), F32)
    b = lax.bitcast_convert_type(p << 16, F32)
    return a, b


def _const_spec(shape):
    return pl.BlockSpec(shape, lambda *_: (0,) * len(shape))


def _inproj_body(lead_ref, x_ref, g0_ref, b0_ref, w_ref, bias_ref, wgate_ref, bgate_ref,
                 u_ref, q_ref, k_ref, v_ref, sg_ref, gp_ref, gg_ref, bf_ref, cb_ref, h_even, h_odd,
                 *, tiles_per_batch):
    i = pl.program_id(0)

    def normalised(is_lead):
        x_in = jnp.where(is_lead, lead_ref[...], x_ref[...])
        return _layer_norm(x_in, g0_ref[...], b0_ref[...]).astype(BF16)

    @pl.when(i == 0)
    def _():
        h_even[...] = normalised(True)

    for parity, (h_ref, h_next_ref) in enumerate(((h_even, h_odd), (h_odd, h_even))):
        @pl.when(i % 2 == parity)
        def _():
            h_next_ref[...] = normalised((i + 1) % tiles_per_batch == 0)
            _inproj_tile(h_ref[...], i % tiles_per_batch == 0, w_ref, bias_ref, wgate_ref, bgate_ref,
                         u_ref, q_ref, k_ref, v_ref, sg_ref, gp_ref, gg_ref, bf_ref, cb_ref)


def _inproj_tile(hb, is_lead, w_ref, bias_ref, wgate_ref, bgate_ref,
                 u_ref, q_ref, k_ref, v_ref, sg_ref, gp_ref, gg_ref, bf_ref, cb_ref):
    def proj(c0, n):
        return _dot(hb, w_ref[:, c0:c0 + n]) + bias_ref[:, c0:c0 + n]

    row = lax.broadcasted_iota(I32, (ROWS, 1), 0)
    valid = row >= jnp.where(is_lead, FRONT - N_META, 0)

    lr3 = proj(C_LR, LANE)
    g = proj(C_G, GLA_DV)
    sg_ref[...] = (g * jax.nn.sigmoid(g)).astype(BF16)
    lr_hi = lr3.astype(BF16)
    lr_lo = (lr3 - lr_hi.astype(F32)).astype(BF16)
    lane = lax.broadcasted_iota(I32, (1, LANE), 1)
    second = jnp.logical_and(lane >= 2 * GATE_RANK, lane < 4 * GATE_RANK)
    xg = _dot(jnp.where(second, lr_lo, lr_hi), wgate_ref[...]) + bgate_ref[...]
    gp_ref[...] = jax.nn.sigmoid(proj(C_GP, D_MODEL)).astype(BF16)
    gk = (jnp.minimum(xg, 0.0) - jnp.log(1.0 + jnp.exp(-jnp.abs(xg)))) * (1.0 / GATE_NORMALIZER)
    gg_ref[...] = jax.nn.sigmoid(proj(C_GG, D_MODEL)).astype(BF16)

    r = lax.broadcasted_iota(I32, (GLA_CHUNK, 2 * GLA_CHUNK), 0)
    c = lax.broadcasted_iota(I32, (GLA_CHUNK, 2 * GLA_CHUNK), 1) % GLA_CHUNK
    tril = (r >= c).astype(BF16)
    triu = (r <= c).astype(BF16)
    for ci in range(ROWS // GLA_CHUNK):
        sl = slice(ci * GLA_CHUNK, (ci + 1) * GLA_CHUNK)
        bf_ref[sl, :] = _split_dot(tril, gk[sl, :GLA_DK])
        cb_ref[sl, :] = _split_dot(triu, gk[sl, GLA_DK:])

    v_ref[...] = jnp.where(valid, proj(C_V, GLA_DV), 0.0).astype(BF16)
    k_ref[...] = jnp.where(valid, proj(C_K, GLA_DK), 0.0).astype(BF16)
    q_ref[...] = (proj(C_Q, GLA_DK) * (HEAD_K ** -0.5)).astype(BF16)
    u_ref[...] = proj(C_POOL, POOL_WIDTH).astype(BF16)


def _in_proj(lead, x2, ln0_g, ln0_b, w_packed, b_packed, w_gate, b_gate, tiles_per_batch):
    x_tiles = tiles_per_batch - FRONT // ROWS
    n = x2.shape[0] // (x_tiles * ROWS) * tiles_per_batch * ROWS
    row_spec = lambda width: pl.BlockSpec((ROWS, width), lambda i: (i, 0))
    n_steps = n // ROWS

    def next_x_tile(i):
        j = jnp.minimum(i + 1, n_steps - 1)
        return (j // tiles_per_batch) * x_tiles + jnp.maximum(j % tiles_per_batch - 1, 0), 0

    x_spec = pl.BlockSpec((ROWS, D_MODEL), next_x_tile)
    out_widths = (POOL_WIDTH, GLA_DK, GLA_DK, GLA_DV, GLA_DV, D_MODEL, D_MODEL)
    out_shape = [jax.ShapeDtypeStruct((n, w), BF16) for w in out_widths]
    out_shape += [jax.ShapeDtypeStruct((n, GLA_DK), F32)] * 2
    out_specs = [row_spec(w) for w in out_widths] + [row_spec(GLA_DK)] * 2
    return pl.pallas_call(
        functools.partial(_inproj_body, tiles_per_batch=tiles_per_batch),
        grid=(n // ROWS,),
        in_specs=[_const_spec((FRONT, D_MODEL)), x_spec,
                  _const_spec((1, D_MODEL)), _const_spec((1, D_MODEL)),
                  _const_spec((D_MODEL, W_PACKED)), _const_spec((1, W_PACKED)),
                  _const_spec((LANE, 2 * GLA_DK)), _const_spec((1, 2 * GLA_DK))],
        out_specs=out_specs,
        out_shape=out_shape,
        scratch_shapes=[pltpu.VMEM((ROWS, D_MODEL), BF16)] * 2,
        compiler_params=pltpu.CompilerParams(dimension_semantics=("arbitrary",),
                                             vmem_limit_bytes=VMEM_LIMIT),
        name="in_proj",
    )(lead, x2, ln0_g, ln0_b, w_packed, b_packed, w_gate, b_gate)


def _gla_direction(q_ref, k_ref, v_ref, b_ref, o_ref, s_ref, reverse):
    c = GLA_CHUNK
    r = lax.broadcasted_iota(I32, (c, c), 0)
    s = lax.broadcasted_iota(I32, (c, c), 1)
    keep = (r <= s) if reverse else (r >= s)
    eye = lax.broadcasted_iota(I32, (HEAD_K, HEAD_K), 0) == lax.broadcasted_iota(I32, (HEAD_K, HEAD_K), 1)
    edge = 0 if reverse else c - 1
    for h in range(GLA_HEADS):
        ks = slice(h * HEAD_K, (h + 1) * HEAD_K)
        vs = slice(h * HEAD_V, (h + 1) * HEAD_V)
        b = b_ref[:, ks]
        qh = q_ref[:, ks].astype(F32)
        kh = k_ref[:, ks].astype(F32)
        vh = v_ref[:, vs]
        b_tot = b[edge:edge + 1, :]
        b_mid = b[c // 2:c // 2 + 1, :]
        q_state = (qh * jnp.exp(b)).astype(BF16)
        q_in = (qh * jnp.exp(b - b_mid)).astype(BF16)
        k_in = (kh * jnp.exp(b_mid - b)).astype(BF16)
        k_state = (kh * jnp.exp(b_tot - b)).T.astype(BF16)
        att = jnp.where(keep, _dot_nt(q_in, k_in), 0.0).astype(BF16)
        state = s_ref[h]
        o = _dot(q_state, state.astype(BF16)) + _dot(att, vh)
        o_ref[:, vs] = o.astype(o_ref.dtype)
        decay_col = jnp.sum(jnp.where(eye, jnp.exp(b_tot), 0.0), axis=1, keepdims=True)
        s_ref[h] = decay_col * state + _dot(k_state, vh)


def _gla_body(qf_ref, kf_ref, vf_ref, bf_ref, qb_ref, kb_ref, vb_ref, cb_ref,
              of_ref, ob_ref, sf_ref, sb_ref):
    @pl.when(pl.program_id(1) == 0)
    def _():
        sf_ref[...] = jnp.zeros_like(sf_ref)
        sb_ref[...] = jnp.zeros_like(sb_ref)

    _gla_direction(qf_ref, kf_ref, vf_ref, bf_ref, of_ref, sf_ref, reverse=False)
    _gla_direction(qb_ref, kb_ref, vb_ref, cb_ref, ob_ref, sb_ref, reverse=True)


def _gla(q, k, v, bf, cb, batch):
    n = q.shape[0]
    nch = n // batch // GLA_CHUNK
    fwd = lambda width: pl.BlockSpec((GLA_CHUNK, width), lambda b, i: (b * nch + i, 0))
    bwd = lambda width: pl.BlockSpec((GLA_CHUNK, width), lambda b, i: (b * nch + nch - 1 - i, 0))
    state = pltpu.VMEM((GLA_HEADS, HEAD_K, HEAD_V), F32)
    return pl.pallas_call(
        _gla_body,
        grid=(batch, nch),
        in_specs=[fwd(GLA_DK), fwd(GLA_DK), fwd(GLA_DV), fwd(GLA_DK),
                  bwd(GLA_DK), bwd(GLA_DK), bwd(GLA_DV), bwd(GLA_DK)],
        out_specs=[fwd(GLA_DV), bwd(GLA_DV)],
        out_shape=[jax.ShapeDtypeStruct((n, GLA_DV), BF16)] * 2,
        scratch_shapes=[state, state],
        compiler_params=pltpu.CompilerParams(dimension_semantics=("arbitrary", "arbitrary"),
                                             vmem_limit_bytes=VMEM_LIMIT),
        name="gla",
    )(q, k, v, bf, q, k, v, cb)


def _mix_body(x_ref, up_ref, u_ref, un_ref, gp_ref, gg_ref, sg_ref, of_ref, ob_ref,
              g0_ref, b0_ref, pw_ref, ps_ref, gn_ref, wbp_ref, wbg_ref, wo_ref, g1_ref, b1_ref,
              h1_ref, h1p_ref, *, tiles_per_batch, seq):
    j = pl.program_id(0) % tiles_per_batch
    u_main = u_ref[...]
    u_ext = jnp.concatenate([up_ref[...], u_main, un_ref[...]], axis=0)
    r = lax.broadcasted_iota(I32, (ROWS, ROWS + 2 * HALO), 0)
    e = lax.broadcasted_iota(I32, (ROWS, ROWS + 2 * HALO), 1) - HALO
    in_seq = (j * ROWS + e) < seq
    pos = j * ROWS + lax.broadcasted_iota(I32, (ROWS, 1), 0)
    y_pool = []
    for gi, w in enumerate(POOL_WINDOWS):
        cs = slice(gi * POOL_GROUP, (gi + 1) * POOL_GROUP)
        band = jnp.logical_and(jnp.logical_and(e >= r - w // 2, e < r + w - w // 2), in_seq)
        count = w - jnp.maximum(pos + (w - w // 2) - seq, 0)
        mean = _dot(band.astype(BF16), u_ext[:, cs]) / count.astype(F32)
        d = mean - u_main[:, cs].astype(F32)
        y_pool.append(_dot(d.astype(BF16), pw_ref[gi]))
    y_pool = (jnp.concatenate(y_pool, axis=1) * ps_ref[...]).astype(BF16)

    o = of_ref[...].astype(F32) + ob_ref[...].astype(F32)
    y_gla = []
    for hd in range(GLA_HEADS):
        oh = o[:, hd * HEAD_V:(hd + 1) * HEAD_V]
        y_gla.append(oh * lax.rsqrt(jnp.mean(oh * oh, axis=-1, keepdims=True) + RMS_EPS))
    y_gla = (jnp.concatenate(y_gla, axis=1) * gn_ref[...] * sg_ref[...].astype(F32)).astype(BF16)

    merged = (gp_ref[...].astype(F32) * _dot(y_pool, wbp_ref[...])
              + gg_ref[...].astype(F32) * _dot(y_gla, wbg_ref[...]))
    y = _dot(merged.astype(BF16), wo_ref[...])
    h = _layer_norm(x_ref[...], g0_ref[...], b0_ref[...])
    h1 = _layer_norm(DN_ALPHA * h + y, g1_ref[...], b1_ref[...])
    h1_ref[...] = h1
    h1p_ref[...] = _pack_halves(h1[:, :D_MODEL // 2], h1[:, D_MODEL // 2:])


def _mix(x2, u, gp, gg, sg, o_f, o_b, ln0_g, ln0_b, pool_w, pool_scale, norm_g, wbp, wbg, wo,
         ln1_g, ln1_b, seq):
    t = x2.shape[0]
    tiles_per_batch = seq // ROWS
    padded_tiles = tiles_per_batch + FRONT // ROWS
    halo_per_tile = ROWS // HALO
    last_halo = u.shape[0] // HALO - 1

    def pidx(i):
        return (i // tiles_per_batch) * padded_tiles + FRONT // ROWS + i % tiles_per_batch

    padded = lambda width: pl.BlockSpec((ROWS, width), lambda i: (pidx(i), 0))
    plain = lambda width: pl.BlockSpec((ROWS, width), lambda i: (i, 0))
    prev_halo = pl.BlockSpec((HALO, POOL_WIDTH), lambda i: (pidx(i) * halo_per_tile - 1, 0))
    next_halo = pl.BlockSpec(
        (HALO, POOL_WIDTH), lambda i: (jnp.minimum((pidx(i) + 1) * halo_per_tile, last_halo), 0))
    return pl.pallas_call(
        functools.partial(_mix_body, tiles_per_batch=tiles_per_batch, seq=seq),
        grid=(t // ROWS,),
        in_specs=[plain(D_MODEL), prev_halo, padded(POOL_WIDTH), next_halo,
                  padded(D_MODEL), padded(D_MODEL), padded(GLA_DV), padded(GLA_DV), padded(GLA_DV),
                  _const_spec((1, D_MODEL)), _const_spec((1, D_MODEL)),
                  _const_spec((len(POOL_WINDOWS), POOL_GROUP, POOL_GROUP)), _const_spec((1, POOL_WIDTH)),
                  _const_spec((1, GLA_DV)), _const_spec((POOL_WIDTH, D_MODEL)),
                  _const_spec((GLA_DV, D_MODEL)), _const_spec((D_MODEL, D_MODEL)),
                  _const_spec((1, D_MODEL)), _const_spec((1, D_MODEL))],
        out_specs=[plain(D_MODEL), plain(D_MODEL // 2)],
        out_shape=[jax.ShapeDtypeStruct((t, D_MODEL), F32), jax.ShapeDtypeStruct((t, D_MODEL // 2), U32)],
        compiler_params=pltpu.CompilerParams(dimension_semantics=("arbitrary",),
                                             vmem_limit_bytes=VMEM_LIMIT),
        name="mix",
    )(x2, u, u, u, gp, gg, sg, o_f, o_b, ln0_g, ln0_b, pool_w, pool_scale, norm_g, wbp, wbg, wo,
      ln1_g, ln1_b)


def _first_index_of_max(val, idx, size):
    m = jnp.max(val, axis=0, keepdims=True)
    first = jnp.min(jnp.where(val == m, idx, size), axis=0, keepdims=True)
    return m, first


def _route_body(h_ref, wr_ref, bias_ref, eidx_ref, w_ref, before_ref, cnt_ref, base_ref):
    n = ROUTE_ROWS

    @pl.when(pl.program_id(0) == 0)
    def _():
        base_ref[...] = jnp.zeros_like(base_ref)

    h = h_ref[...]
    h_hi = h.astype(BF16)
    h_mid = (h - h_hi.astype(F32)).astype(BF16)
    logits = _dot_nt(wr_ref[...], jnp.concatenate([h_hi, h_mid, h_hi], axis=1))
    scores = jax.nn.sigmoid(logits)
    sel = scores + bias_ref[...]

    gidx = lax.broadcasted_iota(I32, (GROUP_SIZE, n), 0)
    groups = [sel[g * GROUP_SIZE:(g + 1) * GROUP_SIZE, :] for g in range(N_GROUPS)]
    gscore = []
    for grp in groups:
        m1, first = _first_index_of_max(grp, gidx, GROUP_SIZE)
        m2 = jnp.max(jnp.where(gidx == first, -jnp.inf, grp), axis=0, keepdims=True)
        gscore.append(m1 + m2)

    masked = []
    for gi in range(N_GROUPS):
        beaten = jnp.zeros((1, n), F32)
        for gj in range(N_GROUPS):
            if gj != gi:
                wins = (gscore[gj] >= gscore[gi]) if gj < gi else (gscore[gj] > gscore[gi])
                beaten = beaten + wins.astype(F32)
        masked.append(jnp.where(beaten < TOPK_GROUPS, groups[gi], -jnp.inf))
    val = jnp.concatenate(masked, axis=0)

    eidx = lax.broadcasted_iota(I32, (N_EXPERTS, n), 0)
    candidate = val != -jnp.inf
    picks, weights = [], []
    for _ in range(TOP_K):
        _, first = _first_index_of_max(val, eidx, N_EXPERTS)
        hit = eidx == first
        picks.append(first)
        weights.append(jnp.sum(jnp.where(hit, scores, 0.0), axis=0, keepdims=True))
        val = jnp.where(hit, -jnp.inf, val)
    chosen_f = jnp.where(jnp.logical_and(candidate, val == -jnp.inf), 1.0, 0.0)
    wsum = weights[0]
    for wk in weights[1:]:
        wsum = wsum + wk
    scale = ROUTED_SCALE / wsum

    r = lax.broadcasted_iota(I32, (n, n), 0)
    c = lax.broadcasted_iota(I32, (n, n), 1)
    earlier = (r < c).astype(BF16)
    before_ref[...] = _dot(chosen_f.astype(BF16), earlier) + base_ref[...]
    base_ref[...] = base_ref[...] + jnp.sum(chosen_f, axis=1, keepdims=True)
    cnt_ref[...] = jnp.broadcast_to(base_ref[...], cnt_ref.shape).astype(I32)

    for kk in range(TOP_K):
        eidx_ref[kk:kk + 1, :] = picks[kk]
        w_ref[kk:kk + 1, :] = weights[kk] * scale


def _route(h1, wr_t, bias_col):
    t = h1.shape[0]
    col = lambda dtype: jax.ShapeDtypeStruct((TOP_K, t), dtype)
    kspec = pl.BlockSpec((TOP_K, ROUTE_ROWS), lambda i: (0, i))
    return pl.pallas_call(
        _route_body,
        grid=(t // ROUTE_ROWS,),
        in_specs=[pl.BlockSpec((ROUTE_ROWS, D_MODEL), lambda i: (i, 0)),
                  _const_spec((N_EXPERTS, 3 * D_MODEL)), _const_spec((N_EXPERTS, 1))],
        out_specs=[kspec, kspec, pl.BlockSpec((N_EXPERTS, ROUTE_ROWS), lambda i: (0, i)),
                   _const_spec((N_EXPERTS, LANE))],
        out_shape=[col(I32), col(F32), jax.ShapeDtypeStruct((N_EXPERTS, t), F32),
                   jax.ShapeDtypeStruct((N_EXPERTS, LANE), I32)],
        scratch_shapes=[pltpu.VMEM((N_EXPERTS, 1), F32)],
        compiler_params=pltpu.CompilerParams(dimension_semantics=("arbitrary",),
                                             vmem_limit_bytes=VMEM_LIMIT),
        name="route",
    )(h1, wr_t, bias_col)


def _dest_body(eidx_ref, before_ref, pstart_ref, dest_ref):
    n = eidx_ref.shape[1]
    eiota = lax.broadcasted_iota(I32, (N_EXPERTS, n), 0)
    row_of = pstart_ref[...] + before_ref[...]
    for kk in range(TOP_K):
        dest = jnp.sum(jnp.where(eiota == eidx_ref[kk:kk + 1, :], row_of, 0.0), axis=0, keepdims=True)
        dest = dest.astype(I32)
        for c in range(n // SC_CHUNK):
            dest_ref[c, kk:kk + 1, :] = dest[:, c * SC_CHUNK:(c + 1) * SC_CHUNK]


def _dest(eidx, before, pstart_col):
    t = eidx.shape[1]
    kspec = pl.BlockSpec((TOP_K, ROUTE_ROWS), lambda i: (0, i))
    chunks = ROUTE_ROWS // SC_CHUNK
    return pl.pallas_call(
        _dest_body,
        grid=(t // ROUTE_ROWS,),
        in_specs=[kspec, pl.BlockSpec((N_EXPERTS, ROUTE_ROWS), lambda i: (0, i)),
                  _const_spec((N_EXPERTS, 1))],
        out_specs=pl.BlockSpec((chunks, TOP_K, SC_CHUNK), lambda i: (i, 0, 0)),
        out_shape=jax.ShapeDtypeStruct((t // SC_CHUNK, TOP_K, SC_CHUNK), I32),
        compiler_params=pltpu.CompilerParams(dimension_semantics=("arbitrary",)),
        name="dest",
    )(eidx, before, pstart_col)


def _sc_mesh():
    return plsc.VectorSubcoreMesh(core_axis_name="c", subcore_axis_name="s",
                                  num_cores=SC_CORES, num_subcores=SC_SUBCORES)


def _sc_worker_chunks(t):
    assert t % (SC_CORES * SC_SUBCORES * SC_CHUNK) == 0
    per_worker = t // (SC_CORES * SC_SUBCORES * SC_CHUNK)
    worker = lax.axis_index("s") * SC_CORES + lax.axis_index("c")
    return worker * per_worker, per_worker


def _dispatch(dest, h1p, n_rows):
    t, width = h1p.shape

    @functools.partial(
        pl.kernel, mesh=_sc_mesh(), out_type=jax.ShapeDtypeStruct((n_rows, width), U32),
        scratch_types=[pltpu.VMEM((SC_CHUNK, width), U32), pltpu.VMEM((TOP_K, SC_CHUNK), I32),
                       pltpu.SemaphoreType.DMA],
        name="dispatch")
    def body(h_hbm, dest_hbm, xs_hbm, rows_v, idx_v, sem):
        first, count = _sc_worker_chunks(t)

        @pl.loop(0, count)
        def _(i):
            chunk = first + i
            pltpu.sync_copy(h_hbm.at[pl.ds(pl.multiple_of(chunk * SC_CHUNK, SC_CHUNK), SC_CHUNK)], rows_v)
            pltpu.sync_copy(dest_hbm.at[chunk], idx_v)
            copies = [pltpu.async_copy(rows_v, xs_hbm.at[idx_v.at[kk]], sem) for kk in range(TOP_K)]
            for c in copies:
                c.wait()

    return body(h1p, dest)


def _gather(dest, ys):
    t = dest.shape[0] * SC_CHUNK
    width = ys.shape[1]

    @functools.partial(
        pl.kernel, mesh=_sc_mesh(), out_type=jax.ShapeDtypeStruct((TOP_K, t, width), U32),
        scratch_types=[pltpu.VMEM((SC_CHUNK, width), U32), pltpu.VMEM((TOP_K, SC_CHUNK), I32),
                       pltpu.SemaphoreType.DMA],
        name="gather")
    def body(ys_hbm, dest_hbm, yg_hbm, rows_v, idx_v, sem):
        first, count = _sc_worker_chunks(t)

        @pl.loop(0, count)
        def _(i):
            chunk = first + i
            pltpu.sync_copy(dest_hbm.at[chunk], idx_v)
            for kk in range(TOP_K):
                pltpu.async_copy(ys_hbm.at[idx_v.at[kk]], rows_v, sem).wait()
                pltpu.sync_copy(
                    rows_v, yg_hbm.at[kk, pl.ds(pl.multiple_of(chunk * SC_CHUNK, SC_CHUNK), SC_CHUNK)])

    return body(ys, dest)


def _experts_body(bs_ref, xs_in_ref, wg_ref, wu_ref, wd_ref, xs_ref,
                  xbuf, ybuf, wgb, wub, wdb, sem_in, sem_out):
    del xs_in_ref
    e = pl.program_id(0)
    b0 = bs_ref[e]
    b1 = bs_ref[e + 1]
    n_total = bs_ref[N_EXPERTS]
    half = D_MODEL // 2

    def fetch(b):
        slot = b % IN_SLOTS
        return pltpu.make_async_copy(xs_ref.at[pl.ds(b * MOE_BLOCK, MOE_BLOCK)], xbuf.at[slot],
                                     sem_in.at[slot])

    def store(b):
        slot = b % OUT_SLOTS
        return pltpu.make_async_copy(ybuf.at[slot], xs_ref.at[pl.ds(b * MOE_BLOCK, MOE_BLOCK)],
                                     sem_out.at[slot])

    @pl.when(e == 0)
    def _():
        for b in range(LOOKAHEAD):
            @pl.when(b < n_total)
            def _():
                fetch(b).start(priority=1)

    @pl.when(b1 > b0)
    def _():
        wgb[...] = wg_ref[0].astype(BF16)
        wub[...] = wu_ref[0].astype(BF16)
        wdb[...] = wd_ref[0].astype(BF16)

    def run(blocks):
        for b in blocks:
            @pl.when(b + LOOKAHEAD < n_total)
            def _():
                fetch(b + LOOKAHEAD).start(priority=1)
        for b in blocks:
            fetch(b).wait()
        for b in blocks:
            @pl.when(b >= OUT_SLOTS)
            def _():
                store(b - OUT_SLOTS).wait()

        xa, xb = _unpack_halves(jnp.concatenate([xbuf[b % IN_SLOTS] for b in blocks], axis=0))
        xa = xa.astype(BF16)
        xb = xb.astype(BF16)
        gate = _dot(xa, wgb[:half, :]) + _dot(xb, wgb[half:, :])
        up = _dot(xa, wub[:half, :]) + _dot(xb, wub[half:, :])
        hidden = (gate * jax.nn.sigmoid(gate) * up).astype(BF16)
        y = _dot(hidden, wdb[...])
        packed = _pack_halves(y[:, :half], y[:, half:])
        for i, b in enumerate(blocks):
            ybuf[b % OUT_SLOTS] = packed[i * MOE_BLOCK:(i + 1) * MOE_BLOCK]
            store(b).start(priority=1)

    n = b1 - b0

    def group(i, carry):
        run([b0 + GROUP * i + j for j in range(GROUP)])
        return carry

    lax.fori_loop(0, n // GROUP, group, 0)
    rest = b0 + (n // GROUP) * GROUP
    left = n % GROUP
    size = GROUP // 2
    while size >= 1:
        @pl.when(left & size != 0)
        def _():
            first = rest + (left & ~(2 * size - 1))
            run([first + j for j in range(size)])
        size //= 2

    @pl.when(e == N_EXPERTS - 1)
    def _():
        for back in range(OUT_SLOTS, 0, -1):
            @pl.when(n_total >= back)
            def _():
                store(n_total - back).wait()


def _experts(block_start, xs, w_gate, w_up, w_down):
    n_rows, width = xs.shape
    wspec = lambda shape: pl.BlockSpec((1,) + shape, lambda e, bs: (e, 0, 0))
    grid_spec = pltpu.PrefetchScalarGridSpec(
        num_scalar_prefetch=1,
        grid=(N_EXPERTS,),
        in_specs=[pl.BlockSpec(memory_space=pl.ANY),
                  wspec((D_MODEL, D_EXPERT)), wspec((D_MODEL, D_EXPERT)), wspec((D_EXPERT, D_MODEL))],
        out_specs=pl.BlockSpec(memory_space=pl.ANY),
        scratch_shapes=[pltpu.VMEM((IN_SLOTS, MOE_BLOCK, width), U32),
                        pltpu.VMEM((OUT_SLOTS, MOE_BLOCK, width), U32),
                        pltpu.VMEM((D_MODEL, D_EXPERT), BF16), pltpu.VMEM((D_MODEL, D_EXPERT), BF16),
                        pltpu.VMEM((D_EXPERT, D_MODEL), BF16),
                        pltpu.SemaphoreType.DMA((IN_SLOTS,)), pltpu.SemaphoreType.DMA((OUT_SLOTS,))],
    )
    return pl.pallas_call(
        _experts_body,
        grid_spec=grid_spec,
        out_shape=jax.ShapeDtypeStruct((n_rows, width), U32),
        input_output_aliases={1: 0},
        compiler_params=pltpu.CompilerParams(dimension_semantics=("arbitrary",),
                                             vmem_limit_bytes=VMEM_LIMIT),
        name="experts",
    )(block_start, xs, w_gate, w_up, w_down)


def _combine_body(h1_ref, w_ref, yg_ref, wsg_ref, wsu_ref, wsd_ref, g2_ref, b2_ref, *refs):
    out_ref = refs[-1]
    h1 = h1_ref[...]
    hb = h1.astype(BF16)
    gate = _dot(hb, wsg_ref[...])
    up = _dot(hb, wsu_ref[...])
    shared = _dot((gate * jax.nn.sigmoid(gate) * up).astype(BF16), wsd_ref[...])

    half = D_MODEL // 2
    lo = jnp.zeros((ROWS, half), F32)
    hi = jnp.zeros((ROWS, half), F32)
    for kk in range(TOP_K):
        a, b = _unpack_halves(yg_ref[kk])
        wk = w_ref[:, kk:kk + 1]
        lo = lo + wk * a
        hi = hi + wk * b
    z = DN_ALPHA * h1 + shared + jnp.concatenate([lo, hi], axis=1)
    out_ref[...] = _layer_norm(z, g2_ref[...], b2_ref[...])


def _combine(h1, w_rows, yg, first_tile, out_prev, wsg, wsu, wsd, ln2_g, ln2_b):
    width = yg.shape[2]
    d_shared = wsg.shape[1]
    rows = lambda cols: pl.BlockSpec((ROWS, cols), lambda i: (first_tile + i, 0))
    args = [h1, w_rows, yg, wsg, wsu, wsd, ln2_g, ln2_b]
    in_specs = [rows(D_MODEL), rows(TOP_K),
                pl.BlockSpec((TOP_K, ROWS, width), lambda i: (0, i, 0)),
                _const_spec((D_MODEL, d_shared)), _const_spec((D_MODEL, d_shared)),
                _const_spec((d_shared, D_MODEL)),
                _const_spec((1, D_MODEL)), _const_spec((1, D_MODEL))]
    aliases = {}
    if out_prev is not None:
        aliases = {len(args): 0}
        args.append(out_prev)
        in_specs.append(pl.BlockSpec(memory_space=pl.ANY))
    return pl.pallas_call(
        _combine_body,
        grid=(yg.shape[1] // ROWS,),
        in_specs=in_specs,
        out_specs=rows(D_MODEL),
        out_shape=jax.ShapeDtypeStruct(h1.shape, F32),
        input_output_aliases=aliases,
        compiler_params=pltpu.CompilerParams(dimension_semantics=("arbitrary",),
                                             vmem_limit_bytes=VMEM_LIMIT),
        name="combine",
    )(*args)


def _pack_in_proj(w_in, b_in, gate_w_f, gate_w_b, gate_b_f, gate_b_b):
    o_pool, o_q, o_k, o_v, o_g = 0, 512, 1024, 1536, 2560
    o_lr, o_gp, o_gg = 3584, 3616, 4640
    order = [(o_pool, POOL_WIDTH), (o_q, GLA_DK), (o_k, GLA_DK), (o_v, GLA_DV), (o_g, GLA_DV),
             (o_gp, D_MODEL), (o_gg, D_MODEL)] + [(o_lr, 2 * GATE_RANK)] * 3
    pad = LANE - 6 * GATE_RANK
    w = jnp.concatenate([w_in[:, o:o + n] for o, n in order] + [jnp.zeros((D_MODEL, pad), F32)], axis=1)
    b = jnp.concatenate([b_in[o:o + n] for o, n in order] + [jnp.zeros((pad,), F32)])
    w_gate = jnp.zeros((2 * GATE_RANK, 2 * GLA_DK), F32)
    w_gate = w_gate.at[:GATE_RANK, :GLA_DK].set(gate_w_f)
    w_gate = w_gate.at[GATE_RANK:, GLA_DK:].set(gate_w_b)
    w_hi = w_gate.astype(BF16)
    w_lo = (w_gate - w_hi.astype(F32)).astype(BF16)
    w_gate3 = jnp.concatenate([w_hi, w_hi, w_lo, jnp.zeros((pad, 2 * GLA_DK), BF16)], axis=0)
    b_gate = jnp.concatenate([gate_b_f, gate_b_b])[None, :]
    return w.astype(BF16), b[None, :], w_gate3, b_gate


def kernel(x, meta, ln0_g, ln0_b, w_in, b_in, pool_w, pool_scale, gate_w_f, gate_b_f, gate_w_b, gate_b_b,
           gla_norm_g, w_branch_pool, w_branch_gla, w_out, ln1_g, ln1_b, w_router, router_bias,
           w_exp_gate, w_exp_up, w_exp_down, w_sh_gate, w_sh_up, w_sh_down, ln2_g, ln2_b):
    batch, seq, d = x.shape
    assert d == D_MODEL and seq % ROWS == 0 and w_in.shape[0] == 1
    assert (batch * seq) % ROUTE_ROWS == 0
    t = batch * seq
    row = lambda a: a.reshape(1, -1).astype(F32)

    lead = jnp.concatenate([jnp.zeros((FRONT - N_META, d), x.dtype), meta.astype(x.dtype)], axis=0)
    tiles_per_batch = (seq + FRONT) // ROWS
    x2 = x.reshape(t, d)

    w_packed, b_packed, w_gate, b_gate = _pack_in_proj(
        w_in[0], b_in[0], gate_w_f[0], gate_w_b[0], gate_b_f[0], gate_b_b[0])
    u, q, k, v, sg, gp, gg, bf, cb = _in_proj(
        lead, x2, row(ln0_g), row(ln0_b), w_packed, b_packed, w_gate, b_gate, tiles_per_batch)

    o_f, o_b = _gla(q, k, v, bf, cb, batch)

    h1, h1p = _mix(
        x.reshape(t, d), u, gp, gg, sg, o_f, o_b, row(ln0_g), row(ln0_b),
        pool_w[0].astype(BF16), row(pool_scale[0]), row(jnp.tile(gla_norm_g[0], GLA_HEADS)),
        w_branch_pool[0].astype(BF16), w_branch_gla[0].astype(BF16), w_out[0].astype(BF16),
        row(ln1_g[0]), row(ln1_b[0]), seq)

    wr = w_router[0].T.astype(F32)
    wr_hi = wr.astype(BF16)
    wr_mid = (wr - wr_hi.astype(F32)).astype(BF16)
    wr_terms = jnp.concatenate([wr_hi, wr_hi, wr_mid], axis=1)
    eidx, ew, before, counts = _route(h1, wr_terms, router_bias[0].reshape(N_EXPERTS, 1))

    counts = counts[:, 0]
    padded = ((counts + MOE_BLOCK - 1) // MOE_BLOCK) * MOE_BLOCK
    pend = jnp.cumsum(padded)
    pstart = pend - padded
    dest = _dest(eidx, before, pstart.astype(F32).reshape(N_EXPERTS, 1))
    n_blocks = -(-(t * TOP_K) // MOE_BLOCK) + N_EXPERTS
    block_start = (jnp.concatenate([pstart, pend[-1:]]) // MOE_BLOCK).astype(I32)

    xs = _dispatch(dest, h1p, n_blocks * MOE_BLOCK)
    ys = _experts(block_start, xs, w_exp_gate[0], w_exp_up[0], w_exp_down[0])
    shared_w = (w_sh_gate[0].astype(BF16), w_sh_up[0].astype(BF16), w_sh_down[0].astype(BF16))
    ew_rows = ew.T
    windows = t // SC_CHUNK // COMBINE_CHUNKS
    out = None
    for c in range(COMBINE_CHUNKS):
        yg = _gather(dest[c * windows:(c + 1) * windows], ys)
        out = _combine(h1, ew_rows, yg, c * windows * SC_CHUNK // ROWS, out, *shared_w,
                       row(ln2_g[0]), row(ln2_b[0]))
    return out.reshape(batch, seq, d)
```

```python
import functools

import jax
import jax.numpy as jnp
from jax import lax
from jax.experimental import pallas as pl
from jax.experimental.pallas import tpu as pltpu
from jax.experimental.pallas import tpu_sc as plsc

F32 = jnp.float32
BF16 = jnp.bfloat16
I32 = jnp.int32
U32 = jnp.uint32

D_MODEL = 1024
N_META = 16
POOL_WINDOWS = (2, 4, 8, 16)
POOL_GROUP = 128
POOL_WIDTH = POOL_GROUP * len(POOL_WINDOWS)
GLA_HEADS = 4
GLA_DK = 512
GLA_DV = 1024
HEAD_K = GLA_DK // GLA_HEADS
HEAD_V = GLA_DV // GLA_HEADS
GATE_RANK = 16
GATE_NORMALIZER = 16.0
RMS_EPS = 1e-5
N_EXPERTS = 256
TOP_K = 8
N_GROUPS = 8
GROUP_SIZE = N_EXPERTS // N_GROUPS
TOPK_GROUPS = 4
D_EXPERT = 256
ROUTED_SCALE = 2.5
DN_ALPHA = 2.0 ** 0.25
LN_EPS = 1e-5

LANE = 128
ROWS = 256
FRONT = ROWS
GLA_CHUNK = 256
HALO = 16
ROUTE_ROWS = 1024
MOE_BLOCK = 256
LOOKAHEAD = 6
GROUP = 4
assert GROUP & (GROUP - 1) == 0
IN_SLOTS = LOOKAHEAD + GROUP
OUT_SLOTS = 2 * GROUP
SC_CORES = 2
SC_SUBCORES = 16
SC_CHUNK = 128
COMBINE_CHUNKS = 8
VMEM_LIMIT = 56 * 1024 * 1024

C_POOL, C_Q, C_K, C_V, C_G, C_GP, C_GG, C_LR = 0, 512, 1024, 1536, 2560, 3584, 4608, 5632
W_PACKED = C_LR + LANE


def _layer_norm(x, g, b):
    mu = jnp.mean(x, axis=-1, keepdims=True)
    xc = x - mu
    var = jnp.mean(xc * xc, axis=-1, keepdims=True)
    return xc * lax.rsqrt(var + LN_EPS) * g + b


def _dot(a, b):
    return jnp.dot(a, b, preferred_element_type=F32)


def _dot_nt(a, b):
    return lax.dot_general(a, b, (((1,), (1,)), ((), ())), preferred_element_type=F32)


def _split_dot(m2, g):
    g_hi = g.astype(BF16)
    g_lo = (g - g_hi.astype(F32)).astype(BF16)
    return _dot(m2, jnp.concatenate([g_hi, g_lo], axis=0))


def _pack_halves(a, b):
    pa = lax.bitcast_convert_type(a.astype(BF16).astype(F32), U32)
    pb = lax.bitcast_convert_type(b.astype(BF16).astype(F32), U32)
    return pa | (pb >> 16)


def _unpack_halves(p):
    a = lax.bitcast_convert_type(p & jnp.uint32(0xFFFF0000), F32)
    b = lax.bitcast_convert_type(p << 16, F32)
    return a, b


def _const_spec(shape):
    return pl.BlockSpec(shape, lambda *_: (0,) * len(shape))


def _inproj_body(lead_ref, x_ref, g0_ref, b0_ref, w_ref, bias_ref, wgate_ref, bgate_ref,
                 u_ref, q_ref, k_ref, v_ref, sg_ref, gp_ref, gg_ref, bf_ref, cb_ref, h_even, h_odd,
                 *, tiles_per_batch):
    i = pl.program_id(0)

    def normalised(is_lead):
        x_in = jnp.where(is_lead, lead_ref[...], x_ref[...])
        return _layer_norm(x_in, g0_ref[...], b0_ref[...]).astype(BF16)

    @pl.when(i == 0)
    def _():
        h_even[...] = normalised(True)

    for parity, (h_ref, h_next_ref) in enumerate(((h_even, h_odd), (h_odd, h_even))):
        @pl.when(i % 2 == parity)
        def _():
            h_next_ref[...] = normalised((i + 1) % tiles_per_batch == 0)
            _inproj_tile(h_ref[...], i % tiles_per_batch == 0, w_ref, bias_ref, wgate_ref, bgate_ref,
                         u_ref, q_ref, k_ref, v_ref, sg_ref, gp_ref, gg_ref, bf_ref, cb_ref)


def _inproj_tile(hb, is_lead, w_ref, bias_ref, wgate_ref, bgate_ref,
                 u_ref, q_ref, k_ref, v_ref, sg_ref, gp_ref, gg_ref, bf_ref, cb_ref):
    def proj(c0, n):
        return _dot(hb, w_ref[:, c0:c0 + n]) + bias_ref[:, c0:c0 + n]

    row = lax.broadcasted_iota(I32, (ROWS, 1), 0)
    valid = row >= jnp.where(is_lead, FRONT - N_META, 0)

    lr3 = proj(C_LR, LANE)
    g = proj(C_G, GLA_DV)
    sg_ref[...] = (g * jax.nn.sigmoid(g)).astype(BF16)
    lr_hi = lr3.astype(BF16)
    lr_lo = (lr3 - lr_hi.astype(F32)).astype(BF16)
    lane = lax.broadcasted_iota(I32, (1, LANE), 1)
    second = jnp.logical_and(lane >= 2 * GATE_RANK, lane < 4 * GATE_RANK)
    xg = _dot(jnp.where(second, lr_lo, lr_hi), wgate_ref[...]) + bgate_ref[...]
    gp_ref[...] = jax.nn.sigmoid(proj(C_GP, D_MODEL)).astype(BF16)
    gk = (jnp.minimum(xg, 0.0) - jnp.log(1.0 + jnp.exp(-jnp.abs(xg)))) * (1.0 / GATE_NORMALIZER)
    gg_ref[...] = jax.nn.sigmoid(proj(C_GG, D_MODEL)).astype(BF16)

    r = lax.broadcasted_iota(I32, (GLA_CHUNK, 2 * GLA_CHUNK), 0)
    c = lax.broadcasted_iota(I32, (GLA_CHUNK, 2 * GLA_CHUNK), 1) % GLA_CHUNK
    tril = (r >= c).astype(BF16)
    triu = (r <= c).astype(BF16)
    for ci in range(ROWS // GLA_CHUNK):
        sl = slice(ci * GLA_CHUNK, (ci + 1) * GLA_CHUNK)
        bf_ref[sl, :] = _split_dot(tril, gk[sl, :GLA_DK])
        cb_ref[sl, :] = _split_dot(triu, gk[sl, GLA_DK:])

    v_ref[...] = jnp.where(valid, proj(C_V, GLA_DV), 0.0).astype(BF16)
    k_ref[...] = jnp.where(valid, proj(C_K, GLA_DK), 0.0).astype(BF16)
    q_ref[...] = (proj(C_Q, GLA_DK) * (HEAD_K ** -0.5)).astype(BF16)
    u_ref[...] = proj(C_POOL, POOL_WIDTH).astype(BF16)


def _in_proj(lead, x2, ln0_g, ln0_b, w_packed, b_packed, w_gate, b_gate, tiles_per_batch):
    x_tiles = tiles_per_batch - FRONT // ROWS
    n = x2.shape[0] // (x_tiles * ROWS) * tiles_per_batch * ROWS
    row_spec = lambda width: pl.BlockSpec((ROWS, width), lambda i: (i, 0))
    n_steps = n // ROWS

    def next_x_tile(i):
        j = jnp.minimum(i + 1, n_steps - 1)
        return (j // tiles_per_batch) * x_tiles + jnp.maximum(j % tiles_per_batch - 1, 0), 0

    x_spec = pl.BlockSpec((ROWS, D_MODEL), next_x_tile)
    out_widths = (POOL_WIDTH, GLA_DK, GLA_DK, GLA_DV, GLA_DV, D_MODEL, D_MODEL)
    out_shape = [jax.ShapeDtypeStruct((n, w), BF16) for w in out_widths]
    out_shape += [jax.ShapeDtypeStruct((n, GLA_DK), F32)] * 2
    out_specs = [row_spec(w) for w in out_widths] + [row_spec(GLA_DK)] * 2
    return pl.pallas_call(
        functools.partial(_inproj_body, tiles_per_batch=tiles_per_batch),
        grid=(n // ROWS,),
        in_specs=[_const_spec((FRONT, D_MODEL)), x_spec,
                  _const_spec((1, D_MODEL)), _const_spec((1, D_MODEL)),
                  _const_spec((D_MODEL, W_PACKED)), _const_spec((1, W_PACKED)),
                  _const_spec((LANE, 2 * GLA_DK)), _const_spec((1, 2 * GLA_DK))],
        out_specs=out_specs,
        out_shape=out_shape,
        scratch_shapes=[pltpu.VMEM((ROWS, D_MODEL), BF16)] * 2,
        compiler_params=pltpu.CompilerParams(dimension_semantics=("arbitrary",),
                                             vmem_limit_bytes=VMEM_LIMIT),
        name="in_proj",
    )(lead, x2, ln0_g, ln0_b, w_packed, b_packed, w_gate, b_gate)


def _gla_direction(q_ref, k_ref, v_ref, b_ref, o_ref, s_ref, reverse):
    c = GLA_CHUNK
    r = lax.broadcasted_iota(I32, (c, c), 0)
    s = lax.broadcasted_iota(I32, (c, c), 1)
    keep = (r <= s) if reverse else (r >= s)
    eye = lax.broadcasted_iota(I32, (HEAD_K, HEAD_K), 0) == lax.broadcasted_iota(I32, (HEAD_K, HEAD_K), 1)
    edge = 0 if reverse else c - 1
    for h in range(GLA_HEADS):
        ks = slice(h * HEAD_K, (h + 1) * HEAD_K)
        vs = slice(h * HEAD_V, (h + 1) * HEAD_V)
        b = b_ref[:, ks]
        qh = q_ref[:, ks].astype(F32)
        kh = k_ref[:, ks].astype(F32)
        vh = v_ref[:, vs]
        b_tot = b[edge:edge + 1, :]
        b_mid = b[c // 2:c // 2 + 1, :]
        q_state = (qh * jnp.exp(b)).astype(BF16)
        q_in = (qh * jnp.exp(b - b_mid)).astype(BF16)
        k_in = (kh * jnp.exp(b_mid - b)).astype(BF16)
        k_state = (kh * jnp.exp(b_tot - b)).T.astype(BF16)
        att = jnp.where(keep, _dot_nt(q_in, k_in), 0.0).astype(BF16)
        state = s_ref[h]
        o = _dot(q_state, state.astype(BF16)) + _dot(att, vh)
        o_ref[:, vs] = o.astype(o_ref.dtype)
        decay_col = jnp.sum(jnp.where(eye, jnp.exp(b_tot), 0.0), axis=1, keepdims=True)
        s_ref[h] = decay_col * state + _dot(k_state, vh)


def _gla_body(qf_ref, kf_ref, vf_ref, bf_ref, qb_ref, kb_ref, vb_ref, cb_ref,
              of_ref, ob_ref, sf_ref, sb_ref):
    @pl.when(pl.program_id(0) == 0)
    def _():
        sf_ref[...] = jnp.zeros_like(sf_ref)
        sb_ref[...] = jnp.zeros_like(sb_ref)

    for b in range(qf_ref.shape[0]):
        _gla_direction(qf_ref.at[b], kf_ref.at[b], vf_ref.at[b], bf_ref.at[b], of_ref.at[b], sf_ref.at[b],
                       reverse=False)
        _gla_direction(qb_ref.at[b], kb_ref.at[b], vb_ref.at[b], cb_ref.at[b], ob_ref.at[b], sb_ref.at[b],
                       reverse=True)


def _gla(q, k, v, bf, cb, batch):
    n = q.shape[0]
    nch = n // batch // GLA_CHUNK
    per_batch = lambda a: a.reshape(batch, n // batch, a.shape[1])
    fwd = lambda width: pl.BlockSpec((batch, GLA_CHUNK, width), lambda i: (0, i, 0))
    bwd = lambda width: pl.BlockSpec((batch, GLA_CHUNK, width), lambda i: (0, nch - 1 - i, 0))
    state = pltpu.VMEM((batch, GLA_HEADS, HEAD_K, HEAD_V), F32)
    q, k, v, bf, cb = (per_batch(a) for a in (q, k, v, bf, cb))
    o_f, o_b = pl.pallas_call(
        _gla_body,
        grid=(nch,),
        in_specs=[fwd(GLA_DK), fwd(GLA_DK), fwd(GLA_DV), fwd(GLA_DK),
                  bwd(GLA_DK), bwd(GLA_DK), bwd(GLA_DV), bwd(GLA_DK)],
        out_specs=[fwd(GLA_DV), bwd(GLA_DV)],
        out_shape=[jax.ShapeDtypeStruct((batch, n // batch, GLA_DV), BF16)] * 2,
        scratch_shapes=[state, state],
        compiler_params=pltpu.CompilerParams(dimension_semantics=("arbitrary",),
                                             vmem_limit_bytes=VMEM_LIMIT),
        name="gla",
    )(q, k, v, bf, q, k, v, cb)
    return o_f.reshape(n, GLA_DV), o_b.reshape(n, GLA_DV)


def _mix_body(x_ref, up_ref, u_ref, un_ref, gp_ref, gg_ref, sg_ref, of_ref, ob_ref,
              g0_ref, b0_ref, pw_ref, ps_ref, gn_ref, wbp_ref, wbg_ref, wo_ref, g1_ref, b1_ref,
              h1_ref, h1p_ref, *, tiles_per_batch, seq):
    j = pl.program_id(0) % tiles_per_batch
    u_main = u_ref[...]
    u_ext = jnp.concatenate([up_ref[...], u_main, un_ref[...]], axis=0)
    r = lax.broadcasted_iota(I32, (ROWS, ROWS + 2 * HALO), 0)
    e = lax.broadcasted_iota(I32, (ROWS, ROWS + 2 * HALO), 1) - HALO
    in_seq = (j * ROWS + e) < seq
    pos = j * ROWS + lax.broadcasted_iota(I32, (ROWS, 1), 0)
    y_pool = []
    for gi, w in enumerate(POOL_WINDOWS):
        cs = slice(gi * POOL_GROUP, (gi + 1) * POOL_GROUP)
        band = jnp.logical_and(jnp.logical_and(e >= r - w // 2, e < r + w - w // 2), in_seq)
        count = w - jnp.maximum(pos + (w - w // 2) - seq, 0)
        mean = _dot(band.astype(BF16), u_ext[:, cs]) / count.astype(F32)
        d = mean - u_main[:, cs].astype(F32)
        y_pool.append(_dot(d.astype(BF16), pw_ref[gi]))
    y_pool = (jnp.concatenate(y_pool, axis=1) * ps_ref[...]).astype(BF16)

    o = of_ref[...].astype(F32) + ob_ref[...].astype(F32)
    y_gla = []
    for hd in range(GLA_HEADS):
        oh = o[:, hd * HEAD_V:(hd + 1) * HEAD_V]
        y_gla.append(oh * lax.rsqrt(jnp.mean(oh * oh, axis=-1, keepdims=True) + RMS_EPS))
    y_gla = (jnp.concatenate(y_gla, axis=1) * gn_ref[...] * sg_ref[...].astype(F32)).astype(BF16)

    merged = (gp_ref[...].astype(F32) * _dot(y_pool, wbp_ref[...])
              + gg_ref[...].astype(F32) * _dot(y_gla, wbg_ref[...]))
    y = _dot(merged.astype(BF16), wo_ref[...])
    h = _layer_norm(x_ref[...], g0_ref[...], b0_ref[...])
    h1 = _layer_norm(DN_ALPHA * h + y, g1_ref[...], b1_ref[...])
    h1_ref[...] = h1
    h1p_ref[...] = _pack_halves(h1[:, :D_MODEL // 2], h1[:, D_MODEL // 2:])


def _mix(x2, u, gp, gg, sg, o_f, o_b, ln0_g, ln0_b, pool_w, pool_scale, norm_g, wbp, wbg, wo,
         ln1_g, ln1_b, seq):
    t = x2.shape[0]
    tiles_per_batch = seq // ROWS
    padded_tiles = tiles_per_batch + FRONT // ROWS
    halo_per_tile = ROWS // HALO
    last_halo = u.shape[0] // HALO - 1

    def pidx(i):
        return (i // tiles_per_batch) * padded_tiles + FRONT // ROWS + i % tiles_per_batch

    padded = lambda width: pl.BlockSpec((ROWS, width), lambda i: (pidx(i), 0))
    plain = lambda width: pl.BlockSpec((ROWS, width), lambda i: (i, 0))
    prev_halo = pl.BlockSpec((HALO, POOL_WIDTH), lambda i: (pidx(i) * halo_per_tile - 1, 0))
    next_halo = pl.BlockSpec(
        (HALO, POOL_WIDTH), lambda i: (jnp.minimum((pidx(i) + 1) * halo_per_tile, last_halo), 0))
    return pl.pallas_call(
        functools.partial(_mix_body, tiles_per_batch=tiles_per_batch, seq=seq),
        grid=(t // ROWS,),
        in_specs=[plain(D_MODEL), prev_halo, padded(POOL_WIDTH), next_halo,
                  padded(D_MODEL), padded(D_MODEL), padded(GLA_DV), padded(GLA_DV), padded(GLA_DV),
                  _const_spec((1, D_MODEL)), _const_spec((1, D_MODEL)),
                  _const_spec((len(POOL_WINDOWS), POOL_GROUP, POOL_GROUP)), _const_spec((1, POOL_WIDTH)),
                  _const_spec((1, GLA_DV)), _const_spec((POOL_WIDTH, D_MODEL)),
                  _const_spec((GLA_DV, D_MODEL)), _const_spec((D_MODEL, D_MODEL)),
                  _const_spec((1, D_MODEL)), _const_spec((1, D_MODEL))],
        out_specs=[plain(D_MODEL), plain(D_MODEL // 2)],
        out_shape=[jax.ShapeDtypeStruct((t, D_MODEL), F32), jax.ShapeDtypeStruct((t, D_MODEL // 2), U32)],
        compiler_params=pltpu.CompilerParams(dimension_semantics=("arbitrary",),
                                             vmem_limit_bytes=VMEM_LIMIT),
        name="mix",
    )(x2, u, u, u, gp, gg, sg, o_f, o_b, ln0_g, ln0_b, pool_w, pool_scale, norm_g, wbp, wbg, wo,
      ln1_g, ln1_b)


def _first_index_of_max(val, idx, size):
    m = jnp.max(val, axis=0, keepdims=True)
    first = jnp.min(jnp.where(val == m, idx, size), axis=0, keepdims=True)
    return m, first


def _route_body(h_ref, wr_ref, bias_ref, eidx_ref, w_ref, before_ref, cnt_ref, base_ref):
    n = ROUTE_ROWS

    @pl.when(pl.program_id(0) == 0)
    def _():
        base_ref[...] = jnp.zeros_like(base_ref)

    h = h_ref[...]
    h_hi = h.astype(BF16)
    h_mid = (h - h_hi.astype(F32)).astype(BF16)
    logits = _dot_nt(wr_ref[...], jnp.concatenate([h_hi, h_mid, h_hi], axis=1))
    scores = jax.nn.sigmoid(logits)
    sel = scores + bias_ref[...]

    gidx = lax.broadcasted_iota(I32, (GROUP_SIZE, n), 0)
    groups = [sel[g * GROUP_SIZE:(g + 1) * GROUP_SIZE, :] for g in range(N_GROUPS)]
    gscore = []
    for grp in groups:
        m1, first = _first_index_of_max(grp, gidx, GROUP_SIZE)
        m2 = jnp.max(jnp.where(gidx == first, -jnp.inf, grp), axis=0, keepdims=True)
        gscore.append(m1 + m2)

    masked = []
    for gi in range(N_GROUPS):
        beaten = jnp.zeros((1, n), F32)
        for gj in range(N_GROUPS):
            if gj != gi:
                wins = (gscore[gj] >= gscore[gi]) if gj < gi else (gscore[gj] > gscore[gi])
                beaten = beaten + wins.astype(F32)
        masked.append(jnp.where(beaten < TOPK_GROUPS, groups[gi], -jnp.inf))
    val = jnp.concatenate(masked, axis=0)

    eidx = lax.broadcasted_iota(I32, (N_EXPERTS, n), 0)
    candidate = val != -jnp.inf
    picks, weights = [], []
    for _ in range(TOP_K):
        _, first = _first_index_of_max(val, eidx, N_EXPERTS)
        hit = eidx == first
        picks.append(first)
        weights.append(jnp.sum(jnp.where(hit, scores, 0.0), axis=0, keepdims=True))
        val = jnp.where(hit, -jnp.inf, val)
    chosen_f = jnp.where(jnp.logical_and(candidate, val == -jnp.inf), 1.0, 0.0)
    wsum = weights[0]
    for wk in weights[1:]:
        wsum = wsum + wk
    scale = ROUTED_SCALE / wsum

    r = lax.broadcasted_iota(I32, (n, n), 0)
    c = lax.broadcasted_iota(I32, (n, n), 1)
    earlier = (r < c).astype(BF16)
    before_ref[...] = _dot(chosen_f.astype(BF16), earlier) + base_ref[...]
    base_ref[...] = base_ref[...] + jnp.sum(chosen_f, axis=1, keepdims=True)
    cnt_ref[...] = jnp.broadcast_to(base_ref[...], cnt_ref.shape).astype(I32)

    for kk in range(TOP_K):
        eidx_ref[kk:kk + 1, :] = picks[kk]
        w_ref[kk:kk + 1, :] = weights[kk] * scale


def _route(h1, wr_t, bias_col):
    t = h1.shape[0]
    col = lambda dtype: jax.ShapeDtypeStruct((TOP_K, t), dtype)
    kspec = pl.BlockSpec((TOP_K, ROUTE_ROWS), lambda i: (0, i))
    return pl.pallas_call(
        _route_body,
        grid=(t // ROUTE_ROWS,),
        in_specs=[pl.BlockSpec((ROUTE_ROWS, D_MODEL), lambda i: (i, 0)),
                  _const_spec((N_EXPERTS, 3 * D_MODEL)), _const_spec((N_EXPERTS, 1))],
        out_specs=[kspec, kspec, pl.BlockSpec((N_EXPERTS, ROUTE_ROWS), lambda i: (0, i)),
                   _const_spec((N_EXPERTS, LANE))],
        out_shape=[col(I32), col(F32), jax.ShapeDtypeStruct((N_EXPERTS, t), F32),
                   jax.ShapeDtypeStruct((N_EXPERTS, LANE), I32)],
        scratch_shapes=[pltpu.VMEM((N_EXPERTS, 1), F32)],
        compiler_params=pltpu.CompilerParams(dimension_semantics=("arbitrary",),
                                             vmem_limit_bytes=VMEM_LIMIT),
        name="route",
    )(h1, wr_t, bias_col)


def _dest_body(eidx_ref, before_ref, pstart_ref, dest_ref):
    n = eidx_ref.shape[1]
    eiota = lax.broadcasted_iota(I32, (N_EXPERTS, n), 0)
    row_of = pstart_ref[...] + before_ref[...]
    for kk in range(TOP_K):
        dest = jnp.sum(jnp.where(eiota == eidx_ref[kk:kk + 1, :], row_of, 0.0), axis=0, keepdims=True)
        dest = dest.astype(I32)
        for c in range(n // SC_CHUNK):
            dest_ref[c, kk:kk + 1, :] = dest[:, c * SC_CHUNK:(c + 1) * SC_CHUNK]


def _dest(eidx, before, pstart_col):
    t = eidx.shape[1]
    kspec = pl.BlockSpec((TOP_K, ROUTE_ROWS), lambda i: (0, i))
    chunks = ROUTE_ROWS // SC_CHUNK
    return pl.pallas_call(
        _dest_body,
        grid=(t // ROUTE_ROWS,),
        in_specs=[kspec, pl.BlockSpec((N_EXPERTS, ROUTE_ROWS), lambda i: (0, i)),
                  _const_spec((N_EXPERTS, 1))],
        out_specs=pl.BlockSpec((chunks, TOP_K, SC_CHUNK), lambda i: (i, 0, 0)),
        out_shape=jax.ShapeDtypeStruct((t // SC_CHUNK, TOP_K, SC_CHUNK), I32),
        compiler_params=pltpu.CompilerParams(dimension_semantics=("arbitrary",)),
        name="dest",
    )(eidx, before, pstart_col)


def _sc_mesh():
    return plsc.VectorSubcoreMesh(core_axis_name="c", subcore_axis_name="s",
                                  num_cores=SC_CORES, num_subcores=SC_SUBCORES)


def _sc_worker_chunks(t):
    assert t % (SC_CORES * SC_SUBCORES * SC_CHUNK) == 0
    per_worker = t // (SC_CORES * SC_SUBCORES * SC_CHUNK)
    worker = lax.axis_index("s") * SC_CORES + lax.axis_index("c")
    return worker * per_worker, per_worker


def _dispatch(dest, h1p, n_rows):
    t, width = h1p.shape

    @functools.partial(
        pl.kernel, mesh=_sc_mesh(), out_type=jax.ShapeDtypeStruct((n_rows, width), U32),
        scratch_types=[pltpu.VMEM((SC_CHUNK, width), U32), pltpu.VMEM((TOP_K, SC_CHUNK), I32),
                       pltpu.SemaphoreType.DMA],
        name="dispatch")
    def body(h_hbm, dest_hbm, xs_hbm, rows_v, idx_v, sem):
        first, count = _sc_worker_chunks(t)

        @pl.loop(0, count)
        def _(i):
            chunk = first + i
            pltpu.sync_copy(h_hbm.at[pl.ds(pl.multiple_of(chunk * SC_CHUNK, SC_CHUNK), SC_CHUNK)], rows_v)
            pltpu.sync_copy(dest_hbm.at[chunk], idx_v)
            copies = [pltpu.async_copy(rows_v, xs_hbm.at[idx_v.at[kk]], sem) for kk in range(TOP_K)]
            for c in copies:
                c.wait()

    return body(h1p, dest)


def _gather(dest, ys):
    t = dest.shape[0] * SC_CHUNK
    width = ys.shape[1]

    @functools.partial(
        pl.kernel, mesh=_sc_mesh(), out_type=jax.ShapeDtypeStruct((TOP_K, t, width), U32),
        scratch_types=[pltpu.VMEM((SC_CHUNK, width), U32), pltpu.VMEM((TOP_K, SC_CHUNK), I32),
                       pltpu.SemaphoreType.DMA],
        name="gather")
    def body(ys_hbm, dest_hbm, yg_hbm, rows_v, idx_v, sem):
        first, count = _sc_worker_chunks(t)

        @pl.loop(0, count)
        def _(i):
            chunk = first + i
            pltpu.sync_copy(dest_hbm.at[chunk], idx_v)
            for kk in range(TOP_K):
                pltpu.async_copy(ys_hbm.at[idx_v.at[kk]], rows_v, sem).wait()
                pltpu.sync_copy(
                    rows_v, yg_hbm.at[kk, pl.ds(pl.multiple_of(chunk * SC_CHUNK, SC_CHUNK), SC_CHUNK)])

    return body(ys, dest)


def _experts_body(bs_ref, xs_in_ref, wg_ref, wu_ref, wd_ref, xs_ref,
                  xbuf, ybuf, wgb, wub, wdb, sem_in, sem_out):
    del xs_in_ref
    e = pl.program_id(0)
    b0 = bs_ref[e]
    b1 = bs_ref[e + 1]
    n_total = bs_ref[N_EXPERTS]
    half = D_MODEL // 2

    def fetch(b):
        slot = b % IN_SLOTS
        return pltpu.make_async_copy(xs_ref.at[pl.ds(b * MOE_BLOCK, MOE_BLOCK)], xbuf.at[slot],
                                     sem_in.at[slot])

    def store(b):
        slot = b % OUT_SLOTS
        return pltpu.make_async_copy(ybuf.at[slot], xs_ref.at[pl.ds(b * MOE_BLOCK, MOE_BLOCK)],
                                     sem_out.at[slot])

    @pl.when(e == 0)
    def _():
        for b in range(LOOKAHEAD):
            @pl.when(b < n_total)
            def _():
                fetch(b).start(priority=1)

    @pl.when(b1 > b0)
    def _():
        wgb[...] = wg_ref[0].astype(BF16)
        wub[...] = wu_ref[0].astype(BF16)
        wdb[...] = wd_ref[0].astype(BF16)

    def run(blocks):
        for b in blocks:
            @pl.when(b + LOOKAHEAD < n_total)
            def _():
                fetch(b + LOOKAHEAD).start(priority=1)
        for b in blocks:
            fetch(b).wait()
        for b in blocks:
            @pl.when(b >= OUT_SLOTS)
            def _():
                store(b - OUT_SLOTS).wait()

        xa, xb = _unpack_halves(jnp.concatenate([xbuf[b % IN_SLOTS] for b in blocks], axis=0))
        xa = xa.astype(BF16)
        xb = xb.astype(BF16)
        gate = _dot(xa, wgb[:half, :]) + _dot(xb, wgb[half:, :])
        up = _dot(xa, wub[:half, :]) + _dot(xb, wub[half:, :])
        hidden = (gate * jax.nn.sigmoid(gate) * up).astype(BF16)
        y = _dot(hidden, wdb[...])
        packed = _pack_halves(y[:, :half], y[:, half:])
        for i, b in enumerate(blocks):
            ybuf[b % OUT_SLOTS] = packed[i * MOE_BLOCK:(i + 1) * MOE_BLOCK]
            store(b).start(priority=1)

    n = b1 - b0

    def group(i, carry):
        run([b0 + GROUP * i + j for j in range(GROUP)])
        return carry

    lax.fori_loop(0, n // GROUP, group, 0)
    rest = b0 + (n // GROUP) * GROUP
    left = n % GROUP
    size = GROUP // 2
    while size >= 1:
        @pl.when(left & size != 0)
        def _():
            first = rest + (left & ~(2 * size - 1))
            run([first + j for j in range(size)])
        size //= 2

    @pl.when(e == N_EXPERTS - 1)
    def _():
        for back in range(OUT_SLOTS, 0, -1):
            @pl.when(n_total >= back)
            def _():
                store(n_total - back).wait()


def _experts(block_start, xs, w_gate, w_up, w_down):
    n_rows, width = xs.shape
    wspec = lambda shape: pl.BlockSpec((1,) + shape, lambda e, bs: (e, 0, 0))
    grid_spec = pltpu.PrefetchScalarGridSpec(
        num_scalar_prefetch=1,
        grid=(N_EXPERTS,),
        in_specs=[pl.BlockSpec(memory_space=pl.ANY),
                  wspec((D_MODEL, D_EXPERT)), wspec((D_MODEL, D_EXPERT)), wspec((D_EXPERT, D_MODEL))],
        out_specs=pl.BlockSpec(memory_space=pl.ANY),
        scratch_shapes=[pltpu.VMEM((IN_SLOTS, MOE_BLOCK, width), U32),
                        pltpu.VMEM((OUT_SLOTS, MOE_BLOCK, width), U32),
                        pltpu.VMEM((D_MODEL, D_EXPERT), BF16), pltpu.VMEM((D_MODEL, D_EXPERT), BF16),
                        pltpu.VMEM((D_EXPERT, D_MODEL), BF16),
                        pltpu.SemaphoreType.DMA((IN_SLOTS,)), pltpu.SemaphoreType.DMA((OUT_SLOTS,))],
    )
    return pl.pallas_call(
        _experts_body,
        grid_spec=grid_spec,
        out_shape=jax.ShapeDtypeStruct((n_rows, width), U32),
        input_output_aliases={1: 0},
        compiler_params=pltpu.CompilerParams(dimension_semantics=("arbitrary",),
                                             vmem_limit_bytes=VMEM_LIMIT),
        name="experts",
    )(block_start, xs, w_gate, w_up, w_down)


def _combine_body(h1_ref, w_ref, yg_ref, wsg_ref, wsu_ref, wsd_ref, g2_ref, b2_ref, *refs):
    out_ref = refs[-1]
    h1 = h1_ref[...]
    hb = h1.astype(BF16)
    gate = _dot(hb, wsg_ref[...])
    up = _dot(hb, wsu_ref[...])
    shared = _dot((gate * jax.nn.sigmoid(gate) * up).astype(BF16), wsd_ref[...])

    half = D_MODEL // 2
    lo = jnp.zeros((ROWS, half), F32)
    hi = jnp.zeros((ROWS, half), F32)
    for kk in range(TOP_K):
        a, b = _unpack_halves(yg_ref[kk])
        wk = w_ref[:, kk:kk + 1]
        lo = lo + wk * a
        hi = hi + wk * b
    z = DN_ALPHA * h1 + shared + jnp.concatenate([lo, hi], axis=1)
    out_ref[...] = _layer_norm(z, g2_ref[...], b2_ref[...])


def _combine(h1, w_rows, yg, first_tile, out_prev, wsg, wsu, wsd, ln2_g, ln2_b):
    width = yg.shape[2]
    d_shared = wsg.shape[1]
    rows = lambda cols: pl.BlockSpec((ROWS, cols), lambda i: (first_tile + i, 0))
    args = [h1, w_rows, yg, wsg, wsu, wsd, ln2_g, ln2_b]
    in_specs = [rows(D_MODEL), rows(TOP_K),
                pl.BlockSpec((TOP_K, ROWS, width), lambda i: (0, i, 0)),
                _const_spec((D_MODEL, d_shared)), _const_spec((D_MODEL, d_shared)),
                _const_spec((d_shared, D_MODEL)),
                _const_spec((1, D_MODEL)), _const_spec((1, D_MODEL))]
    aliases = {}
    if out_prev is not None:
        aliases = {len(args): 0}
        args.append(out_prev)
        in_specs.append(pl.BlockSpec(memory_space=pl.ANY))
    return pl.pallas_call(
        _combine_body,
        grid=(yg.shape[1] // ROWS,),
        in_specs=in_specs,
        out_specs=rows(D_MODEL),
        out_shape=jax.ShapeDtypeStruct(h1.shape, F32),
        input_output_aliases=aliases,
        compiler_params=pltpu.CompilerParams(dimension_semantics=("arbitrary",),
                                             vmem_limit_bytes=VMEM_LIMIT),
        name="combine",
    )(*args)


def _pack_in_proj(w_in, b_in, gate_w_f, gate_w_b, gate_b_f, gate_b_b):
    o_pool, o_q, o_k, o_v, o_g = 0, 512, 1024, 1536, 2560
    o_lr, o_gp, o_gg = 3584, 3616, 4640
    order = [(o_pool, POOL_WIDTH), (o_q, GLA_DK), (o_k, GLA_DK), (o_v, GLA_DV), (o_g, GLA_DV),
             (o_gp, D_MODEL), (o_gg, D_MODEL)] + [(o_lr, 2 * GATE_RANK)] * 3
    pad = LANE - 6 * GATE_RANK
    w = jnp.concatenate([w_in[:, o:o + n] for o, n in order] + [jnp.zeros((D_MODEL, pad), F32)], axis=1)
    b = jnp.concatenate([b_in[o:o + n] for o, n in order] + [jnp.zeros((pad,), F32)])
    w_gate = jnp.zeros((2 * GATE_RANK, 2 * GLA_DK), F32)
    w_gate = w_gate.at[:GATE_RANK, :GLA_DK].set(gate_w_f)
    w_gate = w_gate.at[GATE_RANK:, GLA_DK:].set(gate_w_b)
    w_hi = w_gate.astype(BF16)
    w_lo = (w_gate - w_hi.astype(F32)).astype(BF16)
    w_gate3 = jnp.concatenate([w_hi, w_hi, w_lo, jnp.zeros((pad, 2 * GLA_DK), BF16)], axis=0)
    b_gate = jnp.concatenate([gate_b_f, gate_b_b])[None, :]
    return w.astype(BF16), b[None, :], w_gate3, b_gate


def kernel(x, meta, ln0_g, ln0_b, w_in, b_in, pool_w, pool_scale, gate_w_f, gate_b_f, gate_w_b, gate_b_b,
           gla_norm_g, w_branch_pool, w_branch_gla, w_out, ln1_g, ln1_b, w_router, router_bias,
           w_exp_gate, w_exp_up, w_exp_down, w_sh_gate, w_sh_up, w_sh_down, ln2_g, ln2_b):
    batch, seq, d = x.shape
    assert d == D_MODEL and seq % ROWS == 0 and w_in.shape[0] == 1
    assert (batch * seq) % ROUTE_ROWS == 0
    t = batch * seq
    row = lambda a: a.reshape(1, -1).astype(F32)

    lead = jnp.concatenate([jnp.zeros((FRONT - N_META, d), x.dtype), meta.astype(x.dtype)], axis=0)
    tiles_per_batch = (seq + FRONT) // ROWS
    x2 = x.reshape(t, d)

    w_packed, b_packed, w_gate, b_gate = _pack_in_proj(
        w_in[0], b_in[0], gate_w_f[0], gate_w_b[0], gate_b_f[0], gate_b_b[0])
    u, q, k, v, sg, gp, gg, bf, cb = _in_proj(
        lead, x2, row(ln0_g), row(ln0_b), w_packed, b_packed, w_gate, b_gate, tiles_per_batch)

    o_f, o_b = _gla(q, k, v, bf, cb, batch)

    h1, h1p = _mix(
        x.reshape(t, d), u, gp, gg, sg, o_f, o_b, row(ln0_g), row(ln0_b),
        pool_w[0].astype(BF16), row(pool_scale[0]), row(jnp.tile(gla_norm_g[0], GLA_HEADS)),
        w_branch_pool[0].astype(BF16), w_branch_gla[0].astype(BF16), w_out[0].astype(BF16),
        row(ln1_g[0]), row(ln1_b[0]), seq)

    wr = w_router[0].T.astype(F32)
    wr_hi = wr.astype(BF16)
    wr_mid = (wr - wr_hi.astype(F32)).astype(BF16)
    wr_terms = jnp.concatenate([wr_hi, wr_hi, wr_mid], axis=1)
    eidx, ew, before, counts = _route(h1, wr_terms, router_bias[0].reshape(N_EXPERTS, 1))

    counts = counts[:, 0]
    padded = ((counts + MOE_BLOCK - 1) // MOE_BLOCK) * MOE_BLOCK
    pend = jnp.cumsum(padded)
    pstart = pend - padded
    dest = _dest(eidx, before, pstart.astype(F32).reshape(N_EXPERTS, 1))
    n_blocks = -(-(t * TOP_K) // MOE_BLOCK) + N_EXPERTS
    block_start = (jnp.concatenate([pstart, pend[-1:]]) // MOE_BLOCK).astype(I32)

    xs = _dispatch(dest, h1p, n_blocks * MOE_BLOCK)
    ys = _experts(block_start, xs, w_exp_gate[0], w_exp_up[0], w_exp_down[0])
    shared_w = (w_sh_gate[0].astype(BF16), w_sh_up[0].astype(BF16), w_sh_down[0].astype(BF16))
    ew_rows = ew.T
    windows = t // SC_CHUNK // COMBINE_CHUNKS
    out = None
    for c in range(COMBINE_CHUNKS):
        yg = _gather(dest[c * windows:(c + 1) * windows], ys)
        out = _combine(h1, ew_rows, yg, c * windows * SC_CHUNK // ROWS, out, *shared_w,
                       row(ln2_g[0]), row(ln2_b[0]))
    return out.reshape(batch, seq, d)
```

```python
import functools

import jax
import jax.numpy as jnp
from jax import lax
from jax.experimental import pallas as pl
from jax.experimental.pallas import tpu as pltpu
from jax.experimental.pallas import tpu_sc as plsc

F32 = jnp.float32
BF16 = jnp.bfloat16
I32 = jnp.int32
U32 = jnp.uint32

D_MODEL = 1024
N_META = 16
POOL_WINDOWS = (2, 4, 8, 16)
POOL_GROUP = 128
POOL_WIDTH = POOL_GROUP * len(POOL_WINDOWS)
GLA_HEADS = 4
GLA_DK = 512
GLA_DV = 1024
HEAD_K = GLA_DK // GLA_HEADS
HEAD_V = GLA_DV // GLA_HEADS
GATE_RANK = 16
GATE_NORMALIZER = 16.0
RMS_EPS = 1e-5
N_EXPERTS = 256
TOP_K = 8
N_GROUPS = 8
GROUP_SIZE = N_EXPERTS // N_GROUPS
TOPK_GROUPS = 4
D_EXPERT = 256
ROUTED_SCALE = 2.5
DN_ALPHA = 2.0 ** 0.25
LN_EPS = 1e-5

LANE = 128
ROWS = 256
FRONT = ROWS
GLA_CHUNK = 256
HALO = 16
ROUTE_ROWS = 1024
MOE_BLOCK = 256
LOOKAHEAD = 6
GROUP = 4
assert GROUP & (GROUP - 1) == 0
IN_SLOTS = LOOKAHEAD + GROUP
OUT_SLOTS = 2 * GROUP
SC_CORES = 2
SC_SUBCORES = 16
SC_CHUNK = 128
COMBINE_CHUNKS = 8
VMEM_LIMIT = 56 * 1024 * 1024

C_POOL, C_Q, C_K, C_V, C_G, C_GP, C_GG, C_LR = 0, 512, 1024, 1536, 2560, 3584, 4608, 5632
W_PACKED = C_LR + LANE


def _layer_norm(x, g, b):
    mu = jnp.mean(x, axis=-1, keepdims=True)
    xc = x - mu
    var = jnp.mean(xc * xc, axis=-1, keepdims=True)
    return xc * lax.rsqrt(var + LN_EPS) * g + b


def _dot(a, b):
    return jnp.dot(a, b, preferred_element_type=F32)


def _dot_nt(a, b):
    return lax.dot_general(a, b, (((1,), (1,)), ((), ())), preferred_element_type=F32)


def _split_dot(m2, g):
    g_hi = g.astype(BF16)
    g_lo = (g - g_hi.astype(F32)).astype(BF16)
    return _dot(m2, jnp.concatenate([g_hi, g_lo], axis=0))


def _pack_halves(a, b):
    pa = lax.bitcast_convert_type(a.astype(BF16).astype(F32), U32)
    pb = lax.bitcast_convert_type(b.astype(BF16).astype(F32), U32)
    return pa | (pb >> 16)


def _unpack_halves(p):
    a = lax.bitcast_convert_type(p & jnp.uint32(0xFFFF0000), F32)
    b = lax.bitcast_convert_type(p << 16, F32)
    return a, b


def _const_spec(shape):
    return pl.BlockSpec(shape, lambda *_: (0,) * len(shape))


def _inproj_body(lead_ref, x_ref, g0_ref, b0_ref, w_ref, bias_ref, wgate_ref, bgate_ref,
                 u_ref, q_ref, k_ref, v_ref, sg_ref, gp_ref, gg_ref, bf_ref, cb_ref, h_even, h_odd,
                 *, tiles_per_batch):
    i = pl.program_id(0)

    def normalised(is_lead):
        x_in = jnp.where(is_lead, lead_ref[...], x_ref[...])
        return _layer_norm(x_in, g0_ref[...], b0_ref[...]).astype(BF16)

    @pl.when(i == 0)
    def _():
        h_even[...] = normalised(True)

    for parity, (h_ref, h_next_ref) in enumerate(((h_even, h_odd), (h_odd, h_even))):
        @pl.when(i % 2 == parity)
        def _():
            h_next_ref[...] = normalised((i + 1) % tiles_per_batch == 0)
            _inproj_tile(h_ref[...], i % tiles_per_batch == 0, w_ref, bias_ref, wgate_ref, bgate_ref,
                         u_ref, q_ref, k_ref, v_ref, sg_ref, gp_ref, gg_ref, bf_ref, cb_ref)


def _inproj_tile(hb, is_lead, w_ref, bias_ref, wgate_ref, bgate_ref,
                 u_ref, q_ref, k_ref, v_ref, sg_ref, gp_ref, gg_ref, bf_ref, cb_ref):
    def proj(c0, n):
        return _dot(hb, w_ref[:, c0:c0 + n]) + bias_ref[:, c0:c0 + n]

    row = lax.broadcasted_iota(I32, (ROWS, 1), 0)
    valid = row >= jnp.where(is_lead, FRONT - N_META, 0)

    lr3 = proj(C_LR, LANE)
    g = proj(C_G, GLA_DV)
    sg_ref[...] = (g * jax.nn.sigmoid(g)).astype(BF16)
    lr_hi = lr3.astype(BF16)
    lr_lo = (lr3 - lr_hi.astype(F32)).astype(BF16)
    lane = lax.broadcasted_iota(I32, (1, LANE), 1)
    second = jnp.logical_and(lane >= 2 * GATE_RANK, lane < 4 * GATE_RANK)
    xg = _dot(jnp.where(second, lr_lo, lr_hi), wgate_ref[...]) + bgate_ref[...]
    gp_ref[...] = jax.nn.sigmoid(proj(C_GP, D_MODEL)).astype(BF16)
    gk = (jnp.minimum(xg, 0.0) - jnp.log(1.0 + jnp.exp(-jnp.abs(xg)))) * (1.0 / GATE_NORMALIZER)
    gg_ref[...] = jax.nn.sigmoid(proj(C_GG, D_MODEL)).astype(BF16)

    r = lax.broadcasted_iota(I32, (GLA_CHUNK, 2 * GLA_CHUNK), 0)
    c = lax.broadcasted_iota(I32, (GLA_CHUNK, 2 * GLA_CHUNK), 1) % GLA_CHUNK
    tril = (r >= c).astype(BF16)
    triu = (r <= c).astype(BF16)
    for ci in range(ROWS // GLA_CHUNK):
        sl = slice(ci * GLA_CHUNK, (ci + 1) * GLA_CHUNK)
        bf_ref[sl, :] = _split_dot(tril, gk[sl, :GLA_DK])
        cb_ref[sl, :] = _split_dot(triu, gk[sl, GLA_DK:])

    v_ref[...] = jnp.where(valid, proj(C_V, GLA_DV), 0.0).astype(BF16)
    k_ref[...] = jnp.where(valid, proj(C_K, GLA_DK), 0.0).astype(BF16)
    q_ref[...] = (proj(C_Q, GLA_DK) * (HEAD_K ** -0.5)).astype(BF16)
    u_ref[...] = proj(C_POOL, POOL_WIDTH).astype(BF16)


def _in_proj(lead, x2, ln0_g, ln0_b, w_packed, b_packed, w_gate, b_gate, tiles_per_batch):
    x_tiles = tiles_per_batch - FRONT // ROWS
    n = x2.shape[0] // (x_tiles * ROWS) * tiles_per_batch * ROWS
    row_spec = lambda width: pl.BlockSpec((ROWS, width), lambda i: (i, 0))
    n_steps = n // ROWS

    def next_x_tile(i):
        j = jnp.minimum(i + 1, n_steps - 1)
        return (j // tiles_per_batch) * x_tiles + jnp.maximum(j % tiles_per_batch - 1, 0), 0

    x_spec = pl.BlockSpec((ROWS, D_MODEL), next_x_tile)
    out_widths = (POOL_WIDTH, GLA_DK, GLA_DK, GLA_DV, GLA_DV, D_MODEL, D_MODEL)
    out_shape = [jax.ShapeDtypeStruct((n, w), BF16) for w in out_widths]
    out_shape += [jax.ShapeDtypeStruct((n, GLA_DK), F32)] * 2
    out_specs = [row_spec(w) for w in out_widths] + [row_spec(GLA_DK)] * 2
    return pl.pallas_call(
        functools.partial(_inproj_body, tiles_per_batch=tiles_per_batch),
        grid=(n // ROWS,),
        in_specs=[_const_spec((FRONT, D_MODEL)), x_spec,
                  _const_spec((1, D_MODEL)), _const_spec((1, D_MODEL)),
                  _const_spec((D_MODEL, W_PACKED)), _const_spec((1, W_PACKED)),
                  _const_spec((LANE, 2 * GLA_DK)), _const_spec((1, 2 * GLA_DK))],
        out_specs=out_specs,
        out_shape=out_shape,
        scratch_shapes=[pltpu.VMEM((ROWS, D_MODEL), BF16)] * 2,
        compiler_params=pltpu.CompilerParams(dimension_semantics=("arbitrary",),
                                             vmem_limit_bytes=VMEM_LIMIT),
        name="in_proj",
    )(lead, x2, ln0_g, ln0_b, w_packed, b_packed, w_gate, b_gate)


def _gla_direction(q_ref, k_ref, v_ref, b_ref, o_ref, s_ref, reverse):
    c = GLA_CHUNK
    r = lax.broadcasted_iota(I32, (c, c), 0)
    s = lax.broadcasted_iota(I32, (c, c), 1)
    keep = (r <= s) if reverse else (r >= s)
    eye = lax.broadcasted_iota(I32, (HEAD_K, HEAD_K), 0) == lax.broadcasted_iota(I32, (HEAD_K, HEAD_K), 1)
    edge = 0 if reverse else c - 1
    for h in range(GLA_HEADS):
        ks = slice(h * HEAD_K, (h + 1) * HEAD_K)
        vs = slice(h * HEAD_V, (h + 1) * HEAD_V)
        b = b_ref[:, ks]
        qh = q_ref[:, ks].astype(F32)
        kh = k_ref[:, ks].astype(F32)
        vh = v_ref[:, vs]
        b_tot = b[edge:edge + 1, :]
        b_mid = b[c // 2:c // 2 + 1, :]
        q_state = (qh * jnp.exp(b)).astype(BF16)
        q_in = (qh * jnp.exp(b - b_mid)).astype(BF16)
        k_in = (kh * jnp.exp(b_mid - b)).astype(BF16)
        k_state = (kh * jnp.exp(b_tot - b)).T.astype(BF16)
        att = jnp.where(keep, _dot_nt(q_in, k_in), 0.0).astype(BF16)
        state = s_ref[h]
        o = _dot(q_state, state.astype(BF16)) + _dot(att, vh)
        o_ref[:, vs] = o.astype(o_ref.dtype)
        decay_col = jnp.sum(jnp.where(eye, jnp.exp(b_tot), 0.0), axis=1, keepdims=True)
        s_ref[h] = decay_col * state + _dot(k_state, vh)


def _gla_body(qf_ref, kf_ref, vf_ref, bf_ref, qb_ref, kb_ref, vb_ref, cb_ref,
              of_ref, ob_ref, sf_ref, sb_ref):
    @pl.when(pl.program_id(0) == 0)
    def _():
        sf_ref[...] = jnp.zeros_like(sf_ref)
        sb_ref[...] = jnp.zeros_like(sb_ref)

    for b in range(qf_ref.shape[0]):
        _gla_direction(qf_ref.at[b], kf_ref.at[b], vf_ref.at[b], bf_ref.at[b], of_ref.at[b], sf_ref.at[b],
                       reverse=False)
        _gla_direction(qb_ref.at[b], kb_ref.at[b], vb_ref.at[b], cb_ref.at[b], ob_ref.at[b], sb_ref.at[b],
                       reverse=True)


def _gla(q, k, v, bf, cb, batch):
    n = q.shape[0]
    nch = n // batch // GLA_CHUNK
    per_batch = lambda a: a.reshape(batch, n // batch, a.shape[1])
    fwd = lambda width: pl.BlockSpec((batch, GLA_CHUNK, width), lambda i: (0, i, 0))
    bwd = lambda width: pl.BlockSpec((batch, GLA_CHUNK, width), lambda i: (0, nch - 1 - i, 0))
    state = pltpu.VMEM((batch, GLA_HEADS, HEAD_K, HEAD_V), F32)
    q, k, v, bf, cb = (per_batch(a) for a in (q, k, v, bf, cb))
    o_f, o_b = pl.pallas_call(
        _gla_body,
        grid=(nch,),
        in_specs=[fwd(GLA_DK), fwd(GLA_DK), fwd(GLA_DV), fwd(GLA_DK),
                  bwd(GLA_DK), bwd(GLA_DK), bwd(GLA_DV), bwd(GLA_DK)],
        out_specs=[fwd(GLA_DV), bwd(GLA_DV)],
        out_shape=[jax.ShapeDtypeStruct((batch, n // batch, GLA_DV), BF16)] * 2,
        scratch_shapes=[state, state],
        compiler_params=pltpu.CompilerParams(dimension_semantics=("arbitrary",),
                                             vmem_limit_bytes=VMEM_LIMIT),
        name="gla",
    )(q, k, v, bf, q, k, v, cb)
    return o_f.reshape(n, GLA_DV), o_b.reshape(n, GLA_DV)


def _mix_body(x_ref, up_ref, u_ref, un_ref, gp_ref, gg_ref, sg_ref, of_ref, ob_ref,
              g0_ref, b0_ref, pw_ref, ps_ref, gn_ref, wbp_ref, wbg_ref, wo_ref, g1_ref, b1_ref,
              h1_ref, h1p_ref, *, tiles_per_batch, seq):
    j = pl.program_id(0) % tiles_per_batch
    u_main = u_ref[...]
    u_ext = jnp.concatenate([up_ref[...], u_main, un_ref[...]], axis=0)
    r = lax.broadcasted_iota(I32, (ROWS, ROWS + 2 * HALO), 0)
    e = lax.broadcasted_iota(I32, (ROWS, ROWS + 2 * HALO), 1) - HALO
    in_seq = (j * ROWS + e) < seq
    pos = j * ROWS + lax.broadcasted_iota(I32, (ROWS, 1), 0)
    y_pool = []
    for gi, w in enumerate(POOL_WINDOWS):
        cs = slice(gi * POOL_GROUP, (gi + 1) * POOL_GROUP)
        band = jnp.logical_and(jnp.logical_and(e >= r - w // 2, e < r + w - w // 2), in_seq)
        count = w - jnp.maximum(pos + (w - w // 2) - seq, 0)
        mean = _dot(band.astype(BF16), u_ext[:, cs]) / count.astype(F32)
        d = mean - u_main[:, cs].astype(F32)
        y_pool.append(_dot(d.astype(BF16), pw_ref[gi]))
    y_pool = (jnp.concatenate(y_pool, axis=1) * ps_ref[...]).astype(BF16)

    o = of_ref[...].astype(F32) + ob_ref[...].astype(F32)
    y_gla = []
    for hd in range(GLA_HEADS):
        oh = o[:, hd * HEAD_V:(hd + 1) * HEAD_V]
        y_gla.append(oh * lax.rsqrt(jnp.mean(oh * oh, axis=-1, keepdims=True) + RMS_EPS))
    y_gla = (jnp.concatenate(y_gla, axis=1) * gn_ref[...] * sg_ref[...].astype(F32)).astype(BF16)

    merged = (gp_ref[...].astype(F32) * _dot(y_pool, wbp_ref[...])
              + gg_ref[...].astype(F32) * _dot(y_gla, wbg_ref[...]))
    y = _dot(merged.astype(BF16), wo_ref[...])
    h = _layer_norm(x_ref[...], g0_ref[...], b0_ref[...])
    h1 = _layer_norm(DN_ALPHA * h + y, g1_ref[...], b1_ref[...])
    h1_ref[...] = h1
    h1p_ref[...] = _pack_halves(h1[:, :D_MODEL // 2], h1[:, D_MODEL // 2:])


def _mix(x2, u, gp, gg, sg, o_f, o_b, ln0_g, ln0_b, pool_w, pool_scale, norm_g, wbp, wbg, wo,
         ln1_g, ln1_b, seq):
    t = x2.shape[0]
    tiles_per_batch = seq // ROWS
    padded_tiles = tiles_per_batch + FRONT // ROWS
    halo_per_tile = ROWS // HALO
    last_halo = u.shape[0] // HALO - 1

    def pidx(i):
        return (i // tiles_per_batch) * padded_tiles + FRONT // ROWS + i % tiles_per_batch

    padded = lambda width: pl.BlockSpec((ROWS, width), lambda i: (pidx(i), 0))
    plain = lambda width: pl.BlockSpec((ROWS, width), lambda i: (i, 0))
    prev_halo = pl.BlockSpec((HALO, POOL_WIDTH), lambda i: (pidx(i) * halo_per_tile - 1, 0))
    next_halo = pl.BlockSpec(
        (HALO, POOL_WIDTH), lambda i: (jnp.minimum((pidx(i) + 1) * halo_per_tile, last_halo), 0))
    return pl.pallas_call(
        functools.partial(_mix_body, tiles_per_batch=tiles_per_batch, seq=seq),
        grid=(t // ROWS,),
        in_specs=[plain(D_MODEL), prev_halo, padded(POOL_WIDTH), next_halo,
                  padded(D_MODEL), padded(D_MODEL), padded(GLA_DV), padded(GLA_DV), padded(GLA_DV),
                  _const_spec((1, D_MODEL)), _const_spec((1, D_MODEL)),
                  _const_spec((len(POOL_WINDOWS), POOL_GROUP, POOL_GROUP)), _const_spec((1, POOL_WIDTH)),
                  _const_spec((1, GLA_DV)), _const_spec((POOL_WIDTH, D_MODEL)),
                  _const_spec((GLA_DV, D_MODEL)), _const_spec((D_MODEL, D_MODEL)),
                  _const_spec((1, D_MODEL)), _const_spec((1, D_MODEL))],
        out_specs=[plain(D_MODEL), plain(D_MODEL // 2)],
        out_shape=[jax.ShapeDtypeStruct((t, D_MODEL), F32), jax.ShapeDtypeStruct((t, D_MODEL // 2), U32)],
        compiler_params=pltpu.CompilerParams(dimension_semantics=("arbitrary",),
                                             vmem_limit_bytes=VMEM_LIMIT),
        name="mix",
    )(x2, u, u, u, gp, gg, sg, o_f, o_b, ln0_g, ln0_b, pool_w, pool_scale, norm_g, wbp, wbg, wo,
      ln1_g, ln1_b)


def _first_index_of_max(val, idx, size):
    m = jnp.max(val, axis=0, keepdims=True)
    first = jnp.min(jnp.where(val == m, idx, size), axis=0, keepdims=True)
    return m, first


def _route_body(h_ref, wr_ref, bias_ref, eidx_ref, w_ref, before_ref, cnt_ref, base_ref):
    n = ROUTE_ROWS

    @pl.when(pl.program_id(0) == 0)
    def _():
        base_ref[...] = jnp.zeros_like(base_ref)

    h = h_ref[...]
    h_hi = h.astype(BF16)
    h_mid = (h - h_hi.astype(F32)).astype(BF16)
    logits = _dot_nt(wr_ref[...], jnp.concatenate([h_hi, h_mid, h_hi], axis=1))
    scores = jax.nn.sigmoid(logits)
    sel = scores + bias_ref[...]

    gidx = lax.broadcasted_iota(I32, (GROUP_SIZE, n), 0)
    groups = [sel[g * GROUP_SIZE:(g + 1) * GROUP_SIZE, :] for g in range(N_GROUPS)]
    gscore = []
    for grp in groups:
        m1, first = _first_index_of_max(grp, gidx, GROUP_SIZE)
        m2 = jnp.max(jnp.where(gidx == first, -jnp.inf, grp), axis=0, keepdims=True)
        gscore.append(m1 + m2)

    masked = []
    for gi in range(N_GROUPS):
        beaten = jnp.zeros((1, n), F32)
        for gj in range(N_GROUPS):
            if gj != gi:
                wins = (gscore[gj] >= gscore[gi]) if gj < gi else (gscore[gj] > gscore[gi])
                beaten = beaten + wins.astype(F32)
        masked.append(jnp.where(beaten < TOPK_GROUPS, groups[gi], -jnp.inf))
    val = jnp.concatenate(masked, axis=0)

    eidx = lax.broadcasted_iota(I32, (N_EXPERTS, n), 0)
    candidate = val != -jnp.inf
    picks, weights = [], []
    for _ in range(TOP_K):
        _, first = _first_index_of_max(val, eidx, N_EXPERTS)
        hit = eidx == first
        picks.append(first)
        weights.append(jnp.sum(jnp.where(hit, scores, 0.0), axis=0, keepdims=True))
        val = jnp.where(hit, -jnp.inf, val)
    chosen_f = jnp.where(jnp.logical_and(candidate, val == -jnp.inf), 1.0, 0.0)
    wsum = weights[0]
    for wk in weights[1:]:
        wsum = wsum + wk
    scale = ROUTED_SCALE / wsum

    r = lax.broadcasted_iota(I32, (n, n), 0)
    c = lax.broadcasted_iota(I32, (n, n), 1)
    earlier = (r < c).astype(BF16)
    before_ref[...] = _dot(chosen_f.astype(BF16), earlier) + base_ref[...]
    base_ref[...] = base_ref[...] + jnp.sum(chosen_f, axis=1, keepdims=True)
    cnt_ref[...] = jnp.broadcast_to(base_ref[...], cnt_ref.shape).astype(I32)

    for kk in range(TOP_K):
        eidx_ref[kk:kk + 1, :] = picks[kk]
        w_ref[kk:kk + 1, :] = weights[kk] * scale


def _route(h1, wr_t, bias_col):
    t = h1.shape[0]
    col = lambda dtype: jax.ShapeDtypeStruct((TOP_K, t), dtype)
    kspec = pl.BlockSpec((TOP_K, ROUTE_ROWS), lambda i: (0, i))
    return pl.pallas_call(
        _route_body,
        grid=(t // ROUTE_ROWS,),
        in_specs=[pl.BlockSpec((ROUTE_ROWS, D_MODEL), lambda i: (i, 0)),
                  _const_spec((N_EXPERTS, 3 * D_MODEL)), _const_spec((N_EXPERTS, 1))],
        out_specs=[kspec, kspec, pl.BlockSpec((N_EXPERTS, ROUTE_ROWS), lambda i: (0, i)),
                   _const_spec((N_EXPERTS, LANE))],
        out_shape=[col(I32), col(F32), jax.ShapeDtypeStruct((N_EXPERTS, t), F32),
                   jax.ShapeDtypeStruct((N_EXPERTS, LANE), I32)],
        scratch_shapes=[pltpu.VMEM((N_EXPERTS, 1), F32)],
        compiler_params=pltpu.CompilerParams(dimension_semantics=("arbitrary",),
                                             vmem_limit_bytes=VMEM_LIMIT),
        name="route",
    )(h1, wr_t, bias_col)


def _dest_body(eidx_ref, before_ref, pstart_ref, dest_ref):
    n = eidx_ref.shape[1]
    eiota = lax.broadcasted_iota(I32, (N_EXPERTS, n), 0)
    row_of = pstart_ref[...] + before_ref[...]
    for kk in range(TOP_K):
        dest = jnp.sum(jnp.where(eiota == eidx_ref[kk:kk + 1, :], row_of, 0.0), axis=0, keepdims=True)
        dest = dest.astype(I32)
        for c in range(n // SC_CHUNK):
            dest_ref[c, kk:kk + 1, :] = dest[:, c * SC_CHUNK:(c + 1) * SC_CHUNK]


def _dest(eidx, before, pstart_col):
    t = eidx.shape[1]
    kspec = pl.BlockSpec((TOP_K, ROUTE_ROWS), lambda i: (0, i))
    chunks = ROUTE_ROWS // SC_CHUNK
    return pl.pallas_call(
        _dest_body,
        grid=(t // ROUTE_ROWS,),
        in_specs=[kspec, pl.BlockSpec((N_EXPERTS, ROUTE_ROWS), lambda i: (0, i)),
                  _const_spec((N_EXPERTS, 1))],
        out_specs=pl.BlockSpec((chunks, TOP_K, SC_CHUNK), lambda i: (i, 0, 0)),
        out_shape=jax.ShapeDtypeStruct((t // SC_CHUNK, TOP_K, SC_CHUNK), I32),
        compiler_params=pltpu.CompilerParams(dimension_semantics=("arbitrary",)),
        name="dest",
    )(eidx, before, pstart_col)


def _sc_mesh():
    return plsc.VectorSubcoreMesh(core_axis_name="c", subcore_axis_name="s",
                                  num_cores=SC_CORES, num_subcores=SC_SUBCORES)


def _sc_worker_chunks(t):
    assert t % (SC_CORES * SC_SUBCORES * SC_CHUNK) == 0
    per_worker = t // (SC_CORES * SC_SUBCORES * SC_CHUNK)
    worker = lax.axis_index("s") * SC_CORES + lax.axis_index("c")
    return worker * per_worker, per_worker


def _dispatch(dest, h1p, n_rows):
    t, width = h1p.shape

    @functools.partial(
        pl.kernel, mesh=_sc_mesh(), out_type=jax.ShapeDtypeStruct((n_rows, width), U32),
        scratch_types=[pltpu.VMEM((SC_CHUNK, width), U32), pltpu.VMEM((TOP_K, SC_CHUNK), I32),
                       pltpu.SemaphoreType.DMA],
        name="dispatch")
    def body(h_hbm, dest_hbm, xs_hbm, rows_v, idx_v, sem):
        first, count = _sc_worker_chunks(t)

        @pl.loop(0, count)
        def _(i):
            chunk = first + i
            pltpu.sync_copy(h_hbm.at[pl.ds(pl.multiple_of(chunk * SC_CHUNK, SC_CHUNK), SC_CHUNK)], rows_v)
            pltpu.sync_copy(dest_hbm.at[chunk], idx_v)
            copies = [pltpu.async_copy(rows_v, xs_hbm.at[idx_v.at[kk]], sem) for kk in range(TOP_K)]
            for c in copies:
                c.wait()

    return body(h1p, dest)


def _gather(dest, ys):
    t = dest.shape[0] * SC_CHUNK
    width = ys.shape[1]

    @functools.partial(
        pl.kernel, mesh=_sc_mesh(), out_type=jax.ShapeDtypeStruct((TOP_K, t, width), U32),
        scratch_types=[pltpu.VMEM((SC_CHUNK, width), U32), pltpu.VMEM((TOP_K, SC_CHUNK), I32),
                       pltpu.SemaphoreType.DMA],
        name="gather")
    def body(ys_hbm, dest_hbm, yg_hbm, rows_v, idx_v, sem):
        first, count = _sc_worker_chunks(t)

        @pl.loop(0, count)
        def _(i):
            chunk = first + i
            pltpu.sync_copy(dest_hbm.at[chunk], idx_v)
            for kk in range(TOP_K):
                pltpu.async_copy(ys_hbm.at[idx_v.at[kk]], rows_v, sem).wait()
                pltpu.sync_copy(
                    rows_v, yg_hbm.at[kk, pl.ds(pl.multiple_of(chunk * SC_CHUNK, SC_CHUNK), SC_CHUNK)])

    return body(ys, dest)


def _experts_body(bs_ref, xs_in_ref, wg_ref, wu_ref, wd_ref, xs_ref,
                  xbuf, ybuf, wgb, wub, wdb, sem_in, sem_out):
    del xs_in_ref
    e = pl.program_id(0)
    b0 = bs_ref[e]
    b1 = bs_ref[e + 1]
    n_total = bs_ref[N_EXPERTS]
    half = D_MODEL // 2

    def fetch(b):
        slot = b % IN_SLOTS
        return pltpu.make_async_copy(xs_ref.at[pl.ds(b * MOE_BLOCK, MOE_BLOCK)], xbuf.at[slot],
                                     sem_in.at[slot])

    def store(b):
        slot = b % OUT_SLOTS
        return pltpu.make_async_copy(ybuf.at[slot], xs_ref.at[pl.ds(b * MOE_BLOCK, MOE_BLOCK)],
                                     sem_out.at[slot])

    @pl.when(e == 0)
    def _():
        for b in range(LOOKAHEAD):
            @pl.when(b < n_total)
            def _():
                fetch(b).start(priority=1)

    @pl.when(b1 > b0)
    def _():
        wgb[...] = wg_ref[0].astype(BF16)
        wub[...] = wu_ref[0].astype(BF16)
        wdb[...] = wd_ref[0].astype(BF16)

    def run(blocks):
        for b in blocks:
            @pl.when(b + LOOKAHEAD < n_total)
            def _():
                fetch(b + LOOKAHEAD).start(priority=1)
        for b in blocks:
            fetch(b).wait()
        for b in blocks:
            @pl.when(b >= OUT_SLOTS)
            def _():
                store(b - OUT_SLOTS).wait()

        xa, xb = _unpack_halves(jnp.concatenate([xbuf[b % IN_SLOTS] for b in blocks], axis=0))
        xa = xa.astype(BF16)
        xb = xb.astype(BF16)
        gate = _dot(xa, wgb[:half, :]) + _dot(xb, wgb[half:, :])
        up = _dot(xa, wub[:half, :]) + _dot(xb, wub[half:, :])
        hidden = (gate * jax.nn.sigmoid(gate) * up).astype(BF16)
        y = _dot(hidden, wdb[...])
        packed = _pack_halves(y[:, :half], y[:, half:])
        for i, b in enumerate(blocks):
            ybuf[b % OUT_SLOTS] = packed[i * MOE_BLOCK:(i + 1) * MOE_BLOCK]
            store(b).start(priority=1)

    n = b1 - b0

    def group(i, carry):
        run([b0 + GROUP * i + j for j in range(GROUP)])
        return carry

    lax.fori_loop(0, n // GROUP, group, 0)
    rest = b0 + (n // GROUP) * GROUP
    left = n % GROUP
    size = GROUP // 2
    while size >= 1:
        @pl.when(left & size != 0)
        def _():
            first = rest + (left & ~(2 * size - 1))
            run([first + j for j in range(size)])
        size //= 2

    @pl.when(e == N_EXPERTS - 1)
    def _():
        for back in range(OUT_SLOTS, 0, -1):
            @pl.when(n_total >= back)
            def _():
                store(n_total - back).wait()


def _experts(block_start, xs, w_gate, w_up, w_down):
    n_rows, width = xs.shape
    wspec = lambda shape: pl.BlockSpec((1,) + shape, lambda e, bs: (e, 0, 0))
    grid_spec = pltpu.PrefetchScalarGridSpec(
        num_scalar_prefetch=1,
        grid=(N_EXPERTS,),
        in_specs=[pl.BlockSpec(memory_space=pl.ANY),
                  wspec((D_MODEL, D_EXPERT)), wspec((D_MODEL, D_EXPERT)), wspec((D_EXPERT, D_MODEL))],
        out_specs=pl.BlockSpec(memory_space=pl.ANY),
        scratch_shapes=[pltpu.VMEM((IN_SLOTS, MOE_BLOCK, width), U32),
                        pltpu.VMEM((OUT_SLOTS, MOE_BLOCK, width), U32),
                        pltpu.VMEM((D_MODEL, D_EXPERT), BF16), pltpu.VMEM((D_MODEL, D_EXPERT), BF16),
                        pltpu.VMEM((D_EXPERT, D_MODEL), BF16),
                        pltpu.SemaphoreType.DMA((IN_SLOTS,)), pltpu.SemaphoreType.DMA((OUT_SLOTS,))],
    )
    return pl.pallas_call(
        _experts_body,
        grid_spec=grid_spec,
        out_shape=jax.ShapeDtypeStruct((n_rows, width), U32),
        input_output_aliases={1: 0},
        compiler_params=pltpu.CompilerParams(dimension_semantics=("arbitrary",),
                                             vmem_limit_bytes=VMEM_LIMIT),
        name="experts",
    )(block_start, xs, w_gate, w_up, w_down)


def _combine_body(h1_ref, w_ref, yg_ref, wsg_ref, wsu_ref, wsd_ref, g2_ref, b2_ref, *refs):
    out_ref = refs[-1]
    h1 = h1_ref[...]
    hb = h1.astype(BF16)
    gate = _dot(hb, wsg_ref[...])
    up = _dot(hb, wsu_ref[...])
    shared = _dot((gate * jax.nn.sigmoid(gate) * up).astype(BF16), wsd_ref[...])

    half = D_MODEL // 2
    w_cols = w_ref[...].T
    lo = jnp.zeros((ROWS, half), F32)
    hi = jnp.zeros((ROWS, half), F32)
    for kk in range(TOP_K):
        a, b = _unpack_halves(yg_ref[kk])
        wk = w_cols[:, kk:kk + 1]
        lo = lo + wk * a
        hi = hi + wk * b
    z = DN_ALPHA * h1 + shared + jnp.concatenate([lo, hi], axis=1)
    out_ref[...] = _layer_norm(z, g2_ref[...], b2_ref[...])


def _combine(h1, w_rows, yg, first_tile, out_prev, wsg, wsu, wsd, ln2_g, ln2_b):
    width = yg.shape[2]
    d_shared = wsg.shape[1]
    rows = lambda cols: pl.BlockSpec((ROWS, cols), lambda i: (first_tile + i, 0))
    args = [h1, w_rows, yg, wsg, wsu, wsd, ln2_g, ln2_b]
    in_specs = [rows(D_MODEL), pl.BlockSpec((TOP_K, ROWS), lambda i: (0, first_tile + i)),
                pl.BlockSpec((TOP_K, ROWS, width), lambda i: (0, i, 0)),
                _const_spec((D_MODEL, d_shared)), _const_spec((D_MODEL, d_shared)),
                _const_spec((d_shared, D_MODEL)),
                _const_spec((1, D_MODEL)), _const_spec((1, D_MODEL))]
    aliases = {}
    if out_prev is not None:
        aliases = {len(args): 0}
        args.append(out_prev)
        in_specs.append(pl.BlockSpec(memory_space=pl.ANY))
    return pl.pallas_call(
        _combine_body,
        grid=(yg.shape[1] // ROWS,),
        in_specs=in_specs,
        out_specs=rows(D_MODEL),
        out_shape=jax.ShapeDtypeStruct(h1.shape, F32),
        input_output_aliases=aliases,
        compiler_params=pltpu.CompilerParams(dimension_semantics=("arbitrary",),
                                             vmem_limit_bytes=VMEM_LIMIT),
        name="combine",
    )(*args)


def _pack_in_proj(w_in, b_in, gate_w_f, gate_w_b, gate_b_f, gate_b_b):
    o_pool, o_q, o_k, o_v, o_g = 0, 512, 1024, 1536, 2560
    o_lr, o_gp, o_gg = 3584, 3616, 4640
    order = [(o_pool, POOL_WIDTH), (o_q, GLA_DK), (o_k, GLA_DK), (o_v, GLA_DV), (o_g, GLA_DV),
             (o_gp, D_MODEL), (o_gg, D_MODEL)] + [(o_lr, 2 * GATE_RANK)] * 3
    pad = LANE - 6 * GATE_RANK
    w = jnp.concatenate([w_in[:, o:o + n] for o, n in order] + [jnp.zeros((D_MODEL, pad), F32)], axis=1)
    b = jnp.concatenate([b_in[o:o + n] for o, n in order] + [jnp.zeros((pad,), F32)])
    w_gate = jnp.zeros((2 * GATE_RANK, 2 * GLA_DK), F32)
    w_gate = w_gate.at[:GATE_RANK, :GLA_DK].set(gate_w_f)
    w_gate = w_gate.at[GATE_RANK:, GLA_DK:].set(gate_w_b)
    w_hi = w_gate.astype(BF16)
    w_lo = (w_gate - w_hi.astype(F32)).astype(BF16)
    w_gate3 = jnp.concatenate([w_hi, w_hi, w_lo, jnp.zeros((pad, 2 * GLA_DK), BF16)], axis=0)
    b_gate = jnp.concatenate([gate_b_f, gate_b_b])[None, :]
    return w.astype(BF16), b[None, :], w_gate3, b_gate


def kernel(x, meta, ln0_g, ln0_b, w_in, b_in, pool_w, pool_scale, gate_w_f, gate_b_f, gate_w_b, gate_b_b,
           gla_norm_g, w_branch_pool, w_branch_gla, w_out, ln1_g, ln1_b, w_router, router_bias,
           w_exp_gate, w_exp_up, w_exp_down, w_sh_gate, w_sh_up, w_sh_down, ln2_g, ln2_b):
    batch, seq, d = x.shape
    assert d == D_MODEL and seq % ROWS == 0 and w_in.shape[0] == 1
    assert (batch * seq) % ROUTE_ROWS == 0
    t = batch * seq
    row = lambda a: a.reshape(1, -1).astype(F32)

    lead = jnp.concatenate([jnp.zeros((FRONT - N_META, d), x.dtype), meta.astype(x.dtype)], axis=0)
    tiles_per_batch = (seq + FRONT) // ROWS
    x2 = x.reshape(t, d)

    w_packed, b_packed, w_gate, b_gate = _pack_in_proj(
        w_in[0], b_in[0], gate_w_f[0], gate_w_b[0], gate_b_f[0], gate_b_b[0])
    u, q, k, v, sg, gp, gg, bf, cb = _in_proj(
        lead, x2, row(ln0_g), row(ln0_b), w_packed, b_packed, w_gate, b_gate, tiles_per_batch)

    o_f, o_b = _gla(q, k, v, bf, cb, batch)

    h1, h1p = _mix(
        x.reshape(t, d), u, gp, gg, sg, o_f, o_b, row(ln0_g), row(ln0_b),
        pool_w[0].astype(BF16), row(pool_scale[0]), row(jnp.tile(gla_norm_g[0], GLA_HEADS)),
        w_branch_pool[0].astype(BF16), w_branch_gla[0].astype(BF16), w_out[0].astype(BF16),
        row(ln1_g[0]), row(ln1_b[0]), seq)

    wr = w_router[0].T.astype(F32)
    wr_hi = wr.astype(BF16)
    wr_mid = (wr - wr_hi.astype(F32)).astype(BF16)
    wr_terms = jnp.concatenate([wr_hi, wr_hi, wr_mid], axis=1)
    eidx, ew, before, counts = _route(h1, wr_terms, router_bias[0].reshape(N_EXPERTS, 1))

    counts = counts[:, 0]
    padded = ((counts + MOE_BLOCK - 1) // MOE_BLOCK) * MOE_BLOCK
    pend = jnp.cumsum(padded)
    pstart = pend - padded
    dest = _dest(eidx, before, pstart.astype(F32).reshape(N_EXPERTS, 1))
    n_blocks = -(-(t * TOP_K) // MOE_BLOCK) + N_EXPERTS
    block_start = (jnp.concatenate([pstart, pend[-1:]]) // MOE_BLOCK).astype(I32)

    xs = _dispatch(dest, h1p, n_blocks * MOE_BLOCK)
    ys = _experts(block_start, xs, w_exp_gate[0], w_exp_up[0], w_exp_down[0])
    shared_w = (w_sh_gate[0].astype(BF16), w_sh_up[0].astype(BF16), w_sh_down[0].astype(BF16))
    ew_rows = ew
    windows = t // SC_CHUNK // COMBINE_CHUNKS
    out = None
    for c in range(COMBINE_CHUNKS):
        yg = _gather(dest[c * windows:(c + 1) * windows], ys)
        out = _combine(h1, ew_rows, yg, c * windows * SC_CHUNK // ROWS, out, *shared_w,
                       row(ln2_g[0]), row(ln2_b[0]))
    return out.reshape(batch, seq, d)
```

```python
import functools

import jax
import jax.numpy as jnp
from jax import lax
from jax.experimental import pallas as pl
from jax.experimental.pallas import tpu as pltpu
from jax.experimental.pallas import tpu_sc as plsc

F32 = jnp.float32
BF16 = jnp.bfloat16
I32 = jnp.int32
U32 = jnp.uint32

D_MODEL = 1024
N_META = 16
POOL_WINDOWS = (2, 4, 8, 16)
POOL_GROUP = 128
POOL_WIDTH = POOL_GROUP * len(POOL_WINDOWS)
GLA_HEADS = 4
GLA_DK = 512
GLA_DV = 1024
HEAD_K = GLA_DK // GLA_HEADS
HEAD_V = GLA_DV // GLA_HEADS
GATE_RANK = 16
GATE_NORMALIZER = 16.0
RMS_EPS = 1e-5
N_EXPERTS = 256
TOP_K = 8
N_GROUPS = 8
GROUP_SIZE = N_EXPERTS // N_GROUPS
TOPK_GROUPS = 4
D_EXPERT = 256
ROUTED_SCALE = 2.5
DN_ALPHA = 2.0 ** 0.25
LN_EPS = 1e-5

LANE = 128
ROWS = 256
FRONT = ROWS
GLA_CHUNK = 256
GLA_BLOCKS = 2
HALO = 16
ROUTE_ROWS = 1024
MOE_BLOCK = 256
LOOKAHEAD = 6
GROUP = 4
assert GROUP & (GROUP - 1) == 0
IN_SLOTS = LOOKAHEAD + GROUP
OUT_SLOTS = 2 * GROUP
SC_CORES = 2
SC_SUBCORES = 16
SC_CHUNK = 128
COMBINE_CHUNKS = 8
VMEM_LIMIT = 56 * 1024 * 1024

C_POOL, C_Q, C_K, C_V, C_G, C_GP, C_GG, C_LR = 0, 512, 1024, 1536, 2560, 3584, 4608, 5632
W_PACKED = C_LR + LANE


def _layer_norm(x, g, b):
    mu = jnp.mean(x, axis=-1, keepdims=True)
    xc = x - mu
    var = jnp.mean(xc * xc, axis=-1, keepdims=True)
    return xc * lax.rsqrt(var + LN_EPS) * g + b


def _dot(a, b):
    return jnp.dot(a, b, preferred_element_type=F32)


def _dot_nt(a, b):
    return lax.dot_general(a, b, (((1,), (1,)), ((), ())), preferred_element_type=F32)


def _split_dot(m2, g):
    g_hi = g.astype(BF16)
    g_lo = (g - g_hi.astype(F32)).astype(BF16)
    return _dot(m2, jnp.concatenate([g_hi, g_lo], axis=0))


def _pack_halves(a, b):
    pa = lax.bitcast_convert_type(a.astype(BF16).astype(F32), U32)
    pb = lax.bitcast_convert_type(b.astype(BF16).astype(F32), U32)
    return pa | (pb >> 16)


def _unpack_halves(p):
    a = lax.bitcast_convert_type(p & jnp.uint32(0xFFFF0000), F32)
    b = lax.bitcast_convert_type(p << 16, F32)
    return a, b


def _const_spec(shape):
    return pl.BlockSpec(shape, lambda *_: (0,) * len(shape))


def _inproj_body(lead_ref, x_ref, g0_ref, b0_ref, w_ref, bias_ref, wgate_ref, bgate_ref,
                 u_ref, q_ref, k_ref, v_ref, sg_ref, gp_ref, gg_ref, bf_ref, cb_ref, h_even, h_odd,
                 *, tiles_per_batch):
    i = pl.program_id(0)

    def normalised(is_lead):
        x_in = jnp.where(is_lead, lead_ref[...], x_ref[...])
        return _layer_norm(x_in, g0_ref[...], b0_ref[...]).astype(BF16)

    @pl.when(i == 0)
    def _():
        h_even[...] = normalised(True)

    for parity, (h_ref, h_next_ref) in enumerate(((h_even, h_odd), (h_odd, h_even))):
        @pl.when(i % 2 == parity)
        def _():
            h_next_ref[...] = normalised((i + 1) % tiles_per_batch == 0)
            _inproj_tile(h_ref[...], i % tiles_per_batch == 0, w_ref, bias_ref, wgate_ref, bgate_ref,
                         u_ref, q_ref, k_ref, v_ref, sg_ref, gp_ref, gg_ref, bf_ref, cb_ref)


def _inproj_tile(hb, is_lead, w_ref, bias_ref, wgate_ref, bgate_ref,
                 u_ref, q_ref, k_ref, v_ref, sg_ref, gp_ref, gg_ref, bf_ref, cb_ref):
    def proj(c0, n):
        return _dot(hb, w_ref[:, c0:c0 + n]) + bias_ref[:, c0:c0 + n]

    row = lax.broadcasted_iota(I32, (ROWS, 1), 0)
    valid = row >= jnp.where(is_lead, FRONT - N_META, 0)

    lr3 = proj(C_LR, LANE)
    g = proj(C_G, GLA_DV)
    sg_ref[...] = (g * jax.nn.sigmoid(g)).astype(BF16)
    lr_hi = lr3.astype(BF16)
    lr_lo = (lr3 - lr_hi.astype(F32)).astype(BF16)
    lane = lax.broadcasted_iota(I32, (1, LANE), 1)
    second = jnp.logical_and(lane >= 2 * GATE_RANK, lane < 4 * GATE_RANK)
    xg = _dot(jnp.where(second, lr_lo, lr_hi), wgate_ref[...]) + bgate_ref[...]
    gp_ref[...] = jax.nn.sigmoid(proj(C_GP, D_MODEL)).astype(BF16)
    gk = (jnp.minimum(xg, 0.0) - jnp.log(1.0 + jnp.exp(-jnp.abs(xg)))) * (1.0 / GATE_NORMALIZER)
    gg_ref[...] = jax.nn.sigmoid(proj(C_GG, D_MODEL)).astype(BF16)

    r = lax.broadcasted_iota(I32, (GLA_CHUNK, 2 * GLA_CHUNK), 0)
    c = lax.broadcasted_iota(I32, (GLA_CHUNK, 2 * GLA_CHUNK), 1) % GLA_CHUNK
    tril = (r >= c).astype(BF16)
    triu = (r <= c).astype(BF16)
    for ci in range(ROWS // GLA_CHUNK):
        sl = slice(ci * GLA_CHUNK, (ci + 1) * GLA_CHUNK)
        bf_ref[sl, :] = _split_dot(tril, gk[sl, :GLA_DK])
        cb_ref[sl, :] = _split_dot(triu, gk[sl, GLA_DK:])

    v_ref[...] = jnp.where(valid, proj(C_V, GLA_DV), 0.0).astype(BF16)
    k_ref[...] = jnp.where(valid, proj(C_K, GLA_DK), 0.0).astype(BF16)
    q_ref[...] = (proj(C_Q, GLA_DK) * (HEAD_K ** -0.5)).astype(BF16)
    u_ref[...] = proj(C_POOL, POOL_WIDTH).astype(BF16)


def _in_proj(lead, x2, ln0_g, ln0_b, w_packed, b_packed, w_gate, b_gate, tiles_per_batch):
    x_tiles = tiles_per_batch - FRONT // ROWS
    n = x2.shape[0] // (x_tiles * ROWS) * tiles_per_batch * ROWS
    row_spec = lambda width: pl.BlockSpec((ROWS, width), lambda i: (i, 0))
    n_steps = n // ROWS

    def next_x_tile(i):
        j = jnp.minimum(i + 1, n_steps - 1)
        return (j // tiles_per_batch) * x_tiles + jnp.maximum(j % tiles_per_batch - 1, 0), 0

    x_spec = pl.BlockSpec((ROWS, D_MODEL), next_x_tile)
    out_widths = (POOL_WIDTH, GLA_DK, GLA_DK, GLA_DV, GLA_DV, D_MODEL, D_MODEL)
    out_shape = [jax.ShapeDtypeStruct((n, w), BF16) for w in out_widths]
    out_shape += [jax.ShapeDtypeStruct((n, GLA_DK), F32)] * 2
    out_specs = [row_spec(w) for w in out_widths] + [row_spec(GLA_DK)] * 2
    return pl.pallas_call(
        functools.partial(_inproj_body, tiles_per_batch=tiles_per_batch),
        grid=(n // ROWS,),
        in_specs=[_const_spec((FRONT, D_MODEL)), x_spec,
                  _const_spec((1, D_MODEL)), _const_spec((1, D_MODEL)),
                  _const_spec((D_MODEL, W_PACKED)), _const_spec((1, W_PACKED)),
                  _const_spec((LANE, 2 * GLA_DK)), _const_spec((1, 2 * GLA_DK))],
        out_specs=out_specs,
        out_shape=out_shape,
        scratch_shapes=[pltpu.VMEM((ROWS, D_MODEL), BF16)] * 2,
        compiler_params=pltpu.CompilerParams(dimension_semantics=("arbitrary",),
                                             vmem_limit_bytes=VMEM_LIMIT),
        name="in_proj",
    )(lead, x2, ln0_g, ln0_b, w_packed, b_packed, w_gate, b_gate)


def _gla_direction(q_ref, k_ref, v_ref, b_ref, o_ref, s_ref, reverse):
    c = GLA_CHUNK
    m = c // GLA_BLOCKS
    r = lax.broadcasted_iota(I32, (m, m), 0)
    s = lax.broadcasted_iota(I32, (m, m), 1)
    keep = (r <= s) if reverse else (r >= s)
    block_of_row = lax.broadcasted_iota(I32, (c, 1), 0) // m
    eye = lax.broadcasted_iota(I32, (HEAD_K, HEAD_K), 0) == lax.broadcasted_iota(I32, (HEAD_K, HEAD_K), 1)
    edge = 0 if reverse else c - 1
    for h in range(GLA_HEADS):
        ks = slice(h * HEAD_K, (h + 1) * HEAD_K)
        vs = slice(h * HEAD_V, (h + 1) * HEAD_V)
        b = b_ref[:, ks]
        qh = q_ref[:, ks].astype(F32)
        kh = k_ref[:, ks].astype(F32)
        vh = v_ref[:, vs]
        b_tot = b[edge:edge + 1, :]
        q_state = (qh * jnp.exp(b)).astype(BF16)
        b_mid = b[m // 2:m // 2 + 1, :]
        for blk in range(1, GLA_BLOCKS):
            b_mid = jnp.where(block_of_row >= blk, b[blk * m + m // 2:blk * m + m // 2 + 1, :], b_mid)
        q_in = (qh * jnp.exp(b - b_mid)).astype(BF16)
        k_in = (kh * jnp.exp(b_mid - b)).astype(BF16)
        att_rows = []
        for blk in range(GLA_BLOCKS):
            rows = slice(blk * m, (blk + 1) * m)
            diag = jnp.where(keep, _dot_nt(q_in[rows], k_in[rows]), 0.0).astype(BF16)
            if reverse:
                earlier, border = slice((blk + 1) * m, c), (blk + 1) * m
            else:
                earlier, border = slice(0, blk * m), blk * m - 1
            n_earlier = earlier.stop - earlier.start
            off = None
            if n_earlier:
                b_edge = b[border:border + 1, :]
                q_off = (qh[rows] * jnp.exp(b[rows] - b_edge)).astype(BF16)
                k_off = (kh[earlier] * jnp.exp(b_edge - b[earlier])).astype(BF16)
                off = _dot_nt(q_off, k_off).astype(BF16)
            n_later = c - m - n_earlier
            masked = jnp.zeros((m, n_later), BF16) if n_later else None
            row = (masked, diag, off) if reverse else (off, diag, masked)
            att_rows.append(jnp.concatenate([p for p in row if p is not None], axis=1))
        att = jnp.concatenate(att_rows, axis=0)
        k_state = (kh * jnp.exp(b_tot - b)).T.astype(BF16)
        state = s_ref[h]
        o = _dot(q_state, state.astype(BF16)) + _dot(att, vh)
        o_ref[:, vs] = o.astype(o_ref.dtype)
        decay_col = jnp.sum(jnp.where(eye, jnp.exp(b_tot), 0.0), axis=1, keepdims=True)
        s_ref[h] = decay_col * state + _dot(k_state, vh)


def _gla_body(qf_ref, kf_ref, vf_ref, bf_ref, qb_ref, kb_ref, vb_ref, cb_ref,
              of_ref, ob_ref, sf_ref, sb_ref):
    @pl.when(pl.program_id(0) == 0)
    def _():
        sf_ref[...] = jnp.zeros_like(sf_ref)
        sb_ref[...] = jnp.zeros_like(sb_ref)

    for b in range(qf_ref.shape[0]):
        _gla_direction(qf_ref.at[b], kf_ref.at[b], vf_ref.at[b], bf_ref.at[b], of_ref.at[b], sf_ref.at[b],
                       reverse=False)
        _gla_direction(qb_ref.at[b], kb_ref.at[b], vb_ref.at[b], cb_ref.at[b], ob_ref.at[b], sb_ref.at[b],
                       reverse=True)


def _gla(q, k, v, bf, cb, batch):
    n = q.shape[0]
    nch = n // batch // GLA_CHUNK
    per_batch = lambda a: a.reshape(batch, n // batch, a.shape[1])
    fwd = lambda width: pl.BlockSpec((batch, GLA_CHUNK, width), lambda i: (0, i, 0))
    bwd = lambda width: pl.BlockSpec((batch, GLA_CHUNK, width), lambda i: (0, nch - 1 - i, 0))
    state = pltpu.VMEM((batch, GLA_HEADS, HEAD_K, HEAD_V), F32)
    q, k, v, bf, cb = (per_batch(a) for a in (q, k, v, bf, cb))
    o_f, o_b = pl.pallas_call(
        _gla_body,
        grid=(nch,),
        in_specs=[fwd(GLA_DK), fwd(GLA_DK), fwd(GLA_DV), fwd(GLA_DK),
                  bwd(GLA_DK), bwd(GLA_DK), bwd(GLA_DV), bwd(GLA_DK)],
        out_specs=[fwd(GLA_DV), bwd(GLA_DV)],
        out_shape=[jax.ShapeDtypeStruct((batch, n // batch, GLA_DV), BF16)] * 2,
        scratch_shapes=[state, state],
        compiler_params=pltpu.CompilerParams(dimension_semantics=("arbitrary",),
                                             vmem_limit_bytes=VMEM_LIMIT),
        name="gla",
    )(q, k, v, bf, q, k, v, cb)
    return o_f.reshape(n, GLA_DV), o_b.reshape(n, GLA_DV)


def _mix_body(x_ref, up_ref, u_ref, un_ref, gp_ref, gg_ref, sg_ref, of_ref, ob_ref,
              g0_ref, b0_ref, pw_ref, ps_ref, gn_ref, wbp_ref, wbg_ref, wo_ref, g1_ref, b1_ref,
              h1_ref, h1p_ref, *, tiles_per_batch, seq):
    j = pl.program_id(0) % tiles_per_batch
    u_main = u_ref[...]
    u_ext = jnp.concatenate([up_ref[...], u_main, un_ref[...]], axis=0)
    r = lax.broadcasted_iota(I32, (ROWS, ROWS + 2 * HALO), 0)
    e = lax.broadcasted_iota(I32, (ROWS, ROWS + 2 * HALO), 1) - HALO
    in_seq = (j * ROWS + e) < seq
    pos = j * ROWS + lax.broadcasted_iota(I32, (ROWS, 1), 0)
    y_pool = []
    for gi, w in enumerate(POOL_WINDOWS):
        cs = slice(gi * POOL_GROUP, (gi + 1) * POOL_GROUP)
        band = jnp.logical_and(jnp.logical_and(e >= r - w // 2, e < r + w - w // 2), in_seq)
        count = w - jnp.maximum(pos + (w - w // 2) - seq, 0)
        mean = _dot(band.astype(BF16), u_ext[:, cs]) / count.astype(F32)
        d = mean - u_main[:, cs].astype(F32)
        y_pool.append(_dot(d.astype(BF16), pw_ref[gi]))
    y_pool = (jnp.concatenate(y_pool, axis=1) * ps_ref[...]).astype(BF16)

    o = of_ref[...].astype(F32) + ob_ref[...].astype(F32)
    y_gla = []
    for hd in range(GLA_HEADS):
        oh = o[:, hd * HEAD_V:(hd + 1) * HEAD_V]
        y_gla.append(oh * lax.rsqrt(jnp.mean(oh * oh, axis=-1, keepdims=True) + RMS_EPS))
    y_gla = (jnp.concatenate(y_gla, axis=1) * gn_ref[...] * sg_ref[...].astype(F32)).astype(BF16)

    merged = (gp_ref[...].astype(F32) * _dot(y_pool, wbp_ref[...])
              + gg_ref[...].astype(F32) * _dot(y_gla, wbg_ref[...]))
    y = _dot(merged.astype(BF16), wo_ref[...])
    h = _layer_norm(x_ref[...], g0_ref[...], b0_ref[...])
    h1 = _layer_norm(DN_ALPHA * h + y, g1_ref[...], b1_ref[...])
    h1_ref[...] = h1
    h1p_ref[...] = _pack_halves(h1[:, :D_MODEL // 2], h1[:, D_MODEL // 2:])


def _mix(x2, u, gp, gg, sg, o_f, o_b, ln0_g, ln0_b, pool_w, pool_scale, norm_g, wbp, wbg, wo,
         ln1_g, ln1_b, seq):
    t = x2.shape[0]
    tiles_per_batch = seq // ROWS
    padded_tiles = tiles_per_batch + FRONT // ROWS
    halo_per_tile = ROWS // HALO
    last_halo = u.shape[0] // HALO - 1

    def pidx(i):
        return (i // tiles_per_batch) * padded_tiles + FRONT // ROWS + i % tiles_per_batch

    padded = lambda width: pl.BlockSpec((ROWS, width), lambda i: (pidx(i), 0))
    plain = lambda width: pl.BlockSpec((ROWS, width), lambda i: (i, 0))
    prev_halo = pl.BlockSpec((HALO, POOL_WIDTH), lambda i: (pidx(i) * halo_per_tile - 1, 0))
    next_halo = pl.BlockSpec(
        (HALO, POOL_WIDTH), lambda i: (jnp.minimum((pidx(i) + 1) * halo_per_tile, last_halo), 0))
    return pl.pallas_call(
        functools.partial(_mix_body, tiles_per_batch=tiles_per_batch, seq=seq),
        grid=(t // ROWS,),
        in_specs=[plain(D_MODEL), prev_halo, padded(POOL_WIDTH), next_halo,
                  padded(D_MODEL), padded(D_MODEL), padded(GLA_DV), padded(GLA_DV), padded(GLA_DV),
                  _const_spec((1, D_MODEL)), _const_spec((1, D_MODEL)),
                  _const_spec((len(POOL_WINDOWS), POOL_GROUP, POOL_GROUP)), _const_spec((1, POOL_WIDTH)),
                  _const_spec((1, GLA_DV)), _const_spec((POOL_WIDTH, D_MODEL)),
                  _const_spec((GLA_DV, D_MODEL)), _const_spec((D_MODEL, D_MODEL)),
                  _const_spec((1, D_MODEL)), _const_spec((1, D_MODEL))],
        out_specs=[plain(D_MODEL), plain(D_MODEL // 2)],
        out_shape=[jax.ShapeDtypeStruct((t, D_MODEL), F32), jax.ShapeDtypeStruct((t, D_MODEL // 2), U32)],
        compiler_params=pltpu.CompilerParams(dimension_semantics=("arbitrary",),
                                             vmem_limit_bytes=VMEM_LIMIT),
        name="mix",
    )(x2, u, u, u, gp, gg, sg, o_f, o_b, ln0_g, ln0_b, pool_w, pool_scale, norm_g, wbp, wbg, wo,
      ln1_g, ln1_b)


def _first_index_of_max(val, idx, size):
    m = jnp.max(val, axis=0, keepdims=True)
    first = jnp.min(jnp.where(val == m, idx, size), axis=0, keepdims=True)
    return m, first


def _route_body(h_ref, wr_ref, bias_ref, eidx_ref, w_ref, before_ref, cnt_ref, base_ref):
    n = ROUTE_ROWS

    @pl.when(pl.program_id(0) == 0)
    def _():
        base_ref[...] = jnp.zeros_like(base_ref)

    h = h_ref[...]
    h_hi = h.astype(BF16)
    h_mid = (h - h_hi.astype(F32)).astype(BF16)
    logits = _dot_nt(wr_ref[...], jnp.concatenate([h_hi, h_mid, h_hi], axis=1))
    scores = jax.nn.sigmoid(logits)
    sel = scores + bias_ref[...]

    gidx = lax.broadcasted_iota(I32, (GROUP_SIZE, n), 0)
    groups = [sel[g * GROUP_SIZE:(g + 1) * GROUP_SIZE, :] for g in range(N_GROUPS)]
    gscore = []
    for grp in groups:
        m1, first = _first_index_of_max(grp, gidx, GROUP_SIZE)
        m2 = jnp.max(jnp.where(gidx == first, -jnp.inf, grp), axis=0, keepdims=True)
        gscore.append(m1 + m2)

    masked = []
    for gi in range(N_GROUPS):
        beaten = jnp.zeros((1, n), F32)
        for gj in range(N_GROUPS):
            if gj != gi:
                wins = (gscore[gj] >= gscore[gi]) if gj < gi else (gscore[gj] > gscore[gi])
                beaten = beaten + wins.astype(F32)
        masked.append(jnp.where(beaten < TOPK_GROUPS, groups[gi], -jnp.inf))
    val = jnp.concatenate(masked, axis=0)

    eidx = lax.broadcasted_iota(I32, (N_EXPERTS, n), 0)
    candidate = val != -jnp.inf
    picks, weights = [], []
    for _ in range(TOP_K):
        _, first = _first_index_of_max(val, eidx, N_EXPERTS)
        hit = eidx == first
        picks.append(first)
        weights.append(jnp.sum(jnp.where(hit, scores, 0.0), axis=0, keepdims=True))
        val = jnp.where(hit, -jnp.inf, val)
    chosen_f = jnp.where(jnp.logical_and(candidate, val == -jnp.inf), 1.0, 0.0)
    wsum = weights[0]
    for wk in weights[1:]:
        wsum = wsum + wk
    scale = ROUTED_SCALE / wsum

    r = lax.broadcasted_iota(I32, (n, n), 0)
    c = lax.broadcasted_iota(I32, (n, n), 1)
    earlier = (r < c).astype(BF16)
    before_ref[...] = _dot(chosen_f.astype(BF16), earlier) + base_ref[...]
    base_ref[...] = base_ref[...] + jnp.sum(chosen_f, axis=1, keepdims=True)
    cnt_ref[...] = jnp.broadcast_to(base_ref[...], cnt_ref.shape).astype(I32)

    for kk in range(TOP_K):
        eidx_ref[kk:kk + 1, :] = picks[kk]
        w_ref[kk:kk + 1, :] = weights[kk] * scale


def _route(h1, wr_t, bias_col):
    t = h1.shape[0]
    col = lambda dtype: jax.ShapeDtypeStruct((TOP_K, t), dtype)
    kspec = pl.BlockSpec((TOP_K, ROUTE_ROWS), lambda i: (0, i))
    return pl.pallas_call(
        _route_body,
        grid=(t // ROUTE_ROWS,),
        in_specs=[pl.BlockSpec((ROUTE_ROWS, D_MODEL), lambda i: (i, 0)),
                  _const_spec((N_EXPERTS, 3 * D_MODEL)), _const_spec((N_EXPERTS, 1))],
        out_specs=[kspec, kspec, pl.BlockSpec((N_EXPERTS, ROUTE_ROWS), lambda i: (0, i)),
                   _const_spec((N_EXPERTS, LANE))],
        out_shape=[col(I32), col(F32), jax.ShapeDtypeStruct((N_EXPERTS, t), F32),
                   jax.ShapeDtypeStruct((N_EXPERTS, LANE), I32)],
        scratch_shapes=[pltpu.VMEM((N_EXPERTS, 1), F32)],
        compiler_params=pltpu.CompilerParams(dimension_semantics=("arbitrary",),
                                             vmem_limit_bytes=VMEM_LIMIT),
        name="route",
    )(h1, wr_t, bias_col)


def _dest_body(eidx_ref, before_ref, pstart_ref, dest_ref):
    n = eidx_ref.shape[1]
    eiota = lax.broadcasted_iota(I32, (N_EXPERTS, n), 0)
    row_of = pstart_ref[...] + before_ref[...]
    for kk in range(TOP_K):
        dest = jnp.sum(jnp.where(eiota == eidx_ref[kk:kk + 1, :], row_of, 0.0), axis=0, keepdims=True)
        dest = dest.astype(I32)
        for c in range(n // SC_CHUNK):
            dest_ref[c, kk:kk + 1, :] = dest[:, c * SC_CHUNK:(c + 1) * SC_CHUNK]


def _dest(eidx, before, pstart_col):
    t = eidx.shape[1]
    kspec = pl.BlockSpec((TOP_K, ROUTE_ROWS), lambda i: (0, i))
    chunks = ROUTE_ROWS // SC_CHUNK
    return pl.pallas_call(
        _dest_body,
        grid=(t // ROUTE_ROWS,),
        in_specs=[kspec, pl.BlockSpec((N_EXPERTS, ROUTE_ROWS), lambda i: (0, i)),
                  _const_spec((N_EXPERTS, 1))],
        out_specs=pl.BlockSpec((chunks, TOP_K, SC_CHUNK), lambda i: (i, 0, 0)),
        out_shape=jax.ShapeDtypeStruct((t // SC_CHUNK, TOP_K, SC_CHUNK), I32),
        compiler_params=pltpu.CompilerParams(dimension_semantics=("arbitrary",)),
        name="dest",
    )(eidx, before, pstart_col)


def _sc_mesh():
    return plsc.VectorSubcoreMesh(core_axis_name="c", subcore_axis_name="s",
                                  num_cores=SC_CORES, num_subcores=SC_SUBCORES)


def _sc_worker_chunks(t):
    assert t % (SC_CORES * SC_SUBCORES * SC_CHUNK) == 0
    per_worker = t // (SC_CORES * SC_SUBCORES * SC_CHUNK)
    worker = lax.axis_index("s") * SC_CORES + lax.axis_index("c")
    return worker * per_worker, per_worker


def _dispatch(dest, h1p, n_rows):
    t, width = h1p.shape

    @functools.partial(
        pl.kernel, mesh=_sc_mesh(), out_type=jax.ShapeDtypeStruct((n_rows, width), U32),
        scratch_types=[pltpu.VMEM((SC_CHUNK, width), U32), pltpu.VMEM((TOP_K, SC_CHUNK), I32),
                       pltpu.SemaphoreType.DMA],
        name="dispatch")
    def body(h_hbm, dest_hbm, xs_hbm, rows_v, idx_v, sem):
        first, count = _sc_worker_chunks(t)

        @pl.loop(0, count)
        def _(i):
            chunk = first + i
            pltpu.sync_copy(h_hbm.at[pl.ds(pl.multiple_of(chunk * SC_CHUNK, SC_CHUNK), SC_CHUNK)], rows_v)
            pltpu.sync_copy(dest_hbm.at[chunk], idx_v)
            copies = [pltpu.async_copy(rows_v, xs_hbm.at[idx_v.at[kk]], sem) for kk in range(TOP_K)]
            for c in copies:
                c.wait()

    return body(h1p, dest)


def _gather(dest, ys):
    t = dest.shape[0] * SC_CHUNK
    width = ys.shape[1]

    @functools.partial(
        pl.kernel, mesh=_sc_mesh(), out_type=jax.ShapeDtypeStruct((TOP_K, t, width), U32),
        scratch_types=[pltpu.VMEM((SC_CHUNK, width), U32), pltpu.VMEM((TOP_K, SC_CHUNK), I32),
                       pltpu.SemaphoreType.DMA],
        name="gather")
    def body(ys_hbm, dest_hbm, yg_hbm, rows_v, idx_v, sem):
        first, count = _sc_worker_chunks(t)

        @pl.loop(0, count)
        def _(i):
            chunk = first + i
            pltpu.sync_copy(dest_hbm.at[chunk], idx_v)
            for kk in range(TOP_K):
                pltpu.async_copy(ys_hbm.at[idx_v.at[kk]], rows_v, sem).wait()
                pltpu.sync_copy(
                    rows_v, yg_hbm.at[kk, pl.ds(pl.multiple_of(chunk * SC_CHUNK, SC_CHUNK), SC_CHUNK)])

    return body(ys, dest)


def _experts_body(bs_ref, xs_in_ref, wg_ref, wu_ref, wd_ref, xs_ref,
                  xbuf, ybuf, wgb, wub, wdb, sem_in, sem_out):
    del xs_in_ref
    e = pl.program_id(0)
    b0 = bs_ref[e]
    b1 = bs_ref[e + 1]
    n_total = bs_ref[N_EXPERTS]
    half = D_MODEL // 2

    def fetch(b):
        slot = b % IN_SLOTS
        return pltpu.make_async_copy(xs_ref.at[pl.ds(b * MOE_BLOCK, MOE_BLOCK)], xbuf.at[slot],
                                     sem_in.at[slot])

    def store(b):
        slot = b % OUT_SLOTS
        return pltpu.make_async_copy(ybuf.at[slot], xs_ref.at[pl.ds(b * MOE_BLOCK, MOE_BLOCK)],
                                     sem_out.at[slot])

    @pl.when(e == 0)
    def _():
        for b in range(LOOKAHEAD):
            @pl.when(b < n_total)
            def _():
                fetch(b).start(priority=1)

    @pl.when(b1 > b0)
    def _():
        wgb[...] = wg_ref[0].astype(BF16)
        wub[...] = wu_ref[0].astype(BF16)
        wdb[...] = wd_ref[0].astype(BF16)

    def run(blocks):
        for b in blocks:
            @pl.when(b + LOOKAHEAD < n_total)
            def _():
                fetch(b + LOOKAHEAD).start(priority=1)
        for b in blocks:
            fetch(b).wait()
        for b in blocks:
            @pl.when(b >= OUT_SLOTS)
            def _():
                store(b - OUT_SLOTS).wait()

        xa, xb = _unpack_halves(jnp.concatenate([xbuf[b % IN_SLOTS] for b in blocks], axis=0))
        xa = xa.astype(BF16)
        xb = xb.astype(BF16)
        gate = _dot(xa, wgb[:half, :]) + _dot(xb, wgb[half:, :])
        up = _dot(xa, wub[:half, :]) + _dot(xb, wub[half:, :])
        hidden = (gate * jax.nn.sigmoid(gate) * up).astype(BF16)
        y = _dot(hidden, wdb[...])
        packed = _pack_halves(y[:, :half], y[:, half:])
        for i, b in enumerate(blocks):
            ybuf[b % OUT_SLOTS] = packed[i * MOE_BLOCK:(i + 1) * MOE_BLOCK]
            store(b).start(priority=1)

    n = b1 - b0

    def group(i, carry):
        run([b0 + GROUP * i + j for j in range(GROUP)])
        return carry

    lax.fori_loop(0, n // GROUP, group, 0)
    rest = b0 + (n // GROUP) * GROUP
    left = n % GROUP
    size = GROUP // 2
    while size >= 1:
        @pl.when(left & size != 0)
        def _():
            first = rest + (left & ~(2 * size - 1))
            run([first + j for j in range(size)])
        size //= 2

    @pl.when(e == N_EXPERTS - 1)
    def _():
        for back in range(OUT_SLOTS, 0, -1):
            @pl.when(n_total >= back)
            def _():
                store(n_total - back).wait()


def _experts(block_start, xs, w_gate, w_up, w_down):
    n_rows, width = xs.shape
    wspec = lambda shape: pl.BlockSpec((1,) + shape, lambda e, bs: (e, 0, 0))
    grid_spec = pltpu.PrefetchScalarGridSpec(
        num_scalar_prefetch=1,
        grid=(N_EXPERTS,),
        in_specs=[pl.BlockSpec(memory_space=pl.ANY),
                  wspec((D_MODEL, D_EXPERT)), wspec((D_MODEL, D_EXPERT)), wspec((D_EXPERT, D_MODEL))],
        out_specs=pl.BlockSpec(memory_space=pl.ANY),
        scratch_shapes=[pltpu.VMEM((IN_SLOTS, MOE_BLOCK, width), U32),
                        pltpu.VMEM((OUT_SLOTS, MOE_BLOCK, width), U32),
                        pltpu.VMEM((D_MODEL, D_EXPERT), BF16), pltpu.VMEM((D_MODEL, D_EXPERT), BF16),
                        pltpu.VMEM((D_EXPERT, D_MODEL), BF16),
                        pltpu.SemaphoreType.DMA((IN_SLOTS,)), pltpu.SemaphoreType.DMA((OUT_SLOTS,))],
    )
    return pl.pallas_call(
        _experts_body,
        grid_spec=grid_spec,
        out_shape=jax.ShapeDtypeStruct((n_rows, width), U32),
        input_output_aliases={1: 0},
        compiler_params=pltpu.CompilerParams(dimension_semantics=("arbitrary",),
                                             vmem_limit_bytes=VMEM_LIMIT),
        name="experts",
    )(block_start, xs, w_gate, w_up, w_down)


def _combine_body(h1_ref, w_ref, yg_ref, wsg_ref, wsu_ref, wsd_ref, g2_ref, b2_ref, *refs):
    out_ref = refs[-1]
    h1 = h1_ref[...]
    hb = h1.astype(BF16)
    gate = _dot(hb, wsg_ref[...])
    up = _dot(hb, wsu_ref[...])
    shared = _dot((gate * jax.nn.sigmoid(gate) * up).astype(BF16), wsd_ref[...])

    half = D_MODEL // 2
    lo = jnp.zeros((ROWS, half), F32)
    hi = jnp.zeros((ROWS, half), F32)
    for kk in range(TOP_K):
        a, b = _unpack_halves(yg_ref[kk])
        wk = w_ref[:, kk:kk + 1]
        lo = lo + wk * a
        hi = hi + wk * b
    z = DN_ALPHA * h1 + shared + jnp.concatenate([lo, hi], axis=1)
    out_ref[...] = _layer_norm(z, g2_ref[...], b2_ref[...])


def _combine(h1, w_rows, yg, first_tile, out_prev, wsg, wsu, wsd, ln2_g, ln2_b):
    width = yg.shape[2]
    d_shared = wsg.shape[1]
    rows = lambda cols: pl.BlockSpec((ROWS, cols), lambda i: (first_tile + i, 0))
    args = [h1, w_rows, yg, wsg, wsu, wsd, ln2_g, ln2_b]
    in_specs = [rows(D_MODEL), rows(TOP_K),
                pl.BlockSpec((TOP_K, ROWS, width), lambda i: (0, i, 0)),
                _const_spec((D_MODEL, d_shared)), _const_spec((D_MODEL, d_shared)),
                _const_spec((d_shared, D_MODEL)),
                _const_spec((1, D_MODEL)), _const_spec((1, D_MODEL))]
    aliases = {}
    if out_prev is not None:
        aliases = {len(args): 0}
        args.append(out_prev)
        in_specs.append(pl.BlockSpec(memory_space=pl.ANY))
    return pl.pallas_call(
        _combine_body,
        grid=(yg.shape[1] // ROWS,),
        in_specs=in_specs,
        out_specs=rows(D_MODEL),
        out_shape=jax.ShapeDtypeStruct(h1.shape, F32),
        input_output_aliases=aliases,
        compiler_params=pltpu.CompilerParams(dimension_semantics=("arbitrary",),
                                             vmem_limit_bytes=VMEM_LIMIT),
        name="combine",
    )(*args)


def _pack_in_proj(w_in, b_in, gate_w_f, gate_w_b, gate_b_f, gate_b_b):
    o_pool, o_q, o_k, o_v, o_g = 0, 512, 1024, 1536, 2560
    o_lr, o_gp, o_gg = 3584, 3616, 4640
    order = [(o_pool, POOL_WIDTH), (o_q, GLA_DK), (o_k, GLA_DK), (o_v, GLA_DV), (o_g, GLA_DV),
             (o_gp, D_MODEL), (o_gg, D_MODEL)] + [(o_lr, 2 * GATE_RANK)] * 3
    pad = LANE - 6 * GATE_RANK
    w = jnp.concatenate([w_in[:, o:o + n] for o, n in order] + [jnp.zeros((D_MODEL, pad), F32)], axis=1)
    b = jnp.concatenate([b_in[o:o + n] for o, n in order] + [jnp.zeros((pad,), F32)])
    w_gate = jnp.zeros((2 * GATE_RANK, 2 * GLA_DK), F32)
    w_gate = w_gate.at[:GATE_RANK, :GLA_DK].set(gate_w_f)
    w_gate = w_gate.at[GATE_RANK:, GLA_DK:].set(gate_w_b)
    w_hi = w_gate.astype(BF16)
    w_lo = (w_gate - w_hi.astype(F32)).astype(BF16)
    w_gate3 = jnp.concatenate([w_hi, w_hi, w_lo, jnp.zeros((pad, 2 * GLA_DK), BF16)], axis=0)
    b_gate = jnp.concatenate([gate_b_f, gate_b_b])[None, :]
    return w.astype(BF16), b[None, :], w_gate3, b_gate


def kernel(x, meta, ln0_g, ln0_b, w_in, b_in, pool_w, pool_scale, gate_w_f, gate_b_f, gate_w_b, gate_b_b,
           gla_norm_g, w_branch_pool, w_branch_gla, w_out, ln1_g, ln1_b, w_router, router_bias,
           w_exp_gate, w_exp_up, w_exp_down, w_sh_gate, w_sh_up, w_sh_down, ln2_g, ln2_b):
    batch, seq, d = x.shape
    assert d == D_MODEL and seq % ROWS == 0 and w_in.shape[0] == 1
    assert (batch * seq) % ROUTE_ROWS == 0
    t = batch * seq
    row = lambda a: a.reshape(1, -1).astype(F32)

    lead = jnp.concatenate([jnp.zeros((FRONT - N_META, d), x.dtype), meta.astype(x.dtype)], axis=0)
    tiles_per_batch = (seq + FRONT) // ROWS
    x2 = x.reshape(t, d)

    w_packed, b_packed, w_gate, b_gate = _pack_in_proj(
        w_in[0], b_in[0], gate_w_f[0], gate_w_b[0], gate_b_f[0], gate_b_b[0])
    u, q, k, v, sg, gp, gg, bf, cb = _in_proj(
        lead, x2, row(ln0_g), row(ln0_b), w_packed, b_packed, w_gate, b_gate, tiles_per_batch)

    o_f, o_b = _gla(q, k, v, bf, cb, batch)

    h1, h1p = _mix(
        x.reshape(t, d), u, gp, gg, sg, o_f, o_b, row(ln0_g), row(ln0_b),
        pool_w[0].astype(BF16), row(pool_scale[0]), row(jnp.tile(gla_norm_g[0], GLA_HEADS)),
        w_branch_pool[0].astype(BF16), w_branch_gla[0].astype(BF16), w_out[0].astype(BF16),
        row(ln1_g[0]), row(ln1_b[0]), seq)

    wr = w_router[0].T.astype(F32)
    wr_hi = wr.astype(BF16)
    wr_mid = (wr - wr_hi.astype(F32)).astype(BF16)
    wr_terms = jnp.concatenate([wr_hi, wr_hi, wr_mid], axis=1)
    eidx, ew, before, counts = _route(h1, wr_terms, router_bias[0].reshape(N_EXPERTS, 1))

    counts = counts[:, 0]
    padded = ((counts + MOE_BLOCK - 1) // MOE_BLOCK) * MOE_BLOCK
    pend = jnp.cumsum(padded)
    pstart = pend - padded
    dest = _dest(eidx, before, pstart.astype(F32).reshape(N_EXPERTS, 1))
    n_blocks = -(-(t * TOP_K) // MOE_BLOCK) + N_EXPERTS
    block_start = (jnp.concatenate([pstart, pend[-1:]]) // MOE_BLOCK).astype(I32)

    xs = _dispatch(dest, h1p, n_blocks * MOE_BLOCK)
    ys = _experts(block_start, xs, w_exp_gate[0], w_exp_up[0], w_exp_down[0])
    shared_w = (w_sh_gate[0].astype(BF16), w_sh_up[0].astype(BF16), w_sh_down[0].astype(BF16))
    ew_rows = ew.T
    windows = t // SC_CHUNK // COMBINE_CHUNKS
    out = None
    for c in range(COMBINE_CHUNKS):
        yg = _gather(dest[c * windows:(c + 1) * windows], ys)
        out = _combine(h1, ew_rows, yg, c * windows * SC_CHUNK // ROWS, out, *shared_w,
                       row(ln2_g[0]), row(ln2_b[0]))
    return out.reshape(batch, seq, d)
```

```python
import functools

import jax
import jax.numpy as jnp
from jax import lax
from jax.experimental import pallas as pl
from jax.experimental.pallas import tpu as pltpu
from jax.experimental.pallas import tpu_sc as plsc

F32 = jnp.float32
BF16 = jnp.bfloat16
I32 = jnp.int32
U32 = jnp.uint32

D_MODEL = 1024
N_META = 16
POOL_WINDOWS = (2, 4, 8, 16)
POOL_GROUP = 128
POOL_WIDTH = POOL_GROUP * len(POOL_WINDOWS)
GLA_HEADS = 4
GLA_DK = 512
GLA_DV = 1024
HEAD_K = GLA_DK // GLA_HEADS
HEAD_V = GLA_DV // GLA_HEADS
GATE_RANK = 16
GATE_NORMALIZER = 16.0
RMS_EPS = 1e-5
N_EXPERTS = 256
TOP_K = 8
N_GROUPS = 8
GROUP_SIZE = N_EXPERTS // N_GROUPS
TOPK_GROUPS = 4
D_EXPERT = 256
ROUTED_SCALE = 2.5
DN_ALPHA = 2.0 ** 0.25
LN_EPS = 1e-5

LANE = 128
ROWS = 256
FRONT = ROWS
GLA_CHUNK = 256
GLA_BLOCKS = 2
HALO = 16
ROUTE_ROWS = 1024
ROUTE_STRIP = 128
MOE_BLOCK = 256
LOOKAHEAD = 6
GROUP = 4
assert GROUP & (GROUP - 1) == 0
IN_SLOTS = LOOKAHEAD + GROUP
OUT_SLOTS = 2 * GROUP
SC_CORES = 2
SC_SUBCORES = 16
SC_CHUNK = 128
COMBINE_CHUNKS = 8
VMEM_LIMIT = 56 * 1024 * 1024

C_POOL, C_Q, C_K, C_V, C_G, C_GP, C_GG, C_LR = 0, 512, 1024, 1536, 2560, 3584, 4608, 5632
W_PACKED = C_LR + LANE


def _layer_norm(x, g, b):
    mu = jnp.mean(x, axis=-1, keepdims=True)
    xc = x - mu
    var = jnp.mean(xc * xc, axis=-1, keepdims=True)
    return xc * lax.rsqrt(var + LN_EPS) * g + b


def _dot(a, b):
    return jnp.dot(a, b, preferred_element_type=F32)


def _dot_nt(a, b):
    return lax.dot_general(a, b, (((1,), (1,)), ((), ())), preferred_element_type=F32)


def _split_dot(m2, g):
    g_hi = g.astype(BF16)
    g_lo = (g - g_hi.astype(F32)).astype(BF16)
    return _dot(m2, jnp.concatenate([g_hi, g_lo], axis=0))


def _pack_halves(a, b):
    pa = lax.bitcast_convert_type(a.astype(BF16).astype(F32), U32)
    pb = lax.bitcast_convert_type(b.astype(BF16).astype(F32), U32)
    return pa | (pb >> 16)


def _unpack_halves(p):
    a = lax.bitcast_convert_type(p & jnp.uint32(0xFFFF0000), F32)
    b = lax.bitcast_convert_type(p << 16, F32)
    return a, b


def _const_spec(shape):
    return pl.BlockSpec(shape, lambda *_: (0,) * len(shape))


def _inproj_body(lead_ref, x_ref, g0_ref, b0_ref, w_ref, bias_ref, wgate_ref, bgate_ref,
                 u_ref, q_ref, k_ref, v_ref, sg_ref, gp_ref, gg_ref, bf_ref, cb_ref, h_even, h_odd,
                 *, tiles_per_batch):
    i = pl.program_id(0)

    def normalised(is_lead):
        x_in = jnp.where(is_lead, lead_ref[...], x_ref[...])
        return _layer_norm(x_in, g0_ref[...], b0_ref[...]).astype(BF16)

    @pl.when(i == 0)
    def _():
        h_even[...] = normalised(True)

    for parity, (h_ref, h_next_ref) in enumerate(((h_even, h_odd), (h_odd, h_even))):
        @pl.when(i % 2 == parity)
        def _():
            h_next_ref[...] = normalised((i + 1) % tiles_per_batch == 0)
            _inproj_tile(h_ref[...], i % tiles_per_batch == 0, w_ref, bias_ref, wgate_ref, bgate_ref,
                         u_ref, q_ref, k_ref, v_ref, sg_ref, gp_ref, gg_ref, bf_ref, cb_ref)


def _inproj_tile(hb, is_lead, w_ref, bias_ref, wgate_ref, bgate_ref,
                 u_ref, q_ref, k_ref, v_ref, sg_ref, gp_ref, gg_ref, bf_ref, cb_ref):
    def proj(c0, n):
        return _dot(hb, w_ref[:, c0:c0 + n]) + bias_ref[:, c0:c0 + n]

    row = lax.broadcasted_iota(I32, (ROWS, 1), 0)
    valid = row >= jnp.where(is_lead, FRONT - N_META, 0)

    lr3 = proj(C_LR, LANE)
    g = proj(C_G, GLA_DV)
    sg_ref[...] = (g * jax.nn.sigmoid(g)).astype(BF16)
    lr_hi = lr3.astype(BF16)
    lr_lo = (lr3 - lr_hi.astype(F32)).astype(BF16)
    lane = lax.broadcasted_iota(I32, (1, LANE), 1)
    second = jnp.logical_and(lane >= 2 * GATE_RANK, lane < 4 * GATE_RANK)
    xg = _dot(jnp.where(second, lr_lo, lr_hi), wgate_ref[...]) + bgate_ref[...]
    gp_ref[...] = jax.nn.sigmoid(proj(C_GP, D_MODEL)).astype(BF16)
    gk = (jnp.minimum(xg, 0.0) - jnp.log(1.0 + jnp.exp(-jnp.abs(xg)))) * (1.0 / GATE_NORMALIZER)
    gg_ref[...] = jax.nn.sigmoid(proj(C_GG, D_MODEL)).astype(BF16)

    r = lax.broadcasted_iota(I32, (GLA_CHUNK, 2 * GLA_CHUNK), 0)
    c = lax.broadcasted_iota(I32, (GLA_CHUNK, 2 * GLA_CHUNK), 1) % GLA_CHUNK
    tril = (r >= c).astype(BF16)
    triu = (r <= c).astype(BF16)
    for ci in range(ROWS // GLA_CHUNK):
        sl = slice(ci * GLA_CHUNK, (ci + 1) * GLA_CHUNK)
        bf_ref[sl, :] = _split_dot(tril, gk[sl, :GLA_DK])
        cb_ref[sl, :] = _split_dot(triu, gk[sl, GLA_DK:])

    v_ref[...] = jnp.where(valid, proj(C_V, GLA_DV), 0.0).astype(BF16)
    k_ref[...] = jnp.where(valid, proj(C_K, GLA_DK), 0.0).astype(BF16)
    q_ref[...] = (proj(C_Q, GLA_DK) * (HEAD_K ** -0.5)).astype(BF16)
    u_ref[...] = proj(C_POOL, POOL_WIDTH).astype(BF16)


def _in_proj(lead, x2, ln0_g, ln0_b, w_packed, b_packed, w_gate, b_gate, tiles_per_batch):
    x_tiles = tiles_per_batch - FRONT // ROWS
    n = x2.shape[0] // (x_tiles * ROWS) * tiles_per_batch * ROWS
    row_spec = lambda width: pl.BlockSpec((ROWS, width), lambda i: (i, 0))
    n_steps = n // ROWS

    def next_x_tile(i):
        j = jnp.minimum(i + 1, n_steps - 1)
        return (j // tiles_per_batch) * x_tiles + jnp.maximum(j % tiles_per_batch - 1, 0), 0

    x_spec = pl.BlockSpec((ROWS, D_MODEL), next_x_tile)
    out_widths = (POOL_WIDTH, GLA_DK, GLA_DK, GLA_DV, GLA_DV, D_MODEL, D_MODEL)
    out_shape = [jax.ShapeDtypeStruct((n, w), BF16) for w in out_widths]
    out_shape += [jax.ShapeDtypeStruct((n, GLA_DK), F32)] * 2
    out_specs = [row_spec(w) for w in out_widths] + [row_spec(GLA_DK)] * 2
    return pl.pallas_call(
        functools.partial(_inproj_body, tiles_per_batch=tiles_per_batch),
        grid=(n // ROWS,),
        in_specs=[_const_spec((FRONT, D_MODEL)), x_spec,
                  _const_spec((1, D_MODEL)), _const_spec((1, D_MODEL)),
                  _const_spec((D_MODEL, W_PACKED)), _const_spec((1, W_PACKED)),
                  _const_spec((LANE, 2 * GLA_DK)), _const_spec((1, 2 * GLA_DK))],
        out_specs=out_specs,
        out_shape=out_shape,
        scratch_shapes=[pltpu.VMEM((ROWS, D_MODEL), BF16)] * 2,
        compiler_params=pltpu.CompilerParams(dimension_semantics=("arbitrary",),
                                             vmem_limit_bytes=VMEM_LIMIT),
        name="in_proj",
    )(lead, x2, ln0_g, ln0_b, w_packed, b_packed, w_gate, b_gate)


def _gla_direction(q_ref, k_ref, v_ref, b_ref, o_ref, s_ref, reverse):
    c = GLA_CHUNK
    m = c // GLA_BLOCKS
    r = lax.broadcasted_iota(I32, (m, m), 0)
    s = lax.broadcasted_iota(I32, (m, m), 1)
    keep = (r <= s) if reverse else (r >= s)
    block_of_row = lax.broadcasted_iota(I32, (c, 1), 0) // m
    eye = lax.broadcasted_iota(I32, (HEAD_K, HEAD_K), 0) == lax.broadcasted_iota(I32, (HEAD_K, HEAD_K), 1)
    edge = 0 if reverse else c - 1
    for h in range(GLA_HEADS):
        ks = slice(h * HEAD_K, (h + 1) * HEAD_K)
        vs = slice(h * HEAD_V, (h + 1) * HEAD_V)
        b = b_ref[:, ks]
        qh = q_ref[:, ks].astype(F32)
        kh = k_ref[:, ks].astype(F32)
        vh = v_ref[:, vs]
        b_tot = b[edge:edge + 1, :]
        q_state = (qh * jnp.exp(b)).astype(BF16)
        b_mid = b[m // 2:m // 2 + 1, :]
        for blk in range(1, GLA_BLOCKS):
            b_mid = jnp.where(block_of_row >= blk, b[blk * m + m // 2:blk * m + m // 2 + 1, :], b_mid)
        q_in = (qh * jnp.exp(b - b_mid)).astype(BF16)
        k_in = (kh * jnp.exp(b_mid - b)).astype(BF16)
        att_rows = []
        for blk in range(GLA_BLOCKS):
            rows = slice(blk * m, (blk + 1) * m)
            diag = jnp.where(keep, _dot_nt(q_in[rows], k_in[rows]), 0.0).astype(BF16)
            if reverse:
                earlier, border = slice((blk + 1) * m, c), (blk + 1) * m
            else:
                earlier, border = slice(0, blk * m), blk * m - 1
            n_earlier = earlier.stop - earlier.start
            off = None
            if n_earlier:
                b_edge = b[border:border + 1, :]
                q_off = (qh[rows] * jnp.exp(b[rows] - b_edge)).astype(BF16)
                k_off = (kh[earlier] * jnp.exp(b_edge - b[earlier])).astype(BF16)
                off = _dot_nt(q_off, k_off).astype(BF16)
            n_later = c - m - n_earlier
            masked = jnp.zeros((m, n_later), BF16) if n_later else None
            row = (masked, diag, off) if reverse else (off, diag, masked)
            att_rows.append(jnp.concatenate([p for p in row if p is not None], axis=1))
        att = jnp.concatenate(att_rows, axis=0)
        k_state = (kh * jnp.exp(b_tot - b)).T.astype(BF16)
        state = s_ref[h]
        o = _dot(q_state, state.astype(BF16)) + _dot(att, vh)
        o_ref[:, vs] = o.astype(o_ref.dtype)
        decay_col = jnp.sum(jnp.where(eye, jnp.exp(b_tot), 0.0), axis=1, keepdims=True)
        s_ref[h] = decay_col * state + _dot(k_state, vh)


def _gla_body(qf_ref, kf_ref, vf_ref, bf_ref, qb_ref, kb_ref, vb_ref, cb_ref,
              of_ref, ob_ref, sf_ref, sb_ref):
    @pl.when(pl.program_id(0) == 0)
    def _():
        sf_ref[...] = jnp.zeros_like(sf_ref)
        sb_ref[...] = jnp.zeros_like(sb_ref)

    for b in range(qf_ref.shape[0]):
        _gla_direction(qf_ref.at[b], kf_ref.at[b], vf_ref.at[b], bf_ref.at[b], of_ref.at[b], sf_ref.at[b],
                       reverse=False)
        _gla_direction(qb_ref.at[b], kb_ref.at[b], vb_ref.at[b], cb_ref.at[b], ob_ref.at[b], sb_ref.at[b],
                       reverse=True)


def _gla(q, k, v, bf, cb, batch):
    n = q.shape[0]
    nch = n // batch // GLA_CHUNK
    per_batch = lambda a: a.reshape(batch, n // batch, a.shape[1])
    fwd = lambda width: pl.BlockSpec((batch, GLA_CHUNK, width), lambda i: (0, i, 0))
    bwd = lambda width: pl.BlockSpec((batch, GLA_CHUNK, width), lambda i: (0, nch - 1 - i, 0))
    state = pltpu.VMEM((batch, GLA_HEADS, HEAD_K, HEAD_V), F32)
    q, k, v, bf, cb = (per_batch(a) for a in (q, k, v, bf, cb))
    o_f, o_b = pl.pallas_call(
        _gla_body,
        grid=(nch,),
        in_specs=[fwd(GLA_DK), fwd(GLA_DK), fwd(GLA_DV), fwd(GLA_DK),
                  bwd(GLA_DK), bwd(GLA_DK), bwd(GLA_DV), bwd(GLA_DK)],
        out_specs=[fwd(GLA_DV), bwd(GLA_DV)],
        out_shape=[jax.ShapeDtypeStruct((batch, n // batch, GLA_DV), BF16)] * 2,
        scratch_shapes=[state, state],
        compiler_params=pltpu.CompilerParams(dimension_semantics=("arbitrary",),
                                             vmem_limit_bytes=VMEM_LIMIT),
        name="gla",
    )(q, k, v, bf, q, k, v, cb)
    return o_f.reshape(n, GLA_DV), o_b.reshape(n, GLA_DV)


def _mix_body(x_ref, up_ref, u_ref, un_ref, gp_ref, gg_ref, sg_ref, of_ref, ob_ref,
              g0_ref, b0_ref, pw_ref, ps_ref, gn_ref, wbp_ref, wbg_ref, wo_ref, g1_ref, b1_ref,
              h1_ref, h1p_ref, *, tiles_per_batch, seq):
    j = pl.program_id(0) % tiles_per_batch
    u_main = u_ref[...]
    u_ext = jnp.concatenate([up_ref[...], u_main, un_ref[...]], axis=0)
    r = lax.broadcasted_iota(I32, (ROWS, ROWS + 2 * HALO), 0)
    e = lax.broadcasted_iota(I32, (ROWS, ROWS + 2 * HALO), 1) - HALO
    in_seq = (j * ROWS + e) < seq
    pos = j * ROWS + lax.broadcasted_iota(I32, (ROWS, 1), 0)
    y_pool = []
    for gi, w in enumerate(POOL_WINDOWS):
        cs = slice(gi * POOL_GROUP, (gi + 1) * POOL_GROUP)
        band = jnp.logical_and(jnp.logical_and(e >= r - w // 2, e < r + w - w // 2), in_seq)
        count = w - jnp.maximum(pos + (w - w // 2) - seq, 0)
        mean = _dot(band.astype(BF16), u_ext[:, cs]) / count.astype(F32)
        d = mean - u_main[:, cs].astype(F32)
        y_pool.append(_dot(d.astype(BF16), pw_ref[gi]))
    y_pool = (jnp.concatenate(y_pool, axis=1) * ps_ref[...]).astype(BF16)

    o = of_ref[...].astype(F32) + ob_ref[...].astype(F32)
    y_gla = []
    for hd in range(GLA_HEADS):
        oh = o[:, hd * HEAD_V:(hd + 1) * HEAD_V]
        y_gla.append(oh * lax.rsqrt(jnp.mean(oh * oh, axis=-1, keepdims=True) + RMS_EPS))
    y_gla = (jnp.concatenate(y_gla, axis=1) * gn_ref[...] * sg_ref[...].astype(F32)).astype(BF16)

    merged = (gp_ref[...].astype(F32) * _dot(y_pool, wbp_ref[...])
              + gg_ref[...].astype(F32) * _dot(y_gla, wbg_ref[...]))
    y = _dot(merged.astype(BF16), wo_ref[...])
    h = _layer_norm(x_ref[...], g0_ref[...], b0_ref[...])
    h1 = _layer_norm(DN_ALPHA * h + y, g1_ref[...], b1_ref[...])
    h1_ref[...] = h1
    h1p_ref[...] = _pack_halves(h1[:, :D_MODEL // 2], h1[:, D_MODEL // 2:])


def _mix(x2, u, gp, gg, sg, o_f, o_b, ln0_g, ln0_b, pool_w, pool_scale, norm_g, wbp, wbg, wo,
         ln1_g, ln1_b, seq):
    t = x2.shape[0]
    tiles_per_batch = seq // ROWS
    padded_tiles = tiles_per_batch + FRONT // ROWS
    halo_per_tile = ROWS // HALO
    last_halo = u.shape[0] // HALO - 1

    def pidx(i):
        return (i // tiles_per_batch) * padded_tiles + FRONT // ROWS + i % tiles_per_batch

    padded = lambda width: pl.BlockSpec((ROWS, width), lambda i: (pidx(i), 0))
    plain = lambda width: pl.BlockSpec((ROWS, width), lambda i: (i, 0))
    prev_halo = pl.BlockSpec((HALO, POOL_WIDTH), lambda i: (pidx(i) * halo_per_tile - 1, 0))
    next_halo = pl.BlockSpec(
        (HALO, POOL_WIDTH), lambda i: (jnp.minimum((pidx(i) + 1) * halo_per_tile, last_halo), 0))
    return pl.pallas_call(
        functools.partial(_mix_body, tiles_per_batch=tiles_per_batch, seq=seq),
        grid=(t // ROWS,),
        in_specs=[plain(D_MODEL), prev_halo, padded(POOL_WIDTH), next_halo,
                  padded(D_MODEL), padded(D_MODEL), padded(GLA_DV), padded(GLA_DV), padded(GLA_DV),
                  _const_spec((1, D_MODEL)), _const_spec((1, D_MODEL)),
                  _const_spec((len(POOL_WINDOWS), POOL_GROUP, POOL_GROUP)), _const_spec((1, POOL_WIDTH)),
                  _const_spec((1, GLA_DV)), _const_spec((POOL_WIDTH, D_MODEL)),
                  _const_spec((GLA_DV, D_MODEL)), _const_spec((D_MODEL, D_MODEL)),
                  _const_spec((1, D_MODEL)), _const_spec((1, D_MODEL))],
        out_specs=[plain(D_MODEL), plain(D_MODEL // 2)],
        out_shape=[jax.ShapeDtypeStruct((t, D_MODEL), F32), jax.ShapeDtypeStruct((t, D_MODEL // 2), U32)],
        compiler_params=pltpu.CompilerParams(dimension_semantics=("arbitrary",),
                                             vmem_limit_bytes=VMEM_LIMIT),
        name="mix",
    )(x2, u, u, u, gp, gg, sg, o_f, o_b, ln0_g, ln0_b, pool_w, pool_scale, norm_g, wbp, wbg, wo,
      ln1_g, ln1_b)


def _first_index_of_max(val, idx, size):
    m = jnp.max(val, axis=0, keepdims=True)
    first = jnp.min(jnp.where(val == m, idx, size), axis=0, keepdims=True)
    return m, first


def _select_experts(scores, sel):
    n = scores.shape[1]
    gidx = lax.broadcasted_iota(I32, (GROUP_SIZE, n), 0)
    groups = [sel[g * GROUP_SIZE:(g + 1) * GROUP_SIZE, :] for g in range(N_GROUPS)]
    gscore = []
    for grp in groups:
        m1, first = _first_index_of_max(grp, gidx, GROUP_SIZE)
        m2 = jnp.max(jnp.where(gidx == first, -jnp.inf, grp), axis=0, keepdims=True)
        gscore.append(m1 + m2)

    masked = []
    for gi in range(N_GROUPS):
        beaten = jnp.zeros((1, n), F32)
        for gj in range(N_GROUPS):
            if gj != gi:
                wins = (gscore[gj] >= gscore[gi]) if gj < gi else (gscore[gj] > gscore[gi])
                beaten = beaten + wins.astype(F32)
        masked.append(jnp.where(beaten < TOPK_GROUPS, groups[gi], -jnp.inf))
    val = jnp.concatenate(masked, axis=0)

    eidx = lax.broadcasted_iota(I32, (N_EXPERTS, n), 0)
    candidate = val != -jnp.inf
    picks, weights = [], []
    for _ in range(TOP_K):
        _, first = _first_index_of_max(val, eidx, N_EXPERTS)
        hit = eidx == first
        picks.append(first)
        weights.append(jnp.sum(jnp.where(hit, scores, 0.0), axis=0, keepdims=True))
        val = jnp.where(hit, -jnp.inf, val)
    chosen_f = jnp.where(jnp.logical_and(candidate, val == -jnp.inf), 1.0, 0.0)
    return picks, weights, chosen_f


def _route_body(h_ref, wr_ref, bias_ref, eidx_ref, w_ref, before_ref, cnt_ref, base_ref):
    n = ROUTE_ROWS

    @pl.when(pl.program_id(0) == 0)
    def _():
        base_ref[...] = jnp.zeros_like(base_ref)

    h = h_ref[...]
    h_hi = h.astype(BF16)
    h_mid = (h - h_hi.astype(F32)).astype(BF16)
    logits = _dot_nt(wr_ref[...], jnp.concatenate([h_hi, h_mid, h_hi], axis=1))
    scores = jax.nn.sigmoid(logits)
    sel = scores + bias_ref[...]

    strips = [_select_experts(scores[:, c0:c0 + ROUTE_STRIP], sel[:, c0:c0 + ROUTE_STRIP])
              for c0 in range(0, n, ROUTE_STRIP)]
    picks = [jnp.concatenate([st[0][kk] for st in strips], axis=1) for kk in range(TOP_K)]
    weights = [jnp.concatenate([st[1][kk] for st in strips], axis=1) for kk in range(TOP_K)]
    chosen_f = jnp.concatenate([st[2] for st in strips], axis=1)
    wsum = weights[0]
    for wk in weights[1:]:
        wsum = wsum + wk
    scale = ROUTED_SCALE / wsum

    r = lax.broadcasted_iota(I32, (n, n), 0)
    c = lax.broadcasted_iota(I32, (n, n), 1)
    earlier = (r < c).astype(BF16)
    before_ref[...] = _dot(chosen_f.astype(BF16), earlier) + base_ref[...]
    base_ref[...] = base_ref[...] + jnp.sum(chosen_f, axis=1, keepdims=True)
    cnt_ref[...] = jnp.broadcast_to(base_ref[...], cnt_ref.shape).astype(I32)

    for kk in range(TOP_K):
        eidx_ref[kk:kk + 1, :] = picks[kk]
        w_ref[kk:kk + 1, :] = weights[kk] * scale


def _route(h1, wr_t, bias_col):
    t = h1.shape[0]
    col = lambda dtype: jax.ShapeDtypeStruct((TOP_K, t), dtype)
    kspec = pl.BlockSpec((TOP_K, ROUTE_ROWS), lambda i: (0, i))
    return pl.pallas_call(
        _route_body,
        grid=(t // ROUTE_ROWS,),
        in_specs=[pl.BlockSpec((ROUTE_ROWS, D_MODEL), lambda i: (i, 0)),
                  _const_spec((N_EXPERTS, 3 * D_MODEL)), _const_spec((N_EXPERTS, 1))],
        out_specs=[kspec, kspec, pl.BlockSpec((N_EXPERTS, ROUTE_ROWS), lambda i: (0, i)),
                   _const_spec((N_EXPERTS, LANE))],
        out_shape=[col(I32), col(F32), jax.ShapeDtypeStruct((N_EXPERTS, t), F32),
                   jax.ShapeDtypeStruct((N_EXPERTS, LANE), I32)],
        scratch_shapes=[pltpu.VMEM((N_EXPERTS, 1), F32)],
        compiler_params=pltpu.CompilerParams(dimension_semantics=("arbitrary",),
                                             vmem_limit_bytes=VMEM_LIMIT),
        name="route",
    )(h1, wr_t, bias_col)


def _dest_body(eidx_ref, before_ref, pstart_ref, dest_ref):
    n = eidx_ref.shape[1]
    eiota = lax.broadcasted_iota(I32, (N_EXPERTS, n), 0)
    row_of = pstart_ref[...] + before_ref[...]
    for kk in range(TOP_K):
        dest = jnp.sum(jnp.where(eiota == eidx_ref[kk:kk + 1, :], row_of, 0.0), axis=0, keepdims=True)
        dest = dest.astype(I32)
        for c in range(n // SC_CHUNK):
            dest_ref[c, kk:kk + 1, :] = dest[:, c * SC_CHUNK:(c + 1) * SC_CHUNK]


def _dest(eidx, before, pstart_col):
    t = eidx.shape[1]
    kspec = pl.BlockSpec((TOP_K, ROUTE_ROWS), lambda i: (0, i))
    chunks = ROUTE_ROWS // SC_CHUNK
    return pl.pallas_call(
        _dest_body,
        grid=(t // ROUTE_ROWS,),
        in_specs=[kspec, pl.BlockSpec((N_EXPERTS, ROUTE_ROWS), lambda i: (0, i)),
                  _const_spec((N_EXPERTS, 1))],
        out_specs=pl.BlockSpec((chunks, TOP_K, SC_CHUNK), lambda i: (i, 0, 0)),
        out_shape=jax.ShapeDtypeStruct((t // SC_CHUNK, TOP_K, SC_CHUNK), I32),
        compiler_params=pltpu.CompilerParams(dimension_semantics=("arbitrary",)),
        name="dest",
    )(eidx, before, pstart_col)


def _sc_mesh():
    return plsc.VectorSubcoreMesh(core_axis_name="c", subcore_axis_name="s",
                                  num_cores=SC_CORES, num_subcores=SC_SUBCORES)


def _sc_worker_chunks(t):
    assert t % (SC_CORES * SC_SUBCORES * SC_CHUNK) == 0
    per_worker = t // (SC_CORES * SC_SUBCORES * SC_CHUNK)
    worker = lax.axis_index("s") * SC_CORES + lax.axis_index("c")
    return worker * per_worker, per_worker


def _dispatch(dest, h1p, n_rows):
    t, width = h1p.shape

    @functools.partial(
        pl.kernel, mesh=_sc_mesh(), out_type=jax.ShapeDtypeStruct((n_rows, width), U32),
        scratch_types=[pltpu.VMEM((SC_CHUNK, width), U32), pltpu.VMEM((TOP_K, SC_CHUNK), I32),
                       pltpu.SemaphoreType.DMA],
        name="dispatch")
    def body(h_hbm, dest_hbm, xs_hbm, rows_v, idx_v, sem):
        first, count = _sc_worker_chunks(t)

        @pl.loop(0, count)
        def _(i):
            chunk = first + i
            pltpu.sync_copy(h_hbm.at[pl.ds(pl.multiple_of(chunk * SC_CHUNK, SC_CHUNK), SC_CHUNK)], rows_v)
            pltpu.sync_copy(dest_hbm.at[chunk], idx_v)
            copies = [pltpu.async_copy(rows_v, xs_hbm.at[idx_v.at[kk]], sem) for kk in range(TOP_K)]
            for c in copies:
                c.wait()

    return body(h1p, dest)


def _gather(dest, ys):
    t = dest.shape[0] * SC_CHUNK
    width = ys.shape[1]

    @functools.partial(
        pl.kernel, mesh=_sc_mesh(), out_type=jax.ShapeDtypeStruct((TOP_K, t, width), U32),
        scratch_types=[pltpu.VMEM((SC_CHUNK, width), U32), pltpu.VMEM((TOP_K, SC_CHUNK), I32),
                       pltpu.SemaphoreType.DMA],
        name="gather")
    def body(ys_hbm, dest_hbm, yg_hbm, rows_v, idx_v, sem):
        first, count = _sc_worker_chunks(t)

        @pl.loop(0, count)
        def _(i):
            chunk = first + i
            pltpu.sync_copy(dest_hbm.at[chunk], idx_v)
            for kk in range(TOP_K):
                pltpu.async_copy(ys_hbm.at[idx_v.at[kk]], rows_v, sem).wait()
                pltpu.sync_copy(
                    rows_v, yg_hbm.at[kk, pl.ds(pl.multiple_of(chunk * SC_CHUNK, SC_CHUNK), SC_CHUNK)])

    return body(ys, dest)


def _experts_body(bs_ref, xs_in_ref, wg_ref, wu_ref, wd_ref, xs_ref,
                  xbuf, ybuf, wgb, wub, wdb, sem_in, sem_out):
    del xs_in_ref
    e = pl.program_id(0)
    b0 = bs_ref[e]
    b1 = bs_ref[e + 1]
    n_total = bs_ref[N_EXPERTS]
    half = D_MODEL // 2

    def fetch(b):
        slot = b % IN_SLOTS
        return pltpu.make_async_copy(xs_ref.at[pl.ds(b * MOE_BLOCK, MOE_BLOCK)], xbuf.at[slot],
                                     sem_in.at[slot])

    def store(b):
        slot = b % OUT_SLOTS
        return pltpu.make_async_copy(ybuf.at[slot], xs_ref.at[pl.ds(b * MOE_BLOCK, MOE_BLOCK)],
                                     sem_out.at[slot])

    @pl.when(e == 0)
    def _():
        for b in range(LOOKAHEAD):
            @pl.when(b < n_total)
            def _():
                fetch(b).start(priority=1)

    @pl.when(b1 > b0)
    def _():
        wgb[...] = wg_ref[0].astype(BF16)
        wub[...] = wu_ref[0].astype(BF16)
        wdb[...] = wd_ref[0].astype(BF16)

    def run(blocks):
        for b in blocks:
            @pl.when(b + LOOKAHEAD < n_total)
            def _():
                fetch(b + LOOKAHEAD).start(priority=1)
        for b in blocks:
            fetch(b).wait()
        for b in blocks:
            @pl.when(b >= OUT_SLOTS)
            def _():
                store(b - OUT_SLOTS).wait()

        xa, xb = _unpack_halves(jnp.concatenate([xbuf[b % IN_SLOTS] for b in blocks], axis=0))
        xa = xa.astype(BF16)
        xb = xb.astype(BF16)
        gate = _dot(xa, wgb[:half, :]) + _dot(xb, wgb[half:, :])
        up = _dot(xa, wub[:half, :]) + _dot(xb, wub[half:, :])
        hidden = (gate * jax.nn.sigmoid(gate) * up).astype(BF16)
        y = _dot(hidden, wdb[...])
        packed = _pack_halves(y[:, :half], y[:, half:])
        for i, b in enumerate(blocks):
            ybuf[b % OUT_SLOTS] = packed[i * MOE_BLOCK:(i + 1) * MOE_BLOCK]
            store(b).start(priority=1)

    n = b1 - b0

    def group(i, carry):
        run([b0 + GROUP * i + j for j in range(GROUP)])
        return carry

    lax.fori_loop(0, n // GROUP, group, 0)
    rest = b0 + (n // GROUP) * GROUP
    left = n % GROUP
    size = GROUP // 2
    while size >= 1:
        @pl.when(left & size != 0)
        def _():
            first = rest + (left & ~(2 * size - 1))
            run([first + j for j in range(size)])
        size //= 2

    @pl.when(e == N_EXPERTS - 1)
    def _():
        for back in range(OUT_SLOTS, 0, -1):
            @pl.when(n_total >= back)
            def _():
                store(n_total - back).wait()


def _experts(block_start, xs, w_gate, w_up, w_down):
    n_rows, width = xs.shape
    wspec = lambda shape: pl.BlockSpec((1,) + shape, lambda e, bs: (e, 0, 0))
    grid_spec = pltpu.PrefetchScalarGridSpec(
        num_scalar_prefetch=1,
        grid=(N_EXPERTS,),
        in_specs=[pl.BlockSpec(memory_space=pl.ANY),
                  wspec((D_MODEL, D_EXPERT)), wspec((D_MODEL, D_EXPERT)), wspec((D_EXPERT, D_MODEL))],
        out_specs=pl.BlockSpec(memory_space=pl.ANY),
        scratch_shapes=[pltpu.VMEM((IN_SLOTS, MOE_BLOCK, width), U32),
                        pltpu.VMEM((OUT_SLOTS, MOE_BLOCK, width), U32),
                        pltpu.VMEM((D_MODEL, D_EXPERT), BF16), pltpu.VMEM((D_MODEL, D_EXPERT), BF16),
                        pltpu.VMEM((D_EXPERT, D_MODEL), BF16),
                        pltpu.SemaphoreType.DMA((IN_SLOTS,)), pltpu.SemaphoreType.DMA((OUT_SLOTS,))],
    )
    return pl.pallas_call(
        _experts_body,
        grid_spec=grid_spec,
        out_shape=jax.ShapeDtypeStruct((n_rows, width), U32),
        input_output_aliases={1: 0},
        compiler_params=pltpu.CompilerParams(dimension_semantics=("arbitrary",),
                                             vmem_limit_bytes=VMEM_LIMIT),
        name="experts",
    )(block_start, xs, w_gate, w_up, w_down)


def _combine_body(h1_ref, w_ref, yg_ref, wsg_ref, wsu_ref, wsd_ref, g2_ref, b2_ref, *refs):
    out_ref = refs[-1]
    h1 = h1_ref[...]
    hb = h1.astype(BF16)
    gate = _dot(hb, wsg_ref[...])
    up = _dot(hb, wsu_ref[...])
    shared = _dot((gate * jax.nn.sigmoid(gate) * up).astype(BF16), wsd_ref[...])

    half = D_MODEL // 2
    lo = jnp.zeros((ROWS, half), F32)
    hi = jnp.zeros((ROWS, half), F32)
    for kk in range(TOP_K):
        a, b = _unpack_halves(yg_ref[kk])
        wk = w_ref[:, kk:kk + 1]
        lo = lo + wk * a
        hi = hi + wk * b
    z = DN_ALPHA * h1 + shared + jnp.concatenate([lo, hi], axis=1)
    out_ref[...] = _layer_norm(z, g2_ref[...], b2_ref[...])


def _combine(h1, w_rows, yg, first_tile, out_prev, wsg, wsu, wsd, ln2_g, ln2_b):
    width = yg.shape[2]
    d_shared = wsg.shape[1]
    rows = lambda cols: pl.BlockSpec((ROWS, cols), lambda i: (first_tile + i, 0))
    args = [h1, w_rows, yg, wsg, wsu, wsd, ln2_g, ln2_b]
    in_specs = [rows(D_MODEL), rows(TOP_K),
                pl.BlockSpec((TOP_K, ROWS, width), lambda i: (0, i, 0)),
                _const_spec((D_MODEL, d_shared)), _const_spec((D_MODEL, d_shared)),
                _const_spec((d_shared, D_MODEL)),
                _const_spec((1, D_MODEL)), _const_spec((1, D_MODEL))]
    aliases = {}
    if out_prev is not None:
        aliases = {len(args): 0}
        args.append(out_prev)
        in_specs.append(pl.BlockSpec(memory_space=pl.ANY))
    return pl.pallas_call(
        _combine_body,
        grid=(yg.shape[1] // ROWS,),
        in_specs=in_specs,
        out_specs=rows(D_MODEL),
        out_shape=jax.ShapeDtypeStruct(h1.shape, F32),
        input_output_aliases=aliases,
        compiler_params=pltpu.CompilerParams(dimension_semantics=("arbitrary",),
                                             vmem_limit_bytes=VMEM_LIMIT),
        name="combine",
    )(*args)


def _pack_in_proj(w_in, b_in, gate_w_f, gate_w_b, gate_b_f, gate_b_b):
    o_pool, o_q, o_k, o_v, o_g = 0, 512, 1024, 1536, 2560
    o_lr, o_gp, o_gg = 3584, 3616, 4640
    order = [(o_pool, POOL_WIDTH), (o_q, GLA_DK), (o_k, GLA_DK), (o_v, GLA_DV), (o_g, GLA_DV),
             (o_gp, D_MODEL), (o_gg, D_MODEL)] + [(o_lr, 2 * GATE_RANK)] * 3
    pad = LANE - 6 * GATE_RANK
    w = jnp.concatenate([w_in[:, o:o + n] for o, n in order] + [jnp.zeros((D_MODEL, pad), F32)], axis=1)
    b = jnp.concatenate([b_in[o:o + n] for o, n in order] + [jnp.zeros((pad,), F32)])
    w_gate = jnp.zeros((2 * GATE_RANK, 2 * GLA_DK), F32)
    w_gate = w_gate.at[:GATE_RANK, :GLA_DK].set(gate_w_f)
    w_gate = w_gate.at[GATE_RANK:, GLA_DK:].set(gate_w_b)
    w_hi = w_gate.astype(BF16)
    w_lo = (w_gate - w_hi.astype(F32)).astype(BF16)
    w_gate3 = jnp.concatenate([w_hi, w_hi, w_lo, jnp.zeros((pad, 2 * GLA_DK), BF16)], axis=0)
    b_gate = jnp.concatenate([gate_b_f, gate_b_b])[None, :]
    return w.astype(BF16), b[None, :], w_gate3, b_gate


def kernel(x, meta, ln0_g, ln0_b, w_in, b_in, pool_w, pool_scale, gate_w_f, gate_b_f, gate_w_b, gate_b_b,
           gla_norm_g, w_branch_pool, w_branch_gla, w_out, ln1_g, ln1_b, w_router, router_bias,
           w_exp_gate, w_exp_up, w_exp_down, w_sh_gate, w_sh_up, w_sh_down, ln2_g, ln2_b):
    batch, seq, d = x.shape
    assert d == D_MODEL and seq % ROWS == 0 and w_in.shape[0] == 1
    assert (batch * seq) % ROUTE_ROWS == 0
    t = batch * seq
    row = lambda a: a.reshape(1, -1).astype(F32)

    lead = jnp.concatenate([jnp.zeros((FRONT - N_META, d), x.dtype), meta.astype(x.dtype)], axis=0)
    tiles_per_batch = (seq + FRONT) // ROWS
    x2 = x.reshape(t, d)

    w_packed, b_packed, w_gate, b_gate = _pack_in_proj(
        w_in[0], b_in[0], gate_w_f[0], gate_w_b[0], gate_b_f[0], gate_b_b[0])
    u, q, k, v, sg, gp, gg, bf, cb = _in_proj(
        lead, x2, row(ln0_g), row(ln0_b), w_packed, b_packed, w_gate, b_gate, tiles_per_batch)

    o_f, o_b = _gla(q, k, v, bf, cb, batch)

    h1, h1p = _mix(
        x.reshape(t, d), u, gp, gg, sg, o_f, o_b, row(ln0_g), row(ln0_b),
        pool_w[0].astype(BF16), row(pool_scale[0]), row(jnp.tile(gla_norm_g[0], GLA_HEADS)),
        w_branch_pool[0].astype(BF16), w_branch_gla[0].astype(BF16), w_out[0].astype(BF16),
        row(ln1_g[0]), row(ln1_b[0]), seq)

    wr = w_router[0].T.astype(F32)
    wr_hi = wr.astype(BF16)
    wr_mid = (wr - wr_hi.astype(F32)).astype(BF16)
    wr_terms = jnp.concatenate([wr_hi, wr_hi, wr_mid], axis=1)
    eidx, ew, before, counts = _route(h1, wr_terms, router_bias[0].reshape(N_EXPERTS, 1))

    counts = counts[:, 0]
    padded = ((counts + MOE_BLOCK - 1) // MOE_BLOCK) * MOE_BLOCK
    pend = jnp.cumsum(padded)
    pstart = pend - padded
    dest = _dest(eidx, before, pstart.astype(F32).reshape(N_EXPERTS, 1))
    n_blocks = -(-(t * TOP_K) // MOE_BLOCK) + N_EXPERTS
    block_start = (jnp.concatenate([pstart, pend[-1:]]) // MOE_BLOCK).astype(I32)

    xs = _dispatch(dest, h1p, n_blocks * MOE_BLOCK)
    ys = _experts(block_start, xs, w_exp_gate[0], w_exp_up[0], w_exp_down[0])
    shared_w = (w_sh_gate[0].astype(BF16), w_sh_up[0].astype(BF16), w_sh_down[0].astype(BF16))
    ew_rows = ew.T
    windows = t // SC_CHUNK // COMBINE_CHUNKS
    out = None
    for c in range(COMBINE_CHUNKS):
        yg = _gather(dest[c * windows:(c + 1) * windows], ys)
        out = _combine(h1, ew_rows, yg, c * windows * SC_CHUNK // ROWS, out, *shared_w,
                       row(ln2_g[0]), row(ln2_b[0]))
    return out.reshape(batch, seq, d)
```

```python
import functools

import jax
import jax.numpy as jnp
from jax import lax
from jax.experimental import pallas as pl
from jax.experimental.pallas import tpu as pltpu
from jax.experimental.pallas import tpu_sc as plsc

F32 = jnp.float32
BF16 = jnp.bfloat16
I32 = jnp.int32
U32 = jnp.uint32

D_MODEL = 1024
N_META = 16
POOL_WINDOWS = (2, 4, 8, 16)
POOL_GROUP = 128
POOL_WIDTH = POOL_GROUP * len(POOL_WINDOWS)
GLA_HEADS = 4
GLA_DK = 512
GLA_DV = 1024
HEAD_K = GLA_DK // GLA_HEADS
HEAD_V = GLA_DV // GLA_HEADS
GATE_RANK = 16
GATE_NORMALIZER = 16.0
RMS_EPS = 1e-5
N_EXPERTS = 256
TOP_K = 8
N_GROUPS = 8
GROUP_SIZE = N_EXPERTS // N_GROUPS
TOPK_GROUPS = 4
D_EXPERT = 256
ROUTED_SCALE = 2.5
DN_ALPHA = 2.0 ** 0.25
LN_EPS = 1e-5

LANE = 128
ROWS = 256
FRONT = ROWS
GLA_CHUNK = 256
GLA_BLOCKS = 2
HALO = 16
ROUTE_ROWS = 1024
ROUTE_STRIP = 128
MOE_BLOCK = 256
LOOKAHEAD = 6
GROUP = 4
assert GROUP & (GROUP - 1) == 0
IN_SLOTS = LOOKAHEAD + GROUP
OUT_SLOTS = 2 * GROUP
SC_CORES = 2
SC_SUBCORES = 16
SC_CHUNK = 128
COMBINE_CHUNKS = 8
VMEM_LIMIT = 56 * 1024 * 1024

C_POOL, C_Q, C_K, C_V, C_G, C_GP, C_GG, C_LR = 0, 512, 1024, 1536, 2560, 3584, 4608, 5632
W_PACKED = C_LR + LANE


def _layer_norm(x, g, b):
    mu = jnp.mean(x, axis=-1, keepdims=True)
    xc = x - mu
    var = jnp.mean(xc * xc, axis=-1, keepdims=True)
    return xc * lax.rsqrt(var + LN_EPS) * g + b


def _dot(a, b):
    return jnp.dot(a, b, preferred_element_type=F32)


def _dot_nt(a, b):
    return lax.dot_general(a, b, (((1,), (1,)), ((), ())), preferred_element_type=F32)


def _split_dot(m2, g):
    g_hi = g.astype(BF16)
    g_lo = (g - g_hi.astype(F32)).astype(BF16)
    return _dot(m2, jnp.concatenate([g_hi, g_lo], axis=0))


def _pack_halves(a, b):
    pa = lax.bitcast_convert_type(a.astype(BF16).astype(F32), U32)
    pb = lax.bitcast_convert_type(b.astype(BF16).astype(F32), U32)
    return pa | (pb >> 16)


def _unpack_halves(p):
    a = lax.bitcast_convert_type(p & jnp.uint32(0xFFFF0000), F32)
    b = lax.bitcast_convert_type(p << 16, F32)
    return a, b


def _const_spec(shape):
    return pl.BlockSpec(shape, lambda *_: (0,) * len(shape), pipeline_mode=pl.Buffered(1))


def _inproj_body(lead_ref, x_ref, g0_ref, b0_ref, w_ref, bias_ref, wgate_ref, bgate_ref,
                 u_ref, q_ref, k_ref, v_ref, sg_ref, gp_ref, gg_ref, bf_ref, cb_ref, h_even, h_odd,
                 *, tiles_per_batch):
    i = pl.program_id(0)

    def normalised(is_lead):
        x_in = jnp.where(is_lead, lead_ref[...], x_ref[...])
        return _layer_norm(x_in, g0_ref[...], b0_ref[...]).astype(BF16)

    @pl.when(i == 0)
    def _():
        h_even[...] = normalised(True)

    for parity, (h_ref, h_next_ref) in enumerate(((h_even, h_odd), (h_odd, h_even))):
        @pl.when(i % 2 == parity)
        def _():
            h_next_ref[...] = normalised((i + 1) % tiles_per_batch == 0)
            _inproj_tile(h_ref[...], i % tiles_per_batch == 0, w_ref, bias_ref, wgate_ref, bgate_ref,
                         u_ref, q_ref, k_ref, v_ref, sg_ref, gp_ref, gg_ref, bf_ref, cb_ref)


def _inproj_tile(hb, is_lead, w_ref, bias_ref, wgate_ref, bgate_ref,
                 u_ref, q_ref, k_ref, v_ref, sg_ref, gp_ref, gg_ref, bf_ref, cb_ref):
    def proj(c0, n):
        return _dot(hb, w_ref[:, c0:c0 + n]) + bias_ref[:, c0:c0 + n]

    row = lax.broadcasted_iota(I32, (ROWS, 1), 0)
    valid = row >= jnp.where(is_lead, FRONT - N_META, 0)

    lr3 = proj(C_LR, LANE)
    g = proj(C_G, GLA_DV)
    sg_ref[...] = (g * jax.nn.sigmoid(g)).astype(BF16)
    lr_hi = lr3.astype(BF16)
    lr_lo = (lr3 - lr_hi.astype(F32)).astype(BF16)
    lane = lax.broadcasted_iota(I32, (1, LANE), 1)
    second = jnp.logical_and(lane >= 2 * GATE_RANK, lane < 4 * GATE_RANK)
    xg = _dot(jnp.where(second, lr_lo, lr_hi), wgate_ref[...]) + bgate_ref[...]
    gp_ref[...] = jax.nn.sigmoid(proj(C_GP, D_MODEL)).astype(BF16)
    gk = (jnp.minimum(xg, 0.0) - jnp.log(1.0 + jnp.exp(-jnp.abs(xg)))) * (1.0 / GATE_NORMALIZER)
    gg_ref[...] = jax.nn.sigmoid(proj(C_GG, D_MODEL)).astype(BF16)

    r = lax.broadcasted_iota(I32, (GLA_CHUNK, 2 * GLA_CHUNK), 0)
    c = lax.broadcasted_iota(I32, (GLA_CHUNK, 2 * GLA_CHUNK), 1) % GLA_CHUNK
    tril = (r >= c).astype(BF16)
    triu = (r <= c).astype(BF16)
    for ci in range(ROWS // GLA_CHUNK):
        sl = slice(ci * GLA_CHUNK, (ci + 1) * GLA_CHUNK)
        bf_ref[sl, :] = _split_dot(tril, gk[sl, :GLA_DK])
        cb_ref[sl, :] = _split_dot(triu, gk[sl, GLA_DK:])

    v_ref[...] = jnp.where(valid, proj(C_V, GLA_DV), 0.0).astype(BF16)
    k_ref[...] = jnp.where(valid, proj(C_K, GLA_DK), 0.0).astype(BF16)
    q_ref[...] = (proj(C_Q, GLA_DK) * (HEAD_K ** -0.5)).astype(BF16)
    u_ref[...] = proj(C_POOL, POOL_WIDTH).astype(BF16)


def _in_proj(lead, x2, ln0_g, ln0_b, w_packed, b_packed, w_gate, b_gate, tiles_per_batch):
    x_tiles = tiles_per_batch - FRONT // ROWS
    n = x2.shape[0] // (x_tiles * ROWS) * tiles_per_batch * ROWS
    row_spec = lambda width: pl.BlockSpec((ROWS, width), lambda i: (i, 0))
    n_steps = n // ROWS

    def next_x_tile(i):
        j = jnp.minimum(i + 1, n_steps - 1)
        return (j // tiles_per_batch) * x_tiles + jnp.maximum(j % tiles_per_batch - 1, 0), 0

    x_spec = pl.BlockSpec((ROWS, D_MODEL), next_x_tile)
    out_widths = (POOL_WIDTH, GLA_DK, GLA_DK, GLA_DV, GLA_DV, D_MODEL, D_MODEL)
    out_shape = [jax.ShapeDtypeStruct((n, w), BF16) for w in out_widths]
    out_shape += [jax.ShapeDtypeStruct((n, GLA_DK), F32)] * 2
    out_specs = [row_spec(w) for w in out_widths] + [row_spec(GLA_DK)] * 2
    return pl.pallas_call(
        functools.partial(_inproj_body, tiles_per_batch=tiles_per_batch),
        grid=(n // ROWS,),
        in_specs=[_const_spec((FRONT, D_MODEL)), x_spec,
                  _const_spec((1, D_MODEL)), _const_spec((1, D_MODEL)),
                  _const_spec((D_MODEL, W_PACKED)), _const_spec((1, W_PACKED)),
                  _const_spec((LANE, 2 * GLA_DK)), _const_spec((1, 2 * GLA_DK))],
        out_specs=out_specs,
        out_shape=out_shape,
        scratch_shapes=[pltpu.VMEM((ROWS, D_MODEL), BF16)] * 2,
        compiler_params=pltpu.CompilerParams(dimension_semantics=("arbitrary",),
                                             vmem_limit_bytes=VMEM_LIMIT),
        name="in_proj",
    )(lead, x2, ln0_g, ln0_b, w_packed, b_packed, w_gate, b_gate)


def _gla_direction(q_ref, k_ref, v_ref, b_ref, o_ref, s_ref, reverse):
    c = GLA_CHUNK
    m = c // GLA_BLOCKS
    r = lax.broadcasted_iota(I32, (m, m), 0)
    s = lax.broadcasted_iota(I32, (m, m), 1)
    keep = (r <= s) if reverse else (r >= s)
    block_of_row = lax.broadcasted_iota(I32, (c, 1), 0) // m
    eye = lax.broadcasted_iota(I32, (HEAD_K, HEAD_K), 0) == lax.broadcasted_iota(I32, (HEAD_K, HEAD_K), 1)
    edge = 0 if reverse else c - 1
    for h in range(GLA_HEADS):
        ks = slice(h * HEAD_K, (h + 1) * HEAD_K)
        vs = slice(h * HEAD_V, (h + 1) * HEAD_V)
        b = b_ref[:, ks]
        qh = q_ref[:, ks].astype(F32)
        kh = k_ref[:, ks].astype(F32)
        vh = v_ref[:, vs]
        b_tot = b[edge:edge + 1, :]
        q_state = (qh * jnp.exp(b)).astype(BF16)
        b_mid = b[m // 2:m // 2 + 1, :]
        for blk in range(1, GLA_BLOCKS):
            b_mid = jnp.where(block_of_row >= blk, b[blk * m + m // 2:blk * m + m // 2 + 1, :], b_mid)
        q_in = (qh * jnp.exp(b - b_mid)).astype(BF16)
        k_in = (kh * jnp.exp(b_mid - b)).astype(BF16)
        att_rows = []
        for blk in range(GLA_BLOCKS):
            rows = slice(blk * m, (blk + 1) * m)
            diag = jnp.where(keep, _dot_nt(q_in[rows], k_in[rows]), 0.0).astype(BF16)
            if reverse:
                earlier, border = slice((blk + 1) * m, c), (blk + 1) * m
            else:
                earlier, border = slice(0, blk * m), blk * m - 1
            n_earlier = earlier.stop - earlier.start
            off = None
            if n_earlier:
                b_edge = b[border:border + 1, :]
                q_off = (qh[rows] * jnp.exp(b[rows] - b_edge)).astype(BF16)
                k_off = (kh[earlier] * jnp.exp(b_edge - b[earlier])).astype(BF16)
                off = _dot_nt(q_off, k_off).astype(BF16)
            n_later = c - m - n_earlier
            masked = jnp.zeros((m, n_later), BF16) if n_later else None
            row = (masked, diag, off) if reverse else (off, diag, masked)
            att_rows.append(jnp.concatenate([p for p in row if p is not None], axis=1))
        att = jnp.concatenate(att_rows, axis=0)
        k_state = (kh * jnp.exp(b_tot - b)).T.astype(BF16)
        state = s_ref[h]
        o = _dot(q_state, state.astype(BF16)) + _dot(att, vh)
        o_ref[:, vs] = o.astype(o_ref.dtype)
        decay_col = jnp.sum(jnp.where(eye, jnp.exp(b_tot), 0.0), axis=1, keepdims=True)
        s_ref[h] = decay_col * state + _dot(k_state, vh)


def _gla_body(qf_ref, kf_ref, vf_ref, bf_ref, qb_ref, kb_ref, vb_ref, cb_ref,
              of_ref, ob_ref, sf_ref, sb_ref):
    @pl.when(pl.program_id(0) == 0)
    def _():
        sf_ref[...] = jnp.zeros_like(sf_ref)
        sb_ref[...] = jnp.zeros_like(sb_ref)

    for b in range(qf_ref.shape[0]):
        _gla_direction(qf_ref.at[b], kf_ref.at[b], vf_ref.at[b], bf_ref.at[b], of_ref.at[b], sf_ref.at[b],
                       reverse=False)
        _gla_direction(qb_ref.at[b], kb_ref.at[b], vb_ref.at[b], cb_ref.at[b], ob_ref.at[b], sb_ref.at[b],
                       reverse=True)


def _gla(q, k, v, bf, cb, batch):
    n = q.shape[0]
    nch = n // batch // GLA_CHUNK
    per_batch = lambda a: a.reshape(batch, n // batch, a.shape[1])
    fwd = lambda width: pl.BlockSpec((batch, GLA_CHUNK, width), lambda i: (0, i, 0))
    bwd = lambda width: pl.BlockSpec((batch, GLA_CHUNK, width), lambda i: (0, nch - 1 - i, 0))
    state = pltpu.VMEM((batch, GLA_HEADS, HEAD_K, HEAD_V), F32)
    q, k, v, bf, cb = (per_batch(a) for a in (q, k, v, bf, cb))
    o_f, o_b = pl.pallas_call(
        _gla_body,
        grid=(nch,),
        in_specs=[fwd(GLA_DK), fwd(GLA_DK), fwd(GLA_DV), fwd(GLA_DK),
                  bwd(GLA_DK), bwd(GLA_DK), bwd(GLA_DV), bwd(GLA_DK)],
        out_specs=[fwd(GLA_DV), bwd(GLA_DV)],
        out_shape=[jax.ShapeDtypeStruct((batch, n // batch, GLA_DV), BF16)] * 2,
        scratch_shapes=[state, state],
        compiler_params=pltpu.CompilerParams(dimension_semantics=("arbitrary",),
                                             vmem_limit_bytes=VMEM_LIMIT),
        name="gla",
    )(q, k, v, bf, q, k, v, cb)
    return o_f.reshape(n, GLA_DV), o_b.reshape(n, GLA_DV)


def _mix_body(x_ref, up_ref, u_ref, un_ref, gp_ref, gg_ref, sg_ref, of_ref, ob_ref,
              g0_ref, b0_ref, pw_ref, ps_ref, gn_ref, wbp_ref, wbg_ref, wo_ref, g1_ref, b1_ref,
              h1_ref, h1p_ref, *, tiles_per_batch, seq):
    j = pl.program_id(0) % tiles_per_batch
    u_main = u_ref[...]
    u_ext = jnp.concatenate([up_ref[...], u_main, un_ref[...]], axis=0)
    r = lax.broadcasted_iota(I32, (ROWS, ROWS + 2 * HALO), 0)
    e = lax.broadcasted_iota(I32, (ROWS, ROWS + 2 * HALO), 1) - HALO
    in_seq = (j * ROWS + e) < seq
    pos = j * ROWS + lax.broadcasted_iota(I32, (ROWS, 1), 0)
    y_pool = []
    for gi, w in enumerate(POOL_WINDOWS):
        cs = slice(gi * POOL_GROUP, (gi + 1) * POOL_GROUP)
        band = jnp.logical_and(jnp.logical_and(e >= r - w // 2, e < r + w - w // 2), in_seq)
        count = w - jnp.maximum(pos + (w - w // 2) - seq, 0)
        mean = _dot(band.astype(BF16), u_ext[:, cs]) / count.astype(F32)
        d = mean - u_main[:, cs].astype(F32)
        y_pool.append(_dot(d.astype(BF16), pw_ref[gi]))
    y_pool = (jnp.concatenate(y_pool, axis=1) * ps_ref[...]).astype(BF16)

    o = of_ref[...].astype(F32) + ob_ref[...].astype(F32)
    y_gla = []
    for hd in range(GLA_HEADS):
        oh = o[:, hd * HEAD_V:(hd + 1) * HEAD_V]
        y_gla.append(oh * lax.rsqrt(jnp.mean(oh * oh, axis=-1, keepdims=True) + RMS_EPS))
    y_gla = (jnp.concatenate(y_gla, axis=1) * gn_ref[...] * sg_ref[...].astype(F32)).astype(BF16)

    merged = (gp_ref[...].astype(F32) * _dot(y_pool, wbp_ref[...])
              + gg_ref[...].astype(F32) * _dot(y_gla, wbg_ref[...]))
    y = _dot(merged.astype(BF16), wo_ref[...])
    h = _layer_norm(x_ref[...], g0_ref[...], b0_ref[...])
    h1 = _layer_norm(DN_ALPHA * h + y, g1_ref[...], b1_ref[...])
    h1_ref[...] = h1
    h1p_ref[...] = _pack_halves(h1[:, :D_MODEL // 2], h1[:, D_MODEL // 2:])


def _mix(x2, u, gp, gg, sg, o_f, o_b, ln0_g, ln0_b, pool_w, pool_scale, norm_g, wbp, wbg, wo,
         ln1_g, ln1_b, seq):
    t = x2.shape[0]
    tiles_per_batch = seq // ROWS
    padded_tiles = tiles_per_batch + FRONT // ROWS
    halo_per_tile = ROWS // HALO
    last_halo = u.shape[0] // HALO - 1

    def pidx(i):
        return (i // tiles_per_batch) * padded_tiles + FRONT // ROWS + i % tiles_per_batch

    padded = lambda width: pl.BlockSpec((ROWS, width), lambda i: (pidx(i), 0))
    plain = lambda width: pl.BlockSpec((ROWS, width), lambda i: (i, 0))
    prev_halo = pl.BlockSpec((HALO, POOL_WIDTH), lambda i: (pidx(i) * halo_per_tile - 1, 0))
    next_halo = pl.BlockSpec(
        (HALO, POOL_WIDTH), lambda i: (jnp.minimum((pidx(i) + 1) * halo_per_tile, last_halo), 0))
    return pl.pallas_call(
        functools.partial(_mix_body, tiles_per_batch=tiles_per_batch, seq=seq),
        grid=(t // ROWS,),
        in_specs=[plain(D_MODEL), prev_halo, padded(POOL_WIDTH), next_halo,
                  padded(D_MODEL), padded(D_MODEL), padded(GLA_DV), padded(GLA_DV), padded(GLA_DV),
                  _const_spec((1, D_MODEL)), _const_spec((1, D_MODEL)),
                  _const_spec((len(POOL_WINDOWS), POOL_GROUP, POOL_GROUP)), _const_spec((1, POOL_WIDTH)),
                  _const_spec((1, GLA_DV)), _const_spec((POOL_WIDTH, D_MODEL)),
                  _const_spec((GLA_DV, D_MODEL)), _const_spec((D_MODEL, D_MODEL)),
                  _const_spec((1, D_MODEL)), _const_spec((1, D_MODEL))],
        out_specs=[plain(D_MODEL), plain(D_MODEL // 2)],
        out_shape=[jax.ShapeDtypeStruct((t, D_MODEL), F32), jax.ShapeDtypeStruct((t, D_MODEL // 2), U32)],
        compiler_params=pltpu.CompilerParams(dimension_semantics=("arbitrary",),
                                             vmem_limit_bytes=VMEM_LIMIT),
        name="mix",
    )(x2, u, u, u, gp, gg, sg, o_f, o_b, ln0_g, ln0_b, pool_w, pool_scale, norm_g, wbp, wbg, wo,
      ln1_g, ln1_b)


def _first_index_of_max(val, idx, size):
    m = jnp.max(val, axis=0, keepdims=True)
    first = jnp.min(jnp.where(val == m, idx, size), axis=0, keepdims=True)
    return m, first


def _select_experts(scores, sel):
    n = scores.shape[1]
    gidx = lax.broadcasted_iota(I32, (GROUP_SIZE, n), 0)
    groups = [sel[g * GROUP_SIZE:(g + 1) * GROUP_SIZE, :] for g in range(N_GROUPS)]
    gscore = []
    for grp in groups:
        m1, first = _first_index_of_max(grp, gidx, GROUP_SIZE)
        m2 = jnp.max(jnp.where(gidx == first, -jnp.inf, grp), axis=0, keepdims=True)
        gscore.append(m1 + m2)

    masked = []
    for gi in range(N_GROUPS):
        beaten = jnp.zeros((1, n), F32)
        for gj in range(N_GROUPS):
            if gj != gi:
                wins = (gscore[gj] >= gscore[gi]) if gj < gi else (gscore[gj] > gscore[gi])
                beaten = beaten + wins.astype(F32)
        masked.append(jnp.where(beaten < TOPK_GROUPS, groups[gi], -jnp.inf))
    val = jnp.concatenate(masked, axis=0)

    eidx = lax.broadcasted_iota(I32, (N_EXPERTS, n), 0)
    candidate = val != -jnp.inf
    picks, weights = [], []
    for _ in range(TOP_K):
        _, first = _first_index_of_max(val, eidx, N_EXPERTS)
        hit = eidx == first
        picks.append(first)
        weights.append(jnp.sum(jnp.where(hit, scores, 0.0), axis=0, keepdims=True))
        val = jnp.where(hit, -jnp.inf, val)
    chosen_f = jnp.where(jnp.logical_and(candidate, val == -jnp.inf), 1.0, 0.0)
    return picks, weights, chosen_f


def _route_body(h_ref, wr_ref, bias_ref, eidx_ref, w_ref, before_ref, cnt_ref, base_ref):
    n = ROUTE_ROWS

    @pl.when(pl.program_id(0) == 0)
    def _():
        base_ref[...] = jnp.zeros_like(base_ref)

    h = h_ref[...]
    h_hi = h.astype(BF16)
    h_mid = (h - h_hi.astype(F32)).astype(BF16)
    logits = _dot_nt(wr_ref[...], jnp.concatenate([h_hi, h_mid, h_hi], axis=1))
    scores = jax.nn.sigmoid(logits)
    sel = scores + bias_ref[...]

    strips = [_select_experts(scores[:, c0:c0 + ROUTE_STRIP], sel[:, c0:c0 + ROUTE_STRIP])
              for c0 in range(0, n, ROUTE_STRIP)]
    picks = [jnp.concatenate([st[0][kk] for st in strips], axis=1) for kk in range(TOP_K)]
    weights = [jnp.concatenate([st[1][kk] for st in strips], axis=1) for kk in range(TOP_K)]
    chosen_f = jnp.concatenate([st[2] for st in strips], axis=1)
    wsum = weights[0]
    for wk in weights[1:]:
        wsum = wsum + wk
    scale = ROUTED_SCALE / wsum

    r = lax.broadcasted_iota(I32, (n, n), 0)
    c = lax.broadcasted_iota(I32, (n, n), 1)
    earlier = (r < c).astype(BF16)
    before_ref[...] = _dot(chosen_f.astype(BF16), earlier) + base_ref[...]
    base_ref[...] = base_ref[...] + jnp.sum(chosen_f, axis=1, keepdims=True)
    cnt_ref[...] = jnp.broadcast_to(base_ref[...], cnt_ref.shape).astype(I32)

    for kk in range(TOP_K):
        eidx_ref[kk:kk + 1, :] = picks[kk]
        w_ref[kk:kk + 1, :] = weights[kk] * scale


def _route(h1, wr_t, bias_col):
    t = h1.shape[0]
    col = lambda dtype: jax.ShapeDtypeStruct((TOP_K, t), dtype)
    kspec = pl.BlockSpec((TOP_K, ROUTE_ROWS), lambda i: (0, i))
    return pl.pallas_call(
        _route_body,
        grid=(t // ROUTE_ROWS,),
        in_specs=[pl.BlockSpec((ROUTE_ROWS, D_MODEL), lambda i: (i, 0)),
                  _const_spec((N_EXPERTS, 3 * D_MODEL)), _const_spec((N_EXPERTS, 1))],
        out_specs=[kspec, kspec, pl.BlockSpec((N_EXPERTS, ROUTE_ROWS), lambda i: (0, i)),
                   pl.BlockSpec((N_EXPERTS, LANE), lambda i: (0, 0))],
        out_shape=[col(I32), col(F32), jax.ShapeDtypeStruct((N_EXPERTS, t), F32),
                   jax.ShapeDtypeStruct((N_EXPERTS, LANE), I32)],
        scratch_shapes=[pltpu.VMEM((N_EXPERTS, 1), F32)],
        compiler_params=pltpu.CompilerParams(dimension_semantics=("arbitrary",),
                                             vmem_limit_bytes=VMEM_LIMIT),
        name="route",
    )(h1, wr_t, bias_col)


def _dest_body(eidx_ref, before_ref, pstart_ref, dest_ref):
    n = eidx_ref.shape[1]
    eiota = lax.broadcasted_iota(I32, (N_EXPERTS, n), 0)
    row_of = pstart_ref[...] + before_ref[...]
    for kk in range(TOP_K):
        dest = jnp.sum(jnp.where(eiota == eidx_ref[kk:kk + 1, :], row_of, 0.0), axis=0, keepdims=True)
        dest = dest.astype(I32)
        for c in range(n // SC_CHUNK):
            dest_ref[c, kk:kk + 1, :] = dest[:, c * SC_CHUNK:(c + 1) * SC_CHUNK]


def _dest(eidx, before, pstart_col):
    t = eidx.shape[1]
    kspec = pl.BlockSpec((TOP_K, ROUTE_ROWS), lambda i: (0, i))
    chunks = ROUTE_ROWS // SC_CHUNK
    return pl.pallas_call(
        _dest_body,
        grid=(t // ROUTE_ROWS,),
        in_specs=[kspec, pl.BlockSpec((N_EXPERTS, ROUTE_ROWS), lambda i: (0, i)),
                  _const_spec((N_EXPERTS, 1))],
        out_specs=pl.BlockSpec((chunks, TOP_K, SC_CHUNK), lambda i: (i, 0, 0)),
        out_shape=jax.ShapeDtypeStruct((t // SC_CHUNK, TOP_K, SC_CHUNK), I32),
        compiler_params=pltpu.CompilerParams(dimension_semantics=("arbitrary",)),
        name="dest",
    )(eidx, before, pstart_col)


def _sc_mesh():
    return plsc.VectorSubcoreMesh(core_axis_name="c", subcore_axis_name="s",
                                  num_cores=SC_CORES, num_subcores=SC_SUBCORES)


def _sc_worker_chunks(t):
    assert t % (SC_CORES * SC_SUBCORES * SC_CHUNK) == 0
    per_worker = t // (SC_CORES * SC_SUBCORES * SC_CHUNK)
    worker = lax.axis_index("s") * SC_CORES + lax.axis_index("c")
    return worker * per_worker, per_worker


def _dispatch(dest, h1p, n_rows):
    t, width = h1p.shape

    @functools.partial(
        pl.kernel, mesh=_sc_mesh(), out_type=jax.ShapeDtypeStruct((n_rows, width), U32),
        scratch_types=[pltpu.VMEM((SC_CHUNK, width), U32), pltpu.VMEM((TOP_K, SC_CHUNK), I32),
                       pltpu.SemaphoreType.DMA],
        name="dispatch")
    def body(h_hbm, dest_hbm, xs_hbm, rows_v, idx_v, sem):
        first, count = _sc_worker_chunks(t)

        @pl.loop(0, count)
        def _(i):
            chunk = first + i
            pltpu.sync_copy(h_hbm.at[pl.ds(pl.multiple_of(chunk * SC_CHUNK, SC_CHUNK), SC_CHUNK)], rows_v)
            pltpu.sync_copy(dest_hbm.at[chunk], idx_v)
            copies = [pltpu.async_copy(rows_v, xs_hbm.at[idx_v.at[kk]], sem) for kk in range(TOP_K)]
            for c in copies:
                c.wait()

    return body(h1p, dest)


def _gather(dest, ys):
    t = dest.shape[0] * SC_CHUNK
    width = ys.shape[1]

    @functools.partial(
        pl.kernel, mesh=_sc_mesh(), out_type=jax.ShapeDtypeStruct((TOP_K, t, width), U32),
        scratch_types=[pltpu.VMEM((SC_CHUNK, width), U32), pltpu.VMEM((TOP_K, SC_CHUNK), I32),
                       pltpu.SemaphoreType.DMA],
        name="gather")
    def body(ys_hbm, dest_hbm, yg_hbm, rows_v, idx_v, sem):
        first, count = _sc_worker_chunks(t)

        @pl.loop(0, count)
        def _(i):
            chunk = first + i
            pltpu.sync_copy(dest_hbm.at[chunk], idx_v)
            for kk in range(TOP_K):
                pltpu.async_copy(ys_hbm.at[idx_v.at[kk]], rows_v, sem).wait()
                pltpu.sync_copy(
                    rows_v, yg_hbm.at[kk, pl.ds(pl.multiple_of(chunk * SC_CHUNK, SC_CHUNK), SC_CHUNK)])

    return body(ys, dest)


def _experts_body(bs_ref, xs_in_ref, wg_ref, wu_ref, wd_ref, xs_ref,
                  xbuf, ybuf, wgb, wub, wdb, sem_in, sem_out):
    del xs_in_ref
    e = pl.program_id(0)
    b0 = bs_ref[e]
    b1 = bs_ref[e + 1]
    n_total = bs_ref[N_EXPERTS]
    half = D_MODEL // 2

    def fetch(b):
        slot = b % IN_SLOTS
        return pltpu.make_async_copy(xs_ref.at[pl.ds(b * MOE_BLOCK, MOE_BLOCK)], xbuf.at[slot],
                                     sem_in.at[slot])

    def store(b):
        slot = b % OUT_SLOTS
        return pltpu.make_async_copy(ybuf.at[slot], xs_ref.at[pl.ds(b * MOE_BLOCK, MOE_BLOCK)],
                                     sem_out.at[slot])

    @pl.when(e == 0)
    def _():
        for b in range(LOOKAHEAD):
            @pl.when(b < n_total)
            def _():
                fetch(b).start(priority=1)

    @pl.when(b1 > b0)
    def _():
        wgb[...] = wg_ref[0].astype(BF16)
        wub[...] = wu_ref[0].astype(BF16)
        wdb[...] = wd_ref[0].astype(BF16)

    def run(blocks):
        for b in blocks:
            @pl.when(b + LOOKAHEAD < n_total)
            def _():
                fetch(b + LOOKAHEAD).start(priority=1)
        for b in blocks:
            fetch(b).wait()
        for b in blocks:
            @pl.when(b >= OUT_SLOTS)
            def _():
                store(b - OUT_SLOTS).wait()

        xa, xb = _unpack_halves(jnp.concatenate([xbuf[b % IN_SLOTS] for b in blocks], axis=0))
        xa = xa.astype(BF16)
        xb = xb.astype(BF16)
        gate = _dot(xa, wgb[:half, :]) + _dot(xb, wgb[half:, :])
        up = _dot(xa, wub[:half, :]) + _dot(xb, wub[half:, :])
        hidden = (gate * jax.nn.sigmoid(gate) * up).astype(BF16)
        y = _dot(hidden, wdb[...])
        packed = _pack_halves(y[:, :half], y[:, half:])
        for i, b in enumerate(blocks):
            ybuf[b % OUT_SLOTS] = packed[i * MOE_BLOCK:(i + 1) * MOE_BLOCK]
            store(b).start(priority=1)

    n = b1 - b0

    def group(i, carry):
        run([b0 + GROUP * i + j for j in range(GROUP)])
        return carry

    lax.fori_loop(0, n // GROUP, group, 0)
    rest = b0 + (n // GROUP) * GROUP
    left = n % GROUP
    size = GROUP // 2
    while size >= 1:
        @pl.when(left & size != 0)
        def _():
            first = rest + (left & ~(2 * size - 1))
            run([first + j for j in range(size)])
        size //= 2

    @pl.when(e == N_EXPERTS - 1)
    def _():
        for back in range(OUT_SLOTS, 0, -1):
            @pl.when(n_total >= back)
            def _():
                store(n_total - back).wait()


def _experts(block_start, xs, w_gate, w_up, w_down):
    n_rows, width = xs.shape
    wspec = lambda shape: pl.BlockSpec((1,) + shape, lambda e, bs: (e, 0, 0))
    grid_spec = pltpu.PrefetchScalarGridSpec(
        num_scalar_prefetch=1,
        grid=(N_EXPERTS,),
        in_specs=[pl.BlockSpec(memory_space=pl.ANY),
                  wspec((D_MODEL, D_EXPERT)), wspec((D_MODEL, D_EXPERT)), wspec((D_EXPERT, D_MODEL))],
        out_specs=pl.BlockSpec(memory_space=pl.ANY),
        scratch_shapes=[pltpu.VMEM((IN_SLOTS, MOE_BLOCK, width), U32),
                        pltpu.VMEM((OUT_SLOTS, MOE_BLOCK, width), U32),
                        pltpu.VMEM((D_MODEL, D_EXPERT), BF16), pltpu.VMEM((D_MODEL, D_EXPERT), BF16),
                        pltpu.VMEM((D_EXPERT, D_MODEL), BF16),
                        pltpu.SemaphoreType.DMA((IN_SLOTS,)), pltpu.SemaphoreType.DMA((OUT_SLOTS,))],
    )
    return pl.pallas_call(
        _experts_body,
        grid_spec=grid_spec,
        out_shape=jax.ShapeDtypeStruct((n_rows, width), U32),
        input_output_aliases={1: 0},
        compiler_params=pltpu.CompilerParams(dimension_semantics=("arbitrary",),
                                             vmem_limit_bytes=VMEM_LIMIT),
        name="experts",
    )(block_start, xs, w_gate, w_up, w_down)


def _combine_body(h1_ref, w_ref, yg_ref, wsg_ref, wsu_ref, wsd_ref, g2_ref, b2_ref, *refs):
    out_ref = refs[-1]
    h1 = h1_ref[...]
    hb = h1.astype(BF16)
    gate = _dot(hb, wsg_ref[...])
    up = _dot(hb, wsu_ref[...])
    shared = _dot((gate * jax.nn.sigmoid(gate) * up).astype(BF16), wsd_ref[...])

    half = D_MODEL // 2
    lo = jnp.zeros((ROWS, half), F32)
    hi = jnp.zeros((ROWS, half), F32)
    for kk in range(TOP_K):
        a, b = _unpack_halves(yg_ref[kk])
        wk = w_ref[:, kk:kk + 1]
        lo = lo + wk * a
        hi = hi + wk * b
    z = DN_ALPHA * h1 + shared + jnp.concatenate([lo, hi], axis=1)
    out_ref[...] = _layer_norm(z, g2_ref[...], b2_ref[...])


def _combine(h1, w_rows, yg, first_tile, out_prev, wsg, wsu, wsd, ln2_g, ln2_b):
    width = yg.shape[2]
    d_shared = wsg.shape[1]
    rows = lambda cols: pl.BlockSpec((ROWS, cols), lambda i: (first_tile + i, 0))
    args = [h1, w_rows, yg, wsg, wsu, wsd, ln2_g, ln2_b]
    in_specs = [rows(D_MODEL), rows(TOP_K),
                pl.BlockSpec((TOP_K, ROWS, width), lambda i: (0, i, 0)),
                _const_spec((D_MODEL, d_shared)), _const_spec((D_MODEL, d_shared)),
                _const_spec((d_shared, D_MODEL)),
                _const_spec((1, D_MODEL)), _const_spec((1, D_MODEL))]
    aliases = {}
    if out_prev is not None:
        aliases = {len(args): 0}
        args.append(out_prev)
        in_specs.append(pl.BlockSpec(memory_space=pl.ANY))
    return pl.pallas_call(
        _combine_body,
        grid=(yg.shape[1] // ROWS,),
        in_specs=in_specs,
        out_specs=rows(D_MODEL),
        out_shape=jax.ShapeDtypeStruct(h1.shape, F32),
        input_output_aliases=aliases,
        compiler_params=pltpu.CompilerParams(dimension_semantics=("arbitrary",),
                                             vmem_limit_bytes=VMEM_LIMIT),
        name="combine",
    )(*args)


def _pack_in_proj(w_in, b_in, gate_w_f, gate_w_b, gate_b_f, gate_b_b):
    o_pool, o_q, o_k, o_v, o_g = 0, 512, 1024, 1536, 2560
    o_lr, o_gp, o_gg = 3584, 3616, 4640
    order = [(o_pool, POOL_WIDTH), (o_q, GLA_DK), (o_k, GLA_DK), (o_v, GLA_DV), (o_g, GLA_DV),
             (o_gp, D_MODEL), (o_gg, D_MODEL)] + [(o_lr, 2 * GATE_RANK)] * 3
    pad = LANE - 6 * GATE_RANK
    w = jnp.concatenate([w_in[:, o:o + n] for o, n in order] + [jnp.zeros((D_MODEL, pad), F32)], axis=1)
    b = jnp.concatenate([b_in[o:o + n] for o, n in order] + [jnp.zeros((pad,), F32)])
    w_gate = jnp.zeros((2 * GATE_RANK, 2 * GLA_DK), F32)
    w_gate = w_gate.at[:GATE_RANK, :GLA_DK].set(gate_w_f)
    w_gate = w_gate.at[GATE_RANK:, GLA_DK:].set(gate_w_b)
    w_hi = w_gate.astype(BF16)
    w_lo = (w_gate - w_hi.astype(F32)).astype(BF16)
    w_gate3 = jnp.concatenate([w_hi, w_hi, w_lo, jnp.zeros((pad, 2 * GLA_DK), BF16)], axis=0)
    b_gate = jnp.concatenate([gate_b_f, gate_b_b])[None, :]
    return w.astype(BF16), b[None, :], w_gate3, b_gate


def kernel(x, meta, ln0_g, ln0_b, w_in, b_in, pool_w, pool_scale, gate_w_f, gate_b_f, gate_w_b, gate_b_b,
           gla_norm_g, w_branch_pool, w_branch_gla, w_out, ln1_g, ln1_b, w_router, router_bias,
           w_exp_gate, w_exp_up, w_exp_down, w_sh_gate, w_sh_up, w_sh_down, ln2_g, ln2_b):
    batch, seq, d = x.shape
    assert d == D_MODEL and seq % ROWS == 0 and w_in.shape[0] == 1
    assert (batch * seq) % ROUTE_ROWS == 0
    t = batch * seq
    row = lambda a: a.reshape(1, -1).astype(F32)

    lead = jnp.concatenate([jnp.zeros((FRONT - N_META, d), x.dtype), meta.astype(x.dtype)], axis=0)
    tiles_per_batch = (seq + FRONT) // ROWS
    x2 = x.reshape(t, d)

    w_packed, b_packed, w_gate, b_gate = _pack_in_proj(
        w_in[0], b_in[0], gate_w_f[0], gate_w_b[0], gate_b_f[0], gate_b_b[0])
    u, q, k, v, sg, gp, gg, bf, cb = _in_proj(
        lead, x2, row(ln0_g), row(ln0_b), w_packed, b_packed, w_gate, b_gate, tiles_per_batch)

    o_f, o_b = _gla(q, k, v, bf, cb, batch)

    h1, h1p = _mix(
        x.reshape(t, d), u, gp, gg, sg, o_f, o_b, row(ln0_g), row(ln0_b),
        pool_w[0].astype(BF16), row(pool_scale[0]), row(jnp.tile(gla_norm_g[0], GLA_HEADS)),
        w_branch_pool[0].astype(BF16), w_branch_gla[0].astype(BF16), w_out[0].astype(BF16),
        row(ln1_g[0]), row(ln1_b[0]), seq)

    wr = w_router[0].T.astype(F32)
    wr_hi = wr.astype(BF16)
    wr_mid = (wr - wr_hi.astype(F32)).astype(BF16)
    wr_terms = jnp.concatenate([wr_hi, wr_hi, wr_mid], axis=1)
    eidx, ew, before, counts = _route(h1, wr_terms, router_bias[0].reshape(N_EXPERTS, 1))

    counts = counts[:, 0]
    padded = ((counts + MOE_BLOCK - 1) // MOE_BLOCK) * MOE_BLOCK
    pend = jnp.cumsum(padded)
    pstart = pend - padded
    dest = _dest(eidx, before, pstart.astype(F32).reshape(N_EXPERTS, 1))
    n_blocks = -(-(t * TOP_K) // MOE_BLOCK) + N_EXPERTS
    block_start = (jnp.concatenate([pstart, pend[-1:]]) // MOE_BLOCK).astype(I32)

    xs = _dispatch(dest, h1p, n_blocks * MOE_BLOCK)
    ys = _experts(block_start, xs, w_exp_gate[0], w_exp_up[0], w_exp_down[0])
    shared_w = (w_sh_gate[0].astype(BF16), w_sh_up[0].astype(BF16), w_sh_down[0].astype(BF16))
    ew_rows = ew.T
    windows = t // SC_CHUNK // COMBINE_CHUNKS
    out = None
    for c in range(COMBINE_CHUNKS):
        yg = _gather(dest[c * windows:(c + 1) * windows], ys)
        out = _combine(h1, ew_rows, yg, c * windows * SC_CHUNK // ROWS, out, *shared_w,
                       row(ln2_g[0]), row(ln2_b[0]))
    return out.reshape(batch, seq, d)
```
